```python
import math
import jax, jax.numpy as jnp
from jax import lax
import numpy as np

D_MODEL = 1024
BATCH = 16
SEQ = 2048
DEPTH = 1
DEC_BATCH = 128
DEC_SEQ = 1
PAST_LEN = 8192
PAGE_SIZE = 128

D_CONV = 512
CONV_W = 3
N_HEADS = 8
Q_LORA = 384
KV_LORA = 256
NOPE_DIM = 64
ROPE_DIM = 32
V_DIM = 64
QK_DIM = NOPE_DIM + ROPE_DIM
ROPE_BASE = 10000.0
ATTN_SCALE = QK_DIM ** -0.5
Q_BLOCK = 128
N_GROUPS = 4
EXP_PER_GROUP = 8
N_EXPERTS = N_GROUPS * EXP_PER_GROUP
TOP_K = 2
D_EXPERT = 512
MOE_BLOCK = 128
DN_ALPHA = (2 * DEPTH) ** 0.25
DN_BETA = (8 * DEPTH) ** -0.25
LN_EPS = 1e-5
RMS_EPS = 1e-6
_IN_SPLITS = (D_CONV, 2 * D_CONV, 3 * D_CONV, 3 * D_CONV + Q_LORA,
              3 * D_CONV + Q_LORA + KV_LORA, 3 * D_CONV + Q_LORA + KV_LORA + ROPE_DIM,
              3 * D_CONV + Q_LORA + KV_LORA + ROPE_DIM + D_MODEL)
IN_COLS = 3 * D_CONV + Q_LORA + KV_LORA + ROPE_DIM + 2 * D_MODEL

kernel_name = 'hybrid_conv_mla_hmoe_step'


def _layernorm(x, g, b):
    xf = x.astype(jnp.float32)
    mu = jnp.mean(xf, axis=-1, keepdims=True)
    var = jnp.mean(jnp.square(xf - mu), axis=-1, keepdims=True)
    return ((xf - mu) * lax.rsqrt(var + LN_EPS) * g.astype(jnp.float32) + b.astype(jnp.float32)).astype(x.dtype)


def _rmsnorm(x, g):
    xf = x.astype(jnp.float32)
    return (xf * lax.rsqrt(jnp.mean(xf * xf, axis=-1, keepdims=True) + RMS_EPS) * g.astype(jnp.float32)).astype(x.dtype)


def _rope_tables(pos):
    half = ROPE_DIM // 2
    inv = ROPE_BASE ** (-(jnp.arange(half, dtype=jnp.float32) * 2.0 / ROPE_DIM))
    ang = pos.astype(jnp.float32)[:, None] * inv[None, :]
    return jnp.cos(ang), jnp.sin(ang)


def _rope(x, cos, sin):
    x1, x2 = jnp.split(x.astype(jnp.float32), 2, axis=-1)
    return jnp.concatenate([x1 * cos - x2 * sin, x1 * sin + x2 * cos], axis=-1).astype(x.dtype)


def _in_proj(x, pos, w_in, q_norm_g, w_uq, kv_norm_g):
    n, t, _ = x.shape
    p = x @ w_in
    b_g, c_g, h, cq, ckv, kr, g_conv, g_attn = jnp.split(p, _IN_SPLITS, axis=-1)
    q = (_rmsnorm(cq, q_norm_g) @ w_uq).reshape(n, t, N_HEADS, QK_DIM)
    cos, sin = _rope_tables(pos)
    q_nope = q[..., :NOPE_DIM]
    q_rope = _rope(q[..., NOPE_DIM:], cos[:, None, :], sin[:, None, :])
    ckv = _rmsnorm(ckv, kv_norm_g)
    kr = _rope(kr, cos, sin)
    return b_g, c_g * h, q_nope, q_rope, ckv, kr, g_conv, g_attn


def _prompt_attention(q_nope, q_rope, ckv, kr, w_uk, w_uv):
    n, t = q_nope.shape[:2]
    k_nope = jnp.einsum('btc,chd->bthd', ckv, w_uk)
    v = jnp.einsum('btc,chd->bthd', ckv, w_uv)
    nb = t // Q_BLOCK

    def blocks(a):
        return jnp.moveaxis(a.reshape((n, nb, Q_BLOCK) + a.shape[2:]), 1, 0)

    kpos = jnp.arange(t)

    def one(args):
        qn, qr, i = args
        s = (jnp.einsum('bqhd,bkhd->bhqk', qn, k_nope)
             + jnp.einsum('bqhr,bkr->bhqk', qr, kr)).astype(jnp.float32) * ATTN_SCALE
        qpos = i * Q_BLOCK + jnp.arange(Q_BLOCK)
        s = jnp.where(kpos[None, :] <= qpos[:, None], s, -jnp.inf)
        pr = jax.nn.softmax(s, axis=-1).astype(v.dtype)
        return jnp.einsum('bhqk,bkhd->bqhd', pr, v)

    o = lax.map(one, (blocks(q_nope), blocks(q_rope), jnp.arange(nb)))
    return jnp.moveaxis(o, 0, 1).reshape(n, t, N_HEADS, V_DIM)


def _sample_attention(q_nope, q_rope, ckv_new, kr_new, past_c, past_r, w_uk, w_uv):
    ts = q_nope.shape[1]
    past = past_c.shape[1]
    keys_c = jnp.concatenate([past_c.astype(ckv_new.dtype), ckv_new], axis=1)
    keys_r = jnp.concatenate([past_r.astype(kr_new.dtype), kr_new], axis=1)
    q_lat = jnp.einsum('bthd,chd->bthc', q_nope, w_uk)
    s = (jnp.einsum('bthc,blc->bhtl', q_lat, keys_c)
         + jnp.einsum('bthr,blr->bhtl', q_rope, keys_r)).astype(jnp.float32) * ATTN_SCALE
    kpos = jnp.arange(past + ts)
    qpos = past + jnp.arange(ts)
    s = jnp.where(kpos[None, :] <= qpos[:, None], s, -jnp.inf)
    pr = jax.nn.softmax(s, axis=-1).astype(keys_c.dtype)
    o_lat = jnp.einsum('bhtl,blc->bthc', pr, keys_c)
    return jnp.einsum('bthc,chd->bthd', o_lat, w_uv)


def _merge(b_g, u_ext, o, g_conv, g_attn, conv_w, w_br_conv, w_br_attn, w_o):
    n, t = o.shape[:2]
    conv = conv_w[0] * u_ext[:, 0:t]
    for k in range(1, CONV_W):
        conv = conv + conv_w[k] * u_ext[:, k:k + t]
    y_conv = (b_g * conv) @ w_br_conv
    y_attn = o.reshape(n, t, N_HEADS * V_DIM) @ w_br_attn
    m = jax.nn.sigmoid(g_conv) * y_conv + jax.nn.sigmoid(g_attn) * y_attn
    return m @ w_o


def _moe(x, router_w_group, router_b_group, router_w_expert, router_b_expert, w_gate, w_up, w_down):
    shp = x.shape
    xt = x.reshape(-1, D_MODEL)
    t = xt.shape[0]
    lg = (xt @ router_w_group).astype(jnp.float32) + router_b_group.astype(jnp.float32)
    pg = jax.nn.softmax(lg, axis=-1)
    grp = jnp.argmax(pg, axis=-1)
    p_grp = jnp.max(pg, axis=-1)
    le = ((xt @ router_w_expert).astype(jnp.float32) + router_b_expert.astype(jnp.float32)).reshape(t, N_GROUPS, EXP_PER_GROUP)
    le = jnp.take_along_axis(le, grp[:, None, None], axis=1)[:, 0]
    top_l, top_i = lax.top_k(le, TOP_K)
    wts = p_grp[:, None] * jax.nn.softmax(top_l, axis=-1)
    eid = grp[:, None] * EXP_PER_GROUP + top_i
    a = t * TOP_K
    eid_f = eid.reshape(a)
    tok_f = jnp.repeat(jnp.arange(t, dtype=jnp.int32), TOP_K)
    wt_f = wts.reshape(a)
    order = jnp.argsort(eid_f)
    e_s, t_s, w_s = eid_f[order], tok_f[order], wt_f[order]
    counts = jnp.zeros((N_EXPERTS,), jnp.int32).at[eid_f].add(1)
    start = jnp.cumsum(counts) - counts
    pcounts = (counts + MOE_BLOCK - 1) // MOE_BLOCK * MOE_BLOCK
    pend = jnp.cumsum(pcounts)
    pstart = pend - pcounts
    dest = pstart[e_s] + jnp.arange(a, dtype=jnp.int32) - start[e_s]
    n_blocks = -(-a // MOE_BLOCK) + N_EXPERTS
    n_slots = n_blocks * MOE_BLOCK
    slot_tok = jnp.full((n_slots,), t, jnp.int32).at[dest].set(t_s)
    slot_w = jnp.zeros((n_slots,), jnp.float32).at[dest].set(w_s)
    blk_e = jnp.minimum(jnp.searchsorted(pend, jnp.arange(n_blocks, dtype=jnp.int32) * MOE_BLOCK, side='right'), N_EXPERTS - 1)
    x_pad = jnp.concatenate([xt, jnp.zeros((1, D_MODEL), xt.dtype)], axis=0)

    def run_block(args):
        e, toks, w = args
        xb = x_pad[toks]
        h = jax.nn.silu(xb @ w_gate[e]) * (xb @ w_up[e])
        return (h @ w_down[e]) * w[:, None].astype(xb.dtype)

    yb = lax.map(run_block, (blk_e, slot_tok.reshape(n_blocks, MOE_BLOCK), slot_w.reshape(n_blocks, MOE_BLOCK)))
    y = jnp.zeros((t + 1, D_MODEL), xt.dtype).at[slot_tok].add(yb.reshape(n_slots, D_MODEL))[:t]
    return y.reshape(shp)


def _post_norm_block(x, mix, ln1_g, ln1_b, ln2_g, ln2_b, rwg, rbg, rwe, rbe, wg, wu, wd):
    h = _layernorm(DN_ALPHA * x + mix, ln1_g, ln1_b)
    return _layernorm(DN_ALPHA * h + _moe(h, rwg, rbg, rwe, rbe, wg, wu, wd), ln2_g, ln2_b)


def setup_inputs(seed: int = 0) -> dict:
    key = jax.random.key(seed)
    ks = jax.random.split(key, 32)
    n_pages = PAST_LEN // PAGE_SIZE
    n_used = DEC_BATCH * n_pages
    n_phys = n_used + n_used // 4
    L = DEPTH

    def nrm(k, shape, scale):
        return jax.random.normal(k, shape, jnp.float32) * scale

    page_table = jax.random.permutation(ks[5], n_phys)[:n_used].reshape(DEC_BATCH, n_pages).astype(jnp.int32)
    return {
        'x_prompt': nrm(ks[0], (BATCH, SEQ, D_MODEL), 1.0),
        'x_sample': nrm(ks[1], (DEC_BATCH, DEC_SEQ, D_MODEL), 1.0),
        'cache_ckv': nrm(ks[2], (L, n_phys, PAGE_SIZE, KV_LORA), 1.0),
        'cache_krope': nrm(ks[3], (L, n_phys, PAGE_SIZE, ROPE_DIM), 1.0),
        'state_conv': nrm(ks[4], (L, DEC_BATCH, CONV_W - 1, D_CONV), 0.5),
        'page_table': page_table,
        'w_in': nrm(ks[6], (L, D_MODEL, IN_COLS), D_MODEL ** -0.5),
        'conv_w': nrm(ks[7], (L, CONV_W, D_CONV), CONV_W ** -0.5),
        'q_norm_g': 1.0 + nrm(ks[8], (L, Q_LORA), 0.02),
        'w_uq': nrm(ks[9], (L, Q_LORA, N_HEADS * QK_DIM), Q_LORA ** -0.5),
        'kv_norm_g': 1.0 + nrm(ks[10], (L, KV_LORA), 0.02),
        'w_uk': nrm(ks[11], (L, KV_LORA, N_HEADS, NOPE_DIM), KV_LORA ** -0.5),
        'w_uv': nrm(ks[12], (L, KV_LORA, N_HEADS, V_DIM), DN_BETA * KV_LORA ** -0.5),
        'w_br_conv': nrm(ks[13], (L, D_CONV, D_MODEL), DN_BETA * D_CONV ** -0.5),
        'w_br_attn': nrm(ks[14], (L, N_HEADS * V_DIM, D_MODEL), DN_BETA * (N_HEADS * V_DIM) ** -0.5),
        'w_o': nrm(ks[15], (L, D_MODEL, D_MODEL), DN_BETA * D_MODEL ** -0.5),
        'ln1_g': 1.0 + nrm(ks[16], (L, D_MODEL), 0.02),
        'ln1_b': nrm(ks[17], (L, D_MODEL), 0.02),
        'router_w_group': nrm(ks[18], (L, D_MODEL, N_GROUPS), D_MODEL ** -0.5),
        'router_b_group': nrm(ks[19], (L, N_GROUPS), 0.01),
        'router_w_expert': nrm(ks[20], (L, D_MODEL, N_EXPERTS), D_MODEL ** -0.5),
        'router_b_expert': nrm(ks[21], (L, N_EXPERTS), 0.01),
        'w_gate': nrm(ks[22], (L, N_EXPERTS, D_MODEL, D_EXPERT), D_MODEL ** -0.5),
        'w_up': nrm(ks[23], (L, N_EXPERTS, D_MODEL, D_EXPERT), DN_BETA * D_MODEL ** -0.5),
        'w_down': nrm(ks[24], (L, N_EXPERTS, D_EXPERT, D_MODEL), DN_BETA * D_EXPERT ** -0.5),
        'ln2_g': 1.0 + nrm(ks[25], (L, D_MODEL), 0.02),
        'ln2_b': nrm(ks[26], (L, D_MODEL), 0.02),
    }


def reference(x_prompt, x_sample, cache_ckv, cache_krope, state_conv, page_table,
              w_in, conv_w, q_norm_g, w_uq, kv_norm_g, w_uk, w_uv, w_br_conv, w_br_attn, w_o,
              ln1_g, ln1_b, router_w_group, router_b_group, router_w_expert, router_b_expert,
              w_gate, w_up, w_down, ln2_g, ln2_b):
    n_p, t_p, _ = x_prompt.shape
    n_s, t_s, _ = x_sample.shape
    past = page_table.shape[1] * PAGE_SIZE
    pos_p = jnp.arange(t_p)
    pos_s = past + jnp.arange(t_s)
    h_p, h_s = x_prompt, x_sample
    ckv_p, kr_p, cv_p, ckv_s, kr_s, cv_s = [], [], [], [], [], []
    for l in range(DEPTH):
        mix_w = (w_in[l], q_norm_g[l], w_uq[l], kv_norm_g[l])
        out_w = (conv_w[l], w_br_conv[l], w_br_attn[l], w_o[l])
        ffn_w = (ln1_g[l], ln1_b[l], ln2_g[l], ln2_b[l], router_w_group[l], router_b_group[l],
                 router_w_expert[l], router_b_expert[l], w_gate[l], w_up[l], w_down[l])
        b_g, u, q_nope, q_rope, ckv, kr, g_conv, g_attn = _in_proj(h_p, pos_p, *mix_w)
        u_ext = jnp.concatenate([jnp.zeros((n_p, CONV_W - 1, D_CONV), u.dtype), u], axis=1)
        o = _prompt_attention(q_nope, q_rope, ckv, kr, w_uk[l], w_uv[l])
        mix = _merge(b_g, u_ext, o, g_conv, g_attn, *out_w)
        h_p = _post_norm_block(h_p, mix, *ffn_w)
        ckv_p.append(ckv)
        kr_p.append(kr)
        cv_p.append(u_ext[:, -(CONV_W - 1):])
        b_g, u, q_nope, q_rope, ckv, kr, g_conv, g_attn = _in_proj(h_s, pos_s, *mix_w)
        u_ext = jnp.concatenate([state_conv[l].astype(u.dtype), u], axis=1)
        past_c = cache_ckv[l][page_table].reshape(n_s, past, KV_LORA)
        past_r = cache_krope[l][page_table].reshape(n_s, past, ROPE_DIM)
        o = _sample_attention(q_nope, q_rope, ckv, kr, past_c, past_r, w_uk[l], w_uv[l])
        mix = _merge(b_g, u_ext, o, g_conv, g_attn, *out_w)
        h_s = _post_norm_block(h_s, mix, *ffn_w)
        ckv_s.append(ckv)
        kr_s.append(kr)
        cv_s.append(u_ext[:, -(CONV_W - 1):])
    return (h_p, h_s, jnp.stack(ckv_p), jnp.stack(kr_p), jnp.stack(cv_p),
            jnp.stack(ckv_s), jnp.stack(kr_s), jnp.stack(cv_s))
```

```python
import functools

import jax
import jax.numpy as jnp
from jax import lax
from jax.experimental import pallas as pl
from jax.experimental.pallas import tpu as pltpu

F32 = jnp.float32
BF16 = jnp.bfloat16

D_MODEL = 1024
D_CONV = 512
CONV_W = 3
N_HEADS = 8
Q_LORA = 384
KV_LORA = 256
NOPE_DIM = 64
ROPE_DIM = 32
ROPE_HALF = ROPE_DIM // 2
V_DIM = 64
QK_DIM = NOPE_DIM + ROPE_DIM
ROPE_BASE = 10000.0
ATTN_SCALE = QK_DIM ** -0.5
N_GROUPS = 4
EXP_PER_GROUP = 8
N_EXPERTS = N_GROUPS * EXP_PER_GROUP
TOP_K = 2
D_EXPERT = 512
MOE_BLOCK = 128
PAGE_SIZE = 128
LN_EPS = 1e-5
RMS_EPS = 1e-6

LANES = 128
HEAD_W = LANES
QK_W = N_HEADS * HEAD_W
V_W = N_HEADS * V_DIM
_C_BG, _C_CG, _C_H = 0, D_CONV, 2 * D_CONV
_C_CQ = 3 * D_CONV
_C_CKV = _C_CQ + Q_LORA
_C_KR = _C_CKV + KV_LORA
W1_COLS = _C_KR + LANES
ROUTE_W = 2 * LANES
NEG = -1e30
VMEM_LIMIT = 56 * 1024 * 1024

TM_PROMPT = 512
TQ = 256
KEY_CHUNK = 2048
TM_COMBINE = 128


def _cparams(*sem):
    return pltpu.CompilerParams(dimension_semantics=sem, vmem_limit_bytes=VMEM_LIMIT)


def _dot(a, b):
    return jnp.dot(a, b, preferred_element_type=F32)


def _dot_nt(a, b):
    return lax.dot_general(a, b, (((1,), (1,)), ((), ())), preferred_element_type=F32)


def _rms(x, g):
    return x * lax.rsqrt(jnp.mean(x * x, axis=-1, keepdims=True) + RMS_EPS) * g


def _layernorm(x, g, b):
    mu = jnp.mean(x, axis=-1, keepdims=True)
    xc = x - mu
    var = jnp.mean(xc * xc, axis=-1, keepdims=True)
    return xc * lax.rsqrt(var + LN_EPS) * g + b


def _rope(xh, a, bm, cp):
    return xh * a + pltpu.roll(xh, LANES - ROPE_HALF, 1) * bm + pltpu.roll(xh, ROPE_HALF, 1) * cp


def _inproj_common(x_ref, tab_ref, w1_ref, wuq_ref, qg_ref, kvg_ref, q_ref, ckv_ref, kr_ref):
    xb = x_ref[...].astype(BF16)

    def seg(lo, hi):
        return _dot(xb, w1_ref[:, lo:hi])

    b_g = seg(_C_BG, _C_CG)
    u = seg(_C_CG, _C_H) * seg(_C_H, _C_CQ)
    cqn = _rms(seg(_C_CQ, _C_CKV), qg_ref[...]).astype(BF16)
    qf = _dot(cqn, wuq_ref[...])
    for h in range(N_HEADS):
        sl = slice(h * HEAD_W, (h + 1) * HEAD_W)
        q_ref[:, sl] = _rope(qf[:, sl], tab_ref[0], tab_ref[1], tab_ref[2]).astype(BF16)
    ckvn = _rms(seg(_C_CKV, _C_KR), kvg_ref[...])
    ckv_ref[...] = ckvn
    krr = _rope(seg(_C_KR, W1_COLS), tab_ref[3], tab_ref[4], tab_ref[5])
    kr_ref[...] = krr[:, 0:ROPE_DIM]
    return b_g, u, ckvn.astype(BF16), krr


def _inproj_prompt_body(tiles_per_seq, x_ref, tab_ref, w1_ref, wuq_ref, wuk_ref, wuv_ref, qg_ref, kvg_ref,
                        cw_ref, bconv_ref, q_ref, k_ref, v_ref, ckv_ref, kr_ref, nconv_ref, uext_ref):
    tm = x_ref.shape[0]
    b_g, u, cb, krr = _inproj_common(x_ref, tab_ref, w1_ref, wuq_ref, qg_ref, kvg_ref, q_ref, ckv_ref, kr_ref)
    kn = _dot(cb, wuk_ref[...])
    for h in range(N_HEADS):
        sl = slice(h * HEAD_W, (h + 1) * HEAD_W)
        k_ref[:, sl] = (kn[:, sl] + krr).astype(BF16)
    v_ref[...] = _dot(cb, wuv_ref[...]).astype(BF16)

    first = (pl.program_id(0) % tiles_per_seq) == 0

    @pl.when(first)
    def _():
        uext_ref[0:8, :] = jnp.zeros((8, D_CONV), F32)

    @pl.when(jnp.logical_not(first))
    def _():
        uext_ref[0:8, :] = uext_ref[tm:tm + 8, :]

    uext_ref[8:8 + tm, :] = u
    conv = cw_ref[0:1, :] * uext_ref[6:6 + tm, :] + cw_ref[1:2, :] * uext_ref[7:7 + tm, :] + cw_ref[2:3, :] * u
    bconv_ref[...] = (b_g * conv).astype(BF16)
    nconv_ref[0] = u[tm - (CONV_W - 1):tm, :]


def _inproj_sample_body(x_ref, tab_ref, w1_ref, wuq_ref, wukt_ref, qg_ref, kvg_ref, cw_ref, s0_ref, s1_ref,
                        bconv_ref, q_ref, qlat_ref, ckv_ref, kr_ref, u_ref):
    b_g, u, _, _ = _inproj_common(x_ref, tab_ref, w1_ref, wuq_ref, qg_ref, kvg_ref, q_ref, ckv_ref, kr_ref)
    for h in range(N_HEADS):
        qh = q_ref[:, h * HEAD_W:(h + 1) * HEAD_W]
        qlat_ref[:, h * KV_LORA:(h + 1) * KV_LORA] = _dot(qh, wukt_ref[h]).astype(BF16)
    conv = cw_ref[0:1, :] * s0_ref[...] + cw_ref[1:2, :] * s1_ref[...] + cw_ref[2:3, :] * u
    bconv_ref[...] = (b_g * conv).astype(BF16)
    u_ref[...] = u


def _full(shape):
    nd = len(shape)
    return pl.BlockSpec(shape, lambda *_: (0,) * nd)


def _inproj_prompt(x, tab, w1, wuq, wuk, wuv, qg, kvg, cw, n_seq, seq):
    t = x.shape[0]
    tm = TM_PROMPT
    tiles_per_seq = seq // tm
    rows = lambda w: pl.BlockSpec((tm, w), lambda i: (i, 0))
    return pl.pallas_call(
        functools.partial(_inproj_prompt_body, tiles_per_seq),
        grid=(t // tm,),
        in_specs=[rows(D_MODEL),
                  pl.BlockSpec((6, tm, LANES), lambda i: (0, i % tiles_per_seq, 0)),
                  _full(w1.shape), _full(wuq.shape), _full(wuk.shape), _full(wuv.shape),
                  _full(qg.shape), _full(kvg.shape), _full(cw.shape)],
        out_specs=[rows(D_CONV), rows(QK_W), rows(QK_W), rows(V_W), rows(KV_LORA), rows(ROPE_DIM),
                   pl.BlockSpec((1, CONV_W - 1, D_CONV), lambda i: (i // tiles_per_seq, 0, 0))],
        out_shape=[jax.ShapeDtypeStruct((t, D_CONV), BF16), jax.ShapeDtypeStruct((t, QK_W), BF16),
                   jax.ShapeDtypeStruct((t, QK_W), BF16), jax.ShapeDtypeStruct((t, V_W), BF16),
                   jax.ShapeDtypeStruct((t, KV_LORA), F32), jax.ShapeDtypeStruct((t, ROPE_DIM), F32),
                   jax.ShapeDtypeStruct((n_seq, CONV_W - 1, D_CONV), F32)],
        scratch_shapes=[pltpu.VMEM((tm + 8, D_CONV), F32)],
        compiler_params=_cparams("arbitrary"),
    )(x, tab, w1, wuq, wuk, wuv, qg, kvg, cw)


def _inproj_sample(x, tab, w1, wuq, wukt, qg, kvg, cw, s0, s1):
    n = x.shape[0]
    args = (x, tab, w1, wuq, wukt, qg, kvg, cw, s0, s1)
    out_shape = [jax.ShapeDtypeStruct((n, D_CONV), BF16), jax.ShapeDtypeStruct((n, QK_W), BF16),
                 jax.ShapeDtypeStruct((n, N_HEADS * KV_LORA), BF16), jax.ShapeDtypeStruct((n, KV_LORA), F32),
                 jax.ShapeDtypeStruct((n, ROPE_DIM), F32), jax.ShapeDtypeStruct((n, D_CONV), F32)]
    return pl.pallas_call(
        _inproj_sample_body,
        grid=(1,),
        in_specs=[_full(a.shape) for a in args],
        out_specs=[_full(s.shape) for s in out_shape],
        out_shape=out_shape,
        compiler_params=_cparams("arbitrary"),
    )(*args)


def _prompt_attention_body(q_ref, k_ref, v_ref, o_ref):
    tq = q_ref.shape[0]
    i = pl.program_id(1)
    row = lax.broadcasted_iota(jnp.int32, (tq, tq), 0)
    col = lax.broadcasted_iota(jnp.int32, (tq, tq), 1)
    causal = col <= row
    low_half = lax.broadcasted_iota(jnp.int32, (tq, LANES), 1) < V_DIM

    for hp in range(N_HEADS // 2):
        heads = (2 * hp, 2 * hp + 1)
        qs = [q_ref[:, h * HEAD_W:(h + 1) * HEAD_W] for h in heads]
        vsl = slice(hp * LANES, (hp + 1) * LANES)

        def step(j, carry, masked):
            off = pl.multiple_of(j * tq, tq)
            vblk = v_ref[pl.ds(off, tq), vsl]
            out = []
            for idx, h in enumerate(heads):
                m, l, acc = carry[idx]
                kblk = k_ref[pl.ds(off, tq), h * HEAD_W:(h + 1) * HEAD_W]
                s = _dot_nt(qs[idx], kblk)
                if masked:
                    s = jnp.where(causal, s, NEG)
                m_new = jnp.maximum(m, jnp.max(s, axis=-1, keepdims=True))
                alpha = jnp.exp(m - m_new)
                p = jnp.exp(s - m_new)
                l_new = alpha * l + jnp.sum(p, axis=-1, keepdims=True)
                acc_new = alpha * acc + _dot(p.astype(BF16), vblk)
                out.append((m_new, l_new, acc_new))
            return tuple(out)

        init = tuple((jnp.full((tq, 1), NEG, F32), jnp.zeros((tq, 1), F32), jnp.zeros((tq, LANES), F32))
                     for _ in heads)
        carry = lax.fori_loop(0, i, lambda j, c: step(j, c, False), init)
        (_, l0, a0), (_, l1, a1) = step(i, carry, True)
        o_ref[:, vsl] = jnp.where(low_half, a0 / l0, a1 / l1).astype(BF16)


def _prompt_attention(q, k, v, n_seq, seq):
    t = q.shape[0]
    nq = seq // TQ
    return pl.pallas_call(
        _prompt_attention_body,
        grid=(n_seq, nq),
        in_specs=[pl.BlockSpec((TQ, QK_W), lambda b, i: (b * nq + i, 0)),
                  pl.BlockSpec((seq, QK_W), lambda b, i: (b, 0)),
                  pl.BlockSpec((seq, V_W), lambda b, i: (b, 0))],
        out_specs=pl.BlockSpec((TQ, V_W), lambda b, i: (b * nq + i, 0)),
        out_shape=jax.ShapeDtypeStruct((t, V_W), BF16),
        compiler_params=_cparams("arbitrary", "arbitrary"),
    )(q, k, v)


def _sample_attention_body(n_pages, pt_ref, qlat_ref, q_ref, ckvn_ref, krn_ref, cc_ref, cr_ref, olat_ref,
                           bufc, bufr, kcb, krb, sem):
    b = pl.program_id(0)
    nb = pl.num_programs(0)
    slot = b % 2
    past = n_pages * PAGE_SIZE

    def page_copies(bb, sl, p):
        pg = pt_ref[bb, p]
        rows = pl.ds(pl.multiple_of(p * PAGE_SIZE, PAGE_SIZE), PAGE_SIZE)
        return (pltpu.make_async_copy(cc_ref.at[pg], bufc.at[sl, rows], sem.at[0, sl]),
                pltpu.make_async_copy(cr_ref.at[pg], bufr.at[sl, rows], sem.at[1, sl]))

    def issue(bb, sl):
        def body(p, _):
            for c in page_copies(bb, sl, p):
                c.start()
            return 0
        lax.fori_loop(0, n_pages, body, 0)

    @pl.when(b == 0)
    def _():
        issue(0, 0)

    @pl.when(b + 1 < nb)
    def _():
        issue(b + 1, 1 - slot)

    def wait_body(p, _):
        for c in page_copies(b, slot, p):
            c.wait()
        return 0
    lax.fori_loop(0, n_pages, wait_body, 0)

    ql = qlat_ref[0]
    qr = q_ref[0][:, 0:ROPE_DIM]
    cn = ckvn_ref[0]
    rn = krn_ref[0]
    chunk = min(KEY_CHUNK, past)
    scores = []
    for c in range(past // chunk):
        rows = slice(c * chunk, (c + 1) * chunk)
        kcb[rows, :] = bufc[slot, rows, :].astype(BF16)
        krb[rows, :] = bufr[slot, rows, :].astype(BF16)
        scores.append(_dot_nt(ql, kcb[rows, :]) + _dot_nt(qr, krb[rows, :]))
    s_new = (jnp.sum(ql.astype(F32) * cn, axis=-1, keepdims=True)
             + jnp.sum(qr.astype(F32) * rn, axis=-1, keepdims=True))
    m = s_new
    for s in scores:
        m = jnp.maximum(m, jnp.max(s, axis=-1, keepdims=True))
    p_new = jnp.exp(s_new - m)
    l = p_new
    o = p_new * cn
    for c, s in enumerate(scores):
        p = jnp.exp(s - m)
        l = l + jnp.sum(p, axis=-1, keepdims=True)
        o = o + _dot(p.astype(BF16), kcb[c * chunk:(c + 1) * chunk, :])
    olat_ref[0] = o / l


def _sample_attention(page_table, qlat, q, ckvn, krn, cache_c, cache_r):
    n, n_pages = page_table.shape
    past = n_pages * PAGE_SIZE
    blk = lambda d1, d2: pl.BlockSpec((1, d1, d2), lambda b, pt: (b, 0, 0))
    grid_spec = pltpu.PrefetchScalarGridSpec(
        num_scalar_prefetch=1,
        grid=(n,),
        in_specs=[blk(N_HEADS, KV_LORA), blk(N_HEADS, HEAD_W), blk(1, KV_LORA), blk(1, ROPE_DIM),
                  pl.BlockSpec(memory_space=pl.ANY), pl.BlockSpec(memory_space=pl.ANY)],
        out_specs=blk(N_HEADS, KV_LORA),
        scratch_shapes=[pltpu.VMEM((2, past, KV_LORA), F32), pltpu.VMEM((2, past, ROPE_DIM), F32),
                        pltpu.VMEM((past, KV_LORA), BF16), pltpu.VMEM((past, ROPE_DIM), BF16),
                        pltpu.SemaphoreType.DMA((2, 2))],
    )
    return pl.pallas_call(
        functools.partial(_sample_attention_body, n_pages),
        grid_spec=grid_spec,
        out_shape=jax.ShapeDtypeStruct((n, N_HEADS, KV_LORA), F32),
        compiler_params=_cparams("arbitrary"),
    )(page_table, qlat.reshape(n, N_HEADS, KV_LORA), q.reshape(n, N_HEADS, HEAD_W),
      ckvn.reshape(n, 1, KV_LORA), krn.reshape(n, 1, ROPE_DIM), cache_c, cache_r)


def _route(logits):
    lane_i = lax.broadcasted_iota(jnp.int32, (logits.shape[0], LANES), 1)
    lane = lane_i.astype(F32)
    first_at = lambda hit: jnp.min(jnp.where(hit, lane, float(LANES)), axis=-1, keepdims=True)
    lg = jnp.where(lane_i < N_GROUPS, logits[:, 0:LANES], NEG)
    mg = jnp.max(lg, axis=-1, keepdims=True)
    p_grp = 1.0 / jnp.sum(jnp.exp(lg - mg), axis=-1, keepdims=True)
    grp = first_at(lg == mg)
    lane_grp = (lane_i // EXP_PER_GROUP).astype(F32)
    le = jnp.where(lane_grp == grp, logits[:, LANES:2 * LANES], NEG)
    top1 = jnp.max(le, axis=-1, keepdims=True)
    i1 = first_at(le == top1)
    le2 = jnp.where(lane == i1, NEG, le)
    top2 = jnp.max(le2, axis=-1, keepdims=True)
    i2 = first_at(le2 == top2)
    e2 = jnp.exp(top2 - top1)
    w1 = p_grp / (1.0 + e2)
    w2 = p_grp * e2 / (1.0 + e2)
    lane = lane_i
    out = jnp.where(lane == 0, i1, 0.0)
    out = jnp.where(lane == 1, i2, out)
    out = jnp.where(lane == 2, w1, out)
    return jnp.where(lane == 3, w2, out)


def _merge_rows(alpha, x, bconv, o, wg_ref, wbc_ref, wba_ref, wo_ref, lg_ref, lb_ref, wr_ref, br_ref):
    g = _dot(x.astype(BF16), wg_ref[...])
    y_conv = _dot(bconv, wbc_ref[...])
    y_attn = _dot(o, wba_ref[...])
    m = jax.nn.sigmoid(g[:, 0:D_MODEL]) * y_conv + jax.nn.sigmoid(g[:, D_MODEL:]) * y_attn
    mix = _dot(m.astype(BF16), wo_ref[...])
    h1 = _layernorm(alpha * x + mix, lg_ref[...], lb_ref[...])
    return h1, _route(_dot(h1.astype(BF16), wr_ref[...]) + br_ref[...])


def _merge_sample_body(alpha, x_ref, bconv_ref, olat_ref, wbd_ref, *refs):
    w_refs, (h1_ref, route_ref) = refs[:-2], refs[-2:]
    o = _dot(olat_ref[...].astype(BF16), wbd_ref[...]).astype(BF16)
    h1_ref[...], route_ref[...] = _merge_rows(alpha, x_ref[...], bconv_ref[...], o, *w_refs)


def _merge_prompt_body(alpha, n_tiles, x_ref, bconv_ref, o_ref, h1s_ref, routes_ref, *refs):
    w_refs, (h1_ref, route_ref) = refs[:-2], refs[-2:]
    i = pl.program_id(0)

    @pl.when(i < n_tiles)
    def _():
        h1_ref[...], route_ref[...] = _merge_rows(alpha, x_ref[...], bconv_ref[...], o_ref[...], *w_refs)

    @pl.when(i == n_tiles)
    def _():
        n_s = h1s_ref.shape[0]
        h1_ref[...] = jnp.zeros(h1_ref.shape, F32)
        route_ref[...] = jnp.zeros(route_ref.shape, F32)
        h1_ref[0:n_s, :] = h1s_ref[...]
        route_ref[0:n_s, :] = routes_ref[...]


def _merge_sample(alpha, x, bconv, olat, wbd, ws):
    n = x.shape[0]
    args = (x, bconv, olat, wbd) + tuple(ws)
    out_shape = [jax.ShapeDtypeStruct((n, D_MODEL), F32), jax.ShapeDtypeStruct((n, LANES), F32)]
    return pl.pallas_call(
        functools.partial(_merge_sample_body, alpha),
        grid=(1,),
        in_specs=[_full(a.shape) for a in args],
        out_specs=[_full(s.shape) for s in out_shape],
        out_shape=out_shape,
        compiler_params=_cparams("arbitrary"),
    )(*args)


def _merge_prompt(alpha, x, bconv, o, h1_s, route_s, ws):
    t = x.shape[0]
    tm = TM_PROMPT
    n_tiles = t // tm
    t_all = t + h1_s.shape[0]
    assert h1_s.shape[0] <= tm
    rows_in = lambda w: pl.BlockSpec((tm, w), lambda i: (jnp.minimum(i, n_tiles - 1), 0))
    rows_out = lambda w: pl.BlockSpec((tm, w), lambda i: (i, 0))
    return pl.pallas_call(
        functools.partial(_merge_prompt_body, alpha, n_tiles),
        grid=(n_tiles + 1,),
        in_specs=[rows_in(D_MODEL), rows_in(D_CONV), rows_in(V_W), _full(h1_s.shape), _full(route_s.shape)]
                 + [_full(w.shape) for w in ws],
        out_specs=[rows_out(D_MODEL), rows_out(LANES)],
        out_shape=[jax.ShapeDtypeStruct((t_all, D_MODEL), F32), jax.ShapeDtypeStruct((t_all, LANES), F32)],
        compiler_params=_cparams("arbitrary"),
    )(x, bconv, o, h1_s, route_s, *ws)


def _experts_body(blk_e_ref, used_ref, tok_ref, h1_ref, wg_ref, wu_ref, wd_ref, ys_ref, xbuf, sem):
    del blk_e_ref
    b = pl.program_id(0)
    nb = pl.num_programs(0)
    slot = b % 2
    used = used_ref[0]

    def row_copy(bb, sl, r):
        tok = tok_ref[bb * MOE_BLOCK + r]
        return pltpu.make_async_copy(h1_ref.at[pl.ds(tok, 1)], xbuf.at[sl, pl.ds(r, 1)], sem.at[sl])

    def issue(bb, sl):
        def body(r, _):
            row_copy(bb, sl, r).start()
            return 0
        lax.fori_loop(0, MOE_BLOCK, body, 0)

    @pl.when(jnp.logical_and(b == 0, used > 0))
    def _():
        issue(0, 0)

    @pl.when(b + 1 < jnp.minimum(used, nb))
    def _():
        issue(b + 1, 1 - slot)

    @pl.when(b < used)
    def _():
        def wait_body(r, _):
            row_copy(b, slot, r).wait()
            return 0
        lax.fori_loop(0, MOE_BLOCK, wait_body, 0)
        x = xbuf[slot].astype(BF16)
        g = _dot(x, wg_ref[0])
        u = _dot(x, wu_ref[0])
        h = (g * jax.nn.sigmoid(g) * u).astype(BF16)
        ys_ref[...] = _dot(h, wd_ref[0])

    @pl.when(b >= used)
    def _():
        ys_ref[...] = jnp.zeros(ys_ref.shape, F32)


def _experts(blk_e, used, slot_tok, h1, wg, wu, wd):
    n_blocks = blk_e.shape[0]
    grid_spec = pltpu.PrefetchScalarGridSpec(
        num_scalar_prefetch=3,
        grid=(n_blocks,),
        in_specs=[pl.BlockSpec(memory_space=pl.ANY),
                  pl.BlockSpec((1, D_MODEL, D_EXPERT), lambda b, e, u, t: (e[b], 0, 0)),
                  pl.BlockSpec((1, D_MODEL, D_EXPERT), lambda b, e, u, t: (e[b], 0, 0)),
                  pl.BlockSpec((1, D_EXPERT, D_MODEL), lambda b, e, u, t: (e[b], 0, 0))],
        out_specs=pl.BlockSpec((MOE_BLOCK, D_MODEL), lambda b, e, u, t: (b, 0)),
        scratch_shapes=[pltpu.VMEM((2, MOE_BLOCK, D_MODEL), F32), pltpu.SemaphoreType.DMA((2,))],
    )
    return pl.pallas_call(
        _experts_body,
        grid_spec=grid_spec,
        out_shape=jax.ShapeDtypeStruct((n_blocks * MOE_BLOCK, D_MODEL), F32),
        compiler_params=_cparams("arbitrary"),
    )(blk_e, used, slot_tok, h1, wg, wu, wd)


def _combine_body(alpha, n_prompt_tiles, dest_ref, ys_ref, h1_ref, route_ref, lg_ref, lb_ref, yp_ref, ys_out_ref,
                  gbuf, sem):
    i = pl.program_id(0)
    nt = pl.num_programs(0)
    slot = i % 2
    tm = h1_ref.shape[0]

    def row_copy(ii, sl, r, k):
        d = dest_ref[(ii * tm + r) * TOP_K + k]
        return pltpu.make_async_copy(ys_ref.at[pl.ds(d, 1)], gbuf.at[sl, k, pl.ds(r, 1)], sem.at[sl])

    def issue(ii, sl):
        def body(r, _):
            for k in range(TOP_K):
                row_copy(ii, sl, r, k).start()
            return 0
        lax.fori_loop(0, tm, body, 0)

    @pl.when(i == 0)
    def _():
        issue(0, 0)

    @pl.when(i + 1 < nt)
    def _():
        issue(i + 1, 1 - slot)

    def wait_body(r, _):
        for k in range(TOP_K):
            row_copy(i, slot, r, k).wait()
        return 0
    lax.fori_loop(0, tm, wait_body, 0)

    route = route_ref[...]
    y = route[:, 2:3] * gbuf[slot, 0] + route[:, 3:4] * gbuf[slot, 1]
    out = _layernorm(alpha * h1_ref[...] + y, lg_ref[...], lb_ref[...])

    @pl.when(i < n_prompt_tiles)
    def _():
        yp_ref[...] = out

    @pl.when(i >= n_prompt_tiles)
    def _():
        ys_out_ref[...] = out


def _combine(alpha, t_prompt, dest, ys, h1, route, lg, lb):
    t_all = h1.shape[0]
    tm = TM_COMBINE
    n_prompt_tiles = t_prompt // tm
    n_tiles = t_all // tm
    last_p = n_prompt_tiles - 1
    grid_spec = pltpu.PrefetchScalarGridSpec(
        num_scalar_prefetch=1,
        grid=(n_tiles,),
        in_specs=[pl.BlockSpec(memory_space=pl.ANY),
                  pl.BlockSpec((tm, D_MODEL), lambda i, d: (i, 0)),
                  pl.BlockSpec((tm, LANES), lambda i, d: (i, 0)),
                  pl.BlockSpec(lg.shape, lambda i, d: (0, 0)), pl.BlockSpec(lb.shape, lambda i, d: (0, 0))],
        out_specs=[pl.BlockSpec((tm, D_MODEL), lambda i, d: (jnp.minimum(i, last_p), 0)),
                   pl.BlockSpec((tm, D_MODEL), lambda i, d: (jnp.maximum(i - n_prompt_tiles, 0), 0))],
        scratch_shapes=[pltpu.VMEM((2, TOP_K, tm, D_MODEL), F32), pltpu.SemaphoreType.DMA((2,))],
    )
    return pl.pallas_call(
        functools.partial(_combine_body, alpha, n_prompt_tiles),
        grid_spec=grid_spec,
        out_shape=[jax.ShapeDtypeStruct((t_prompt, D_MODEL), F32),
                   jax.ShapeDtypeStruct((t_all - t_prompt, D_MODEL), F32)],
        compiler_params=_cparams("arbitrary"),
    )(dest, ys, h1, route, lg, lb)


def _rope_tables(pos):
    inv = ROPE_BASE ** (-(jnp.arange(ROPE_HALF, dtype=F32) * 2.0 / ROPE_DIM))
    ang = pos.astype(F32)[:, None] * inv[None, :]
    cos, sin = jnp.cos(ang), jnp.sin(ang)
    n = pos.shape[0]
    a = jnp.concatenate([cos, cos, jnp.ones((n, LANES - ROPE_DIM), F32)], axis=1)
    bm = jnp.concatenate([-sin, jnp.zeros((n, LANES - ROPE_HALF), F32)], axis=1)
    cp = jnp.concatenate([jnp.zeros((n, ROPE_HALF), F32), sin, jnp.zeros((n, LANES - ROPE_DIM), F32)], axis=1)
    k_tabs = jnp.stack([a, bm, cp])
    return jnp.concatenate([k_tabs * ATTN_SCALE, k_tabs])


def _head_blocks(w_rope, w_nope):
    k = w_nope.shape[0]
    pad = jnp.zeros((k, N_HEADS, HEAD_W - QK_DIM), w_nope.dtype)
    return jnp.concatenate([w_rope, w_nope, pad], axis=-1).reshape(k, QK_W)


def _pack_weights(w_in, w_uq, w_uk, w_uv, router_w_group, router_b_group, router_w_expert, router_b_expert):
    d = w_in.shape[0]
    c_kr = 3 * D_CONV + Q_LORA + KV_LORA
    w1 = jnp.concatenate([w_in[:, :c_kr], w_in[:, c_kr:c_kr + ROPE_DIM],
                          jnp.zeros((d, LANES - ROPE_DIM), w_in.dtype)], axis=1).astype(BF16)
    wg = w_in[:, c_kr + ROPE_DIM:].astype(BF16)
    uq = w_uq.reshape(Q_LORA, N_HEADS, QK_DIM)
    wuq = _head_blocks(uq[..., NOPE_DIM:], uq[..., :NOPE_DIM]).astype(BF16)
    wuk = _head_blocks(jnp.zeros((KV_LORA, N_HEADS, ROPE_DIM), w_uk.dtype), w_uk).astype(BF16)
    wukt = jnp.transpose(wuk.reshape(KV_LORA, N_HEADS, HEAD_W), (1, 2, 0))
    wuv = w_uv.reshape(KV_LORA, V_W).astype(BF16)
    eye = jnp.eye(N_HEADS, dtype=w_uv.dtype)
    wbd = jnp.einsum('chd,hg->hcgd', w_uv, eye).reshape(N_HEADS * KV_LORA, V_W).astype(BF16)
    wr = jnp.zeros((d, ROUTE_W), F32)
    wr = wr.at[:, 0:N_GROUPS].set(router_w_group).at[:, LANES:LANES + N_EXPERTS].set(router_w_expert).astype(BF16)
    br = jnp.zeros((1, ROUTE_W), F32)
    br = br.at[0, 0:N_GROUPS].set(router_b_group).at[0, LANES:LANES + N_EXPERTS].set(router_b_expert)
    return w1, wg, wuq, wuk, wukt, wuv, wbd, wr, br


def _dispatch_plan(route, n_blocks):
    eid = route[:, 0:TOP_K].astype(jnp.int32).reshape(-1)
    a = eid.shape[0]
    onehot = (eid[:, None] == jnp.arange(N_EXPERTS, dtype=jnp.int32)[None, :]).astype(jnp.int32)
    csum = jnp.cumsum(onehot, axis=0)
    rank = jnp.sum(csum * onehot, axis=1) - 1
    counts = csum[-1]
    pcounts = (counts + MOE_BLOCK - 1) // MOE_BLOCK * MOE_BLOCK
    pend = jnp.cumsum(pcounts)
    pstart = pend - pcounts
    dest = (jnp.sum(onehot * pstart[None, :], axis=1) + rank).astype(jnp.int32)
    tok = jnp.arange(a, dtype=jnp.int32) // TOP_K
    slot_tok = jnp.zeros((n_blocks * MOE_BLOCK,), jnp.int32).at[dest].set(tok)
    blk_start = jnp.arange(n_blocks, dtype=jnp.int32) * MOE_BLOCK
    blk_e = jnp.minimum(jnp.searchsorted(pend, blk_start, side='right'), N_EXPERTS - 1).astype(jnp.int32)
    used = (pend[-1] // MOE_BLOCK).astype(jnp.int32).reshape(1)
    return dest, slot_tok, blk_e, used


def kernel(x_prompt, x_sample, cache_ckv, cache_krope, state_conv, page_table, w_in, conv_w, q_norm_g, w_uq,
           kv_norm_g, w_uk, w_uv, w_br_conv, w_br_attn, w_o, ln1_g, ln1_b, router_w_group, router_b_group,
           router_w_expert, router_b_expert, w_gate, w_up, w_down, ln2_g, ln2_b):
    depth = w_in.shape[0]
    alpha = (2 * depth) ** 0.25
    n_p, t_p, _ = x_prompt.shape
    n_s, t_s, _ = x_sample.shape
    assert t_s == 1 and t_p % TM_PROMPT == 0 and n_s % MOE_BLOCK == 0
    rows_p = n_p * t_p
    rows_all = rows_p + n_s
    past = page_table.shape[1] * PAGE_SIZE
    tab_p = _rope_tables(jnp.arange(t_p))
    tab_s = _rope_tables(jnp.full((n_s,), past, jnp.int32))
    n_blocks = -(-(rows_all * TOP_K) // MOE_BLOCK) + N_EXPERTS

    h_p = x_prompt.reshape(rows_p, D_MODEL)
    h_s = x_sample.reshape(n_s, D_MODEL)
    ckv_p, kr_p, cv_p, ckv_s, kr_s, cv_s = [], [], [], [], [], []
    row = lambda v: v.reshape(1, -1)
    for l in range(depth):
        w1, wg, wuq, wuk, wukt, wuv, wbd, wr, br = _pack_weights(
            w_in[l], w_uq[l], w_uk[l], w_uv[l], router_w_group[l], router_b_group[l],
            router_w_expert[l], router_b_expert[l])
        qg, kvg = row(q_norm_g[l]), row(kv_norm_g[l])
        merge_w = (wg, w_br_conv[l].astype(BF16), w_br_attn[l].astype(BF16), w_o[l].astype(BF16),
                   row(ln1_g[l]), row(ln1_b[l]), wr, br)

        bconv, q, k, v, ckv, kr, nconv = _inproj_prompt(h_p, tab_p, w1, wuq, wuk, wuv, qg, kvg, conv_w[l], n_p, t_p)
        o = _prompt_attention(q, k, v, n_p, t_p)

        st = state_conv[l]
        bconv_s, q_s, qlat, ckvn_s, krn_s, u_s = _inproj_sample(
            h_s, tab_s, w1, wuq, wukt, qg, kvg, conv_w[l], st[:, 0], st[:, 1])
        olat = _sample_attention(page_table, qlat, q_s, ckvn_s, krn_s, cache_ckv[l], cache_krope[l])
        h1_s, route_s = _merge_sample(alpha, h_s, bconv_s, olat.reshape(n_s, N_HEADS * KV_LORA), wbd, merge_w)
        h1_all, route_all = _merge_prompt(alpha, h_p, bconv, o, h1_s, route_s, merge_w)

        dest, slot_tok, blk_e, used = _dispatch_plan(route_all, n_blocks)
        ys = _experts(blk_e, used, slot_tok, h1_all, w_gate[l].astype(BF16), w_up[l].astype(BF16),
                      w_down[l].astype(BF16))
        h_p, h_s = _combine(alpha, rows_p, dest, ys, h1_all, route_all, row(ln2_g[l]), row(ln2_b[l]))

        ckv_p.append(ckv.reshape(n_p, t_p, KV_LORA))
        kr_p.append(kr.reshape(n_p, t_p, ROPE_DIM))
        cv_p.append(nconv)
        ckv_s.append(ckvn_s.reshape(n_s, 1, KV_LORA))
        kr_s.append(krn_s.reshape(n_s, 1, ROPE_DIM))
        cv_s.append(jnp.stack([st[:, 1], u_s], axis=1))
    return (h_p.reshape(n_p, t_p, D_MODEL), h_s.reshape(n_s, 1, D_MODEL), jnp.stack(ckv_p), jnp.stack(kr_p),
            jnp.stack(cv_p), jnp.stack(ckv_s), jnp.stack(kr_s), jnp.stack(cv_s))
```

```python
import functools

import jax
import jax.numpy as jnp
from jax import lax
from jax.experimental import pallas as pl
from jax.experimental.pallas import tpu as pltpu

F32 = jnp.float32
BF16 = jnp.bfloat16

D_MODEL = 1024
D_CONV = 512
CONV_W = 3
N_HEADS = 8
Q_LORA = 384
KV_LORA = 256
NOPE_DIM = 64
ROPE_DIM = 32
ROPE_HALF = ROPE_DIM // 2
V_DIM = 64
QK_DIM = NOPE_DIM + ROPE_DIM
ROPE_BASE = 10000.0
ATTN_SCALE = QK_DIM ** -0.5
LOG2E = 1.4426950408889634
N_GROUPS = 4
EXP_PER_GROUP = 8
N_EXPERTS = N_GROUPS * EXP_PER_GROUP
TOP_K = 2
D_EXPERT = 512
MOE_BLOCK = 128
PAGE_SIZE = 128
LN_EPS = 1e-5
RMS_EPS = 1e-6

LANES = 128
SUBLANES = 8
ROW_TILE = D_MODEL // LANES
assert ROW_TILE == SUBLANES
HEAD_W = LANES
QK_W = N_HEADS * HEAD_W
V_W = N_HEADS * V_DIM
_C_BG, _C_CG, _C_H = 0, D_CONV, 2 * D_CONV
_C_CQ = 3 * D_CONV
_C_CKV = _C_CQ + Q_LORA
_C_KR = _C_CKV + KV_LORA
W1_COLS = _C_KR + LANES
ROUTE_W = 2 * LANES
R_EID, R_WT, R_RANK = 0, 2, 4
NEG = -1e30
VMEM_LIMIT = 56 * 1024 * 1024

TM = 512
KEY_CHUNK = 2048


def _cparams(*sem):
    return pltpu.CompilerParams(dimension_semantics=sem, vmem_limit_bytes=VMEM_LIMIT)


def _dot(a, b):
    return jnp.dot(a, b, preferred_element_type=F32)


def _dot_nt(a, b):
    return lax.dot_general(a, b, (((1,), (1,)), ((), ())), preferred_element_type=F32)


def _dot_tn(a, b):
    return lax.dot_general(a, b, (((0,), (0,)), ((), ())), preferred_element_type=F32)


def _rms(x, g):
    return x * lax.rsqrt(jnp.mean(x * x, axis=-1, keepdims=True) + RMS_EPS) * g


def _layernorm(x, g, b):
    mu = jnp.mean(x, axis=-1, keepdims=True)
    xc = x - mu
    var = jnp.mean(xc * xc, axis=-1, keepdims=True)
    return xc * lax.rsqrt(var + LN_EPS) * g + b


def _rope(xh, a, bm, cp):
    return xh * a + pltpu.roll(xh, LANES - ROPE_HALF, 1) * bm + pltpu.roll(xh, ROPE_HALF, 1) * cp


def _full(shape):
    nd = len(shape)
    return pl.BlockSpec(shape, lambda *_: (0,) * nd)


def _tile_rows(ref, c, n):
    return ref.at[pl.ds(c, n, stride=ROW_TILE), :]


def _inproj_common(x_ref, tab_ref, w1_ref, wuq_ref, qg_ref, kvg_ref, q_ref, ckv_ref, kr_ref):
    xb = x_ref[...].astype(BF16)

    def seg(lo, hi):
        return _dot(xb, w1_ref[:, lo:hi])

    b_g = seg(_C_BG, _C_CG)
    u = seg(_C_CG, _C_H) * seg(_C_H, _C_CQ)
    cqn = _rms(seg(_C_CQ, _C_CKV), qg_ref[...]).astype(BF16)
    qf = _dot(cqn, wuq_ref[...])
    for h in range(N_HEADS):
        sl = slice(h * HEAD_W, (h + 1) * HEAD_W)
        q_ref[:, sl] = _rope(qf[:, sl], tab_ref[0], tab_ref[1], tab_ref[2]).astype(BF16)
    ckvn = _rms(seg(_C_CKV, _C_KR), kvg_ref[...])
    ckv_ref[...] = ckvn
    krr = _rope(seg(_C_KR, W1_COLS), tab_ref[3], tab_ref[4], tab_ref[5])
    kr_ref[...] = krr[:, 0:ROPE_DIM]
    return b_g, u, ckvn.astype(BF16), krr


def _inproj_prompt_body(tiles_per_seq, x_ref, tab_ref, w1_ref, wuq_ref, wuk_ref, wuvt_ref, qg_ref, kvg_ref,
                        cw_ref, bconv_ref, q_ref, k_ref, vt_ref, ckv_ref, kr_ref, nconv_ref, uext_ref):
    tm = x_ref.shape[0]
    b_g, u, cb, krr = _inproj_common(x_ref, tab_ref, w1_ref, wuq_ref, qg_ref, kvg_ref, q_ref, ckv_ref, kr_ref)
    kn = _dot(cb, wuk_ref[...])
    for h in range(N_HEADS):
        sl = slice(h * HEAD_W, (h + 1) * HEAD_W)
        k_ref[:, sl] = (kn[:, sl] + krr).astype(BF16)
    vt_ref[...] = _dot_nt(wuvt_ref[...], cb).astype(BF16)

    first = (pl.program_id(0) % tiles_per_seq) == 0

    @pl.when(first)
    def _():
        uext_ref[0:8, :] = jnp.zeros((8, D_CONV), F32)

    @pl.when(jnp.logical_not(first))
    def _():
        uext_ref[0:8, :] = uext_ref[tm:tm + 8, :]

    uext_ref[8:8 + tm, :] = u
    conv = cw_ref[0:1, :] * uext_ref[6:6 + tm, :] + cw_ref[1:2, :] * uext_ref[7:7 + tm, :] + cw_ref[2:3, :] * u
    bconv_ref[...] = (b_g * conv).astype(BF16)
    nconv_ref[0] = u[tm - (CONV_W - 1):tm, :]


def _inproj_sample_body(x_ref, tab_ref, w1_ref, wuq_ref, wukt_ref, qg_ref, kvg_ref, cw_ref, s0_ref, s1_ref,
                        bconv_ref, q_ref, qlat_ref, ckv_ref, kr_ref, u_ref):
    b_g, u, _, _ = _inproj_common(x_ref, tab_ref, w1_ref, wuq_ref, qg_ref, kvg_ref, q_ref, ckv_ref, kr_ref)
    for h in range(N_HEADS):
        qh = q_ref[:, h * HEAD_W:(h + 1) * HEAD_W]
        qlat_ref[:, h * KV_LORA:(h + 1) * KV_LORA] = _dot(qh, wukt_ref[h]).astype(BF16)
    conv = cw_ref[0:1, :] * s0_ref[...] + cw_ref[1:2, :] * s1_ref[...] + cw_ref[2:3, :] * u
    bconv_ref[...] = (b_g * conv).astype(BF16)
    u_ref[...] = u


def _inproj_prompt(x, tab, w1, wuq, wuk, wuvt, qg, kvg, cw, n_seq, seq):
    t = x.shape[0]
    tm = TM
    tiles_per_seq = seq // tm
    rows = lambda w: pl.BlockSpec((tm, w), lambda i: (i, 0))
    return pl.pallas_call(
        functools.partial(_inproj_prompt_body, tiles_per_seq),
        grid=(t // tm,),
        in_specs=[rows(D_MODEL),
                  pl.BlockSpec((6, tm, LANES), lambda i: (0, i % tiles_per_seq, 0)),
                  _full(w1.shape), _full(wuq.shape), _full(wuk.shape), _full(wuvt.shape),
                  _full(qg.shape), _full(kvg.shape), _full(cw.shape)],
        out_specs=[rows(D_CONV), rows(QK_W), rows(QK_W), pl.BlockSpec((V_W, tm), lambda i: (0, i)),
                   rows(KV_LORA), rows(ROPE_DIM),
                   pl.BlockSpec((1, CONV_W - 1, D_CONV), lambda i: (i // tiles_per_seq, 0, 0))],
        out_shape=[jax.ShapeDtypeStruct((t, D_CONV), BF16), jax.ShapeDtypeStruct((t, QK_W), BF16),
                   jax.ShapeDtypeStruct((t, QK_W), BF16), jax.ShapeDtypeStruct((V_W, t), BF16),
                   jax.ShapeDtypeStruct((t, KV_LORA), F32), jax.ShapeDtypeStruct((t, ROPE_DIM), F32),
                   jax.ShapeDtypeStruct((n_seq, CONV_W - 1, D_CONV), F32)],
        scratch_shapes=[pltpu.VMEM((tm + 8, D_CONV), F32)],
        compiler_params=_cparams("arbitrary"),
    )(x, tab, w1, wuq, wuk, wuvt, qg, kvg, cw)


def _inproj_sample(x, tab, w1, wuq, wukt, qg, kvg, cw, s0, s1):
    n = x.shape[0]
    args = (x, tab, w1, wuq, wukt, qg, kvg, cw, s0, s1)
    out_shape = [jax.ShapeDtypeStruct((n, D_CONV), BF16), jax.ShapeDtypeStruct((n, QK_W), BF16),
                 jax.ShapeDtypeStruct((n, N_HEADS * KV_LORA), BF16), jax.ShapeDtypeStruct((n, KV_LORA), F32),
                 jax.ShapeDtypeStruct((n, ROPE_DIM), F32), jax.ShapeDtypeStruct((n, D_CONV), F32)]
    return pl.pallas_call(
        _inproj_sample_body,
        grid=(1,),
        in_specs=[_full(a.shape) for a in args],
        out_specs=[_full(s.shape) for s in out_shape],
        out_shape=out_shape,
        compiler_params=_cparams("arbitrary"),
    )(*args)


def _prompt_attention_body(q_ref, k_ref, vt_ref, ot_ref, acc_ref):
    tq = q_ref.shape[0]
    i = pl.program_id(1)
    krow = lax.broadcasted_iota(jnp.int32, (tq, tq), 0)
    qcol = lax.broadcasted_iota(jnp.int32, (tq, tq), 1)
    causal = krow <= qcol

    for hp in range(N_HEADS // 2):
        heads = (2 * hp, 2 * hp + 1)
        qs = [q_ref[:, h * HEAD_W:(h + 1) * HEAD_W] for h in heads]
        acc_ref[...] = jnp.zeros(acc_ref.shape, F32)

        def step(j, carry, masked):
            off = pl.multiple_of(j * tq, tq)
            out = []
            for idx, h in enumerate(heads):
                m, l = carry[idx]
                kblk = k_ref[pl.ds(off, tq), h * HEAD_W:(h + 1) * HEAD_W]
                s = _dot_nt(kblk, qs[idx])
                if masked:
                    s = jnp.where(causal, s, NEG)
                m_new = jnp.maximum(m, jnp.max(s, axis=0, keepdims=True))
                alpha = jnp.exp2(m - m_new)
                p = jnp.exp2(s - m_new)
                l_new = alpha * l + jnp.sum(p, axis=0, keepdims=True)
                vblk = vt_ref[h * V_DIM:(h + 1) * V_DIM, pl.ds(off, tq)]
                acc_ref[idx] = alpha * acc_ref[idx] + _dot(vblk, p.astype(BF16))
                out.append((m_new, l_new))
            return tuple(out)

        init = tuple((jnp.full((1, tq), NEG, F32), jnp.zeros((1, tq), F32)) for _ in heads)
        carry = lax.fori_loop(0, i, lambda j, c: step(j, c, False), init)
        carry = step(i, carry, True)
        for idx, h in enumerate(heads):
            ot_ref[h * V_DIM:(h + 1) * V_DIM, :] = (acc_ref[idx] / carry[idx][1]).astype(BF16)


def _prompt_attention(q, k, vt, n_seq, seq):
    t = q.shape[0]
    tq = TM
    nq = seq // tq
    return pl.pallas_call(
        _prompt_attention_body,
        grid=(n_seq, nq),
        in_specs=[pl.BlockSpec((tq, QK_W), lambda b, i: (b * nq + i, 0)),
                  pl.BlockSpec((seq, QK_W), lambda b, i: (b, 0)),
                  pl.BlockSpec((V_W, seq), lambda b, i: (0, b))],
        out_specs=pl.BlockSpec((V_W, tq), lambda b, i: (0, b * nq + i)),
        out_shape=jax.ShapeDtypeStruct((V_W, t), BF16),
        scratch_shapes=[pltpu.VMEM((2, V_DIM, tq), F32)],
        compiler_params=_cparams("arbitrary", "arbitrary"),
    )(q, k, vt)


def _sample_attention_body(n_pages, pt_ref, qlat_ref, q_ref, ckvn_ref, krn_ref, cc_ref, cr_ref, olat_ref,
                           bufc, bufr, kcb, krb, sem):
    b = pl.program_id(0)
    nb = pl.num_programs(0)
    slot = b % 2
    past = n_pages * PAGE_SIZE

    def page_copies(bb, sl, p):
        pg = pt_ref[bb, p]
        pos = pl.ds(pl.multiple_of(p * PAGE_SIZE, PAGE_SIZE), PAGE_SIZE)
        return (pltpu.make_async_copy(cc_ref.at[pg], bufc.at[sl, pos], sem.at[0, sl]),
                pltpu.make_async_copy(cr_ref.at[pg], bufr.at[sl, :, pos], sem.at[1, sl]))

    def issue(bb, sl):
        def body(p, _):
            for c in page_copies(bb, sl, p):
                c.start()
            return 0
        lax.fori_loop(0, n_pages, body, 0)

    @pl.when(b == 0)
    def _():
        issue(0, 0)

    @pl.when(b + 1 < nb)
    def _():
        issue(b + 1, 1 - slot)

    def wait_body(p, _):
        for c in page_copies(b, slot, p):
            c.wait()
        return 0
    lax.fori_loop(0, n_pages, wait_body, 0)

    ql = qlat_ref[0]
    qr = q_ref[0][:, 0:ROPE_DIM]
    cn = ckvn_ref[0]
    rn = krn_ref[0]
    chunk = min(KEY_CHUNK, past)
    scores = []
    for c in range(past // chunk):
        pos = slice(c * chunk, (c + 1) * chunk)
        kcb[pos, :] = bufc[slot, pos, :].astype(BF16)
        krb[:, pos] = bufr[slot, :, pos].astype(BF16)
        scores.append(_dot_nt(ql, kcb[pos, :]) + _dot(qr, krb[:, pos]))
    s_new = (jnp.sum(ql.astype(F32) * cn, axis=-1, keepdims=True)
             + jnp.sum(qr.astype(F32) * rn, axis=-1, keepdims=True))
    m = s_new
    for s in scores:
        m = jnp.maximum(m, jnp.max(s, axis=-1, keepdims=True))
    p_new = jnp.exp2(s_new - m)
    l = p_new
    o = p_new * cn
    for c, s in enumerate(scores):
        p = jnp.exp2(s - m)
        l = l + jnp.sum(p, axis=-1, keepdims=True)
        o = o + _dot(p.astype(BF16), kcb[c * chunk:(c + 1) * chunk, :])
    olat_ref[0] = o / l


def _sample_attention(page_table, qlat, q, ckvn, krn, cache_c, cache_rt):
    n, n_pages = page_table.shape
    past = n_pages * PAGE_SIZE
    blk = lambda d1, d2: pl.BlockSpec((1, d1, d2), lambda b, pt: (b, 0, 0))
    grid_spec = pltpu.PrefetchScalarGridSpec(
        num_scalar_prefetch=1,
        grid=(n,),
        in_specs=[blk(N_HEADS, KV_LORA), blk(N_HEADS, HEAD_W), blk(1, KV_LORA), blk(1, ROPE_DIM),
                  pl.BlockSpec(memory_space=pl.ANY), pl.BlockSpec(memory_space=pl.ANY)],
        out_specs=blk(N_HEADS, KV_LORA),
        scratch_shapes=[pltpu.VMEM((2, past, KV_LORA), F32), pltpu.VMEM((2, ROPE_DIM, past), F32),
                        pltpu.VMEM((past, KV_LORA), BF16), pltpu.VMEM((ROPE_DIM, past), BF16),
                        pltpu.SemaphoreType.DMA((2, 2))],
    )
    return pl.pallas_call(
        functools.partial(_sample_attention_body, n_pages),
        grid_spec=grid_spec,
        out_shape=jax.ShapeDtypeStruct((n, N_HEADS, KV_LORA), F32),
        compiler_params=_cparams("arbitrary"),
    )(page_table, qlat.reshape(n, N_HEADS, KV_LORA), q.reshape(n, N_HEADS, HEAD_W),
      ckvn.reshape(n, 1, KV_LORA), krn.reshape(n, 1, ROPE_DIM), cache_c, cache_rt)


def _route(logits):
    lane_i = lax.broadcasted_iota(jnp.int32, (logits.shape[0], LANES), 1)
    lane = lane_i.astype(F32)
    first_at = lambda hit: jnp.min(jnp.where(hit, lane, float(LANES)), axis=-1, keepdims=True)
    lg = jnp.where(lane_i < N_GROUPS, logits[:, 0:LANES], NEG)
    mg = jnp.max(lg, axis=-1, keepdims=True)
    p_grp = 1.0 / jnp.sum(jnp.exp(lg - mg), axis=-1, keepdims=True)
    grp = first_at(lg == mg)
    lane_grp = (lane_i // EXP_PER_GROUP).astype(F32)
    le = jnp.where(lane_grp == grp, logits[:, LANES:2 * LANES], NEG)
    top1 = jnp.max(le, axis=-1, keepdims=True)
    i1 = first_at(le == top1)
    le2 = jnp.where(lane == i1, NEG, le)
    top2 = jnp.max(le2, axis=-1, keepdims=True)
    i2 = first_at(le2 == top2)
    e2 = jnp.exp(top2 - top1)
    w1 = p_grp / (1.0 + e2)
    w2 = p_grp * e2 / (1.0 + e2)
    out = jnp.where(lane_i == R_EID, i1, 0.0)
    out = jnp.where(lane_i == R_EID + 1, i2, out)
    out = jnp.where(lane_i == R_WT, w1, out)
    return jnp.where(lane_i == R_WT + 1, w2, out)


def _add_ranks(route, ltri, count_ref):
    n = route.shape[0]
    lane_i = lax.broadcasted_iota(jnp.int32, (n, LANES), 1)
    lane = lane_i.astype(F32)
    hit1 = lane == route[:, R_EID:R_EID + 1]
    hit2 = lane == route[:, R_EID + 1:R_EID + 2]
    chosen = jnp.where(hit1, 1.0, jnp.where(hit2, 1.0, 0.0))
    before = _dot(ltri, chosen.astype(BF16)) + count_ref[...]
    r1 = jnp.sum(jnp.where(hit1, before, 0.0), axis=-1, keepdims=True)
    r2 = jnp.sum(jnp.where(hit2, before, 0.0), axis=-1, keepdims=True)
    count_ref[...] = count_ref[...] + jnp.sum(chosen, axis=0, keepdims=True)
    route = jnp.where(lane_i == R_RANK, r1, route)
    return jnp.where(lane_i == R_RANK + 1, r2, route)


def _merge_rows(alpha, x, bconv, y_attn, wg_ref, wbc_ref, wo_ref, lg_ref, lb_ref, wr_ref, br_ref):
    g = _dot(x.astype(BF16), wg_ref[...])
    y_conv = _dot(bconv, wbc_ref[...])
    m = jax.nn.sigmoid(g[:, 0:D_MODEL]) * y_conv + jax.nn.sigmoid(g[:, D_MODEL:]) * y_attn
    mix = _dot(m.astype(BF16), wo_ref[...])
    h1 = _layernorm(alpha * x + mix, lg_ref[...], lb_ref[...])
    return h1, _route(_dot(h1.astype(BF16), wr_ref[...]) + br_ref[...])


def _merge_sample_body(alpha, x_ref, bconv_ref, olat_ref, wbd_ref, wba_ref, *refs):
    w_refs, (h1_ref, route_ref) = refs[:-2], refs[-2:]
    o = _dot(olat_ref[...].astype(BF16), wbd_ref[...]).astype(BF16)
    h1_ref[...], route_ref[...] = _merge_rows(alpha, x_ref[...], bconv_ref[...], _dot(o, wba_ref[...]), *w_refs)


def _merge_prompt_body(alpha, n_tiles, x_ref, bconv_ref, ot_ref, h1s_ref, routes_ref, wba_ref, *refs):
    w_refs, (h1t_ref, route_ref, counts_ref, ltri_ref, count_ref) = refs[:-5], refs[-5:]
    i = pl.program_id(0)
    tm = x_ref.shape[0]

    @pl.when(i == 0)
    def _():
        r = lax.broadcasted_iota(jnp.int32, (tm, tm), 0)
        c = lax.broadcasted_iota(jnp.int32, (tm, tm), 1)
        ltri_ref[...] = jnp.where(c < r, 1.0, 0.0).astype(BF16)
        count_ref[...] = jnp.zeros(count_ref.shape, F32)

    @pl.when(i < n_tiles)
    def _():
        y_attn = _dot_tn(ot_ref[...], wba_ref[...])
        h1, route = _merge_rows(alpha, x_ref[...], bconv_ref[...], y_attn, *w_refs)
        for c in range(ROW_TILE):
            _tile_rows(h1t_ref, c, tm)[...] = h1[:, c * LANES:(c + 1) * LANES]
        route_ref[...] = _add_ranks(route, ltri_ref[...], count_ref)

    @pl.when(i == n_tiles)
    def _():
        n_s = h1s_ref.shape[0]
        h1t_ref[...] = jnp.zeros(h1t_ref.shape, F32)
        route_ref[...] = jnp.zeros(route_ref.shape, F32)
        for c in range(ROW_TILE):
            _tile_rows(h1t_ref, c, n_s)[...] = h1s_ref[:, c * LANES:(c + 1) * LANES]
        route_ref[0:n_s, :] = _add_ranks(routes_ref[...], ltri_ref[0:n_s, 0:n_s], count_ref)

    counts_ref[...] = jnp.broadcast_to(count_ref[...], counts_ref.shape)


def _merge_sample(alpha, x, bconv, olat, wbd, wba, ws):
    n = x.shape[0]
    args = (x, bconv, olat, wbd, wba) + tuple(ws)
    out_shape = [jax.ShapeDtypeStruct((n, D_MODEL), F32), jax.ShapeDtypeStruct((n, LANES), F32)]
    return pl.pallas_call(
        functools.partial(_merge_sample_body, alpha),
        grid=(1,),
        in_specs=[_full(a.shape) for a in args],
        out_specs=[_full(s.shape) for s in out_shape],
        out_shape=out_shape,
        compiler_params=_cparams("arbitrary"),
    )(*args)


def _merge_prompt(alpha, x, bconv, ot, h1_s, route_s, wba, ws):
    t = x.shape[0]
    tm = TM
    n_tiles = t // tm
    t_all = t + h1_s.shape[0]
    assert h1_s.shape[0] <= tm
    clamp = lambda i: jnp.minimum(i, n_tiles - 1)
    rows_in = lambda w: pl.BlockSpec((tm, w), lambda i: (clamp(i), 0))
    return pl.pallas_call(
        functools.partial(_merge_prompt_body, alpha, n_tiles),
        grid=(n_tiles + 1,),
        in_specs=[rows_in(D_MODEL), rows_in(D_CONV), pl.BlockSpec((V_W, tm), lambda i: (0, clamp(i))),
                  _full(h1_s.shape), _full(route_s.shape), _full(wba.shape)] + [_full(w.shape) for w in ws],
        out_specs=[pl.BlockSpec((tm * ROW_TILE, LANES), lambda i: (i, 0)), pl.BlockSpec((tm, LANES), lambda i: (i, 0)),
                   _full((SUBLANES, LANES))],
        out_shape=[jax.ShapeDtypeStruct((t_all * ROW_TILE, LANES), F32), jax.ShapeDtypeStruct((t_all, LANES), F32),
                   jax.ShapeDtypeStruct((SUBLANES, LANES), F32)],
        scratch_shapes=[pltpu.VMEM((tm, tm), BF16), pltpu.VMEM((1, LANES), F32)],
        compiler_params=_cparams("arbitrary"),
    )(x, bconv, ot, h1_s, route_s, wba, *ws)


def _slot_tokens_body(rows_ref, dest_ref, tok_ref):
    def clear_row(j, c):
        for l in range(LANES):
            tok_ref[j, l] = 0
        return c
    lax.fori_loop(0, rows_ref[1], clear_row, 0)

    def fill_row(j, c):
        for l in range(LANES):
            d = dest_ref[j, l]
            tok_ref[d // LANES, d % LANES] = (j * LANES + l) // TOP_K
        return c
    lax.fori_loop(0, rows_ref[0], fill_row, 0)


def _slot_tokens(dest, n_blocks):
    assert MOE_BLOCK == LANES and dest.shape[0] % LANES == 0
    dest = dest.reshape(-1, LANES)
    grid_spec = pltpu.PrefetchScalarGridSpec(
        num_scalar_prefetch=2,
        grid=(1,),
        in_specs=[],
        out_specs=pl.BlockSpec(memory_space=pltpu.SMEM),
    )
    return pl.pallas_call(
        _slot_tokens_body,
        grid_spec=grid_spec,
        out_shape=jax.ShapeDtypeStruct((n_blocks, LANES), jnp.int32),
        compiler_params=_cparams("arbitrary"),
    )(jnp.array([dest.shape[0], n_blocks], jnp.int32), dest)


def _experts_body(blk_e_ref, used_ref, tok_ref, h1t_ref, wg_ref, wu_ref, wd_ref, ys_ref, xbuf, xb, wgb, wub, wdb,
                  sem):
    b = pl.program_id(0)
    slot = b % 2
    used = used_ref[0]
    changed = jnp.logical_or(b == 0, blk_e_ref[b] != blk_e_ref[jnp.maximum(b - 1, 0)])

    def row_copy(bb, sl, r):
        tok = tok_ref[bb, r]
        return pltpu.make_async_copy(h1t_ref.at[pl.ds(pl.multiple_of(tok * ROW_TILE, ROW_TILE), ROW_TILE)],
                                     xbuf.at[sl, pl.ds(pl.multiple_of(r * ROW_TILE, ROW_TILE), ROW_TILE)],
                                     sem.at[sl])

    def issue(bb, sl):
        def body(r, c):
            row_copy(bb, sl, r).start()
            return c
        lax.fori_loop(0, MOE_BLOCK, body, 0, unroll=8)

    @pl.when(jnp.logical_and(b == 0, used > 0))
    def _():
        issue(0, 0)

    @pl.when(b + 1 < used)
    def _():
        issue(b + 1, 1 - slot)

    @pl.when(jnp.logical_and(b < used, changed))
    def _():
        wgb[...] = wg_ref[0].astype(BF16)
        wub[...] = wu_ref[0].astype(BF16)
        wdb[...] = wd_ref[0].astype(BF16)

    @pl.when(b < used)
    def _():
        def wait_body(r, c):
            row_copy(b, slot, r).wait()
            return c
        lax.fori_loop(0, MOE_BLOCK, wait_body, 0, unroll=8)
        for c in range(ROW_TILE):
            xb[:, c * LANES:(c + 1) * LANES] = _tile_rows(xbuf.at[slot], c, MOE_BLOCK)[...].astype(BF16)
        x = xb[...]
        g = _dot(x, wgb[...])
        u = _dot(x, wub[...])
        h = (g * jax.nn.sigmoid(g) * u).astype(BF16)
        y = _dot(h, wdb[...])
        for c in range(ROW_TILE):
            _tile_rows(ys_ref, c, MOE_BLOCK)[...] = y[:, c * LANES:(c + 1) * LANES]

    @pl.when(b >= used)
    def _():
        ys_ref[...] = jnp.zeros(ys_ref.shape, F32)


def _experts(blk_e, used, slot_tok, h1t, wg, wu, wd):
    n_blocks = blk_e.shape[0]
    blk_rows = MOE_BLOCK * ROW_TILE
    grid_spec = pltpu.PrefetchScalarGridSpec(
        num_scalar_prefetch=3,
        grid=(n_blocks,),
        in_specs=[pl.BlockSpec(memory_space=pl.ANY),
                  pl.BlockSpec((1, D_MODEL, D_EXPERT), lambda b, e, u, t: (e[b], 0, 0)),
                  pl.BlockSpec((1, D_MODEL, D_EXPERT), lambda b, e, u, t: (e[b], 0, 0)),
                  pl.BlockSpec((1, D_EXPERT, D_MODEL), lambda b, e, u, t: (e[b], 0, 0))],
        out_specs=pl.BlockSpec((blk_rows, LANES), lambda b, e, u, t: (b, 0)),
        scratch_shapes=[pltpu.VMEM((2, blk_rows, LANES), F32), pltpu.VMEM((MOE_BLOCK, D_MODEL), BF16),
                        pltpu.VMEM((D_MODEL, D_EXPERT), BF16), pltpu.VMEM((D_MODEL, D_EXPERT), BF16),
                        pltpu.VMEM((D_EXPERT, D_MODEL), BF16), pltpu.SemaphoreType.DMA((2,))],
    )
    return pl.pallas_call(
        _experts_body,
        grid_spec=grid_spec,
        out_shape=jax.ShapeDtypeStruct((n_blocks * blk_rows, LANES), F32),
        compiler_params=_cparams("arbitrary"),
    )(blk_e, used, slot_tok, h1t, wg, wu, wd)


def _combine_body(alpha, tok0, dest_ref, ys_ref, h1t_ref, route_ref, lg_ref, lb_ref, out_ref, gbuf, sem):
    i = pl.program_id(0)
    nt = pl.num_programs(0)
    slot = i % 2
    tm = out_ref.shape[0]

    def row_copy(ii, sl, r, k):
        d = dest_ref[(tok0 + ii * tm + r) * TOP_K + k]
        return pltpu.make_async_copy(ys_ref.at[pl.ds(pl.multiple_of(d * ROW_TILE, ROW_TILE), ROW_TILE)],
                                     gbuf.at[sl, k, pl.ds(pl.multiple_of(r * ROW_TILE, ROW_TILE), ROW_TILE)],
                                     sem.at[sl])

    def issue(ii, sl):
        def body(r, _):
            for k in range(TOP_K):
                row_copy(ii, sl, r, k).start()
            return 0
        lax.fori_loop(0, tm, body, 0)

    @pl.when(i == 0)
    def _():
        issue(0, 0)

    @pl.when(i + 1 < nt)
    def _():
        issue(i + 1, 1 - slot)

    def wait_body(r, _):
        for k in range(TOP_K):
            row_copy(i, slot, r, k).wait()
        return 0
    lax.fori_loop(0, tm, wait_body, 0)

    route = route_ref[...]
    w0 = route[:, R_WT:R_WT + 1]
    w1 = route[:, R_WT + 1:R_WT + 2]
    z = [alpha * _tile_rows(h1t_ref, c, tm)[...]
         + w0 * _tile_rows(gbuf.at[slot, 0], c, tm)[...] + w1 * _tile_rows(gbuf.at[slot, 1], c, tm)[...]
         for c in range(ROW_TILE)]
    mu = sum(jnp.sum(zc, axis=-1, keepdims=True) for zc in z) / D_MODEL
    zc = [v - mu for v in z]
    var = sum(jnp.sum(v * v, axis=-1, keepdims=True) for v in zc) / D_MODEL
    rstd = lax.rsqrt(var + LN_EPS)
    for c in range(ROW_TILE):
        sl = slice(c * LANES, (c + 1) * LANES)
        out_ref[:, sl] = zc[c] * rstd * lg_ref[:, sl] + lb_ref[:, sl]


def _combine(alpha, tok0, n_tok, tm, dest, ys, h1t, route, lg, lb):
    assert n_tok % tm == 0 and tok0 % tm == 0
    blk0 = tok0 // tm
    grid_spec = pltpu.PrefetchScalarGridSpec(
        num_scalar_prefetch=1,
        grid=(n_tok // tm,),
        in_specs=[pl.BlockSpec(memory_space=pl.ANY),
                  pl.BlockSpec((tm * ROW_TILE, LANES), lambda i, d: (blk0 + i, 0)),
                  pl.BlockSpec((tm, LANES), lambda i, d: (blk0 + i, 0)),
                  pl.BlockSpec(lg.shape, lambda i, d: (0, 0)), pl.BlockSpec(lb.shape, lambda i, d: (0, 0))],
        out_specs=pl.BlockSpec((tm, D_MODEL), lambda i, d: (i, 0)),
        scratch_shapes=[pltpu.VMEM((2, TOP_K, tm * ROW_TILE, LANES), F32), pltpu.SemaphoreType.DMA((2,))],
    )
    return pl.pallas_call(
        functools.partial(_combine_body, alpha, tok0),
        grid_spec=grid_spec,
        out_shape=jax.ShapeDtypeStruct((n_tok, D_MODEL), F32),
        compiler_params=_cparams("arbitrary"),
    )(dest, ys, h1t, route, lg, lb)


def _rope_tables(pos):
    inv = ROPE_BASE ** (-(jnp.arange(ROPE_HALF, dtype=F32) * 2.0 / ROPE_DIM))
    ang = pos.astype(F32)[:, None] * inv[None, :]
    cos, sin = jnp.cos(ang), jnp.sin(ang)
    n = pos.shape[0]
    a = jnp.concatenate([cos, cos, jnp.ones((n, LANES - ROPE_DIM), F32)], axis=1)
    bm = jnp.concatenate([-sin, jnp.zeros((n, LANES - ROPE_HALF), F32)], axis=1)
    cp = jnp.concatenate([jnp.zeros((n, ROPE_HALF), F32), sin, jnp.zeros((n, LANES - ROPE_DIM), F32)], axis=1)
    k_tabs = jnp.stack([a, bm, cp])
    return jnp.concatenate([k_tabs * (ATTN_SCALE * LOG2E), k_tabs])


def _head_blocks(w_rope, w_nope):
    k = w_nope.shape[0]
    pad = jnp.zeros((k, N_HEADS, HEAD_W - QK_DIM), w_nope.dtype)
    return jnp.concatenate([w_rope, w_nope, pad], axis=-1).reshape(k, QK_W)


def _pack_weights(w_in, w_uq, w_uk, w_uv, router_w_group, router_b_group, router_w_expert, router_b_expert):
    d = w_in.shape[0]
    c_kr = 3 * D_CONV + Q_LORA + KV_LORA
    w1 = jnp.concatenate([w_in[:, :c_kr], w_in[:, c_kr:c_kr + ROPE_DIM],
                          jnp.zeros((d, LANES - ROPE_DIM), w_in.dtype)], axis=1).astype(BF16)
    wg = w_in[:, c_kr + ROPE_DIM:].astype(BF16)
    uq = w_uq.reshape(Q_LORA, N_HEADS, QK_DIM)
    wuq = _head_blocks(uq[..., NOPE_DIM:], uq[..., :NOPE_DIM]).astype(BF16)
    wuk = _head_blocks(jnp.zeros((KV_LORA, N_HEADS, ROPE_DIM), w_uk.dtype), w_uk).astype(BF16)
    wukt = jnp.transpose(wuk.reshape(KV_LORA, N_HEADS, HEAD_W), (1, 2, 0))
    wuvt = w_uv.reshape(KV_LORA, V_W).T.astype(BF16)
    eye = jnp.eye(N_HEADS, dtype=w_uv.dtype)
    wbd = jnp.einsum('chd,hg->hcgd', w_uv, eye).reshape(N_HEADS * KV_LORA, V_W).astype(BF16)
    wr = jnp.zeros((d, ROUTE_W), F32)
    wr = wr.at[:, 0:N_GROUPS].set(router_w_group).at[:, LANES:LANES + N_EXPERTS].set(router_w_expert).astype(BF16)
    br = jnp.zeros((1, ROUTE_W), F32)
    br = br.at[0, 0:N_GROUPS].set(router_b_group).at[0, LANES:LANES + N_EXPERTS].set(router_b_expert)
    return w1, wg, wuq, wuk, wukt, wuvt, wbd, wr, br


def _dispatch_plan(route, counts, n_blocks):
    counts = counts[0, 0:N_EXPERTS].astype(jnp.int32)
    pcounts = (counts + MOE_BLOCK - 1) // MOE_BLOCK * MOE_BLOCK
    pend = jnp.cumsum(pcounts).astype(jnp.int32)
    pstart = (pend - pcounts).astype(F32)
    eid = route[:, R_EID:R_EID + TOP_K]
    rank = route[:, R_RANK:R_RANK + TOP_K]
    onehot = eid[:, :, None] == jnp.arange(N_EXPERTS, dtype=F32)[None, None, :]
    dest = (jnp.sum(jnp.where(onehot, pstart[None, None, :], 0.0), axis=-1) + rank).astype(jnp.int32).reshape(-1)
    blk_start = jnp.arange(n_blocks, dtype=jnp.int32) * MOE_BLOCK
    blk_e = jnp.sum((blk_start[:, None] >= pend[None, :]).astype(jnp.int32), axis=1)
    blk_e = jnp.minimum(blk_e, N_EXPERTS - 1).astype(jnp.int32)
    used = (pend[-1:] // MOE_BLOCK).astype(jnp.int32)
    return dest, blk_e, used


def kernel(x_prompt, x_sample, cache_ckv, cache_krope, state_conv, page_table, w_in, conv_w, q_norm_g, w_uq,
           kv_norm_g, w_uk, w_uv, w_br_conv, w_br_attn, w_o, ln1_g, ln1_b, router_w_group, router_b_group,
           router_w_expert, router_b_expert, w_gate, w_up, w_down, ln2_g, ln2_b):
    depth = w_in.shape[0]
    alpha = (2 * depth) ** 0.25
    n_p, t_p, _ = x_prompt.shape
    n_s, t_s, _ = x_sample.shape
    assert t_s == 1 and t_p % TM == 0 and n_s % MOE_BLOCK == 0
    rows_p = n_p * t_p
    rows_all = rows_p + n_s
    past = page_table.shape[1] * PAGE_SIZE
    tab_p = _rope_tables(jnp.arange(t_p))
    tab_s = _rope_tables(jnp.full((n_s,), past, jnp.int32))
    n_blocks = -(-(rows_all * TOP_K) // MOE_BLOCK) + N_EXPERTS

    h_p = x_prompt.reshape(rows_p, D_MODEL)
    h_s = x_sample.reshape(n_s, D_MODEL)
    ckv_p, kr_p, cv_p, ckv_s, kr_s, cv_s = [], [], [], [], [], []
    row = lambda v: v.reshape(1, -1)
    for l in range(depth):
        w1, wg, wuq, wuk, wukt, wuvt, wbd, wr, br = _pack_weights(
            w_in[l], w_uq[l], w_uk[l], w_uv[l], router_w_group[l], router_b_group[l],
            router_w_expert[l], router_b_expert[l])
        qg, kvg = row(q_norm_g[l]), row(kv_norm_g[l])
        wba = w_br_attn[l].astype(BF16)
        merge_w = (wg, w_br_conv[l].astype(BF16), w_o[l].astype(BF16), row(ln1_g[l]), row(ln1_b[l]), wr, br)

        bconv, q, k, vt, ckv, kr, nconv = _inproj_prompt(h_p, tab_p, w1, wuq, wuk, wuvt, qg, kvg, conv_w[l], n_p, t_p)
        ot = _prompt_attention(q, k, vt, n_p, t_p)

        st = state_conv[l]
        bconv_s, q_s, qlat, ckvn_s, krn_s, u_s = _inproj_sample(
            h_s, tab_s, w1, wuq, wukt, qg, kvg, conv_w[l], st[:, 0], st[:, 1])
        olat = _sample_attention(page_table, qlat, q_s, ckvn_s, krn_s, cache_ckv[l],
                                 jnp.swapaxes(cache_krope[l], 1, 2))
        h1_s, route_s = _merge_sample(alpha, h_s, bconv_s, olat.reshape(n_s, N_HEADS * KV_LORA), wbd, wba, merge_w)
        h1t, route, counts = _merge_prompt(alpha, h_p, bconv, ot, h1_s, route_s, wba, merge_w)

        dest, blk_e, used = _dispatch_plan(route, counts, n_blocks)
        slot_tok = _slot_tokens(dest, n_blocks)
        ys = _experts(blk_e, used, slot_tok, h1t, w_gate[l], w_up[l], w_down[l])
        ln2 = (row(ln2_g[l]), row(ln2_b[l]))
        h_p = _combine(alpha, 0, rows_p, TM, dest, ys, h1t, route, *ln2)
        h_s = _combine(alpha, rows_p, n_s, n_s, dest, ys, h1t, route, *ln2)

        ckv_p.append(ckv.reshape(n_p, t_p, KV_LORA))
        kr_p.append(kr.reshape(n_p, t_p, ROPE_DIM))
        cv_p.append(nconv)
        ckv_s.append(ckvn_s.reshape(n_s, 1, KV_LORA))
        kr_s.append(krn_s.reshape(n_s, 1, ROPE_DIM))
        cv_s.append(jnp.stack([st[:, 1], u_s], axis=1))
    return (h_p.reshape(n_p, t_p, D_MODEL), h_s.reshape(n_s, 1, D_MODEL), jnp.stack(ckv_p), jnp.stack(kr_p),
            jnp.stack(cv_p), jnp.stack(ckv_s), jnp.stack(kr_s), jnp.stack(cv_s))
```

```python
import functools

import jax
import jax.numpy as jnp
from jax import lax
from jax.experimental import pallas as pl
from jax.experimental.pallas import tpu as pltpu

F32 = jnp.float32
BF16 = jnp.bfloat16

D_MODEL = 1024
D_CONV = 512
CONV_W = 3
N_HEADS = 8
Q_LORA = 384
KV_LORA = 256
NOPE_DIM = 64
ROPE_DIM = 32
ROPE_HALF = ROPE_DIM // 2
V_DIM = 64
QK_DIM = NOPE_DIM + ROPE_DIM
ROPE_BASE = 10000.0
ATTN_SCALE = QK_DIM ** -0.5
LOG2E = 1.4426950408889634
N_GROUPS = 4
EXP_PER_GROUP = 8
N_EXPERTS = N_GROUPS * EXP_PER_GROUP
TOP_K = 2
D_EXPERT = 512
MOE_BLOCK = 128
PAGE_SIZE = 128
LN_EPS = 1e-5
RMS_EPS = 1e-6

LANES = 128
SUBLANES = 8
ROW_TILE = D_MODEL // LANES
assert ROW_TILE == SUBLANES
HEAD_W = LANES
QK_W = N_HEADS * HEAD_W
V_W = N_HEADS * V_DIM
_C_BG, _C_CG, _C_H = 0, D_CONV, 2 * D_CONV
_C_CQ = 3 * D_CONV
_C_CKV = _C_CQ + Q_LORA
_C_KR = _C_CKV + KV_LORA
W1_COLS = _C_KR + LANES
ROUTE_W = 2 * LANES
R_EID, R_WT, R_RANK = 0, 2, 4
NEG = -1e30
VMEM_LIMIT = 56 * 1024 * 1024

TM = 512
KEY_CHUNK = 2048
ATTN_HEADS_PER_LOOP = 4
TQ = 512


def _cparams(*sem):
    return pltpu.CompilerParams(dimension_semantics=sem, vmem_limit_bytes=VMEM_LIMIT)


def _dot(a, b):
    return jnp.dot(a, b, preferred_element_type=F32)


def _dot_nt(a, b):
    return lax.dot_general(a, b, (((1,), (1,)), ((), ())), preferred_element_type=F32)


def _dot_tn(a, b):
    return lax.dot_general(a, b, (((0,), (0,)), ((), ())), preferred_element_type=F32)


def _rms(x, g):
    return x * lax.rsqrt(jnp.mean(x * x, axis=-1, keepdims=True) + RMS_EPS) * g


def _layernorm(x, g, b):
    mu = jnp.mean(x, axis=-1, keepdims=True)
    xc = x - mu
    var = jnp.mean(xc * xc, axis=-1, keepdims=True)
    return xc * lax.rsqrt(var + LN_EPS) * g + b


def _rope(xh, a, bm, cp):
    return xh * a + pltpu.roll(xh, LANES - ROPE_HALF, 1) * bm + pltpu.roll(xh, ROPE_HALF, 1) * cp


def _full(shape):
    nd = len(shape)
    return pl.BlockSpec(shape, lambda *_: (0,) * nd)


def _tile_rows(ref, c, n):
    return ref.at[pl.ds(c, n, stride=ROW_TILE), :]


def _inproj_common(x_ref, tab_ref, w1_ref, wuq_ref, qg_ref, kvg_ref, q_ref, ckv_ref, kr_ref):
    xb = x_ref[...].astype(BF16)

    def seg(lo, hi):
        return _dot(xb, w1_ref[:, lo:hi])

    b_g = seg(_C_BG, _C_CG)
    u = seg(_C_CG, _C_H) * seg(_C_H, _C_CQ)
    cqn = _rms(seg(_C_CQ, _C_CKV), qg_ref[...]).astype(BF16)
    qf = _dot(cqn, wuq_ref[...])
    for h in range(N_HEADS):
        sl = slice(h * HEAD_W, (h + 1) * HEAD_W)
        q_ref[:, sl] = _rope(qf[:, sl], tab_ref[0], tab_ref[1], tab_ref[2]).astype(BF16)
    ckvn = _rms(seg(_C_CKV, _C_KR), kvg_ref[...])
    ckv_ref[...] = ckvn
    krr = _rope(seg(_C_KR, W1_COLS), tab_ref[3], tab_ref[4], tab_ref[5])
    kr_ref[...] = krr[:, 0:ROPE_DIM]
    return b_g, u, ckvn.astype(BF16), krr


def _inproj_prompt_body(tiles_per_seq, x_ref, tab_ref, w1_ref, wuq_ref, wuk_ref, wuvt_ref, qg_ref, kvg_ref,
                        cw_ref, bconv_ref, q_ref, k_ref, vt_ref, ckv_ref, kr_ref, nconv_ref, uext_ref):
    tm = x_ref.shape[0]
    b_g, u, cb, krr = _inproj_common(x_ref, tab_ref, w1_ref, wuq_ref, qg_ref, kvg_ref, q_ref, ckv_ref, kr_ref)
    kn = _dot(cb, wuk_ref[...])
    for h in range(N_HEADS):
        sl = slice(h * HEAD_W, (h + 1) * HEAD_W)
        k_ref[:, sl] = (kn[:, sl] + krr).astype(BF16)
    vt_ref[...] = _dot_nt(wuvt_ref[...], cb).astype(BF16)

    first = (pl.program_id(0) % tiles_per_seq) == 0

    @pl.when(first)
    def _():
        uext_ref[0:8, :] = jnp.zeros((8, D_CONV), F32)

    @pl.when(jnp.logical_not(first))
    def _():
        uext_ref[0:8, :] = uext_ref[tm:tm + 8, :]

    uext_ref[8:8 + tm, :] = u
    conv = cw_ref[0:1, :] * uext_ref[6:6 + tm, :] + cw_ref[1:2, :] * uext_ref[7:7 + tm, :] + cw_ref[2:3, :] * u
    bconv_ref[...] = (b_g * conv).astype(BF16)
    nconv_ref[0] = u[tm - (CONV_W - 1):tm, :]


def _inproj_sample_body(x_ref, tab_ref, w1_ref, wuq_ref, wukt_ref, qg_ref, kvg_ref, cw_ref, s0_ref, s1_ref,
                        bconv_ref, q_ref, qlat_ref, ckv_ref, kr_ref, u_ref):
    b_g, u, _, _ = _inproj_common(x_ref, tab_ref, w1_ref, wuq_ref, qg_ref, kvg_ref, q_ref, ckv_ref, kr_ref)
    for h in range(N_HEADS):
        qh = q_ref[:, h * HEAD_W:(h + 1) * HEAD_W]
        qlat_ref[:, h * KV_LORA:(h + 1) * KV_LORA] = _dot(qh, wukt_ref[h]).astype(BF16)
    conv = cw_ref[0:1, :] * s0_ref[...] + cw_ref[1:2, :] * s1_ref[...] + cw_ref[2:3, :] * u
    bconv_ref[...] = (b_g * conv).astype(BF16)
    u_ref[...] = u


def _inproj_prompt(x, tab, w1, wuq, wuk, wuvt, qg, kvg, cw, n_seq, seq):
    t = x.shape[0]
    tm = TM
    tiles_per_seq = seq // tm
    rows = lambda w: pl.BlockSpec((tm, w), lambda i: (i, 0))
    return pl.pallas_call(
        functools.partial(_inproj_prompt_body, tiles_per_seq),
        grid=(t // tm,),
        in_specs=[rows(D_MODEL),
                  pl.BlockSpec((6, tm, LANES), lambda i: (0, i % tiles_per_seq, 0)),
                  _full(w1.shape), _full(wuq.shape), _full(wuk.shape), _full(wuvt.shape),
                  _full(qg.shape), _full(kvg.shape), _full(cw.shape)],
        out_specs=[rows(D_CONV), rows(QK_W), rows(QK_W), pl.BlockSpec((V_W, tm), lambda i: (0, i)),
                   rows(KV_LORA), rows(ROPE_DIM),
                   pl.BlockSpec((1, CONV_W - 1, D_CONV), lambda i: (i // tiles_per_seq, 0, 0))],
        out_shape=[jax.ShapeDtypeStruct((t, D_CONV), BF16), jax.ShapeDtypeStruct((t, QK_W), BF16),
                   jax.ShapeDtypeStruct((t, QK_W), BF16), jax.ShapeDtypeStruct((V_W, t), BF16),
                   jax.ShapeDtypeStruct((t, KV_LORA), F32), jax.ShapeDtypeStruct((t, ROPE_DIM), F32),
                   jax.ShapeDtypeStruct((n_seq, CONV_W - 1, D_CONV), F32)],
        scratch_shapes=[pltpu.VMEM((tm + 8, D_CONV), F32)],
        compiler_params=_cparams("arbitrary"),
    )(x, tab, w1, wuq, wuk, wuvt, qg, kvg, cw)


def _inproj_sample(x, tab, w1, wuq, wukt, qg, kvg, cw, s0, s1):
    n = x.shape[0]
    args = (x, tab, w1, wuq, wukt, qg, kvg, cw, s0, s1)
    out_shape = [jax.ShapeDtypeStruct((n, D_CONV), BF16), jax.ShapeDtypeStruct((n, QK_W), BF16),
                 jax.ShapeDtypeStruct((n, N_HEADS * KV_LORA), BF16), jax.ShapeDtypeStruct((n, KV_LORA), F32),
                 jax.ShapeDtypeStruct((n, ROPE_DIM), F32), jax.ShapeDtypeStruct((n, D_CONV), F32)]
    return pl.pallas_call(
        _inproj_sample_body,
        grid=(1,),
        in_specs=[_full(a.shape) for a in args],
        out_specs=[_full(s.shape) for s in out_shape],
        out_shape=out_shape,
        compiler_params=_cparams("arbitrary"),
    )(*args)


def _prompt_attention_body(q_ref, k_ref, vt_ref, ot_ref, acc_ref):
    tq = q_ref.shape[0]
    i = pl.program_id(1)
    krow = lax.broadcasted_iota(jnp.int32, (tq, tq), 0)
    qcol = lax.broadcasted_iota(jnp.int32, (tq, tq), 1)
    causal = krow <= qcol

    for hg in range(N_HEADS // ATTN_HEADS_PER_LOOP):
        heads = tuple(range(hg * ATTN_HEADS_PER_LOOP, (hg + 1) * ATTN_HEADS_PER_LOOP))
        qs = [q_ref[:, h * HEAD_W:(h + 1) * HEAD_W] for h in heads]
        acc_ref[...] = jnp.zeros(acc_ref.shape, F32)

        def step(j, carry, masked):
            off = pl.multiple_of(j * tq, tq)
            out = []
            for idx, h in enumerate(heads):
                m, l = carry[idx]
                kblk = k_ref[pl.ds(off, tq), h * HEAD_W:(h + 1) * HEAD_W]
                s = _dot_nt(kblk, qs[idx])
                if masked:
                    s = jnp.where(causal, s, NEG)
                m_new = jnp.maximum(m, jnp.max(s, axis=0, keepdims=True))
                alpha = jnp.exp2(m - m_new)
                p = jnp.exp2(s - m_new)
                l_new = alpha * l + jnp.sum(p, axis=0, keepdims=True)
                vblk = vt_ref[h * V_DIM:(h + 1) * V_DIM, pl.ds(off, tq)]
                acc_ref[idx] = alpha * acc_ref[idx] + _dot(vblk, p.astype(BF16))
                out.append((m_new, l_new))
            return tuple(out)

        init = tuple((jnp.full((1, tq), NEG, F32), jnp.zeros((1, tq), F32)) for _ in heads)
        carry = lax.fori_loop(0, i, lambda j, c: step(j, c, False), init)
        carry = step(i, carry, True)
        for idx, h in enumerate(heads):
            ot_ref[h * V_DIM:(h + 1) * V_DIM, :] = (acc_ref[idx] / carry[idx][1]).astype(BF16)


def _prompt_attention(q, k, vt, n_seq, seq):
    t = q.shape[0]
    tq = TQ
    nq = seq // tq
    return pl.pallas_call(
        _prompt_attention_body,
        grid=(n_seq, nq),
        in_specs=[pl.BlockSpec((tq, QK_W), lambda b, i: (b * nq + i, 0)),
                  pl.BlockSpec((seq, QK_W), lambda b, i: (b, 0)),
                  pl.BlockSpec((V_W, seq), lambda b, i: (0, b))],
        out_specs=pl.BlockSpec((V_W, tq), lambda b, i: (0, b * nq + i)),
        out_shape=jax.ShapeDtypeStruct((V_W, t), BF16),
        scratch_shapes=[pltpu.VMEM((ATTN_HEADS_PER_LOOP, V_DIM, tq), F32)],
        compiler_params=_cparams("arbitrary", "arbitrary"),
    )(q, k, vt)


def _sample_attention_body(n_pages, pt_ref, qlat_ref, q_ref, ckvn_ref, krn_ref, cc_ref, cr_ref, olat_ref,
                           bufc, bufr, kcb, krb, sem):
    b = pl.program_id(0)
    nb = pl.num_programs(0)
    slot = b % 2
    past = n_pages * PAGE_SIZE

    def page_copies(bb, sl, p):
        pg = pt_ref[bb, p]
        pos = pl.ds(pl.multiple_of(p * PAGE_SIZE, PAGE_SIZE), PAGE_SIZE)
        return (pltpu.make_async_copy(cc_ref.at[pg], bufc.at[sl, pos], sem.at[0, sl]),
                pltpu.make_async_copy(cr_ref.at[pg], bufr.at[sl, :, pos], sem.at[1, sl]))

    def issue(bb, sl):
        for p in range(n_pages):
            for c in page_copies(bb, sl, p):
                c.start()

    def wait(sl):
        pltpu.make_async_copy(bufc.at[sl], bufc.at[sl], sem.at[0, sl]).wait()
        pltpu.make_async_copy(bufr.at[sl], bufr.at[sl], sem.at[1, sl]).wait()

    @pl.when(b == 0)
    def _():
        issue(0, 0)

    wait(slot)
    issue(jnp.minimum(b + 1, nb - 1), 1 - slot)

    ql = qlat_ref[0]
    qr = q_ref[0][:, 0:ROPE_DIM]
    cn = ckvn_ref[0]
    rn = krn_ref[0]
    chunk = min(KEY_CHUNK, past)
    scores = []
    for c in range(past // chunk):
        pos = slice(c * chunk, (c + 1) * chunk)
        kcb[pos, :] = bufc[slot, pos, :].astype(BF16)
        krb[:, pos] = bufr[slot, :, pos].astype(BF16)
        scores.append(_dot_nt(ql, kcb[pos, :]) + _dot(qr, krb[:, pos]))
    s_new = (jnp.sum(ql.astype(F32) * cn, axis=-1, keepdims=True)
             + jnp.sum(qr.astype(F32) * rn, axis=-1, keepdims=True))
    m = s_new
    for s in scores:
        m = jnp.maximum(m, jnp.max(s, axis=-1, keepdims=True))
    p_new = jnp.exp2(s_new - m)
    l = p_new
    o = p_new * cn
    for c, s in enumerate(scores):
        p = jnp.exp2(s - m)
        l = l + jnp.sum(p, axis=-1, keepdims=True)
        o = o + _dot(p.astype(BF16), kcb[c * chunk:(c + 1) * chunk, :])
    olat_ref[0] = o / l

    @pl.when(b == nb - 1)
    def _():
        wait(1 - slot)


def _sample_attention(page_table, qlat, q, ckvn, krn, cache_c, cache_rt):
    n, n_pages = page_table.shape
    past = n_pages * PAGE_SIZE
    blk = lambda d1, d2: pl.BlockSpec((1, d1, d2), lambda b, pt: (b, 0, 0))
    grid_spec = pltpu.PrefetchScalarGridSpec(
        num_scalar_prefetch=1,
        grid=(n,),
        in_specs=[blk(N_HEADS, KV_LORA), blk(N_HEADS, HEAD_W), blk(1, KV_LORA), blk(1, ROPE_DIM),
                  pl.BlockSpec(memory_space=pl.ANY), pl.BlockSpec(memory_space=pl.ANY)],
        out_specs=blk(N_HEADS, KV_LORA),
        scratch_shapes=[pltpu.VMEM((2, past, KV_LORA), F32), pltpu.VMEM((2, ROPE_DIM, past), F32),
                        pltpu.VMEM((past, KV_LORA), BF16), pltpu.VMEM((ROPE_DIM, past), BF16),
                        pltpu.SemaphoreType.DMA((2, 2))],
    )
    return pl.pallas_call(
        functools.partial(_sample_attention_body, n_pages),
        grid_spec=grid_spec,
        out_shape=jax.ShapeDtypeStruct((n, N_HEADS, KV_LORA), F32),
        compiler_params=_cparams("arbitrary"),
    )(page_table, qlat.reshape(n, N_HEADS, KV_LORA), q.reshape(n, N_HEADS, HEAD_W),
      ckvn.reshape(n, 1, KV_LORA), krn.reshape(n, 1, ROPE_DIM), cache_c, cache_rt)


def _route(logits):
    lane_i = lax.broadcasted_iota(jnp.int32, (logits.shape[0], LANES), 1)
    lane = lane_i.astype(F32)
    first_at = lambda hit: jnp.min(jnp.where(hit, lane, float(LANES)), axis=-1, keepdims=True)
    lg = jnp.where(lane_i < N_GROUPS, logits[:, 0:LANES], NEG)
    mg = jnp.max(lg, axis=-1, keepdims=True)
    p_grp = 1.0 / jnp.sum(jnp.exp(lg - mg), axis=-1, keepdims=True)
    grp = first_at(lg == mg)
    lane_grp = (lane_i // EXP_PER_GROUP).astype(F32)
    le = jnp.where(lane_grp == grp, logits[:, LANES:2 * LANES], NEG)
    top1 = jnp.max(le, axis=-1, keepdims=True)
    i1 = first_at(le == top1)
    le2 = jnp.where(lane == i1, NEG, le)
    top2 = jnp.max(le2, axis=-1, keepdims=True)
    i2 = first_at(le2 == top2)
    e2 = jnp.exp(top2 - top1)
    w1 = p_grp / (1.0 + e2)
    w2 = p_grp * e2 / (1.0 + e2)
    out = jnp.where(lane_i == R_EID, i1, 0.0)
    out = jnp.where(lane_i == R_EID + 1, i2, out)
    out = jnp.where(lane_i == R_WT, w1, out)
    return jnp.where(lane_i == R_WT + 1, w2, out)


def _add_ranks(route, ltri, count_ref):
    n = route.shape[0]
    lane_i = lax.broadcasted_iota(jnp.int32, (n, LANES), 1)
    lane = lane_i.astype(F32)
    hit1 = lane == route[:, R_EID:R_EID + 1]
    hit2 = lane == route[:, R_EID + 1:R_EID + 2]
    chosen = jnp.where(hit1, 1.0, jnp.where(hit2, 1.0, 0.0))
    before = _dot(ltri, chosen.astype(BF16)) + count_ref[...]
    r1 = jnp.sum(jnp.where(hit1, before, 0.0), axis=-1, keepdims=True)
    r2 = jnp.sum(jnp.where(hit2, before, 0.0), axis=-1, keepdims=True)
    count_ref[...] = count_ref[...] + jnp.sum(chosen, axis=0, keepdims=True)
    route = jnp.where(lane_i == R_RANK, r1, route)
    return jnp.where(lane_i == R_RANK + 1, r2, route)


def _merge_rows(alpha, x, bconv, y_attn, wg_ref, wbc_ref, wo_ref, lg_ref, lb_ref, wr_ref, br_ref):
    g = _dot(x.astype(BF16), wg_ref[...])
    y_conv = _dot(bconv, wbc_ref[...])
    m = jax.nn.sigmoid(g[:, 0:D_MODEL]) * y_conv + jax.nn.sigmoid(g[:, D_MODEL:]) * y_attn
    mix = _dot(m.astype(BF16), wo_ref[...])
    h1 = _layernorm(alpha * x + mix, lg_ref[...], lb_ref[...])
    return h1, _route(_dot(h1.astype(BF16), wr_ref[...]) + br_ref[...])


def _merge_sample_body(alpha, x_ref, bconv_ref, olat_ref, wbd_ref, wba_ref, *refs):
    w_refs, (h1_ref, route_ref) = refs[:-2], refs[-2:]
    o = _dot(olat_ref[...].astype(BF16), wbd_ref[...]).astype(BF16)
    h1_ref[...], route_ref[...] = _merge_rows(alpha, x_ref[...], bconv_ref[...], _dot(o, wba_ref[...]), *w_refs)


def _merge_prompt_body(alpha, n_tiles, x_ref, bconv_ref, ot_ref, h1s_ref, routes_ref, wba_ref, *refs):
    w_refs, (h1t_ref, route_ref, counts_ref, ltri_ref, count_ref) = refs[:-5], refs[-5:]
    i = pl.program_id(0)
    tm = x_ref.shape[0]

    @pl.when(i == 0)
    def _():
        r = lax.broadcasted_iota(jnp.int32, (tm, tm), 0)
        c = lax.broadcasted_iota(jnp.int32, (tm, tm), 1)
        ltri_ref[...] = jnp.where(c < r, 1.0, 0.0).astype(BF16)
        count_ref[...] = jnp.zeros(count_ref.shape, F32)

    @pl.when(i < n_tiles)
    def _():
        y_attn = _dot_tn(ot_ref[...], wba_ref[...])
        h1, route = _merge_rows(alpha, x_ref[...], bconv_ref[...], y_attn, *w_refs)
        for c in range(ROW_TILE):
            _tile_rows(h1t_ref, c, tm)[...] = h1[:, c * LANES:(c + 1) * LANES]
        route_ref[...] = _add_ranks(route, ltri_ref[...], count_ref)

    @pl.when(i == n_tiles)
    def _():
        n_s = h1s_ref.shape[0]
        h1t_ref[...] = jnp.zeros(h1t_ref.shape, F32)
        route_ref[...] = jnp.zeros(route_ref.shape, F32)
        for c in range(ROW_TILE):
            _tile_rows(h1t_ref, c, n_s)[...] = h1s_ref[:, c * LANES:(c + 1) * LANES]
        route_ref[0:n_s, :] = _add_ranks(routes_ref[...], ltri_ref[0:n_s, 0:n_s], count_ref)

    counts_ref[...] = jnp.broadcast_to(count_ref[...], counts_ref.shape)


def _merge_sample(alpha, x, bconv, olat, wbd, wba, ws):
    n = x.shape[0]
    args = (x, bconv, olat, wbd, wba) + tuple(ws)
    out_shape = [jax.ShapeDtypeStruct((n, D_MODEL), F32), jax.ShapeDtypeStruct((n, LANES), F32)]
    return pl.pallas_call(
        functools.partial(_merge_sample_body, alpha),
        grid=(1,),
        in_specs=[_full(a.shape) for a in args],
        out_specs=[_full(s.shape) for s in out_shape],
        out_shape=out_shape,
        compiler_params=_cparams("arbitrary"),
    )(*args)


def _merge_prompt(alpha, x, bconv, ot, h1_s, route_s, wba, ws):
    t = x.shape[0]
    tm = TM
    n_tiles = t // tm
    t_all = t + h1_s.shape[0]
    assert h1_s.shape[0] <= tm
    clamp = lambda i: jnp.minimum(i, n_tiles - 1)
    rows_in = lambda w: pl.BlockSpec((tm, w), lambda i: (clamp(i), 0))
    return pl.pallas_call(
        functools.partial(_merge_prompt_body, alpha, n_tiles),
        grid=(n_tiles + 1,),
        in_specs=[rows_in(D_MODEL), rows_in(D_CONV), pl.BlockSpec((V_W, tm), lambda i: (0, clamp(i))),
                  _full(h1_s.shape), _full(route_s.shape), _full(wba.shape)] + [_full(w.shape) for w in ws],
        out_specs=[pl.BlockSpec((tm * ROW_TILE, LANES), lambda i: (i, 0)), pl.BlockSpec((tm, LANES), lambda i: (i, 0)),
                   _full((SUBLANES, LANES))],
        out_shape=[jax.ShapeDtypeStruct((t_all * ROW_TILE, LANES), F32), jax.ShapeDtypeStruct((t_all, LANES), F32),
                   jax.ShapeDtypeStruct((SUBLANES, LANES), F32)],
        scratch_shapes=[pltpu.VMEM((tm, tm), BF16), pltpu.VMEM((1, LANES), F32)],
        compiler_params=_cparams("arbitrary"),
    )(x, bconv, ot, h1_s, route_s, wba, *ws)


SLOT_GROUP = 16
LANE_SHIFT = LANES.bit_length() - 1


def _slot_assignments_body(rows_ref, dest_ref, asg_ref):
    def clear_row(j, c):
        for l in range(LANES):
            asg_ref[j, l] = 0
        return c
    lax.fori_loop(0, rows_ref[1], clear_row, 0)

    def fill_row(j, c):
        for g in range(0, LANES, SLOT_GROUP):
            ds = [dest_ref[j, g + l] for l in range(SLOT_GROUP)]
            for l, d in enumerate(ds):
                asg_ref[lax.shift_right_logical(d, LANE_SHIFT), lax.bitwise_and(d, LANES - 1)] = j * LANES + g + l
        return c
    lax.fori_loop(0, rows_ref[0], fill_row, 0)


def _slot_assignments(dest, n_blocks):
    assert MOE_BLOCK == LANES and dest.shape[0] % LANES == 0
    dest = dest.reshape(-1, LANES)
    grid_spec = pltpu.PrefetchScalarGridSpec(
        num_scalar_prefetch=2,
        grid=(1,),
        in_specs=[],
        out_specs=pl.BlockSpec(memory_space=pltpu.SMEM),
    )
    return pl.pallas_call(
        _slot_assignments_body,
        grid_spec=grid_spec,
        out_shape=jax.ShapeDtypeStruct((n_blocks, LANES), jnp.int32),
        compiler_params=_cparams("arbitrary"),
    )(jnp.array([dest.shape[0], n_blocks], jnp.int32), dest)


def _experts_body(blk_e_ref, used_ref, nvalid_ref, asg_ref, h1t_ref, wg_ref, wu_ref, wd_ref, y2_ref,
                  xbuf, ybuf, xb, wgb, wub, wdb, sem_in, sem_out):
    b = pl.program_id(0)
    slot = b % 2
    used = used_ref[0]
    blk_rows = MOE_BLOCK * ROW_TILE
    changed = jnp.logical_or(b == 0, blk_e_ref[b] != blk_e_ref[jnp.maximum(b - 1, 0)])

    def tile(i):
        return pl.ds(pl.multiple_of(i * ROW_TILE, ROW_TILE), ROW_TILE)

    def in_copy(bb, sl, r):
        tok = lax.shift_right_logical(asg_ref[bb, r], TOP_K - 1)
        return pltpu.make_async_copy(h1t_ref.at[tile(tok)], xbuf.at[sl, tile(r)], sem_in.at[sl])

    def out_copy(bb, sl, r):
        return pltpu.make_async_copy(ybuf.at[sl, tile(r)], y2_ref.at[tile(asg_ref[bb, r])], sem_out.at[sl])

    def issue_in(bb, sl):
        for r in range(MOE_BLOCK):
            in_copy(bb, sl, r).start()

    def wait_in(sl):
        pltpu.make_async_copy(h1t_ref.at[pl.ds(0, blk_rows)], xbuf.at[sl], sem_in.at[sl]).wait()

    def issue_out(bb, sl):
        n = nvalid_ref[bb]

        @pl.when(n == MOE_BLOCK)
        def _():
            for r in range(MOE_BLOCK):
                out_copy(bb, sl, r).start()

        @pl.when(n < MOE_BLOCK)
        def _():
            lax.fori_loop(0, n, lambda r, c: (out_copy(bb, sl, r).start(), c)[1], 0)

    def wait_out(bb, sl):
        n = nvalid_ref[bb]

        @pl.when(n == MOE_BLOCK)
        def _():
            pltpu.make_async_copy(ybuf.at[sl], y2_ref.at[pl.ds(0, blk_rows)], sem_out.at[sl]).wait()

        @pl.when(n < MOE_BLOCK)
        def _():
            lax.fori_loop(0, n, lambda r, c: (out_copy(bb, sl, r).wait(), c)[1], 0)

    @pl.when(jnp.logical_and(b == 0, used > 0))
    def _():
        issue_in(0, 0)

    @pl.when(jnp.logical_and(b < used, changed))
    def _():
        wgb[...] = wg_ref[0].astype(BF16)
        wub[...] = wu_ref[0].astype(BF16)
        wdb[...] = wd_ref[0].astype(BF16)

    @pl.when(b < used)
    def _():
        wait_in(slot)
        for c in range(ROW_TILE):
            xb[:, c * LANES:(c + 1) * LANES] = _tile_rows(xbuf.at[slot], c, MOE_BLOCK)[...].astype(BF16)
        issue_in(jnp.minimum(b + 1, used - 1), 1 - slot)
        x = xb[...]
        g = _dot(x, wgb[...])
        u = _dot(x, wub[...])
        h = (g * jax.nn.sigmoid(g) * u).astype(BF16)
        y = _dot(h, wdb[...])
        for c in range(ROW_TILE):
            _tile_rows(ybuf.at[slot], c, MOE_BLOCK)[...] = y[:, c * LANES:(c + 1) * LANES]
        issue_out(b, slot)

        @pl.when(b > 0)
        def _():
            wait_out(b - 1, 1 - slot)

        @pl.when(b == used - 1)
        def _():
            wait_out(b, slot)
            wait_in(1 - slot)


def _experts(blk_e, used, nvalid, asg, h1t, wg, wu, wd):
    n_blocks = blk_e.shape[0]
    blk_rows = MOE_BLOCK * ROW_TILE
    n_assign = h1t.shape[0] // ROW_TILE * TOP_K
    smem = lambda b, e, u, n, a: (e[b], 0, 0)
    grid_spec = pltpu.PrefetchScalarGridSpec(
        num_scalar_prefetch=4,
        grid=(n_blocks,),
        in_specs=[pl.BlockSpec(memory_space=pl.ANY),
                  pl.BlockSpec((1, D_MODEL, D_EXPERT), smem), pl.BlockSpec((1, D_MODEL, D_EXPERT), smem),
                  pl.BlockSpec((1, D_EXPERT, D_MODEL), smem)],
        out_specs=pl.BlockSpec(memory_space=pl.ANY),
        scratch_shapes=[pltpu.VMEM((2, blk_rows, LANES), F32), pltpu.VMEM((2, blk_rows, LANES), F32),
                        pltpu.VMEM((MOE_BLOCK, D_MODEL), BF16),
                        pltpu.VMEM((D_MODEL, D_EXPERT), BF16), pltpu.VMEM((D_MODEL, D_EXPERT), BF16),
                        pltpu.VMEM((D_EXPERT, D_MODEL), BF16),
                        pltpu.SemaphoreType.DMA((2,)), pltpu.SemaphoreType.DMA((2,))],
    )
    return pl.pallas_call(
        _experts_body,
        grid_spec=grid_spec,
        out_shape=jax.ShapeDtypeStruct((n_assign * ROW_TILE, LANES), F32),
        compiler_params=_cparams("arbitrary"),
    )(blk_e, used, nvalid, asg, h1t, wg, wu, wd)


def _combine_body(alpha, h1t_ref, y2_ref, route_ref, lg_ref, lb_ref, out_ref):
    tm = out_ref.shape[0]
    pair = TOP_K * ROW_TILE
    route = route_ref[...]
    w0 = jnp.broadcast_to(route[:, R_WT:R_WT + 1], (tm, LANES))
    w1 = jnp.broadcast_to(route[:, R_WT + 1:R_WT + 2], (tm, LANES))
    z = [alpha * _tile_rows(h1t_ref, c, tm)[...]
         + w0 * y2_ref[pl.ds(c, tm, stride=pair), :] + w1 * y2_ref[pl.ds(ROW_TILE + c, tm, stride=pair), :]
         for c in range(ROW_TILE)]
    mu = sum(jnp.sum(zc, axis=-1, keepdims=True) for zc in z) / D_MODEL
    zc = [v - mu for v in z]
    var = sum(jnp.sum(v * v, axis=-1, keepdims=True) for v in zc) / D_MODEL
    rstd = lax.rsqrt(var + LN_EPS)
    for c in range(ROW_TILE):
        sl = slice(c * LANES, (c + 1) * LANES)
        out_ref[:, sl] = zc[c] * rstd * lg_ref[:, sl] + lb_ref[:, sl]


def _combine(alpha, tok0, n_tok, tm, y2, h1t, route, lg, lb):
    assert n_tok % tm == 0 and tok0 % tm == 0
    blk0 = tok0 // tm
    return pl.pallas_call(
        functools.partial(_combine_body, alpha),
        grid=(n_tok // tm,),
        in_specs=[pl.BlockSpec((tm * ROW_TILE, LANES), lambda i: (blk0 + i, 0)),
                  pl.BlockSpec((tm * TOP_K * ROW_TILE, LANES), lambda i: (blk0 + i, 0)),
                  pl.BlockSpec((tm, LANES), lambda i: (blk0 + i, 0)),
                  _full(lg.shape), _full(lb.shape)],
        out_specs=pl.BlockSpec((tm, D_MODEL), lambda i: (i, 0)),
        out_shape=jax.ShapeDtypeStruct((n_tok, D_MODEL), F32),
        compiler_params=_cparams("arbitrary"),
    )(h1t, y2, route, lg, lb)


def _rope_tables(pos):
    inv = ROPE_BASE ** (-(jnp.arange(ROPE_HALF, dtype=F32) * 2.0 / ROPE_DIM))
    ang = pos.astype(F32)[:, None] * inv[None, :]
    cos, sin = jnp.cos(ang), jnp.sin(ang)
    n = pos.shape[0]
    a = jnp.concatenate([cos, cos, jnp.ones((n, LANES - ROPE_DIM), F32)], axis=1)
    bm = jnp.concatenate([-sin, jnp.zeros((n, LANES - ROPE_HALF), F32)], axis=1)
    cp = jnp.concatenate([jnp.zeros((n, ROPE_HALF), F32), sin, jnp.zeros((n, LANES - ROPE_DIM), F32)], axis=1)
    k_tabs = jnp.stack([a, bm, cp])
    return jnp.concatenate([k_tabs * (ATTN_SCALE * LOG2E), k_tabs])


def _head_blocks(w_rope, w_nope):
    k = w_nope.shape[0]
    pad = jnp.zeros((k, N_HEADS, HEAD_W - QK_DIM), w_nope.dtype)
    return jnp.concatenate([w_rope, w_nope, pad], axis=-1).reshape(k, QK_W)


def _pack_weights(w_in, w_uq, w_uk, w_uv, router_w_group, router_b_group, router_w_expert, router_b_expert):
    d = w_in.shape[0]
    c_kr = 3 * D_CONV + Q_LORA + KV_LORA
    w1 = jnp.concatenate([w_in[:, :c_kr], w_in[:, c_kr:c_kr + ROPE_DIM],
                          jnp.zeros((d, LANES - ROPE_DIM), w_in.dtype)], axis=1).astype(BF16)
    wg = w_in[:, c_kr + ROPE_DIM:].astype(BF16)
    uq = w_uq.reshape(Q_LORA, N_HEADS, QK_DIM)
    wuq = _head_blocks(uq[..., NOPE_DIM:], uq[..., :NOPE_DIM]).astype(BF16)
    wuk = _head_blocks(jnp.zeros((KV_LORA, N_HEADS, ROPE_DIM), w_uk.dtype), w_uk).astype(BF16)
    wukt = jnp.transpose(wuk.reshape(KV_LORA, N_HEADS, HEAD_W), (1, 2, 0))
    wuvt = w_uv.reshape(KV_LORA, V_W).T.astype(BF16)
    eye = jnp.eye(N_HEADS, dtype=w_uv.dtype)
    wbd = jnp.einsum('chd,hg->hcgd', w_uv, eye).reshape(N_HEADS * KV_LORA, V_W).astype(BF16)
    wr = jnp.zeros((d, ROUTE_W), F32)
    wr = wr.at[:, 0:N_GROUPS].set(router_w_group).at[:, LANES:LANES + N_EXPERTS].set(router_w_expert).astype(BF16)
    br = jnp.zeros((1, ROUTE_W), F32)
    br = br.at[0, 0:N_GROUPS].set(router_b_group).at[0, LANES:LANES + N_EXPERTS].set(router_b_expert)
    return w1, wg, wuq, wuk, wukt, wuvt, wbd, wr, br


def _dispatch_plan(route, counts, n_blocks):
    counts = counts[0, 0:N_EXPERTS].astype(jnp.int32)
    pcounts = (counts + MOE_BLOCK - 1) // MOE_BLOCK * MOE_BLOCK
    pend = jnp.cumsum(pcounts).astype(jnp.int32)
    pstart = (pend - pcounts).astype(F32)
    eid = route[:, R_EID:R_EID + TOP_K]
    rank = route[:, R_RANK:R_RANK + TOP_K]
    onehot = eid[:, :, None] == jnp.arange(N_EXPERTS, dtype=F32)[None, None, :]
    dest = (jnp.sum(jnp.where(onehot, pstart[None, None, :], 0.0), axis=-1) + rank).astype(jnp.int32).reshape(-1)
    blk_start = jnp.arange(n_blocks, dtype=jnp.int32) * MOE_BLOCK
    blk_e = jnp.sum((blk_start[:, None] >= pend[None, :]).astype(jnp.int32), axis=1)
    blk_e = jnp.minimum(blk_e, N_EXPERTS - 1).astype(jnp.int32)
    used = (pend[-1:] // MOE_BLOCK).astype(jnp.int32)
    filled_end = (pend - pcounts + counts)[blk_e]
    nvalid = jnp.where(blk_start < pend[-1], jnp.clip(filled_end - blk_start, 0, MOE_BLOCK), 0).astype(jnp.int32)
    return dest, blk_e, used, nvalid


def kernel(x_prompt, x_sample, cache_ckv, cache_krope, state_conv, page_table, w_in, conv_w, q_norm_g, w_uq,
           kv_norm_g, w_uk, w_uv, w_br_conv, w_br_attn, w_o, ln1_g, ln1_b, router_w_group, router_b_group,
           router_w_expert, router_b_expert, w_gate, w_up, w_down, ln2_g, ln2_b):
    depth = w_in.shape[0]
    alpha = (2 * depth) ** 0.25
    n_p, t_p, _ = x_prompt.shape
    n_s, t_s, _ = x_sample.shape
    assert t_s == 1 and t_p % TM == 0 and n_s % MOE_BLOCK == 0
    rows_p = n_p * t_p
    rows_all = rows_p + n_s
    past = page_table.shape[1] * PAGE_SIZE
    tab_p = _rope_tables(jnp.arange(t_p))
    tab_s = _rope_tables(jnp.full((n_s,), past, jnp.int32))
    n_blocks = -(-(rows_all * TOP_K) // MOE_BLOCK) + N_EXPERTS

    h_p = x_prompt.reshape(rows_p, D_MODEL)
    h_s = x_sample.reshape(n_s, D_MODEL)
    ckv_p, kr_p, cv_p, ckv_s, kr_s, cv_s = [], [], [], [], [], []
    row = lambda v: v.reshape(1, -1)
    for l in range(depth):
        w1, wg, wuq, wuk, wukt, wuvt, wbd, wr, br = _pack_weights(
            w_in[l], w_uq[l], w_uk[l], w_uv[l], router_w_group[l], router_b_group[l],
            router_w_expert[l], router_b_expert[l])
        qg, kvg = row(q_norm_g[l]), row(kv_norm_g[l])
        wba = w_br_attn[l].astype(BF16)
        merge_w = (wg, w_br_conv[l].astype(BF16), w_o[l].astype(BF16), row(ln1_g[l]), row(ln1_b[l]), wr, br)

        bconv, q, k, vt, ckv, kr, nconv = _inproj_prompt(h_p, tab_p, w1, wuq, wuk, wuvt, qg, kvg, conv_w[l], n_p, t_p)
        ot = _prompt_attention(q, k, vt, n_p, t_p)

        st = state_conv[l]
        bconv_s, q_s, qlat, ckvn_s, krn_s, u_s = _inproj_sample(
            h_s, tab_s, w1, wuq, wukt, qg, kvg, conv_w[l], st[:, 0], st[:, 1])
        olat = _sample_attention(page_table, qlat, q_s, ckvn_s, krn_s, cache_ckv[l],
                                 jnp.swapaxes(cache_krope[l], 1, 2))
        h1_s, route_s = _merge_sample(alpha, h_s, bconv_s, olat.reshape(n_s, N_HEADS * KV_LORA), wbd, wba, merge_w)
        h1t, route, counts = _merge_prompt(alpha, h_p, bconv, ot, h1_s, route_s, wba, merge_w)

        dest, blk_e, used, nvalid = _dispatch_plan(route, counts, n_blocks)
        asg = _slot_assignments(dest, n_blocks)
        y2 = _experts(blk_e, used, nvalid, asg, h1t, w_gate[l], w_up[l], w_down[l])
        ln2 = (row(ln2_g[l]), row(ln2_b[l]))
        h_p = _combine(alpha, 0, rows_p, TM, y2, h1t, route, *ln2)
        h_s = _combine(alpha, rows_p, n_s, n_s, y2, h1t, route, *ln2)

        ckv_p.append(ckv.reshape(n_p, t_p, KV_LORA))
        kr_p.append(kr.reshape(n_p, t_p, ROPE_DIM))
        cv_p.append(nconv)
        ckv_s.append(ckvn_s.reshape(n_s, 1, KV_LORA))
        kr_s.append(krn_s.reshape(n_s, 1, ROPE_DIM))
        cv_s.append(jnp.stack([st[:, 1], u_s], axis=1))
    return (h_p.reshape(n_p, t_p, D_MODEL), h_s.reshape(n_s, 1, D_MODEL), jnp.stack(ckv_p), jnp.stack(kr_p),
            jnp.stack(cv_p), jnp.stack(ckv_s), jnp.stack(kr_s), jnp.stack(cv_s))
```

```python
import functools

import jax
import jax.numpy as jnp
from jax import lax
from jax.experimental import pallas as pl
from jax.experimental.pallas import tpu as pltpu

F32 = jnp.float32
BF16 = jnp.bfloat16

D_MODEL = 1024
D_CONV = 512
CONV_W = 3
N_HEADS = 8
Q_LORA = 384
KV_LORA = 256
NOPE_DIM = 64
ROPE_DIM = 32
ROPE_HALF = ROPE_DIM // 2
V_DIM = 64
QK_DIM = NOPE_DIM + ROPE_DIM
ROPE_BASE = 10000.0
ATTN_SCALE = QK_DIM ** -0.5
LOG2E = 1.4426950408889634
Q_SCALE = ATTN_SCALE * LOG2E
N_GROUPS = 4
EXP_PER_GROUP = 8
N_EXPERTS = N_GROUPS * EXP_PER_GROUP
TOP_K = 2
D_EXPERT = 512
MOE_BLOCK = 128
PAGE_SIZE = 128
LN_EPS = 1e-5
RMS_EPS = 1e-6

LANES = 128
SUBLANES = 8
ROW_TILE = D_MODEL // LANES
assert ROW_TILE == SUBLANES
HEAD_W = LANES
QK_W = N_HEADS * HEAD_W
V_W = N_HEADS * V_DIM
_C_BG, _C_CG, _C_H = 0, D_CONV, 2 * D_CONV
_C_CQ = 3 * D_CONV
_C_CKV = _C_CQ + Q_LORA
_C_KR = _C_CKV + KV_LORA
W1_COLS = _C_KR + LANES
ROUTE_W = 2 * LANES
R_EID, R_WT, R_RANK = 0, 2, 4
NEG = -1e30
VMEM_LIMIT = 56 * 1024 * 1024

TM = 512
KEY_CHUNK = 2048
ATTN_HEADS_PER_LOOP = 4
TQ = 512


def _cparams(*sem):
    return pltpu.CompilerParams(dimension_semantics=sem, vmem_limit_bytes=VMEM_LIMIT)


def _dot(a, b):
    return jnp.dot(a, b, preferred_element_type=F32)


def _dot_nt(a, b):
    return lax.dot_general(a, b, (((1,), (1,)), ((), ())), preferred_element_type=F32)


def _dot_tn(a, b):
    return lax.dot_general(a, b, (((0,), (0,)), ((), ())), preferred_element_type=F32)


def _rms(x, g):
    return x * lax.rsqrt(jnp.mean(x * x, axis=-1, keepdims=True) + RMS_EPS) * g


def _layernorm(x, g, b):
    mu = jnp.mean(x, axis=-1, keepdims=True)
    xc = x - mu
    var = jnp.mean(xc * xc, axis=-1, keepdims=True)
    return xc * lax.rsqrt(var + LN_EPS) * g + b


def _rope(xh, a, bm, cp):
    return xh * a + pltpu.roll(xh, LANES - ROPE_HALF, 1) * bm + pltpu.roll(xh, ROPE_HALF, 1) * cp


def _full(shape):
    nd = len(shape)
    return pl.BlockSpec(shape, lambda *_: (0,) * nd)


def _tile_rows(ref, c, n):
    return ref.at[pl.ds(c, n, stride=ROW_TILE), :]


def _inproj_common(x_ref, tab_ref, w1_ref, qg_ref, kvg_ref, ckv_ref, kr_ref):
    xb = x_ref[...].astype(BF16)

    def seg(lo, hi):
        return _dot(xb, w1_ref[:, lo:hi])

    b_g = seg(_C_BG, _C_CG)
    u = seg(_C_CG, _C_H) * seg(_C_H, _C_CQ)
    cqn = _rms(seg(_C_CQ, _C_CKV), qg_ref[...]).astype(BF16)
    ckvn = _rms(seg(_C_CKV, _C_KR), kvg_ref[...])
    ckv_ref[...] = ckvn
    krr = _rope(seg(_C_KR, W1_COLS), tab_ref[3], tab_ref[4], tab_ref[5])
    kr_ref[...] = krr[:, 0:ROPE_DIM]
    return b_g, u, cqn, ckvn.astype(BF16), krr


def _inproj_prompt_body(tiles_per_seq, x_ref, tab_ref, tabt_ref, w1_ref, wuqt_ref, wuk_ref, wuvt_ref, qg_ref, kvg_ref,
                        cw_ref, bconv_ref, qt_ref, k_ref, vt_ref, ckv_ref, kr_ref, nconv_ref, uext_ref):
    tm = x_ref.shape[0]
    b_g, u, cqn, cb, krr = _inproj_common(x_ref, tab_ref, w1_ref, qg_ref, kvg_ref, ckv_ref, kr_ref)
    qft = _dot_nt(wuqt_ref[...], cqn)
    cos, sin = tabt_ref[0], tabt_ref[1]
    for h in range(N_HEADS):
        r0 = h * HEAD_W
        x1 = qft[r0:r0 + ROPE_HALF, :]
        x2 = qft[r0 + ROPE_HALF:r0 + ROPE_DIM, :]
        qt_ref[r0:r0 + ROPE_HALF, :] = (x1 * cos - x2 * sin).astype(BF16)
        qt_ref[r0 + ROPE_HALF:r0 + ROPE_DIM, :] = (x1 * sin + x2 * cos).astype(BF16)
        qt_ref[r0 + ROPE_DIM:r0 + HEAD_W, :] = (qft[r0 + ROPE_DIM:r0 + HEAD_W, :] * Q_SCALE).astype(BF16)
    kn = _dot(cb, wuk_ref[...])
    for h in range(N_HEADS):
        sl = slice(h * HEAD_W, (h + 1) * HEAD_W)
        k_ref[:, sl] = (kn[:, sl] + krr).astype(BF16)
    vt_ref[...] = _dot_nt(wuvt_ref[...], cb).astype(BF16)

    first = (pl.program_id(0) % tiles_per_seq) == 0

    @pl.when(first)
    def _():
        uext_ref[0:8, :] = jnp.zeros((8, D_CONV), F32)

    @pl.when(jnp.logical_not(first))
    def _():
        uext_ref[0:8, :] = uext_ref[tm:tm + 8, :]

    uext_ref[8:8 + tm, :] = u
    conv = cw_ref[0:1, :] * uext_ref[6:6 + tm, :] + cw_ref[1:2, :] * uext_ref[7:7 + tm, :] + cw_ref[2:3, :] * u
    bconv_ref[...] = (b_g * conv).astype(BF16)
    nconv_ref[0] = u[tm - (CONV_W - 1):tm, :]


def _inproj_sample_body(x_ref, tab_ref, w1_ref, wuq_ref, wukt_ref, qg_ref, kvg_ref, cw_ref, s0_ref, s1_ref,
                        bconv_ref, q_ref, qlat_ref, ckv_ref, kr_ref, u_ref):
    b_g, u, cqn, _, _ = _inproj_common(x_ref, tab_ref, w1_ref, qg_ref, kvg_ref, ckv_ref, kr_ref)
    qf = _dot(cqn, wuq_ref[...])
    for h in range(N_HEADS):
        sl = slice(h * HEAD_W, (h + 1) * HEAD_W)
        qh = _rope(qf[:, sl], tab_ref[0], tab_ref[1], tab_ref[2]).astype(BF16)
        q_ref[:, sl] = qh
        qlat_ref[:, h * KV_LORA:(h + 1) * KV_LORA] = _dot(qh, wukt_ref[h]).astype(BF16)
    conv = cw_ref[0:1, :] * s0_ref[...] + cw_ref[1:2, :] * s1_ref[...] + cw_ref[2:3, :] * u
    bconv_ref[...] = (b_g * conv).astype(BF16)
    u_ref[...] = u


def _inproj_prompt(x, tab, tabt, w1, wuqt, wuk, wuvt, qg, kvg, cw, n_seq, seq):
    t = x.shape[0]
    tm = TM
    tiles_per_seq = seq // tm
    rows = lambda w: pl.BlockSpec((tm, w), lambda i: (i, 0))
    cols = lambda h: pl.BlockSpec((h, tm), lambda i: (0, i))
    return pl.pallas_call(
        functools.partial(_inproj_prompt_body, tiles_per_seq),
        grid=(t // tm,),
        in_specs=[rows(D_MODEL),
                  pl.BlockSpec((6, tm, LANES), lambda i: (0, i % tiles_per_seq, 0)),
                  pl.BlockSpec((2, ROPE_HALF, tm), lambda i: (0, 0, i % tiles_per_seq)),
                  _full(w1.shape), _full(wuqt.shape), _full(wuk.shape), _full(wuvt.shape),
                  _full(qg.shape), _full(kvg.shape), _full(cw.shape)],
        out_specs=[rows(D_CONV), cols(QK_W), rows(QK_W), cols(V_W), rows(KV_LORA), rows(ROPE_DIM),
                   pl.BlockSpec((1, CONV_W - 1, D_CONV), lambda i: (i // tiles_per_seq, 0, 0))],
        out_shape=[jax.ShapeDtypeStruct((t, D_CONV), BF16), jax.ShapeDtypeStruct((QK_W, t), BF16),
                   jax.ShapeDtypeStruct((t, QK_W), BF16), jax.ShapeDtypeStruct((V_W, t), BF16),
                   jax.ShapeDtypeStruct((t, KV_LORA), F32), jax.ShapeDtypeStruct((t, ROPE_DIM), F32),
                   jax.ShapeDtypeStruct((n_seq, CONV_W - 1, D_CONV), F32)],
        scratch_shapes=[pltpu.VMEM((tm + 8, D_CONV), F32)],
        compiler_params=_cparams("arbitrary"),
    )(x, tab, tabt, w1, wuqt, wuk, wuvt, qg, kvg, cw)


def _inproj_sample(x, tab, w1, wuq, wukt, qg, kvg, cw, s0, s1):
    n = x.shape[0]
    args = (x, tab, w1, wuq, wukt, qg, kvg, cw, s0, s1)
    out_shape = [jax.ShapeDtypeStruct((n, D_CONV), BF16), jax.ShapeDtypeStruct((n, QK_W), BF16),
                 jax.ShapeDtypeStruct((n, N_HEADS * KV_LORA), BF16), jax.ShapeDtypeStruct((n, KV_LORA), F32),
                 jax.ShapeDtypeStruct((n, ROPE_DIM), F32), jax.ShapeDtypeStruct((n, D_CONV), F32)]
    return pl.pallas_call(
        _inproj_sample_body,
        grid=(1,),
        in_specs=[_full(a.shape) for a in args],
        out_specs=[_full(s.shape) for s in out_shape],
        out_shape=out_shape,
        compiler_params=_cparams("arbitrary"),
    )(*args)


def _prompt_attention_body(qt_ref, k_ref, vt_ref, ot_ref, acc_ref):
    tq = qt_ref.shape[1]
    i = pl.program_id(1)

    for hg in range(N_HEADS // ATTN_HEADS_PER_LOOP):
        heads = tuple(range(hg * ATTN_HEADS_PER_LOOP, (hg + 1) * ATTN_HEADS_PER_LOOP))
        acc_ref[...] = jnp.zeros(acc_ref.shape, F32)

        def step(j, carry, masked):
            off = pl.multiple_of(j * tq, tq)
            out = []
            for idx, h in enumerate(heads):
                m, l = carry[idx]
                kblk = k_ref[pl.ds(off, tq), h * HEAD_W:(h + 1) * HEAD_W]
                s = _dot(kblk, qt_ref[h * HEAD_W:(h + 1) * HEAD_W, :])
                if masked:
                    krow = lax.broadcasted_iota(jnp.int32, (tq, tq), 0)
                    qcol = lax.broadcasted_iota(jnp.int32, (tq, tq), 1)
                    s = jnp.where(krow <= qcol, s, NEG)
                m_new = jnp.maximum(m, jnp.max(s, axis=0, keepdims=True))
                alpha = jnp.exp2(m - m_new)
                p = jnp.exp2(s - m_new)
                l_new = alpha * l + jnp.sum(p, axis=0, keepdims=True)
                vblk = vt_ref[h * V_DIM:(h + 1) * V_DIM, pl.ds(off, tq)]
                acc_ref[idx] = alpha * acc_ref[idx] + _dot(vblk, p.astype(BF16))
                out.append((m_new, l_new))
            return tuple(out)

        init = tuple((jnp.full((1, tq), NEG, F32), jnp.zeros((1, tq), F32)) for _ in heads)
        carry = lax.fori_loop(0, i, lambda j, c: step(j, c, False), init)
        carry = step(i, carry, True)
        for idx, h in enumerate(heads):
            ot_ref[h * V_DIM:(h + 1) * V_DIM, :] = (acc_ref[idx] / carry[idx][1]).astype(BF16)


def _prompt_attention(qt, k, vt, n_seq, seq):
    t = k.shape[0]
    tq = TQ
    nq = seq // tq
    return pl.pallas_call(
        _prompt_attention_body,
        grid=(n_seq, nq),
        in_specs=[pl.BlockSpec((QK_W, tq), lambda b, i: (0, b * nq + i)),
                  pl.BlockSpec((seq, QK_W), lambda b, i: (b, 0)),
                  pl.BlockSpec((V_W, seq), lambda b, i: (0, b))],
        out_specs=pl.BlockSpec((V_W, tq), lambda b, i: (0, b * nq + i)),
        out_shape=jax.ShapeDtypeStruct((V_W, t), BF16),
        scratch_shapes=[pltpu.VMEM((ATTN_HEADS_PER_LOOP, V_DIM, tq), F32)],
        compiler_params=_cparams("arbitrary", "arbitrary"),
    )(qt, k, vt)


def _sample_attention_body(n_pages, pt_ref, qlat_ref, q_ref, ckvn_ref, krn_ref, cc_ref, cr_ref, olat_ref,
                           bufc, bufr, kcb, krb, sem):
    b = pl.program_id(0)
    nb = pl.num_programs(0)
    slot = b % 2
    past = n_pages * PAGE_SIZE

    def page_copies(bb, sl, p):
        pg = pt_ref[bb, p]
        pos = pl.ds(pl.multiple_of(p * PAGE_SIZE, PAGE_SIZE), PAGE_SIZE)
        return (pltpu.make_async_copy(cc_ref.at[pg], bufc.at[sl, pos], sem.at[0, sl]),
                pltpu.make_async_copy(cr_ref.at[pg], bufr.at[sl, :, pos], sem.at[1, sl]))

    def issue(bb, sl):
        for p in range(n_pages):
            for c in page_copies(bb, sl, p):
                c.start()

    def wait(sl):
        pltpu.make_async_copy(bufc.at[sl], bufc.at[sl], sem.at[0, sl]).wait()
        pltpu.make_async_copy(bufr.at[sl], bufr.at[sl], sem.at[1, sl]).wait()

    @pl.when(b == 0)
    def _():
        issue(0, 0)

    wait(slot)
    issue(jnp.minimum(b + 1, nb - 1), 1 - slot)

    ql = qlat_ref[0]
    qr = q_ref[0][:, 0:ROPE_DIM]
    cn = ckvn_ref[0]
    rn = krn_ref[0]
    chunk = min(KEY_CHUNK, past)
    scores = []
    for c in range(past // chunk):
        pos = slice(c * chunk, (c + 1) * chunk)
        kcb[pos, :] = bufc[slot, pos, :].astype(BF16)
        krb[:, pos] = bufr[slot, :, pos].astype(BF16)
        scores.append(_dot_nt(ql, kcb[pos, :]) + _dot(qr, krb[:, pos]))
    s_new = (jnp.sum(ql.astype(F32) * cn, axis=-1, keepdims=True)
             + jnp.sum(qr.astype(F32) * rn, axis=-1, keepdims=True))
    m = s_new
    for s in scores:
        m = jnp.maximum(m, jnp.max(s, axis=-1, keepdims=True))
    p_new = jnp.exp2(s_new - m)
    l = p_new
    o = p_new * cn
    for c, s in enumerate(scores):
        p = jnp.exp2(s - m)
        l = l + jnp.sum(p, axis=-1, keepdims=True)
        o = o + _dot(p.astype(BF16), kcb[c * chunk:(c + 1) * chunk, :])
    olat_ref[0] = o / l

    @pl.when(b == nb - 1)
    def _():
        wait(1 - slot)


def _sample_attention(page_table, qlat, q, ckvn, krn, cache_c, cache_rt):
    n, n_pages = page_table.shape
    past = n_pages * PAGE_SIZE
    blk = lambda d1, d2: pl.BlockSpec((1, d1, d2), lambda b, pt: (b, 0, 0))
    grid_spec = pltpu.PrefetchScalarGridSpec(
        num_scalar_prefetch=1,
        grid=(n,),
        in_specs=[blk(N_HEADS, KV_LORA), blk(N_HEADS, HEAD_W), blk(1, KV_LORA), blk(1, ROPE_DIM),
                  pl.BlockSpec(memory_space=pl.ANY), pl.BlockSpec(memory_space=pl.ANY)],
        out_specs=blk(N_HEADS, KV_LORA),
        scratch_shapes=[pltpu.VMEM((2, past, KV_LORA), F32), pltpu.VMEM((2, ROPE_DIM, past), F32),
                        pltpu.VMEM((past, KV_LORA), BF16), pltpu.VMEM((ROPE_DIM, past), BF16),
                        pltpu.SemaphoreType.DMA((2, 2))],
    )
    return pl.pallas_call(
        functools.partial(_sample_attention_body, n_pages),
        grid_spec=grid_spec,
        out_shape=jax.ShapeDtypeStruct((n, N_HEADS, KV_LORA), F32),
        compiler_params=_cparams("arbitrary"),
    )(page_table, qlat.reshape(n, N_HEADS, KV_LORA), q.reshape(n, N_HEADS, HEAD_W),
      ckvn.reshape(n, 1, KV_LORA), krn.reshape(n, 1, ROPE_DIM), cache_c, cache_rt)


def _route(logits):
    lane_i = lax.broadcasted_iota(jnp.int32, (logits.shape[0], LANES), 1)
    lane = lane_i.astype(F32)
    first_at = lambda hit: jnp.min(jnp.where(hit, lane, float(LANES)), axis=-1, keepdims=True)
    lg = jnp.where(lane_i < N_GROUPS, logits[:, 0:LANES], NEG)
    mg = jnp.max(lg, axis=-1, keepdims=True)
    p_grp = 1.0 / jnp.sum(jnp.exp(lg - mg), axis=-1, keepdims=True)
    grp = first_at(lg == mg)
    lane_grp = (lane_i // EXP_PER_GROUP).astype(F32)
    le = jnp.where(lane_grp == grp, logits[:, LANES:2 * LANES], NEG)
    top1 = jnp.max(le, axis=-1, keepdims=True)
    i1 = first_at(le == top1)
    le2 = jnp.where(lane == i1, NEG, le)
    top2 = jnp.max(le2, axis=-1, keepdims=True)
    i2 = first_at(le2 == top2)
    e2 = jnp.exp(top2 - top1)
    w1 = p_grp / (1.0 + e2)
    w2 = p_grp * e2 / (1.0 + e2)
    out = jnp.where(lane_i == R_EID, i1, 0.0)
    out = jnp.where(lane_i == R_EID + 1, i2, out)
    out = jnp.where(lane_i == R_WT, w1, out)
    return jnp.where(lane_i == R_WT + 1, w2, out)


def _add_ranks(route, ltri, count_ref):
    n = route.shape[0]
    lane_i = lax.broadcasted_iota(jnp.int32, (n, LANES), 1)
    lane = lane_i.astype(F32)
    hit1 = lane == route[:, R_EID:R_EID + 1]
    hit2 = lane == route[:, R_EID + 1:R_EID + 2]
    chosen = jnp.where(hit1, 1.0, jnp.where(hit2, 1.0, 0.0))
    before = _dot(ltri, chosen.astype(BF16)) + count_ref[...]
    r1 = jnp.sum(jnp.where(hit1, before, 0.0), axis=-1, keepdims=True)
    r2 = jnp.sum(jnp.where(hit2, before, 0.0), axis=-1, keepdims=True)
    count_ref[...] = count_ref[...] + jnp.sum(chosen, axis=0, keepdims=True)
    route = jnp.where(lane_i == R_RANK, r1, route)
    return jnp.where(lane_i == R_RANK + 1, r2, route)


def _merge_rows(alpha, x, bconv, y_attn, wg_ref, wbc_ref, wo_ref, lg_ref, lb_ref, wr_ref, br_ref):
    g = _dot(x.astype(BF16), wg_ref[...])
    y_conv = _dot(bconv, wbc_ref[...])
    m = jax.nn.sigmoid(g[:, 0:D_MODEL]) * y_conv + jax.nn.sigmoid(g[:, D_MODEL:]) * y_attn
    mix = _dot(m.astype(BF16), wo_ref[...])
    h1 = _layernorm(alpha * x + mix, lg_ref[...], lb_ref[...])
    return h1, _route(_dot(h1.astype(BF16), wr_ref[...]) + br_ref[...])


def _merge_sample_body(alpha, x_ref, bconv_ref, olat_ref, wbd_ref, wba_ref, *refs):
    w_refs, (h1_ref, route_ref) = refs[:-2], refs[-2:]
    o = _dot(olat_ref[...].astype(BF16), wbd_ref[...]).astype(BF16)
    h1_ref[...], route_ref[...] = _merge_rows(alpha, x_ref[...], bconv_ref[...], _dot(o, wba_ref[...]), *w_refs)


def _merge_prompt_body(alpha, n_tiles, x_ref, bconv_ref, ot_ref, h1s_ref, routes_ref, wba_ref, *refs):
    w_refs, (h1t_ref, route_ref, counts_ref, ltri_ref, count_ref) = refs[:-5], refs[-5:]
    i = pl.program_id(0)
    tm = x_ref.shape[0]

    @pl.when(i == 0)
    def _():
        r = lax.broadcasted_iota(jnp.int32, (tm, tm), 0)
        c = lax.broadcasted_iota(jnp.int32, (tm, tm), 1)
        ltri_ref[...] = jnp.where(c < r, 1.0, 0.0).astype(BF16)
        count_ref[...] = jnp.zeros(count_ref.shape, F32)

    @pl.when(i < n_tiles)
    def _():
        y_attn = _dot_tn(ot_ref[...], wba_ref[...])
        h1, route = _merge_rows(alpha, x_ref[...], bconv_ref[...], y_attn, *w_refs)
        for c in range(ROW_TILE):
            _tile_rows(h1t_ref, c, tm)[...] = h1[:, c * LANES:(c + 1) * LANES]
        route_ref[...] = _add_ranks(route, ltri_ref[...], count_ref)

    @pl.when(i == n_tiles)
    def _():
        n_s = h1s_ref.shape[0]
        h1t_ref[...] = jnp.zeros(h1t_ref.shape, F32)
        route_ref[...] = jnp.zeros(route_ref.shape, F32)
        for c in range(ROW_TILE):
            _tile_rows(h1t_ref, c, n_s)[...] = h1s_ref[:, c * LANES:(c + 1) * LANES]
        route_ref[0:n_s, :] = _add_ranks(routes_ref[...], ltri_ref[0:n_s, 0:n_s], count_ref)

    counts_ref[...] = jnp.broadcast_to(count_ref[...], counts_ref.shape)


def _merge_sample(alpha, x, bconv, olat, wbd, wba, ws):
    n = x.shape[0]
    args = (x, bconv, olat, wbd, wba) + tuple(ws)
    out_shape = [jax.ShapeDtypeStruct((n, D_MODEL), F32), jax.ShapeDtypeStruct((n, LANES), F32)]
    return pl.pallas_call(
        functools.partial(_merge_sample_body, alpha),
        grid=(1,),
        in_specs=[_full(a.shape) for a in args],
        out_specs=[_full(s.shape) for s in out_shape],
        out_shape=out_shape,
        compiler_params=_cparams("arbitrary"),
    )(*args)


def _merge_prompt(alpha, x, bconv, ot, h1_s, route_s, wba, ws):
    t = x.shape[0]
    tm = TM
    n_tiles = t // tm
    t_all = t + h1_s.shape[0]
    assert h1_s.shape[0] <= tm
    clamp = lambda i: jnp.minimum(i, n_tiles - 1)
    rows_in = lambda w: pl.BlockSpec((tm, w), lambda i: (clamp(i), 0))
    return pl.pallas_call(
        functools.partial(_merge_prompt_body, alpha, n_tiles),
        grid=(n_tiles + 1,),
        in_specs=[rows_in(D_MODEL), rows_in(D_CONV), pl.BlockSpec((V_W, tm), lambda i: (0, clamp(i))),
                  _full(h1_s.shape), _full(route_s.shape), _full(wba.shape)] + [_full(w.shape) for w in ws],
        out_specs=[pl.BlockSpec((tm * ROW_TILE, LANES), lambda i: (i, 0)), pl.BlockSpec((tm, LANES), lambda i: (i, 0)),
                   _full((SUBLANES, LANES))],
        out_shape=[jax.ShapeDtypeStruct((t_all * ROW_TILE, LANES), F32), jax.ShapeDtypeStruct((t_all, LANES), F32),
                   jax.ShapeDtypeStruct((SUBLANES, LANES), F32)],
        scratch_shapes=[pltpu.VMEM((tm, tm), BF16), pltpu.VMEM((1, LANES), F32)],
        compiler_params=_cparams("arbitrary"),
    )(x, bconv, ot, h1_s, route_s, wba, *ws)


SLOT_GROUP = 16
LANE_SHIFT = LANES.bit_length() - 1


def _slot_assignments_body(rows_ref, dest_ref, asg_ref):
    def clear_row(j, c):
        for l in range(LANES):
            asg_ref[j, l] = 0
        return c
    lax.fori_loop(0, rows_ref[1], clear_row, 0)

    def fill_row(j, c):
        for g in range(0, LANES, SLOT_GROUP):
            ds = [dest_ref[j, g + l] for l in range(SLOT_GROUP)]
            for l, d in enumerate(ds):
                asg_ref[lax.shift_right_logical(d, LANE_SHIFT), lax.bitwise_and(d, LANES - 1)] = j * LANES + g + l
        return c
    lax.fori_loop(0, rows_ref[0], fill_row, 0)


def _slot_assignments(dest, n_blocks):
    assert MOE_BLOCK == LANES and dest.shape[0] % LANES == 0
    dest = dest.reshape(-1, LANES)
    grid_spec = pltpu.PrefetchScalarGridSpec(
        num_scalar_prefetch=2,
        grid=(1,),
        in_specs=[],
        out_specs=pl.BlockSpec(memory_space=pltpu.SMEM),
    )
    return pl.pallas_call(
        _slot_assignments_body,
        grid_spec=grid_spec,
        out_shape=jax.ShapeDtypeStruct((n_blocks, LANES), jnp.int32),
        compiler_params=_cparams("arbitrary"),
    )(jnp.array([dest.shape[0], n_blocks], jnp.int32), dest)


def _experts_body(blk_e_ref, used_ref, nvalid_ref, asg_ref, h1t_ref, wg_ref, wu_ref, wd_ref, y2_ref,
                  xbuf, ybuf, xb, wgb, wub, wdb, sem_in, sem_out):
    b = pl.program_id(0)
    slot = b % 2
    used = used_ref[0]
    blk_rows = MOE_BLOCK * ROW_TILE
    changed = jnp.logical_or(b == 0, blk_e_ref[b] != blk_e_ref[jnp.maximum(b - 1, 0)])

    def tile(i):
        return pl.ds(pl.multiple_of(i * ROW_TILE, ROW_TILE), ROW_TILE)

    def in_copy(bb, sl, r):
        tok = lax.shift_right_logical(asg_ref[bb, r], TOP_K - 1)
        return pltpu.make_async_copy(h1t_ref.at[tile(tok)], xbuf.at[sl, tile(r)], sem_in.at[sl])

    def out_copy(bb, sl, r):
        return pltpu.make_async_copy(ybuf.at[sl, tile(r)], y2_ref.at[tile(asg_ref[bb, r])], sem_out.at[sl])

    def issue_in(bb, sl):
        for r in range(MOE_BLOCK):
            in_copy(bb, sl, r).start(priority=r % 2)

    def wait_in(sl):
        pltpu.make_async_copy(h1t_ref.at[pl.ds(0, blk_rows)], xbuf.at[sl], sem_in.at[sl]).wait()

    def issue_out(bb, sl):
        n = nvalid_ref[bb]

        @pl.when(n == MOE_BLOCK)
        def _():
            for r in range(MOE_BLOCK):
                out_copy(bb, sl, r).start(priority=r % 2)

        @pl.when(n < MOE_BLOCK)
        def _():
            lax.fori_loop(0, n, lambda r, c: (out_copy(bb, sl, r).start(), c)[1], 0)

    def wait_out(bb, sl):
        n = nvalid_ref[bb]

        @pl.when(n == MOE_BLOCK)
        def _():
            pltpu.make_async_copy(ybuf.at[sl], y2_ref.at[pl.ds(0, blk_rows)], sem_out.at[sl]).wait()

        @pl.when(n < MOE_BLOCK)
        def _():
            lax.fori_loop(0, n, lambda r, c: (out_copy(bb, sl, r).wait(), c)[1], 0)

    @pl.when(jnp.logical_and(b == 0, used > 0))
    def _():
        issue_in(0, 0)

    @pl.when(jnp.logical_and(b < used, changed))
    def _():
        wgb[...] = wg_ref[0].astype(BF16)
        wub[...] = wu_ref[0].astype(BF16)
        wdb[...] = wd_ref[0].astype(BF16)

    @pl.when(b < used)
    def _():
        wait_in(slot)
        for c in range(ROW_TILE):
            xb[:, c * LANES:(c + 1) * LANES] = _tile_rows(xbuf.at[slot], c, MOE_BLOCK)[...].astype(BF16)
        issue_in(jnp.minimum(b + 1, used - 1), 1 - slot)
        x = xb[...]
        g = _dot(x, wgb[...])
        u = _dot(x, wub[...])
        h = (g * jax.nn.sigmoid(g) * u).astype(BF16)
        y = _dot(h, wdb[...])
        for c in range(ROW_TILE):
            _tile_rows(ybuf.at[slot], c, MOE_BLOCK)[...] = y[:, c * LANES:(c + 1) * LANES]
        issue_out(b, slot)

        @pl.when(b > 0)
        def _():
            wait_out(b - 1, 1 - slot)

        @pl.when(b == used - 1)
        def _():
            wait_out(b, slot)
            wait_in(1 - slot)


def _experts(blk_e, used, nvalid, asg, h1t, wg, wu, wd):
    n_blocks = blk_e.shape[0]
    blk_rows = MOE_BLOCK * ROW_TILE
    n_assign = h1t.shape[0] // ROW_TILE * TOP_K
    smem = lambda b, e, u, n, a: (e[b], 0, 0)
    grid_spec = pltpu.PrefetchScalarGridSpec(
        num_scalar_prefetch=4,
        grid=(n_blocks,),
        in_specs=[pl.BlockSpec(memory_space=pl.ANY),
                  pl.BlockSpec((1, D_MODEL, D_EXPERT), smem), pl.BlockSpec((1, D_MODEL, D_EXPERT), smem),
                  pl.BlockSpec((1, D_EXPERT, D_MODEL), smem)],
        out_specs=pl.BlockSpec(memory_space=pl.ANY),
        scratch_shapes=[pltpu.VMEM((2, blk_rows, LANES), F32), pltpu.VMEM((2, blk_rows, LANES), F32),
                        pltpu.VMEM((MOE_BLOCK, D_MODEL), BF16),
                        pltpu.VMEM((D_MODEL, D_EXPERT), BF16), pltpu.VMEM((D_MODEL, D_EXPERT), BF16),
                        pltpu.VMEM((D_EXPERT, D_MODEL), BF16),
                        pltpu.SemaphoreType.DMA((2,)), pltpu.SemaphoreType.DMA((2,))],
    )
    return pl.pallas_call(
        _experts_body,
        grid_spec=grid_spec,
        out_shape=jax.ShapeDtypeStruct((n_assign * ROW_TILE, LANES), F32),
        compiler_params=_cparams("arbitrary"),
    )(blk_e, used, nvalid, asg, h1t, wg, wu, wd)


def _combine_body(alpha, h1t_ref, y2_ref, route_ref, lg_ref, lb_ref, out_ref):
    tm = out_ref.shape[0]
    pair = TOP_K * ROW_TILE
    route = route_ref[...]
    w0 = jnp.broadcast_to(route[:, R_WT:R_WT + 1], (tm, LANES))
    w1 = jnp.broadcast_to(route[:, R_WT + 1:R_WT + 2], (tm, LANES))
    z = [alpha * _tile_rows(h1t_ref, c, tm)[...]
         + w0 * y2_ref[pl.ds(c, tm, stride=pair), :] + w1 * y2_ref[pl.ds(ROW_TILE + c, tm, stride=pair), :]
         for c in range(ROW_TILE)]
    mu = sum(jnp.sum(zc, axis=-1, keepdims=True) for zc in z) / D_MODEL
    zc = [v - mu for v in z]
    var = sum(jnp.sum(v * v, axis=-1, keepdims=True) for v in zc) / D_MODEL
    rstd = lax.rsqrt(var + LN_EPS)
    for c in range(ROW_TILE):
        sl = slice(c * LANES, (c + 1) * LANES)
        out_ref[:, sl] = zc[c] * rstd * lg_ref[:, sl] + lb_ref[:, sl]


def _combine(alpha, tok0, n_tok, tm, y2, h1t, route, lg, lb):
    assert n_tok % tm == 0 and tok0 % tm == 0
    blk0 = tok0 // tm
    return pl.pallas_call(
        functools.partial(_combine_body, alpha),
        grid=(n_tok // tm,),
        in_specs=[pl.BlockSpec((tm * ROW_TILE, LANES), lambda i: (blk0 + i, 0)),
                  pl.BlockSpec((tm * TOP_K * ROW_TILE, LANES), lambda i: (blk0 + i, 0)),
                  pl.BlockSpec((tm, LANES), lambda i: (blk0 + i, 0)),
                  _full(lg.shape), _full(lb.shape)],
        out_specs=pl.BlockSpec((tm, D_MODEL), lambda i: (i, 0)),
        out_shape=jax.ShapeDtypeStruct((n_tok, D_MODEL), F32),
        compiler_params=_cparams("arbitrary"),
    )(h1t, y2, route, lg, lb)


def _rope_tables(pos):
    inv = ROPE_BASE ** (-(jnp.arange(ROPE_HALF, dtype=F32) * 2.0 / ROPE_DIM))
    ang = pos.astype(F32)[:, None] * inv[None, :]
    cos, sin = jnp.cos(ang), jnp.sin(ang)
    n = pos.shape[0]
    a = jnp.concatenate([cos, cos, jnp.ones((n, LANES - ROPE_DIM), F32)], axis=1)
    bm = jnp.concatenate([-sin, jnp.zeros((n, LANES - ROPE_HALF), F32)], axis=1)
    cp = jnp.concatenate([jnp.zeros((n, ROPE_HALF), F32), sin, jnp.zeros((n, LANES - ROPE_DIM), F32)], axis=1)
    k_tabs = jnp.stack([a, bm, cp])
    return jnp.concatenate([k_tabs * Q_SCALE, k_tabs]), jnp.stack([cos.T, sin.T]) * Q_SCALE


def _head_blocks(w_rope, w_nope):
    k = w_nope.shape[0]
    pad = jnp.zeros((k, N_HEADS, HEAD_W - QK_DIM), w_nope.dtype)
    return jnp.concatenate([w_rope, w_nope, pad], axis=-1).reshape(k, QK_W)


def _pack_weights(w_in, w_uq, w_uk, w_uv, router_w_group, router_b_group, router_w_expert, router_b_expert):
    d = w_in.shape[0]
    c_kr = 3 * D_CONV + Q_LORA + KV_LORA
    w1 = jnp.concatenate([w_in[:, :c_kr], w_in[:, c_kr:c_kr + ROPE_DIM],
                          jnp.zeros((d, LANES - ROPE_DIM), w_in.dtype)], axis=1).astype(BF16)
    wg = w_in[:, c_kr + ROPE_DIM:].astype(BF16)
    uq = w_uq.reshape(Q_LORA, N_HEADS, QK_DIM)
    wuq = _head_blocks(uq[..., NOPE_DIM:], uq[..., :NOPE_DIM]).astype(BF16)
    wuk = _head_blocks(jnp.zeros((KV_LORA, N_HEADS, ROPE_DIM), w_uk.dtype), w_uk).astype(BF16)
    wukt = jnp.transpose(wuk.reshape(KV_LORA, N_HEADS, HEAD_W), (1, 2, 0))
    wuvt = w_uv.reshape(KV_LORA, V_W).T.astype(BF16)
    eye = jnp.eye(N_HEADS, dtype=w_uv.dtype)
    wbd = jnp.einsum('chd,hg->hcgd', w_uv, eye).reshape(N_HEADS * KV_LORA, V_W).astype(BF16)
    wr = jnp.zeros((d, ROUTE_W), F32)
    wr = wr.at[:, 0:N_GROUPS].set(router_w_group).at[:, LANES:LANES + N_EXPERTS].set(router_w_expert).astype(BF16)
    br = jnp.zeros((1, ROUTE_W), F32)
    br = br.at[0, 0:N_GROUPS].set(router_b_group).at[0, LANES:LANES + N_EXPERTS].set(router_b_expert)
    return w1, wg, wuq, wuq.T, wuk, wukt, wuvt, wbd, wr, br


def _dispatch_plan(route, counts, n_blocks):
    counts = counts[0, 0:N_EXPERTS].astype(jnp.int32)
    pcounts = (counts + MOE_BLOCK - 1) // MOE_BLOCK * MOE_BLOCK
    pend = jnp.cumsum(pcounts).astype(jnp.int32)
    pstart = (pend - pcounts).astype(F32)
    eid = route[:, R_EID:R_EID + TOP_K]
    rank = route[:, R_RANK:R_RANK + TOP_K]
    onehot = eid[:, :, None] == jnp.arange(N_EXPERTS, dtype=F32)[None, None, :]
    dest = (jnp.sum(jnp.where(onehot, pstart[None, None, :], 0.0), axis=-1) + rank).astype(jnp.int32).reshape(-1)
    blk_start = jnp.arange(n_blocks, dtype=jnp.int32) * MOE_BLOCK
    blk_e = jnp.sum((blk_start[:, None] >= pend[None, :]).astype(jnp.int32), axis=1)
    blk_e = jnp.minimum(blk_e, N_EXPERTS - 1).astype(jnp.int32)
    used = (pend[-1:] // MOE_BLOCK).astype(jnp.int32)
    filled_end = (pend - pcounts + counts)[blk_e]
    nvalid = jnp.where(blk_start < pend[-1], jnp.clip(filled_end - blk_start, 0, MOE_BLOCK), 0).astype(jnp.int32)
    return dest, blk_e, used, nvalid


def kernel(x_prompt, x_sample, cache_ckv, cache_krope, state_conv, page_table, w_in, conv_w, q_norm_g, w_uq,
           kv_norm_g, w_uk, w_uv, w_br_conv, w_br_attn, w_o, ln1_g, ln1_b, router_w_group, router_b_group,
           router_w_expert, router_b_expert, w_gate, w_up, w_down, ln2_g, ln2_b):
    depth = w_in.shape[0]
    alpha = (2 * depth) ** 0.25
    n_p, t_p, _ = x_prompt.shape
    n_s, t_s, _ = x_sample.shape
    assert t_s == 1 and t_p % TM == 0 and n_s % MOE_BLOCK == 0
    rows_p = n_p * t_p
    rows_all = rows_p + n_s
    past = page_table.shape[1] * PAGE_SIZE
    tab_p, tabt_p = _rope_tables(jnp.arange(t_p))
    tab_s, _ = _rope_tables(jnp.full((n_s,), past, jnp.int32))
    n_blocks = -(-(rows_all * TOP_K) // MOE_BLOCK) + N_EXPERTS

    h_p = x_prompt.reshape(rows_p, D_MODEL)
    h_s = x_sample.reshape(n_s, D_MODEL)
    ckv_p, kr_p, cv_p, ckv_s, kr_s, cv_s = [], [], [], [], [], []
    row = lambda v: v.reshape(1, -1)
    for l in range(depth):
        w1, wg, wuq, wuqt, wuk, wukt, wuvt, wbd, wr, br = _pack_weights(
            w_in[l], w_uq[l], w_uk[l], w_uv[l], router_w_group[l], router_b_group[l],
            router_w_expert[l], router_b_expert[l])
        qg, kvg = row(q_norm_g[l]), row(kv_norm_g[l])
        wba = w_br_attn[l].astype(BF16)
        merge_w = (wg, w_br_conv[l].astype(BF16), w_o[l].astype(BF16), row(ln1_g[l]), row(ln1_b[l]), wr, br)

        bconv, qt, k, vt, ckv, kr, nconv = _inproj_prompt(h_p, tab_p, tabt_p, w1, wuqt, wuk, wuvt, qg, kvg, conv_w[l],
                                                          n_p, t_p)
        ot = _prompt_attention(qt, k, vt, n_p, t_p)

        st = state_conv[l]
        bconv_s, q_s, qlat, ckvn_s, krn_s, u_s = _inproj_sample(
            h_s, tab_s, w1, wuq, wukt, qg, kvg, conv_w[l], st[:, 0], st[:, 1])
        olat = _sample_attention(page_table, qlat, q_s, ckvn_s, krn_s, cache_ckv[l],
                                 jnp.swapaxes(cache_krope[l], 1, 2))
        h1_s, route_s = _merge_sample(alpha, h_s, bconv_s, olat.reshape(n_s, N_HEADS * KV_LORA), wbd, wba, merge_w)
        h1t, route, counts = _merge_prompt(alpha, h_p, bconv, ot, h1_s, route_s, wba, merge_w)

        dest, blk_e, used, nvalid = _dispatch_plan(route, counts, n_blocks)
        asg = _slot_assignments(dest, n_blocks)
        y2 = _experts(blk_e, used, nvalid, asg, h1t, w_gate[l], w_up[l], w_down[l])
        ln2 = (row(ln2_g[l]), row(ln2_b[l]))
        h_p = _combine(alpha, 0, rows_p, TM, y2, h1t, route, *ln2)
        h_s = _combine(alpha, rows_p, n_s, n_s, y2, h1t, route, *ln2)

        ckv_p.append(ckv.reshape(n_p, t_p, KV_LORA))
        kr_p.append(kr.reshape(n_p, t_p, ROPE_DIM))
        cv_p.append(nconv)
        ckv_s.append(ckvn_s.reshape(n_s, 1, KV_LORA))
        kr_s.append(krn_s.reshape(n_s, 1, ROPE_DIM))
        cv_s.append(jnp.stack([st[:, 1], u_s], axis=1))
    return (h_p.reshape(n_p, t_p, D_MODEL), h_s.reshape(n_s, 1, D_MODEL), jnp.stack(ckv_p), jnp.stack(kr_p),
            jnp.stack(cv_p), jnp.stack(ckv_s), jnp.stack(kr_s), jnp.stack(cv_s))
```

```python
import functools

import jax
import jax.numpy as jnp
from jax import lax
from jax.experimental import pallas as pl
from jax.experimental.pallas import tpu as pltpu

F32 = jnp.float32
BF16 = jnp.bfloat16

D_MODEL = 1024
D_CONV = 512
CONV_W = 3
N_HEADS = 8
Q_LORA = 384
KV_LORA = 256
NOPE_DIM = 64
ROPE_DIM = 32
ROPE_HALF = ROPE_DIM // 2
V_DIM = 64
QK_DIM = NOPE_DIM + ROPE_DIM
ROPE_BASE = 10000.0
ATTN_SCALE = QK_DIM ** -0.5
LOG2E = 1.4426950408889634
Q_SCALE = ATTN_SCALE * LOG2E
N_GROUPS = 4
EXP_PER_GROUP = 8
N_EXPERTS = N_GROUPS * EXP_PER_GROUP
TOP_K = 2
D_EXPERT = 512
MOE_BLOCK = 128
PAGE_SIZE = 128
LN_EPS = 1e-5
RMS_EPS = 1e-6

LANES = 128
SUBLANES = 8
ROW_TILE = D_MODEL // LANES
assert ROW_TILE == SUBLANES
HEAD_W = LANES
QK_W = N_HEADS * HEAD_W
V_W = N_HEADS * V_DIM
_C_BG, _C_CG, _C_H = 0, D_CONV, 2 * D_CONV
_C_CQ = 3 * D_CONV
_C_CKV = _C_CQ + Q_LORA
_C_KR = _C_CKV + KV_LORA
W1_COLS = _C_KR + LANES
ROUTE_W = 2 * LANES
R_EID, R_WT, R_RANK = 0, 2, 4
NEG = -1e30
VMEM_LIMIT = 56 * 1024 * 1024

TM = 512
KEY_CHUNK = 2048
ATTN_HEADS_PER_LOOP = 4
TQ = 512
GATHER_RING = 3


def _cparams(*sem):
    return pltpu.CompilerParams(dimension_semantics=sem, vmem_limit_bytes=VMEM_LIMIT)


def _dot(a, b):
    return jnp.dot(a, b, preferred_element_type=F32)


def _dot_nt(a, b):
    return lax.dot_general(a, b, (((1,), (1,)), ((), ())), preferred_element_type=F32)


def _dot_tn(a, b):
    return lax.dot_general(a, b, (((0,), (0,)), ((), ())), preferred_element_type=F32)


def _rms(x, g):
    return x * lax.rsqrt(jnp.mean(x * x, axis=-1, keepdims=True) + RMS_EPS) * g


def _layernorm(x, g, b):
    mu = jnp.mean(x, axis=-1, keepdims=True)
    xc = x - mu
    var = jnp.mean(xc * xc, axis=-1, keepdims=True)
    return xc * lax.rsqrt(var + LN_EPS) * g + b


def _rope(xh, a, bm, cp):
    return xh * a + pltpu.roll(xh, LANES - ROPE_HALF, 1) * bm + pltpu.roll(xh, ROPE_HALF, 1) * cp


def _full(shape):
    nd = len(shape)
    return pl.BlockSpec(shape, lambda *_: (0,) * nd)


def _tile_rows(ref, c, n):
    return ref.at[pl.ds(c, n, stride=ROW_TILE), :]


def _inproj_common(x_ref, tab_ref, w1_ref, qg_ref, kvg_ref, ckv_ref, kr_ref):
    xb = x_ref[...].astype(BF16)

    def seg(lo, hi):
        return _dot(xb, w1_ref[:, lo:hi])

    b_g = seg(_C_BG, _C_CG)
    u = seg(_C_CG, _C_H) * seg(_C_H, _C_CQ)
    cqn = _rms(seg(_C_CQ, _C_CKV), qg_ref[...]).astype(BF16)
    ckvn = _rms(seg(_C_CKV, _C_KR), kvg_ref[...])
    ckv_ref[...] = ckvn
    krr = _rope(seg(_C_KR, W1_COLS), tab_ref[3], tab_ref[4], tab_ref[5])
    kr_ref[...] = krr[:, 0:ROPE_DIM]
    return b_g, u, cqn, ckvn.astype(BF16), krr


def _inproj_prompt_body(tiles_per_seq, x_ref, tab_ref, tabt_ref, w1_ref, wuqt_ref, wuk_ref, wuvt_ref, qg_ref, kvg_ref,
                        cw_ref, bconv_ref, qt_ref, k_ref, vt_ref, ckv_ref, kr_ref, nconv_ref, uext_ref):
    tm = x_ref.shape[0]
    b_g, u, cqn, cb, krr = _inproj_common(x_ref, tab_ref, w1_ref, qg_ref, kvg_ref, ckv_ref, kr_ref)
    qft = _dot_nt(wuqt_ref[...], cqn)
    cos, sin = tabt_ref[0], tabt_ref[1]
    for h in range(N_HEADS):
        r0 = h * HEAD_W
        x1 = qft[r0:r0 + ROPE_HALF, :]
        x2 = qft[r0 + ROPE_HALF:r0 + ROPE_DIM, :]
        qt_ref[r0:r0 + ROPE_HALF, :] = (x1 * cos - x2 * sin).astype(BF16)
        qt_ref[r0 + ROPE_HALF:r0 + ROPE_DIM, :] = (x1 * sin + x2 * cos).astype(BF16)
        qt_ref[r0 + ROPE_DIM:r0 + HEAD_W, :] = (qft[r0 + ROPE_DIM:r0 + HEAD_W, :] * Q_SCALE).astype(BF16)
    kn = _dot(cb, wuk_ref[...])
    for h in range(N_HEADS):
        sl = slice(h * HEAD_W, (h + 1) * HEAD_W)
        k_ref[:, sl] = (kn[:, sl] + krr).astype(BF16)
    vt_ref[...] = _dot_nt(wuvt_ref[...], cb).astype(BF16)

    first = (pl.program_id(0) % tiles_per_seq) == 0

    @pl.when(first)
    def _():
        uext_ref[0:8, :] = jnp.zeros((8, D_CONV), F32)

    @pl.when(jnp.logical_not(first))
    def _():
        uext_ref[0:8, :] = uext_ref[tm:tm + 8, :]

    uext_ref[8:8 + tm, :] = u
    conv = cw_ref[0:1, :] * uext_ref[6:6 + tm, :] + cw_ref[1:2, :] * uext_ref[7:7 + tm, :] + cw_ref[2:3, :] * u
    bconv_ref[...] = (b_g * conv).astype(BF16)
    nconv_ref[0] = u[tm - (CONV_W - 1):tm, :]


def _inproj_sample_body(x_ref, tab_ref, w1_ref, wuq_ref, wukt_ref, qg_ref, kvg_ref, cw_ref, s0_ref, s1_ref,
                        bconv_ref, q_ref, qlat_ref, ckv_ref, kr_ref, u_ref):
    b_g, u, cqn, _, _ = _inproj_common(x_ref, tab_ref, w1_ref, qg_ref, kvg_ref, ckv_ref, kr_ref)
    qf = _dot(cqn, wuq_ref[...])
    for h in range(N_HEADS):
        sl = slice(h * HEAD_W, (h + 1) * HEAD_W)
        qh = _rope(qf[:, sl], tab_ref[0], tab_ref[1], tab_ref[2]).astype(BF16)
        q_ref[:, sl] = qh
        qlat_ref[:, h * KV_LORA:(h + 1) * KV_LORA] = _dot(qh, wukt_ref[h]).astype(BF16)
    conv = cw_ref[0:1, :] * s0_ref[...] + cw_ref[1:2, :] * s1_ref[...] + cw_ref[2:3, :] * u
    bconv_ref[...] = (b_g * conv).astype(BF16)
    u_ref[...] = u


def _inproj_prompt(x, tab, tabt, w1, wuqt, wuk, wuvt, qg, kvg, cw, n_seq, seq):
    t = x.shape[0]
    tm = TM
    tiles_per_seq = seq // tm
    rows = lambda w: pl.BlockSpec((tm, w), lambda i: (i, 0))
    cols = lambda h: pl.BlockSpec((h, tm), lambda i: (0, i))
    return pl.pallas_call(
        functools.partial(_inproj_prompt_body, tiles_per_seq),
        grid=(t // tm,),
        in_specs=[rows(D_MODEL),
                  pl.BlockSpec((6, tm, LANES), lambda i: (0, i % tiles_per_seq, 0)),
                  pl.BlockSpec((2, ROPE_HALF, tm), lambda i: (0, 0, i % tiles_per_seq)),
                  _full(w1.shape), _full(wuqt.shape), _full(wuk.shape), _full(wuvt.shape),
                  _full(qg.shape), _full(kvg.shape), _full(cw.shape)],
        out_specs=[rows(D_CONV), cols(QK_W), rows(QK_W), cols(V_W), rows(KV_LORA), rows(ROPE_DIM),
                   pl.BlockSpec((1, CONV_W - 1, D_CONV), lambda i: (i // tiles_per_seq, 0, 0))],
        out_shape=[jax.ShapeDtypeStruct((t, D_CONV), BF16), jax.ShapeDtypeStruct((QK_W, t), BF16),
                   jax.ShapeDtypeStruct((t, QK_W), BF16), jax.ShapeDtypeStruct((V_W, t), BF16),
                   jax.ShapeDtypeStruct((t, KV_LORA), F32), jax.ShapeDtypeStruct((t, ROPE_DIM), F32),
                   jax.ShapeDtypeStruct((n_seq, CONV_W - 1, D_CONV), F32)],
        scratch_shapes=[pltpu.VMEM((tm + 8, D_CONV), F32)],
        compiler_params=_cparams("arbitrary"),
    )(x, tab, tabt, w1, wuqt, wuk, wuvt, qg, kvg, cw)


def _inproj_sample(x, tab, w1, wuq, wukt, qg, kvg, cw, s0, s1):
    n = x.shape[0]
    args = (x, tab, w1, wuq, wukt, qg, kvg, cw, s0, s1)
    out_shape = [jax.ShapeDtypeStruct((n, D_CONV), BF16), jax.ShapeDtypeStruct((n, QK_W), BF16),
                 jax.ShapeDtypeStruct((n, N_HEADS * KV_LORA), BF16), jax.ShapeDtypeStruct((n, KV_LORA), F32),
                 jax.ShapeDtypeStruct((n, ROPE_DIM), F32), jax.ShapeDtypeStruct((n, D_CONV), F32)]
    return pl.pallas_call(
        _inproj_sample_body,
        grid=(1,),
        in_specs=[_full(a.shape) for a in args],
        out_specs=[_full(s.shape) for s in out_shape],
        out_shape=out_shape,
        compiler_params=_cparams("arbitrary"),
    )(*args)


def _prompt_attention_body(qt_ref, k_ref, vt_ref, ot_ref, acc_ref, m_ref, l_ref, sa_ref, sb_ref):
    tq = qt_ref.shape[1]
    i = pl.program_id(1)

    for hg in range(N_HEADS // ATTN_HEADS_PER_LOOP):
        heads = tuple(range(hg * ATTN_HEADS_PER_LOOP, (hg + 1) * ATTN_HEADS_PER_LOOP))
        acc_ref[...] = jnp.zeros(acc_ref.shape, F32)
        m_ref[...] = jnp.full(m_ref.shape, NEG, F32)
        l_ref[...] = jnp.zeros(l_ref.shape, F32)

        def scores(j, s_ref):
            off = pl.multiple_of(j * tq, tq)
            for idx, h in enumerate(heads):
                s_ref[idx] = _dot(k_ref[pl.ds(off, tq), h * HEAD_W:(h + 1) * HEAD_W],
                                  qt_ref[h * HEAD_W:(h + 1) * HEAD_W, :])

        def consume(j, s_ref, masked):
            off = pl.multiple_of(j * tq, tq)
            for idx, h in enumerate(heads):
                s = s_ref[idx]
                if masked:
                    krow = lax.broadcasted_iota(jnp.int32, (tq, tq), 0)
                    qcol = lax.broadcasted_iota(jnp.int32, (tq, tq), 1)
                    s = jnp.where(krow <= qcol, s, NEG)
                m = m_ref[idx]
                m_new = jnp.maximum(m, jnp.max(s, axis=0, keepdims=True))
                alpha = jnp.exp2(m - m_new)
                p = jnp.exp2(s - m_new)
                m_ref[idx] = m_new
                l_ref[idx] = alpha * l_ref[idx] + jnp.sum(p, axis=0, keepdims=True)
                vblk = vt_ref[h * V_DIM:(h + 1) * V_DIM, pl.ds(off, tq)]
                acc_ref[idx] = alpha * acc_ref[idx] + _dot(vblk, p.astype(BF16))

        scores(0, sa_ref)

        def pair(pp, c):
            j = 2 * pp
            scores(j + 1, sb_ref)
            consume(j, sa_ref, False)
            scores(j + 2, sa_ref)
            consume(j + 1, sb_ref, False)
            return c
        lax.fori_loop(0, i // 2, pair, 0)

        @pl.when(i % 2 == 1)
        def _():
            scores(i, sb_ref)
            consume(i - 1, sa_ref, False)
            consume(i, sb_ref, True)

        @pl.when(i % 2 == 0)
        def _():
            consume(i, sa_ref, True)

        for idx, h in enumerate(heads):
            ot_ref[h * V_DIM:(h + 1) * V_DIM, :] = (acc_ref[idx] / l_ref[idx]).astype(BF16)


def _prompt_attention(qt, k, vt, n_seq, seq):
    t = k.shape[0]
    tq = TQ
    nq = seq // tq
    return pl.pallas_call(
        _prompt_attention_body,
        grid=(n_seq, nq),
        in_specs=[pl.BlockSpec((QK_W, tq), lambda b, i: (0, b * nq + i)),
                  pl.BlockSpec((seq, QK_W), lambda b, i: (b, 0)),
                  pl.BlockSpec((V_W, seq), lambda b, i: (0, b))],
        out_specs=pl.BlockSpec((V_W, tq), lambda b, i: (0, b * nq + i)),
        out_shape=jax.ShapeDtypeStruct((V_W, t), BF16),
        scratch_shapes=[pltpu.VMEM((ATTN_HEADS_PER_LOOP, V_DIM, tq), F32),
                        pltpu.VMEM((ATTN_HEADS_PER_LOOP, 1, tq), F32), pltpu.VMEM((ATTN_HEADS_PER_LOOP, 1, tq), F32),
                        pltpu.VMEM((ATTN_HEADS_PER_LOOP, tq, tq), F32), pltpu.VMEM((ATTN_HEADS_PER_LOOP, tq, tq), F32)],
        compiler_params=_cparams("arbitrary", "arbitrary"),
    )(qt, k, vt)


def _sample_attention_body(n_pages, pt_ref, qlat_ref, q_ref, ckvn_ref, krn_ref, cc_ref, cr_ref, olat_ref,
                           bufc, bufr, kcb, krb, sem):
    b = pl.program_id(0)
    nb = pl.num_programs(0)
    slot = b % 2
    past = n_pages * PAGE_SIZE

    def page_copies(bb, sl, p):
        pg = pt_ref[bb, p]
        pos = pl.ds(pl.multiple_of(p * PAGE_SIZE, PAGE_SIZE), PAGE_SIZE)
        return (pltpu.make_async_copy(cc_ref.at[pg], bufc.at[sl, pos], sem.at[0, sl]),
                pltpu.make_async_copy(cr_ref.at[pg], bufr.at[sl, :, pos], sem.at[1, sl]))

    def issue(bb, sl):
        for p in range(n_pages):
            for c in page_copies(bb, sl, p):
                c.start()

    def wait(sl):
        pltpu.make_async_copy(bufc.at[sl], bufc.at[sl], sem.at[0, sl]).wait()
        pltpu.make_async_copy(bufr.at[sl], bufr.at[sl], sem.at[1, sl]).wait()

    @pl.when(b == 0)
    def _():
        issue(0, 0)

    wait(slot)
    issue(jnp.minimum(b + 1, nb - 1), 1 - slot)

    ql = qlat_ref[0]
    qr = q_ref[0][:, 0:ROPE_DIM]
    cn = ckvn_ref[0]
    rn = krn_ref[0]
    chunk = min(KEY_CHUNK, past)
    scores = []
    for c in range(past // chunk):
        pos = slice(c * chunk, (c + 1) * chunk)
        kcb[pos, :] = bufc[slot, pos, :].astype(BF16)
        krb[:, pos] = bufr[slot, :, pos].astype(BF16)
        scores.append(_dot_nt(ql, kcb[pos, :]) + _dot(qr, krb[:, pos]))
    s_new = (jnp.sum(ql.astype(F32) * cn, axis=-1, keepdims=True)
             + jnp.sum(qr.astype(F32) * rn, axis=-1, keepdims=True))
    m = s_new
    for s in scores:
        m = jnp.maximum(m, jnp.max(s, axis=-1, keepdims=True))
    p_new = jnp.exp2(s_new - m)
    l = p_new
    o = p_new * cn
    for c, s in enumerate(scores):
        p = jnp.exp2(s - m)
        l = l + jnp.sum(p, axis=-1, keepdims=True)
        o = o + _dot(p.astype(BF16), kcb[c * chunk:(c + 1) * chunk, :])
    olat_ref[0] = o / l

    @pl.when(b == nb - 1)
    def _():
        wait(1 - slot)


def _sample_attention(page_table, qlat, q, ckvn, krn, cache_c, cache_rt):
    n, n_pages = page_table.shape
    past = n_pages * PAGE_SIZE
    blk = lambda d1, d2: pl.BlockSpec((1, d1, d2), lambda b, pt: (b, 0, 0))
    grid_spec = pltpu.PrefetchScalarGridSpec(
        num_scalar_prefetch=1,
        grid=(n,),
        in_specs=[blk(N_HEADS, KV_LORA), blk(N_HEADS, HEAD_W), blk(1, KV_LORA), blk(1, ROPE_DIM),
                  pl.BlockSpec(memory_space=pl.ANY), pl.BlockSpec(memory_space=pl.ANY)],
        out_specs=blk(N_HEADS, KV_LORA),
        scratch_shapes=[pltpu.VMEM((2, past, KV_LORA), F32), pltpu.VMEM((2, ROPE_DIM, past), F32),
                        pltpu.VMEM((past, KV_LORA), BF16), pltpu.VMEM((ROPE_DIM, past), BF16),
                        pltpu.SemaphoreType.DMA((2, 2))],
    )
    return pl.pallas_call(
        functools.partial(_sample_attention_body, n_pages),
        grid_spec=grid_spec,
        out_shape=jax.ShapeDtypeStruct((n, N_HEADS, KV_LORA), F32),
        compiler_params=_cparams("arbitrary"),
    )(page_table, qlat.reshape(n, N_HEADS, KV_LORA), q.reshape(n, N_HEADS, HEAD_W),
      ckvn.reshape(n, 1, KV_LORA), krn.reshape(n, 1, ROPE_DIM), cache_c, cache_rt)


def _route(logits):
    lane_i = lax.broadcasted_iota(jnp.int32, (logits.shape[0], LANES), 1)
    lane = lane_i.astype(F32)
    first_at = lambda hit: jnp.min(jnp.where(hit, lane, float(LANES)), axis=-1, keepdims=True)
    lg = jnp.where(lane_i < N_GROUPS, logits[:, 0:LANES], NEG)
    mg = jnp.max(lg, axis=-1, keepdims=True)
    p_grp = 1.0 / jnp.sum(jnp.exp(lg - mg), axis=-1, keepdims=True)
    grp = first_at(lg == mg)
    lane_grp = (lane_i // EXP_PER_GROUP).astype(F32)
    le = jnp.where(lane_grp == grp, logits[:, LANES:2 * LANES], NEG)
    top1 = jnp.max(le, axis=-1, keepdims=True)
    i1 = first_at(le == top1)
    le2 = jnp.where(lane == i1, NEG, le)
    top2 = jnp.max(le2, axis=-1, keepdims=True)
    i2 = first_at(le2 == top2)
    e2 = jnp.exp(top2 - top1)
    w1 = p_grp / (1.0 + e2)
    w2 = p_grp * e2 / (1.0 + e2)
    out = jnp.where(lane_i == R_EID, i1, 0.0)
    out = jnp.where(lane_i == R_EID + 1, i2, out)
    out = jnp.where(lane_i == R_WT, w1, out)
    return jnp.where(lane_i == R_WT + 1, w2, out)


def _add_ranks(route, ltri, count_ref):
    n = route.shape[0]
    lane_i = lax.broadcasted_iota(jnp.int32, (n, LANES), 1)
    lane = lane_i.astype(F32)
    hit1 = lane == route[:, R_EID:R_EID + 1]
    hit2 = lane == route[:, R_EID + 1:R_EID + 2]
    chosen = jnp.where(hit1, 1.0, jnp.where(hit2, 1.0, 0.0))
    before = _dot(ltri, chosen.astype(BF16)) + count_ref[...]
    r1 = jnp.sum(jnp.where(hit1, before, 0.0), axis=-1, keepdims=True)
    r2 = jnp.sum(jnp.where(hit2, before, 0.0), axis=-1, keepdims=True)
    count_ref[...] = count_ref[...] + jnp.sum(chosen, axis=0, keepdims=True)
    route = jnp.where(lane_i == R_RANK, r1, route)
    return jnp.where(lane_i == R_RANK + 1, r2, route)


def _merge_rows(alpha, x, bconv, y_attn, wg_ref, wbc_ref, wo_ref, lg_ref, lb_ref, wr_ref, br_ref):
    g = _dot(x.astype(BF16), wg_ref[...])
    y_conv = _dot(bconv, wbc_ref[...])
    m = jax.nn.sigmoid(g[:, 0:D_MODEL]) * y_conv + jax.nn.sigmoid(g[:, D_MODEL:]) * y_attn
    mix = _dot(m.astype(BF16), wo_ref[...])
    h1 = _layernorm(alpha * x + mix, lg_ref[...], lb_ref[...])
    return h1, _route(_dot(h1.astype(BF16), wr_ref[...]) + br_ref[...])


def _merge_sample_body(alpha, x_ref, bconv_ref, olat_ref, wbd_ref, wba_ref, *refs):
    w_refs, (h1_ref, route_ref) = refs[:-2], refs[-2:]
    o = _dot(olat_ref[...].astype(BF16), wbd_ref[...]).astype(BF16)
    h1_ref[...], route_ref[...] = _merge_rows(alpha, x_ref[...], bconv_ref[...], _dot(o, wba_ref[...]), *w_refs)


def _merge_prompt_body(alpha, n_tiles, x_ref, bconv_ref, ot_ref, h1s_ref, routes_ref, wba_ref, *refs):
    w_refs, (h1t_ref, route_ref, counts_ref, ltri_ref, count_ref) = refs[:-5], refs[-5:]
    i = pl.program_id(0)
    tm = x_ref.shape[0]

    @pl.when(i == 0)
    def _():
        r = lax.broadcasted_iota(jnp.int32, (tm, tm), 0)
        c = lax.broadcasted_iota(jnp.int32, (tm, tm), 1)
        ltri_ref[...] = jnp.where(c < r, 1.0, 0.0).astype(BF16)
        count_ref[...] = jnp.zeros(count_ref.shape, F32)

    @pl.when(i < n_tiles)
    def _():
        y_attn = _dot_tn(ot_ref[...], wba_ref[...])
        h1, route = _merge_rows(alpha, x_ref[...], bconv_ref[...], y_attn, *w_refs)
        for c in range(ROW_TILE):
            _tile_rows(h1t_ref, c, tm)[...] = h1[:, c * LANES:(c + 1) * LANES]
        route_ref[...] = _add_ranks(route, ltri_ref[...], count_ref)

    @pl.when(i == n_tiles)
    def _():
        n_s = h1s_ref.shape[0]
        h1t_ref[...] = jnp.zeros(h1t_ref.shape, F32)
        route_ref[...] = jnp.zeros(route_ref.shape, F32)
        for c in range(ROW_TILE):
            _tile_rows(h1t_ref, c, n_s)[...] = h1s_ref[:, c * LANES:(c + 1) * LANES]
        route_ref[0:n_s, :] = _add_ranks(routes_ref[...], ltri_ref[0:n_s, 0:n_s], count_ref)

    counts_ref[...] = jnp.broadcast_to(count_ref[...], counts_ref.shape)


def _merge_sample(alpha, x, bconv, olat, wbd, wba, ws):
    n = x.shape[0]
    args = (x, bconv, olat, wbd, wba) + tuple(ws)
    out_shape = [jax.ShapeDtypeStruct((n, D_MODEL), F32), jax.ShapeDtypeStruct((n, LANES), F32)]
    return pl.pallas_call(
        functools.partial(_merge_sample_body, alpha),
        grid=(1,),
        in_specs=[_full(a.shape) for a in args],
        out_specs=[_full(s.shape) for s in out_shape],
        out_shape=out_shape,
        compiler_params=_cparams("arbitrary"),
    )(*args)


def _merge_prompt(alpha, x, bconv, ot, h1_s, route_s, wba, ws):
    t = x.shape[0]
    tm = TM
    n_tiles = t // tm
    t_all = t + h1_s.shape[0]
    assert h1_s.shape[0] <= tm
    clamp = lambda i: jnp.minimum(i, n_tiles - 1)
    rows_in = lambda w: pl.BlockSpec((tm, w), lambda i: (clamp(i), 0))
    return pl.pallas_call(
        functools.partial(_merge_prompt_body, alpha, n_tiles),
        grid=(n_tiles + 1,),
        in_specs=[rows_in(D_MODEL), rows_in(D_CONV), pl.BlockSpec((V_W, tm), lambda i: (0, clamp(i))),
                  _full(h1_s.shape), _full(route_s.shape), _full(wba.shape)] + [_full(w.shape) for w in ws],
        out_specs=[pl.BlockSpec((tm * ROW_TILE, LANES), lambda i: (i, 0)), pl.BlockSpec((tm, LANES), lambda i: (i, 0)),
                   _full((SUBLANES, LANES))],
        out_shape=[jax.ShapeDtypeStruct((t_all * ROW_TILE, LANES), F32), jax.ShapeDtypeStruct((t_all, LANES), F32),
                   jax.ShapeDtypeStruct((SUBLANES, LANES), F32)],
        scratch_shapes=[pltpu.VMEM((tm, tm), BF16), pltpu.VMEM((1, LANES), F32)],
        compiler_params=_cparams("arbitrary"),
    )(x, bconv, ot, h1_s, route_s, wba, *ws)


SLOT_GROUP = 16


def _slot_assignments_body(rows_ref, dest_ref, asg_ref):
    def clear_row(j, c):
        for l in range(LANES):
            asg_ref[j * LANES + l] = 0
        return c
    lax.fori_loop(0, rows_ref[1], clear_row, 0)

    def fill_row(j, c):
        for g in range(0, LANES, SLOT_GROUP):
            ds = [dest_ref[j, g + l] for l in range(SLOT_GROUP)]
            for l, d in enumerate(ds):
                asg_ref[d] = j * LANES + g + l
        return c
    lax.fori_loop(0, rows_ref[0], fill_row, 0)


def _slot_assignments(dest, n_blocks):
    assert MOE_BLOCK == LANES and dest.shape[0] % LANES == 0
    dest = dest.reshape(-1, LANES)
    grid_spec = pltpu.PrefetchScalarGridSpec(
        num_scalar_prefetch=2,
        grid=(1,),
        in_specs=[],
        out_specs=pl.BlockSpec(memory_space=pltpu.SMEM),
    )
    return pl.pallas_call(
        _slot_assignments_body,
        grid_spec=grid_spec,
        out_shape=jax.ShapeDtypeStruct((n_blocks * MOE_BLOCK,), jnp.int32),
        compiler_params=_cparams("arbitrary"),
    )(jnp.array([dest.shape[0], n_blocks], jnp.int32), dest)


def _experts_body(blk_e_ref, used_ref, nvalid_ref, asg_ref, h1t_ref, wg_ref, wu_ref, wd_ref, y2_ref,
                  xbuf, ybuf, xb, wgb, wub, wdb, sem_in, sem_out):
    b = pl.program_id(0)
    slot = b % 2
    ring = lambda bb: lax.rem(bb, GATHER_RING)
    used = used_ref[0]
    blk_rows = MOE_BLOCK * ROW_TILE
    changed = jnp.logical_or(b == 0, blk_e_ref[b] != blk_e_ref[jnp.maximum(b - 1, 0)])

    def tile(i):
        return pl.ds(pl.multiple_of(i * ROW_TILE, ROW_TILE), ROW_TILE)

    def in_copy(bb, sl, r):
        tok = lax.shift_right_logical(asg_ref[bb * MOE_BLOCK + r], TOP_K - 1)
        return pltpu.make_async_copy(h1t_ref.at[tile(tok)], xbuf.at[sl, tile(r)], sem_in.at[sl])

    def out_copy(bb, sl, r):
        return pltpu.make_async_copy(ybuf.at[sl, tile(r)], y2_ref.at[tile(asg_ref[bb * MOE_BLOCK + r])],
                                     sem_out.at[sl])

    def issue_in(bb, sl):
        for r in range(MOE_BLOCK):
            in_copy(bb, sl, r).start(priority=r % 2)

    def wait_in(sl):
        pltpu.make_async_copy(h1t_ref.at[pl.ds(0, blk_rows)], xbuf.at[sl], sem_in.at[sl]).wait()

    def issue_out(bb, sl):
        n = nvalid_ref[bb]

        @pl.when(n == MOE_BLOCK)
        def _():
            for r in range(MOE_BLOCK):
                out_copy(bb, sl, r).start(priority=r % 2)

        @pl.when(n < MOE_BLOCK)
        def _():
            lax.fori_loop(0, n, lambda r, c: (out_copy(bb, sl, r).start(), c)[1], 0)

    def wait_out(bb, sl):
        n = nvalid_ref[bb]

        @pl.when(n == MOE_BLOCK)
        def _():
            pltpu.make_async_copy(ybuf.at[sl], y2_ref.at[pl.ds(0, blk_rows)], sem_out.at[sl]).wait()

        @pl.when(n < MOE_BLOCK)
        def _():
            lax.fori_loop(0, n, lambda r, c: (out_copy(bb, sl, r).wait(), c)[1], 0)

    last = used - 1

    @pl.when(jnp.logical_and(b == 0, used > 0))
    def _():
        for a in range(GATHER_RING - 1):
            issue_in(jnp.minimum(a, last), a)

    @pl.when(jnp.logical_and(b < used, changed))
    def _():
        wgb[...] = wg_ref[0].astype(BF16)
        wub[...] = wu_ref[0].astype(BF16)
        wdb[...] = wd_ref[0].astype(BF16)

    @pl.when(b < used)
    def _():
        wait_in(ring(b))
        for c in range(ROW_TILE):
            xb[:, c * LANES:(c + 1) * LANES] = _tile_rows(xbuf.at[ring(b)], c, MOE_BLOCK)[...].astype(BF16)
        ahead = b + GATHER_RING - 1
        issue_in(jnp.minimum(ahead, last), ring(ahead))
        x = xb[...]
        g = _dot(x, wgb[...])
        u = _dot(x, wub[...])
        h = (g * jax.nn.sigmoid(g) * u).astype(BF16)
        y = _dot(h, wdb[...])
        for c in range(ROW_TILE):
            _tile_rows(ybuf.at[slot], c, MOE_BLOCK)[...] = y[:, c * LANES:(c + 1) * LANES]
        issue_out(b, slot)

        @pl.when(b > 0)
        def _():
            wait_out(b - 1, 1 - slot)

        @pl.when(b == last)
        def _():
            wait_out(b, slot)
            for a in range(1, GATHER_RING):
                wait_in(ring(b + a))


def _experts(blk_e, used, nvalid, asg, h1t, wg, wu, wd):
    n_blocks = blk_e.shape[0]
    blk_rows = MOE_BLOCK * ROW_TILE
    n_assign = h1t.shape[0] // ROW_TILE * TOP_K
    smem = lambda b, e, u, n, a: (e[b], 0, 0)
    grid_spec = pltpu.PrefetchScalarGridSpec(
        num_scalar_prefetch=4,
        grid=(n_blocks,),
        in_specs=[pl.BlockSpec(memory_space=pl.ANY),
                  pl.BlockSpec((1, D_MODEL, D_EXPERT), smem), pl.BlockSpec((1, D_MODEL, D_EXPERT), smem),
                  pl.BlockSpec((1, D_EXPERT, D_MODEL), smem)],
        out_specs=pl.BlockSpec(memory_space=pl.ANY),
        scratch_shapes=[pltpu.VMEM((GATHER_RING, blk_rows, LANES), F32), pltpu.VMEM((2, blk_rows, LANES), F32),
                        pltpu.VMEM((MOE_BLOCK, D_MODEL), BF16),
                        pltpu.VMEM((D_MODEL, D_EXPERT), BF16), pltpu.VMEM((D_MODEL, D_EXPERT), BF16),
                        pltpu.VMEM((D_EXPERT, D_MODEL), BF16),
                        pltpu.SemaphoreType.DMA((GATHER_RING,)), pltpu.SemaphoreType.DMA((2,))],
    )
    return pl.pallas_call(
        _experts_body,
        grid_spec=grid_spec,
        out_shape=jax.ShapeDtypeStruct((n_assign * ROW_TILE, LANES), F32),
        compiler_params=_cparams("arbitrary"),
    )(blk_e, used, nvalid, asg, h1t, wg, wu, wd)


def _combine_body(alpha, h1t_ref, y2_ref, route_ref, lg_ref, lb_ref, out_ref):
    tm = out_ref.shape[0]
    pair = TOP_K * ROW_TILE
    route = route_ref[...]
    w0 = jnp.broadcast_to(route[:, R_WT:R_WT + 1], (tm, LANES))
    w1 = jnp.broadcast_to(route[:, R_WT + 1:R_WT + 2], (tm, LANES))
    z = [alpha * _tile_rows(h1t_ref, c, tm)[...]
         + w0 * y2_ref[pl.ds(c, tm, stride=pair), :] + w1 * y2_ref[pl.ds(ROW_TILE + c, tm, stride=pair), :]
         for c in range(ROW_TILE)]
    mu = sum(jnp.sum(zc, axis=-1, keepdims=True) for zc in z) / D_MODEL
    zc = [v - mu for v in z]
    var = sum(jnp.sum(v * v, axis=-1, keepdims=True) for v in zc) / D_MODEL
    rstd = lax.rsqrt(var + LN_EPS)
    for c in range(ROW_TILE):
        sl = slice(c * LANES, (c + 1) * LANES)
        out_ref[:, sl] = zc[c] * rstd * lg_ref[:, sl] + lb_ref[:, sl]


def _combine(alpha, tok0, n_tok, tm, y2, h1t, route, lg, lb):
    assert n_tok % tm == 0 and tok0 % tm == 0
    blk0 = tok0 // tm
    return pl.pallas_call(
        functools.partial(_combine_body, alpha),
        grid=(n_tok // tm,),
        in_specs=[pl.BlockSpec((tm * ROW_TILE, LANES), lambda i: (blk0 + i, 0)),
                  pl.BlockSpec((tm * TOP_K * ROW_TILE, LANES), lambda i: (blk0 + i, 0)),
                  pl.BlockSpec((tm, LANES), lambda i: (blk0 + i, 0)),
                  _full(lg.shape), _full(lb.shape)],
        out_specs=pl.BlockSpec((tm, D_MODEL), lambda i: (i, 0)),
        out_shape=jax.ShapeDtypeStruct((n_tok, D_MODEL), F32),
        compiler_params=_cparams("arbitrary"),
    )(h1t, y2, route, lg, lb)


def _rope_tables(pos):
    inv = ROPE_BASE ** (-(jnp.arange(ROPE_HALF, dtype=F32) * 2.0 / ROPE_DIM))
    ang = pos.astype(F32)[:, None] * inv[None, :]
    cos, sin = jnp.cos(ang), jnp.sin(ang)
    n = pos.shape[0]
    a = jnp.concatenate([cos, cos, jnp.ones((n, LANES - ROPE_DIM), F32)], axis=1)
    bm = jnp.concatenate([-sin, jnp.zeros((n, LANES - ROPE_HALF), F32)], axis=1)
    cp = jnp.concatenate([jnp.zeros((n, ROPE_HALF), F32), sin, jnp.zeros((n, LANES - ROPE_DIM), F32)], axis=1)
    k_tabs = jnp.stack([a, bm, cp])
    return jnp.concatenate([k_tabs * Q_SCALE, k_tabs]), jnp.stack([cos.T, sin.T]) * Q_SCALE


def _head_blocks(w_rope, w_nope):
    k = w_nope.shape[0]
    pad = jnp.zeros((k, N_HEADS, HEAD_W - QK_DIM), w_nope.dtype)
    return jnp.concatenate([w_rope, w_nope, pad], axis=-1).reshape(k, QK_W)


def _pack_weights(w_in, w_uq, w_uk, w_uv, router_w_group, router_b_group, router_w_expert, router_b_expert):
    d = w_in.shape[0]
    c_kr = 3 * D_CONV + Q_LORA + KV_LORA
    w1 = jnp.concatenate([w_in[:, :c_kr], w_in[:, c_kr:c_kr + ROPE_DIM],
                          jnp.zeros((d, LANES - ROPE_DIM), w_in.dtype)], axis=1).astype(BF16)
    wg = w_in[:, c_kr + ROPE_DIM:].astype(BF16)
    uq = w_uq.reshape(Q_LORA, N_HEADS, QK_DIM)
    wuq = _head_blocks(uq[..., NOPE_DIM:], uq[..., :NOPE_DIM]).astype(BF16)
    wuk = _head_blocks(jnp.zeros((KV_LORA, N_HEADS, ROPE_DIM), w_uk.dtype), w_uk).astype(BF16)
    wukt = jnp.transpose(wuk.reshape(KV_LORA, N_HEADS, HEAD_W), (1, 2, 0))
    wuvt = w_uv.reshape(KV_LORA, V_W).T.astype(BF16)
    eye = jnp.eye(N_HEADS, dtype=w_uv.dtype)
    wbd = jnp.einsum('chd,hg->hcgd', w_uv, eye).reshape(N_HEADS * KV_LORA, V_W).astype(BF16)
    wr = jnp.zeros((d, ROUTE_W), F32)
    wr = wr.at[:, 0:N_GROUPS].set(router_w_group).at[:, LANES:LANES + N_EXPERTS].set(router_w_expert).astype(BF16)
    br = jnp.zeros((1, ROUTE_W), F32)
    br = br.at[0, 0:N_GROUPS].set(router_b_group).at[0, LANES:LANES + N_EXPERTS].set(router_b_expert)
    return w1, wg, wuq, wuq.T, wuk, wukt, wuvt, wbd, wr, br


def _dispatch_plan(route, counts, n_blocks):
    counts = counts[0, 0:N_EXPERTS].astype(jnp.int32)
    pcounts = (counts + MOE_BLOCK - 1) // MOE_BLOCK * MOE_BLOCK
    pend = jnp.cumsum(pcounts).astype(jnp.int32)
    pstart = (pend - pcounts).astype(F32)
    eid = route[:, R_EID:R_EID + TOP_K]
    rank = route[:, R_RANK:R_RANK + TOP_K]
    onehot = eid[:, :, None] == jnp.arange(N_EXPERTS, dtype=F32)[None, None, :]
    dest = (jnp.sum(jnp.where(onehot, pstart[None, None, :], 0.0), axis=-1) + rank).astype(jnp.int32).reshape(-1)
    blk_start = jnp.arange(n_blocks, dtype=jnp.int32) * MOE_BLOCK
    blk_e = jnp.sum((blk_start[:, None] >= pend[None, :]).astype(jnp.int32), axis=1)
    blk_e = jnp.minimum(blk_e, N_EXPERTS - 1).astype(jnp.int32)
    used = (pend[-1:] // MOE_BLOCK).astype(jnp.int32)
    filled_end = (pend - pcounts + counts)[blk_e]
    nvalid = jnp.where(blk_start < pend[-1], jnp.clip(filled_end - blk_start, 0, MOE_BLOCK), 0).astype(jnp.int32)
    return dest, blk_e, used, nvalid


def kernel(x_prompt, x_sample, cache_ckv, cache_krope, state_conv, page_table, w_in, conv_w, q_norm_g, w_uq,
           kv_norm_g, w_uk, w_uv, w_br_conv, w_br_attn, w_o, ln1_g, ln1_b, router_w_group, router_b_group,
           router_w_expert, router_b_expert, w_gate, w_up, w_down, ln2_g, ln2_b):
    depth = w_in.shape[0]
    alpha = (2 * depth) ** 0.25
    n_p, t_p, _ = x_prompt.shape
    n_s, t_s, _ = x_sample.shape
    assert t_s == 1 and t_p % TM == 0 and n_s % MOE_BLOCK == 0
    rows_p = n_p * t_p
    rows_all = rows_p + n_s
    past = page_table.shape[1] * PAGE_SIZE
    tab_p, tabt_p = _rope_tables(jnp.arange(t_p))
    tab_s, _ = _rope_tables(jnp.full((n_s,), past, jnp.int32))
    n_blocks = -(-(rows_all * TOP_K) // MOE_BLOCK) + N_EXPERTS

    h_p = x_prompt.reshape(rows_p, D_MODEL)
    h_s = x_sample.reshape(n_s, D_MODEL)
    ckv_p, kr_p, cv_p, ckv_s, kr_s, cv_s = [], [], [], [], [], []
    row = lambda v: v.reshape(1, -1)
    for l in range(depth):
        w1, wg, wuq, wuqt, wuk, wukt, wuvt, wbd, wr, br = _pack_weights(
            w_in[l], w_uq[l], w_uk[l], w_uv[l], router_w_group[l], router_b_group[l],
            router_w_expert[l], router_b_expert[l])
        qg, kvg = row(q_norm_g[l]), row(kv_norm_g[l])
        wba = w_br_attn[l].astype(BF16)
        merge_w = (wg, w_br_conv[l].astype(BF16), w_o[l].astype(BF16), row(ln1_g[l]), row(ln1_b[l]), wr, br)

        bconv, qt, k, vt, ckv, kr, nconv = _inproj_prompt(h_p, tab_p, tabt_p, w1, wuqt, wuk, wuvt, qg, kvg, conv_w[l],
                                                          n_p, t_p)
        ot = _prompt_attention(qt, k, vt, n_p, t_p)

        st = state_conv[l]
        bconv_s, q_s, qlat, ckvn_s, krn_s, u_s = _inproj_sample(
            h_s, tab_s, w1, wuq, wukt, qg, kvg, conv_w[l], st[:, 0], st[:, 1])
        olat = _sample_attention(page_table, qlat, q_s, ckvn_s, krn_s, cache_ckv[l],
                                 jnp.swapaxes(cache_krope[l], 1, 2))
        h1_s, route_s = _merge_sample(alpha, h_s, bconv_s, olat.reshape(n_s, N_HEADS * KV_LORA), wbd, wba, merge_w)
        h1t, route, counts = _merge_prompt(alpha, h_p, bconv, ot, h1_s, route_s, wba, merge_w)

        dest, blk_e, used, nvalid = _dispatch_plan(route, counts, n_blocks)
        asg = _slot_assignments(dest, n_blocks)
        y2 = _experts(blk_e, used, nvalid, asg, h1t, w_gate[l], w_up[l], w_down[l])
        ln2 = (row(ln2_g[l]), row(ln2_b[l]))
        h_p = _combine(alpha, 0, rows_p, TM, y2, h1t, route, *ln2)
        h_s = _combine(alpha, rows_p, n_s, n_s, y2, h1t, route, *ln2)

        ckv_p.append(ckv.reshape(n_p, t_p, KV_LORA))
        kr_p.append(kr.reshape(n_p, t_p, ROPE_DIM))
        cv_p.append(nconv)
        ckv_s.append(ckvn_s.reshape(n_s, 1, KV_LORA))
        kr_s.append(krn_s.reshape(n_s, 1, ROPE_DIM))
        cv_s.append(jnp.stack([st[:, 1], u_s], axis=1))
    return (h_p.reshape(n_p, t_p, D_MODEL), h_s.reshape(n_s, 1, D_MODEL), jnp.stack(ckv_p), jnp.stack(kr_p),
            jnp.stack(cv_p), jnp.stack(ckv_s), jnp.stack(kr_s), jnp.stack(cv_s))
```

```python
import functools

import jax
import jax.numpy as jnp
from jax import lax
from jax.experimental import pallas as pl
from jax.experimental.pallas import tpu as pltpu

F32 = jnp.float32
BF16 = jnp.bfloat16

D_MODEL = 1024
D_CONV = 512
CONV_W = 3
N_HEADS = 8
Q_LORA = 384
KV_LORA = 256
NOPE_DIM = 64
ROPE_DIM = 32
ROPE_HALF = ROPE_DIM // 2
V_DIM = 64
QK_DIM = NOPE_DIM + ROPE_DIM
ROPE_BASE = 10000.0
ATTN_SCALE = QK_DIM ** -0.5
LOG2E = 1.4426950408889634
Q_SCALE = ATTN_SCALE * LOG2E
N_GROUPS = 4
EXP_PER_GROUP = 8
N_EXPERTS = N_GROUPS * EXP_PER_GROUP
TOP_K = 2
D_EXPERT = 512
MOE_BLOCK = 128
PAGE_SIZE = 128
LN_EPS = 1e-5
RMS_EPS = 1e-6

LANES = 128
SUBLANES = 8
ROW_TILE = D_MODEL // LANES
assert ROW_TILE == SUBLANES
HEAD_W = LANES
QK_W = N_HEADS * HEAD_W
V_W = N_HEADS * V_DIM
_C_BG, _C_CG, _C_H = 0, D_CONV, 2 * D_CONV
_C_CQ = 3 * D_CONV
_C_CKV = _C_CQ + Q_LORA
_C_KR = _C_CKV + KV_LORA
W1_COLS = _C_KR + LANES
ROUTE_W = 2 * LANES
R_EID, R_WT, R_RANK = 0, 2, 4
NEG = -1e30
VMEM_LIMIT = 56 * 1024 * 1024

TM = 512
KEY_CHUNK = 2048
ATTN_HEADS_PER_LOOP = 4
TQ = 512
GATHER_RING = 3


def _cparams(*sem):
    return pltpu.CompilerParams(dimension_semantics=sem, vmem_limit_bytes=VMEM_LIMIT)


def _dot(a, b):
    return jnp.dot(a, b, preferred_element_type=F32)


def _dot_nt(a, b):
    return lax.dot_general(a, b, (((1,), (1,)), ((), ())), preferred_element_type=F32)


def _dot_tn(a, b):
    return lax.dot_general(a, b, (((0,), (0,)), ((), ())), preferred_element_type=F32)


def _rms(x, g):
    return x * lax.rsqrt(jnp.mean(x * x, axis=-1, keepdims=True) + RMS_EPS) * g


def _layernorm(x, g, b):
    mu = jnp.mean(x, axis=-1, keepdims=True)
    xc = x - mu
    var = jnp.mean(xc * xc, axis=-1, keepdims=True)
    return xc * lax.rsqrt(var + LN_EPS) * g + b


def _rope(xh, a, bm, cp):
    return xh * a + pltpu.roll(xh, LANES - ROPE_HALF, 1) * bm + pltpu.roll(xh, ROPE_HALF, 1) * cp


def _full(shape):
    nd = len(shape)
    return pl.BlockSpec(shape, lambda *_: (0,) * nd)


def _tile_rows(ref, c, n):
    return ref.at[pl.ds(c, n, stride=ROW_TILE), :]


def _inproj_common(x_ref, tab_ref, w1_ref, qg_ref, kvg_ref, ckv_ref, kr_ref):
    xb = x_ref[...].astype(BF16)

    def seg(lo, hi):
        return _dot(xb, w1_ref[:, lo:hi])

    b_g = seg(_C_BG, _C_CG)
    u = seg(_C_CG, _C_H) * seg(_C_H, _C_CQ)
    cqn = _rms(seg(_C_CQ, _C_CKV), qg_ref[...]).astype(BF16)
    ckvn = _rms(seg(_C_CKV, _C_KR), kvg_ref[...])
    ckv_ref[...] = ckvn
    krr = _rope(seg(_C_KR, W1_COLS), tab_ref[3], tab_ref[4], tab_ref[5])
    kr_ref[...] = krr[:, 0:ROPE_DIM]
    return b_g, u, cqn, ckvn.astype(BF16), krr


def _inproj_prompt_body(tiles_per_seq, x_ref, tab_ref, tabt_ref, w1_ref, wuqt_ref, wuk_ref, wuvt_ref, qg_ref, kvg_ref,
                        cw_ref, bconv_ref, qt_ref, k_ref, vt_ref, ckv_ref, kr_ref, nconv_ref, uext_ref):
    tm = x_ref.shape[0]
    b_g, u, cqn, cb, krr = _inproj_common(x_ref, tab_ref, w1_ref, qg_ref, kvg_ref, ckv_ref, kr_ref)
    qft = _dot_nt(wuqt_ref[...], cqn)
    cos, sin = tabt_ref[0], tabt_ref[1]
    for h in range(N_HEADS):
        r0 = h * HEAD_W
        x1 = qft[r0:r0 + ROPE_HALF, :]
        x2 = qft[r0 + ROPE_HALF:r0 + ROPE_DIM, :]
        qt_ref[r0:r0 + ROPE_HALF, :] = (x1 * cos - x2 * sin).astype(BF16)
        qt_ref[r0 + ROPE_HALF:r0 + ROPE_DIM, :] = (x1 * sin + x2 * cos).astype(BF16)
        qt_ref[r0 + ROPE_DIM:r0 + HEAD_W, :] = (qft[r0 + ROPE_DIM:r0 + HEAD_W, :] * Q_SCALE).astype(BF16)
    kn = _dot(cb, wuk_ref[...])
    for h in range(N_HEADS):
        sl = slice(h * HEAD_W, (h + 1) * HEAD_W)
        k_ref[:, sl] = (kn[:, sl] + krr).astype(BF16)
    vt_ref[...] = _dot_nt(wuvt_ref[...], cb).astype(BF16)

    first = (pl.program_id(0) % tiles_per_seq) == 0

    @pl.when(first)
    def _():
        uext_ref[0:8, :] = jnp.zeros((8, D_CONV), F32)

    @pl.when(jnp.logical_not(first))
    def _():
        uext_ref[0:8, :] = uext_ref[tm:tm + 8, :]

    uext_ref[8:8 + tm, :] = u
    conv = cw_ref[0:1, :] * uext_ref[6:6 + tm, :] + cw_ref[1:2, :] * uext_ref[7:7 + tm, :] + cw_ref[2:3, :] * u
    bconv_ref[...] = (b_g * conv).astype(BF16)
    nconv_ref[0] = u[tm - (CONV_W - 1):tm, :]


def _inproj_sample_body(x_ref, tab_ref, w1_ref, wuq_ref, wukt_ref, qg_ref, kvg_ref, cw_ref, s0_ref, s1_ref,
                        bconv_ref, q_ref, qlat_ref, ckv_ref, kr_ref, u_ref):
    b_g, u, cqn, _, _ = _inproj_common(x_ref, tab_ref, w1_ref, qg_ref, kvg_ref, ckv_ref, kr_ref)
    qf = _dot(cqn, wuq_ref[...])
    for h in range(N_HEADS):
        sl = slice(h * HEAD_W, (h + 1) * HEAD_W)
        qh = _rope(qf[:, sl], tab_ref[0], tab_ref[1], tab_ref[2]).astype(BF16)
        q_ref[:, sl] = qh
        qlat_ref[:, h * KV_LORA:(h + 1) * KV_LORA] = _dot(qh, wukt_ref[h]).astype(BF16)
    conv = cw_ref[0:1, :] * s0_ref[...] + cw_ref[1:2, :] * s1_ref[...] + cw_ref[2:3, :] * u
    bconv_ref[...] = (b_g * conv).astype(BF16)
    u_ref[...] = u


def _inproj_prompt(x, tab, tabt, w1, wuqt, wuk, wuvt, qg, kvg, cw, n_seq, seq):
    t = x.shape[0]
    tm = TM
    tiles_per_seq = seq // tm
    rows = lambda w: pl.BlockSpec((tm, w), lambda i: (i, 0))
    cols = lambda h: pl.BlockSpec((h, tm), lambda i: (0, i))
    return pl.pallas_call(
        functools.partial(_inproj_prompt_body, tiles_per_seq),
        grid=(t // tm,),
        in_specs=[rows(D_MODEL),
                  pl.BlockSpec((6, tm, LANES), lambda i: (0, i % tiles_per_seq, 0)),
                  pl.BlockSpec((2, ROPE_HALF, tm), lambda i: (0, 0, i % tiles_per_seq)),
                  _full(w1.shape), _full(wuqt.shape), _full(wuk.shape), _full(wuvt.shape),
                  _full(qg.shape), _full(kvg.shape), _full(cw.shape)],
        out_specs=[rows(D_CONV), cols(QK_W), rows(QK_W), cols(V_W), rows(KV_LORA), rows(ROPE_DIM),
                   pl.BlockSpec((1, CONV_W - 1, D_CONV), lambda i: (i // tiles_per_seq, 0, 0))],
        out_shape=[jax.ShapeDtypeStruct((t, D_CONV), BF16), jax.ShapeDtypeStruct((QK_W, t), BF16),
                   jax.ShapeDtypeStruct((t, QK_W), BF16), jax.ShapeDtypeStruct((V_W, t), BF16),
                   jax.ShapeDtypeStruct((t, KV_LORA), F32), jax.ShapeDtypeStruct((t, ROPE_DIM), F32),
                   jax.ShapeDtypeStruct((n_seq, CONV_W - 1, D_CONV), F32)],
        scratch_shapes=[pltpu.VMEM((tm + 8, D_CONV), F32)],
        compiler_params=_cparams("arbitrary"),
    )(x, tab, tabt, w1, wuqt, wuk, wuvt, qg, kvg, cw)


def _inproj_sample(x, tab, w1, wuq, wukt, qg, kvg, cw, s0, s1):
    n = x.shape[0]
    args = (x, tab, w1, wuq, wukt, qg, kvg, cw, s0, s1)
    out_shape = [jax.ShapeDtypeStruct((n, D_CONV), BF16), jax.ShapeDtypeStruct((n, QK_W), BF16),
                 jax.ShapeDtypeStruct((n, N_HEADS * KV_LORA), BF16), jax.ShapeDtypeStruct((n, KV_LORA), F32),
                 jax.ShapeDtypeStruct((n, ROPE_DIM), F32), jax.ShapeDtypeStruct((n, D_CONV), F32)]
    return pl.pallas_call(
        _inproj_sample_body,
        grid=(1,),
        in_specs=[_full(a.shape) for a in args],
        out_specs=[_full(s.shape) for s in out_shape],
        out_shape=out_shape,
        compiler_params=_cparams("arbitrary"),
    )(*args)


def _prompt_attention_body(qt_ref, k_ref, vt_ref, ot_ref, acc_ref, m_ref, l_ref, sa_ref, sb_ref):
    tq = qt_ref.shape[1]
    i = pl.program_id(1)

    for hg in range(N_HEADS // ATTN_HEADS_PER_LOOP):
        heads = tuple(range(hg * ATTN_HEADS_PER_LOOP, (hg + 1) * ATTN_HEADS_PER_LOOP))
        acc_ref[...] = jnp.zeros(acc_ref.shape, F32)
        m_ref[...] = jnp.full(m_ref.shape, NEG, F32)
        l_ref[...] = jnp.zeros(l_ref.shape, F32)

        def scores(j, s_ref):
            off = pl.multiple_of(j * tq, tq)
            for idx, h in enumerate(heads):
                s_ref[idx] = _dot(k_ref[pl.ds(off, tq), h * HEAD_W:(h + 1) * HEAD_W],
                                  qt_ref[h * HEAD_W:(h + 1) * HEAD_W, :])

        def consume(j, s_ref, masked):
            off = pl.multiple_of(j * tq, tq)
            for idx, h in enumerate(heads):
                s = s_ref[idx]
                if masked:
                    krow = lax.broadcasted_iota(jnp.int32, (tq, tq), 0)
                    qcol = lax.broadcasted_iota(jnp.int32, (tq, tq), 1)
                    s = jnp.where(krow <= qcol, s, NEG)
                m = m_ref[idx]
                m_new = jnp.maximum(m, jnp.max(s, axis=0, keepdims=True))
                alpha = jnp.exp2(m - m_new)
                p = jnp.exp2(s - m_new)
                m_ref[idx] = m_new
                l_ref[idx] = alpha * l_ref[idx] + jnp.sum(p, axis=0, keepdims=True)
                vblk = vt_ref[h * V_DIM:(h + 1) * V_DIM, pl.ds(off, tq)]
                acc_ref[idx] = alpha * acc_ref[idx] + _dot(vblk, p.astype(BF16))

        scores(0, sa_ref)

        def pair(pp, c):
            j = 2 * pp
            scores(j + 1, sb_ref)
            consume(j, sa_ref, False)
            scores(j + 2, sa_ref)
            consume(j + 1, sb_ref, False)
            return c
        lax.fori_loop(0, i // 2, pair, 0)

        @pl.when(i % 2 == 1)
        def _():
            scores(i, sb_ref)
            consume(i - 1, sa_ref, False)
            consume(i, sb_ref, True)

        @pl.when(i % 2 == 0)
        def _():
            consume(i, sa_ref, True)

        for idx, h in enumerate(heads):
            ot_ref[h * V_DIM:(h + 1) * V_DIM, :] = (acc_ref[idx] / l_ref[idx]).astype(BF16)


def _prompt_attention(qt, k, vt, n_seq, seq):
    t = k.shape[0]
    tq = TQ
    nq = seq // tq
    return pl.pallas_call(
        _prompt_attention_body,
        grid=(n_seq, nq),
        in_specs=[pl.BlockSpec((QK_W, tq), lambda b, i: (0, b * nq + i)),
                  pl.BlockSpec((seq, QK_W), lambda b, i: (b, 0)),
                  pl.BlockSpec((V_W, seq), lambda b, i: (0, b))],
        out_specs=pl.BlockSpec((V_W, tq), lambda b, i: (0, b * nq + i)),
        out_shape=jax.ShapeDtypeStruct((V_W, t), BF16),
        scratch_shapes=[pltpu.VMEM((ATTN_HEADS_PER_LOOP, V_DIM, tq), F32),
                        pltpu.VMEM((ATTN_HEADS_PER_LOOP, 1, tq), F32), pltpu.VMEM((ATTN_HEADS_PER_LOOP, 1, tq), F32),
                        pltpu.VMEM((ATTN_HEADS_PER_LOOP, tq, tq), F32), pltpu.VMEM((ATTN_HEADS_PER_LOOP, tq, tq), F32)],
        compiler_params=_cparams("arbitrary", "arbitrary"),
    )(qt, k, vt)


def _sample_attention_body(n_pages, pt_ref, qlat_ref, q_ref, ckvn_ref, krn_ref, cc_ref, cr_ref, olat_ref,
                           bufc, bufr, kcb, krb, sem):
    b = pl.program_id(0)
    nb = pl.num_programs(0)
    slot = b % 2
    past = n_pages * PAGE_SIZE

    def page_copies(bb, sl, p):
        pg = pt_ref[bb, p]
        pos = pl.ds(pl.multiple_of(p * PAGE_SIZE, PAGE_SIZE), PAGE_SIZE)
        return (pltpu.make_async_copy(cc_ref.at[pg], bufc.at[sl, pos], sem.at[0, sl]),
                pltpu.make_async_copy(cr_ref.at[pg], bufr.at[sl, :, pos], sem.at[1, sl]))

    def issue(bb, sl):
        for p in range(n_pages):
            for c in page_copies(bb, sl, p):
                c.start()

    def wait(sl):
        pltpu.make_async_copy(bufc.at[sl], bufc.at[sl], sem.at[0, sl]).wait()
        pltpu.make_async_copy(bufr.at[sl], bufr.at[sl], sem.at[1, sl]).wait()

    @pl.when(b == 0)
    def _():
        issue(0, 0)

    wait(slot)
    issue(jnp.minimum(b + 1, nb - 1), 1 - slot)

    ql = qlat_ref[0]
    qr = q_ref[0][:, 0:ROPE_DIM]
    cn = ckvn_ref[0]
    rn = krn_ref[0]
    chunk = min(KEY_CHUNK, past)
    scores = []
    for c in range(past // chunk):
        pos = slice(c * chunk, (c + 1) * chunk)
        kcb[pos, :] = bufc[slot, pos, :].astype(BF16)
        krb[:, pos] = bufr[slot, :, pos].astype(BF16)
        scores.append(_dot_nt(ql, kcb[pos, :]) + _dot(qr, krb[:, pos]))
    s_new = (jnp.sum(ql.astype(F32) * cn, axis=-1, keepdims=True)
             + jnp.sum(qr.astype(F32) * rn, axis=-1, keepdims=True))
    m = s_new
    for s in scores:
        m = jnp.maximum(m, jnp.max(s, axis=-1, keepdims=True))
    p_new = jnp.exp2(s_new - m)
    l = p_new
    o = p_new * cn
    for c, s in enumerate(scores):
        p = jnp.exp2(s - m)
        l = l + jnp.sum(p, axis=-1, keepdims=True)
        o = o + _dot(p.astype(BF16), kcb[c * chunk:(c + 1) * chunk, :])
    olat_ref[0] = o / l

    @pl.when(b == nb - 1)
    def _():
        wait(1 - slot)


def _sample_attention(page_table, qlat, q, ckvn, krn, cache_c, cache_rt):
    n, n_pages = page_table.shape
    past = n_pages * PAGE_SIZE
    blk = lambda d1, d2: pl.BlockSpec((1, d1, d2), lambda b, pt: (b, 0, 0))
    grid_spec = pltpu.PrefetchScalarGridSpec(
        num_scalar_prefetch=1,
        grid=(n,),
        in_specs=[blk(N_HEADS, KV_LORA), blk(N_HEADS, HEAD_W), blk(1, KV_LORA), blk(1, ROPE_DIM),
                  pl.BlockSpec(memory_space=pl.ANY), pl.BlockSpec(memory_space=pl.ANY)],
        out_specs=blk(N_HEADS, KV_LORA),
        scratch_shapes=[pltpu.VMEM((2, past, KV_LORA), F32), pltpu.VMEM((2, ROPE_DIM, past), F32),
                        pltpu.VMEM((past, KV_LORA), BF16), pltpu.VMEM((ROPE_DIM, past), BF16),
                        pltpu.SemaphoreType.DMA((2, 2))],
    )
    return pl.pallas_call(
        functools.partial(_sample_attention_body, n_pages),
        grid_spec=grid_spec,
        out_shape=jax.ShapeDtypeStruct((n, N_HEADS, KV_LORA), F32),
        compiler_params=_cparams("arbitrary"),
    )(page_table, qlat.reshape(n, N_HEADS, KV_LORA), q.reshape(n, N_HEADS, HEAD_W),
      ckvn.reshape(n, 1, KV_LORA), krn.reshape(n, 1, ROPE_DIM), cache_c, cache_rt)


def _route(logits):
    lane_i = lax.broadcasted_iota(jnp.int32, (logits.shape[0], LANES), 1)
    lane = lane_i.astype(F32)
    first_at = lambda hit: jnp.min(jnp.where(hit, lane, float(LANES)), axis=-1, keepdims=True)
    lg = jnp.where(lane_i < N_GROUPS, logits[:, 0:LANES], NEG)
    mg = jnp.max(lg, axis=-1, keepdims=True)
    p_grp = 1.0 / jnp.sum(jnp.exp(lg - mg), axis=-1, keepdims=True)
    grp = first_at(lg == mg)
    lane_grp = (lane_i // EXP_PER_GROUP).astype(F32)
    le = jnp.where(lane_grp == grp, logits[:, LANES:2 * LANES], NEG)
    top1 = jnp.max(le, axis=-1, keepdims=True)
    i1 = first_at(le == top1)
    le2 = jnp.where(lane == i1, NEG, le)
    top2 = jnp.max(le2, axis=-1, keepdims=True)
    i2 = first_at(le2 == top2)
    e2 = jnp.exp(top2 - top1)
    w1 = p_grp / (1.0 + e2)
    w2 = p_grp * e2 / (1.0 + e2)
    out = jnp.where(lane_i == R_EID, i1, 0.0)
    out = jnp.where(lane_i == R_EID + 1, i2, out)
    out = jnp.where(lane_i == R_WT, w1, out)
    return jnp.where(lane_i == R_WT + 1, w2, out)


def _add_ranks(route, ltri, count_ref):
    n = route.shape[0]
    lane_i = lax.broadcasted_iota(jnp.int32, (n, LANES), 1)
    lane = lane_i.astype(F32)
    hit1 = lane == route[:, R_EID:R_EID + 1]
    hit2 = lane == route[:, R_EID + 1:R_EID + 2]
    chosen = jnp.where(hit1, 1.0, jnp.where(hit2, 1.0, 0.0))
    before = _dot(ltri, chosen.astype(BF16)) + count_ref[...]
    r1 = jnp.sum(jnp.where(hit1, before, 0.0), axis=-1, keepdims=True)
    r2 = jnp.sum(jnp.where(hit2, before, 0.0), axis=-1, keepdims=True)
    count_ref[...] = count_ref[...] + jnp.sum(chosen, axis=0, keepdims=True)
    route = jnp.where(lane_i == R_RANK, r1, route)
    return jnp.where(lane_i == R_RANK + 1, r2, route)


def _merge_rows(alpha, x, bconv, y_attn, wg_ref, wbc_ref, wo_ref, lg_ref, lb_ref, wr_ref, br_ref):
    g = _dot(x.astype(BF16), wg_ref[...])
    y_conv = _dot(bconv, wbc_ref[...])
    m = jax.nn.sigmoid(g[:, 0:D_MODEL]) * y_conv + jax.nn.sigmoid(g[:, D_MODEL:]) * y_attn
    mix = _dot(m.astype(BF16), wo_ref[...])
    h1 = _layernorm(alpha * x + mix, lg_ref[...], lb_ref[...])
    return h1, _route(_dot(h1.astype(BF16), wr_ref[...]) + br_ref[...])


def _merge_sample_body(alpha, x_ref, bconv_ref, olat_ref, wbd_ref, wba_ref, *refs):
    w_refs, (h1_ref, route_ref) = refs[:-2], refs[-2:]
    o = _dot(olat_ref[...].astype(BF16), wbd_ref[...]).astype(BF16)
    h1_ref[...], route_ref[...] = _merge_rows(alpha, x_ref[...], bconv_ref[...], _dot(o, wba_ref[...]), *w_refs)


def _merge_prompt_body(alpha, n_tiles, x_ref, bconv_ref, ot_ref, h1s_ref, routes_ref, wba_ref, *refs):
    w_refs, (h1t_ref, route_ref, counts_ref, ltri_ref, count_ref) = refs[:-5], refs[-5:]
    i = pl.program_id(0)
    tm = x_ref.shape[0]

    @pl.when(i == 0)
    def _():
        r = lax.broadcasted_iota(jnp.int32, (tm, tm), 0)
        c = lax.broadcasted_iota(jnp.int32, (tm, tm), 1)
        ltri_ref[...] = jnp.where(c < r, 1.0, 0.0).astype(BF16)
        count_ref[...] = jnp.zeros(count_ref.shape, F32)

    @pl.when(i < n_tiles)
    def _():
        y_attn = _dot_tn(ot_ref[...], wba_ref[...])
        h1, route = _merge_rows(alpha, x_ref[...], bconv_ref[...], y_attn, *w_refs)
        for c in range(ROW_TILE):
            _tile_rows(h1t_ref, c, tm)[...] = h1[:, c * LANES:(c + 1) * LANES]
        route_ref[...] = _add_ranks(route, ltri_ref[...], count_ref)

    @pl.when(i == n_tiles)
    def _():
        n_s = h1s_ref.shape[0]
        h1t_ref[...] = jnp.zeros(h1t_ref.shape, F32)
        route_ref[...] = jnp.zeros(route_ref.shape, F32)
        for c in range(ROW_TILE):
            _tile_rows(h1t_ref, c, n_s)[...] = h1s_ref[:, c * LANES:(c + 1) * LANES]
        route_ref[0:n_s, :] = _add_ranks(routes_ref[...], ltri_ref[0:n_s, 0:n_s], count_ref)

    counts_ref[...] = jnp.broadcast_to(count_ref[...], counts_ref.shape)


def _merge_sample(alpha, x, bconv, olat, wbd, wba, ws):
    n = x.shape[0]
    args = (x, bconv, olat, wbd, wba) + tuple(ws)
    out_shape = [jax.ShapeDtypeStruct((n, D_MODEL), F32), jax.ShapeDtypeStruct((n, LANES), F32)]
    return pl.pallas_call(
        functools.partial(_merge_sample_body, alpha),
        grid=(1,),
        in_specs=[_full(a.shape) for a in args],
        out_specs=[_full(s.shape) for s in out_shape],
        out_shape=out_shape,
        compiler_params=_cparams("arbitrary"),
    )(*args)


def _merge_prompt(alpha, x, bconv, ot, h1_s, route_s, wba, ws):
    t = x.shape[0]
    tm = TM
    n_tiles = t // tm
    t_all = t + h1_s.shape[0]
    assert h1_s.shape[0] <= tm
    clamp = lambda i: jnp.minimum(i, n_tiles - 1)
    rows_in = lambda w: pl.BlockSpec((tm, w), lambda i: (clamp(i), 0))
    return pl.pallas_call(
        functools.partial(_merge_prompt_body, alpha, n_tiles),
        grid=(n_tiles + 1,),
        in_specs=[rows_in(D_MODEL), rows_in(D_CONV), pl.BlockSpec((V_W, tm), lambda i: (0, clamp(i))),
                  _full(h1_s.shape), _full(route_s.shape), _full(wba.shape)] + [_full(w.shape) for w in ws],
        out_specs=[pl.BlockSpec((tm * ROW_TILE, LANES), lambda i: (i, 0)), pl.BlockSpec((tm, LANES), lambda i: (i, 0)),
                   _full((SUBLANES, LANES))],
        out_shape=[jax.ShapeDtypeStruct((t_all * ROW_TILE, LANES), F32), jax.ShapeDtypeStruct((t_all, LANES), F32),
                   jax.ShapeDtypeStruct((SUBLANES, LANES), F32)],
        scratch_shapes=[pltpu.VMEM((tm, tm), BF16), pltpu.VMEM((1, LANES), F32)],
        compiler_params=_cparams("arbitrary"),
    )(x, bconv, ot, h1_s, route_s, wba, *ws)


SLOT_GROUP = 16


def _slot_assignments_body(rows_ref, dest_ref, asg_ref):
    def clear_row(j, c):
        for l in range(LANES):
            asg_ref[j * LANES + l] = 0
        return c
    lax.fori_loop(0, rows_ref[1], clear_row, 0)

    def fill_row(j, c):
        for g in range(0, LANES, SLOT_GROUP):
            ds = [dest_ref[j, g + l] for l in range(SLOT_GROUP)]
            for l, d in enumerate(ds):
                asg_ref[d] = j * LANES + g + l
        return c
    lax.fori_loop(0, rows_ref[0], fill_row, 0)


def _slot_assignments(dest, n_blocks):
    assert MOE_BLOCK == LANES and dest.shape[0] % LANES == 0
    dest = dest.reshape(-1, LANES)
    grid_spec = pltpu.PrefetchScalarGridSpec(
        num_scalar_prefetch=2,
        grid=(1,),
        in_specs=[],
        out_specs=pl.BlockSpec(memory_space=pltpu.SMEM),
    )
    return pl.pallas_call(
        _slot_assignments_body,
        grid_spec=grid_spec,
        out_shape=jax.ShapeDtypeStruct((n_blocks * MOE_BLOCK,), jnp.int32),
        compiler_params=_cparams("arbitrary"),
    )(jnp.array([dest.shape[0], n_blocks], jnp.int32), dest)


def _experts_body(blk_e_ref, used_ref, nvalid_ref, asg_ref, h1t_ref, wg_ref, wu_ref, wd_ref, y2_ref,
                  xbuf, ybuf, xb, wgb, wub, wdb, sem_in, sem_out):
    b = pl.program_id(0)
    ring = lambda bb: lax.rem(bb, GATHER_RING)
    used = used_ref[0]
    blk_rows = MOE_BLOCK * ROW_TILE
    changed = jnp.logical_or(b == 0, blk_e_ref[b] != blk_e_ref[jnp.maximum(b - 1, 0)])

    def tile(i):
        return pl.ds(pl.multiple_of(i * ROW_TILE, ROW_TILE), ROW_TILE)

    def in_copy(bb, sl, r):
        tok = lax.shift_right_logical(asg_ref[bb * MOE_BLOCK + r], TOP_K - 1)
        return pltpu.make_async_copy(h1t_ref.at[tile(tok)], xbuf.at[sl, tile(r)], sem_in.at[sl])

    def out_copy(bb, sl, r):
        return pltpu.make_async_copy(ybuf.at[sl, tile(r)], y2_ref.at[tile(asg_ref[bb * MOE_BLOCK + r])],
                                     sem_out.at[sl])

    def issue_in(bb, sl):
        for r in range(MOE_BLOCK):
            in_copy(bb, sl, r).start(priority=r % 2)

    def wait_in(sl):
        pltpu.make_async_copy(h1t_ref.at[pl.ds(0, blk_rows)], xbuf.at[sl], sem_in.at[sl]).wait()

    def issue_out_full(bb):
        for r in range(MOE_BLOCK):
            out_copy(bb, ring(bb), r).start(priority=r % 2)

    def issue_out_partial(bb):
        lax.fori_loop(0, nvalid_ref[bb], lambda r, c: (out_copy(bb, ring(bb), r).start(), c)[1], 0)

    def wait_out(bb):
        n = nvalid_ref[bb]
        sl = ring(bb)

        @pl.when(n == MOE_BLOCK)
        def _():
            pltpu.make_async_copy(ybuf.at[sl], y2_ref.at[pl.ds(0, blk_rows)], sem_out.at[sl]).wait()

        @pl.when(n < MOE_BLOCK)
        def _():
            lax.fori_loop(0, n, lambda r, c: (out_copy(bb, sl, r).wait(), c)[1], 0)

    last = used - 1

    @pl.when(jnp.logical_and(b == 0, used > 0))
    def _():
        for a in range(GATHER_RING - 1):
            issue_in(jnp.minimum(a, last), a)

    @pl.when(jnp.logical_and(b < used, changed))
    def _():
        wgb[...] = wg_ref[0].astype(BF16)
        wub[...] = wu_ref[0].astype(BF16)
        wdb[...] = wd_ref[0].astype(BF16)

    def block(scatter_prev):
        wait_in(ring(b))
        for c in range(ROW_TILE):
            xb[:, c * LANES:(c + 1) * LANES] = _tile_rows(xbuf.at[ring(b)], c, MOE_BLOCK)[...].astype(BF16)
        ahead = b + GATHER_RING - 1
        issue_in(jnp.minimum(ahead, last), ring(ahead))
        if scatter_prev:
            issue_out_full(b - 1)
        x = xb[...]
        g = _dot(x, wgb[...])
        u = _dot(x, wub[...])
        h = (g * jax.nn.sigmoid(g) * u).astype(BF16)
        y = _dot(h, wdb[...])
        for c in range(ROW_TILE):
            _tile_rows(ybuf.at[ring(b)], c, MOE_BLOCK)[...] = y[:, c * LANES:(c + 1) * LANES]

    prev_full = jnp.logical_and(b > 0, nvalid_ref[jnp.maximum(b - 1, 0)] == MOE_BLOCK)

    @pl.when(jnp.logical_and(b < used, prev_full))
    def _():
        block(True)

    @pl.when(jnp.logical_and(b < used, jnp.logical_not(prev_full)))
    def _():
        block(False)

        @pl.when(b > 0)
        def _():
            issue_out_partial(b - 1)

    @pl.when(jnp.logical_and(b < used, b >= GATHER_RING - 1))
    def _():
        wait_out(b - (GATHER_RING - 1))

    @pl.when(b == last)
    def _():
        @pl.when(nvalid_ref[b] == MOE_BLOCK)
        def _():
            issue_out_full(b)

        @pl.when(nvalid_ref[b] < MOE_BLOCK)
        def _():
            issue_out_partial(b)
        for a in range(GATHER_RING - 2, -1, -1):
            @pl.when(b - a >= 0)
            def _():
                wait_out(b - a)
        for a in range(1, GATHER_RING):
            wait_in(ring(b + a))


def _experts(blk_e, used, nvalid, asg, h1t, wg, wu, wd):
    n_blocks = blk_e.shape[0]
    blk_rows = MOE_BLOCK * ROW_TILE
    n_assign = h1t.shape[0] // ROW_TILE * TOP_K
    smem = lambda b, e, u, n, a: (e[b], 0, 0)
    grid_spec = pltpu.PrefetchScalarGridSpec(
        num_scalar_prefetch=4,
        grid=(n_blocks,),
        in_specs=[pl.BlockSpec(memory_space=pl.ANY),
                  pl.BlockSpec((1, D_MODEL, D_EXPERT), smem), pl.BlockSpec((1, D_MODEL, D_EXPERT), smem),
                  pl.BlockSpec((1, D_EXPERT, D_MODEL), smem)],
        out_specs=pl.BlockSpec(memory_space=pl.ANY),
        scratch_shapes=[pltpu.VMEM((GATHER_RING, blk_rows, LANES), F32),
                        pltpu.VMEM((GATHER_RING, blk_rows, LANES), F32),
                        pltpu.VMEM((MOE_BLOCK, D_MODEL), BF16),
                        pltpu.VMEM((D_MODEL, D_EXPERT), BF16), pltpu.VMEM((D_MODEL, D_EXPERT), BF16),
                        pltpu.VMEM((D_EXPERT, D_MODEL), BF16),
                        pltpu.SemaphoreType.DMA((GATHER_RING,)), pltpu.SemaphoreType.DMA((GATHER_RING,))],
    )
    return pl.pallas_call(
        _experts_body,
        grid_spec=grid_spec,
        out_shape=jax.ShapeDtypeStruct((n_assign * ROW_TILE, LANES), F32),
        compiler_params=_cparams("arbitrary"),
    )(blk_e, used, nvalid, asg, h1t, wg, wu, wd)


def _combine_body(alpha, h1t_ref, y2_ref, route_ref, lg_ref, lb_ref, out_ref):
    tm = out_ref.shape[0]
    pair = TOP_K * ROW_TILE
    route = route_ref[...]
    w0 = jnp.broadcast_to(route[:, R_WT:R_WT + 1], (tm, LANES))
    w1 = jnp.broadcast_to(route[:, R_WT + 1:R_WT + 2], (tm, LANES))
    z = [alpha * _tile_rows(h1t_ref, c, tm)[...]
         + w0 * y2_ref[pl.ds(c, tm, stride=pair), :] + w1 * y2_ref[pl.ds(ROW_TILE + c, tm, stride=pair), :]
         for c in range(ROW_TILE)]
    mu = sum(jnp.sum(zc, axis=-1, keepdims=True) for zc in z) / D_MODEL
    zc = [v - mu for v in z]
    var = sum(jnp.sum(v * v, axis=-1, keepdims=True) for v in zc) / D_MODEL
    rstd = lax.rsqrt(var + LN_EPS)
    for c in range(ROW_TILE):
        sl = slice(c * LANES, (c + 1) * LANES)
        out_ref[:, sl] = zc[c] * rstd * lg_ref[:, sl] + lb_ref[:, sl]


def _combine(alpha, tok0, n_tok, tm, y2, h1t, route, lg, lb):
    assert n_tok % tm == 0 and tok0 % tm == 0
    blk0 = tok0 // tm
    return pl.pallas_call(
        functools.partial(_combine_body, alpha),
        grid=(n_tok // tm,),
        in_specs=[pl.BlockSpec((tm * ROW_TILE, LANES), lambda i: (blk0 + i, 0)),
                  pl.BlockSpec((tm * TOP_K * ROW_TILE, LANES), lambda i: (blk0 + i, 0)),
                  pl.BlockSpec((tm, LANES), lambda i: (blk0 + i, 0)),
                  _full(lg.shape), _full(lb.shape)],
        out_specs=pl.BlockSpec((tm, D_MODEL), lambda i: (i, 0)),
        out_shape=jax.ShapeDtypeStruct((n_tok, D_MODEL), F32),
        compiler_params=_cparams("arbitrary"),
    )(h1t, y2, route, lg, lb)


def _rope_tables(pos):
    inv = ROPE_BASE ** (-(jnp.arange(ROPE_HALF, dtype=F32) * 2.0 / ROPE_DIM))
    ang = pos.astype(F32)[:, None] * inv[None, :]
    cos, sin = jnp.cos(ang), jnp.sin(ang)
    n = pos.shape[0]
    a = jnp.concatenate([cos, cos, jnp.ones((n, LANES - ROPE_DIM), F32)], axis=1)
    bm = jnp.concatenate([-sin, jnp.zeros((n, LANES - ROPE_HALF), F32)], axis=1)
    cp = jnp.concatenate([jnp.zeros((n, ROPE_HALF), F32), sin, jnp.zeros((n, LANES - ROPE_DIM), F32)], axis=1)
    k_tabs = jnp.stack([a, bm, cp])
    return jnp.concatenate([k_tabs * Q_SCALE, k_tabs]), jnp.stack([cos.T, sin.T]) * Q_SCALE


def _head_blocks(w_rope, w_nope):
    k = w_nope.shape[0]
    pad = jnp.zeros((k, N_HEADS, HEAD_W - QK_DIM), w_nope.dtype)
    return jnp.concatenate([w_rope, w_nope, pad], axis=-1).reshape(k, QK_W)


def _pack_weights(w_in, w_uq, w_uk, w_uv, router_w_group, router_b_group, router_w_expert, router_b_expert):
    d = w_in.shape[0]
    c_kr = 3 * D_CONV + Q_LORA + KV_LORA
    w1 = jnp.concatenate([w_in[:, :c_kr], w_in[:, c_kr:c_kr + ROPE_DIM],
                          jnp.zeros((d, LANES - ROPE_DIM), w_in.dtype)], axis=1).astype(BF16)
    wg = w_in[:, c_kr + ROPE_DIM:].astype(BF16)
    uq = w_uq.reshape(Q_LORA, N_HEADS, QK_DIM)
    wuq = _head_blocks(uq[..., NOPE_DIM:], uq[..., :NOPE_DIM]).astype(BF16)
    wuk = _head_blocks(jnp.zeros((KV_LORA, N_HEADS, ROPE_DIM), w_uk.dtype), w_uk).astype(BF16)
    wukt = jnp.transpose(wuk.reshape(KV_LORA, N_HEADS, HEAD_W), (1, 2, 0))
    wuvt = w_uv.reshape(KV_LORA, V_W).T.astype(BF16)
    eye = jnp.eye(N_HEADS, dtype=w_uv.dtype)
    wbd = jnp.einsum('chd,hg->hcgd', w_uv, eye).reshape(N_HEADS * KV_LORA, V_W).astype(BF16)
    wr = jnp.zeros((d, ROUTE_W), F32)
    wr = wr.at[:, 0:N_GROUPS].set(router_w_group).at[:, LANES:LANES + N_EXPERTS].set(router_w_expert).astype(BF16)
    br = jnp.zeros((1, ROUTE_W), F32)
    br = br.at[0, 0:N_GROUPS].set(router_b_group).at[0, LANES:LANES + N_EXPERTS].set(router_b_expert)
    return w1, wg, wuq, wuq.T, wuk, wukt, wuvt, wbd, wr, br


def _dispatch_plan(route, counts, n_blocks):
    counts = counts[0, 0:N_EXPERTS].astype(jnp.int32)
    pcounts = (counts + MOE_BLOCK - 1) // MOE_BLOCK * MOE_BLOCK
    pend = jnp.cumsum(pcounts).astype(jnp.int32)
    pstart = (pend - pcounts).astype(F32)
    eid = route[:, R_EID:R_EID + TOP_K]
    rank = route[:, R_RANK:R_RANK + TOP_K]
    onehot = eid[:, :, None] == jnp.arange(N_EXPERTS, dtype=F32)[None, None, :]
    dest = (jnp.sum(jnp.where(onehot, pstart[None, None, :], 0.0), axis=-1) + rank).astype(jnp.int32).reshape(-1)
    blk_start = jnp.arange(n_blocks, dtype=jnp.int32) * MOE_BLOCK
    blk_e = jnp.sum((blk_start[:, None] >= pend[None, :]).astype(jnp.int32), axis=1)
    blk_e = jnp.minimum(blk_e, N_EXPERTS - 1).astype(jnp.int32)
    used = (pend[-1:] // MOE_BLOCK).astype(jnp.int32)
    filled_end = (pend - pcounts + counts)[blk_e]
    nvalid = jnp.where(blk_start < pend[-1], jnp.clip(filled_end - blk_start, 0, MOE_BLOCK), 0).astype(jnp.int32)
    return dest, blk_e, used, nvalid


def kernel(x_prompt, x_sample, cache_ckv, cache_krope, state_conv, page_table, w_in, conv_w, q_norm_g, w_uq,
           kv_norm_g, w_uk, w_uv, w_br_conv, w_br_attn, w_o, ln1_g, ln1_b, router_w_group, router_b_group,
           router_w_expert, router_b_expert, w_gate, w_up, w_down, ln2_g, ln2_b):
    depth = w_in.shape[0]
    alpha = (2 * depth) ** 0.25
    n_p, t_p, _ = x_prompt.shape
    n_s, t_s, _ = x_sample.shape
    assert t_s == 1 and t_p % TM == 0 and n_s % MOE_BLOCK == 0
    rows_p = n_p * t_p
    rows_all = rows_p + n_s
    past = page_table.shape[1] * PAGE_SIZE
    tab_p, tabt_p = _rope_tables(jnp.arange(t_p))
    tab_s, _ = _rope_tables(jnp.full((n_s,), past, jnp.int32))
    n_blocks = -(-(rows_all * TOP_K) // MOE_BLOCK) + N_EXPERTS

    h_p = x_prompt.reshape(rows_p, D_MODEL)
    h_s = x_sample.reshape(n_s, D_MODEL)
    ckv_p, kr_p, cv_p, ckv_s, kr_s, cv_s = [], [], [], [], [], []
    row = lambda v: v.reshape(1, -1)
    for l in range(depth):
        w1, wg, wuq, wuqt, wuk, wukt, wuvt, wbd, wr, br = _pack_weights(
            w_in[l], w_uq[l], w_uk[l], w_uv[l], router_w_group[l], router_b_group[l],
            router_w_expert[l], router_b_expert[l])
        qg, kvg = row(q_norm_g[l]), row(kv_norm_g[l])
        wba = w_br_attn[l].astype(BF16)
        merge_w = (wg, w_br_conv[l].astype(BF16), w_o[l].astype(BF16), row(ln1_g[l]), row(ln1_b[l]), wr, br)

        bconv, qt, k, vt, ckv, kr, nconv = _inproj_prompt(h_p, tab_p, tabt_p, w1, wuqt, wuk, wuvt, qg, kvg, conv_w[l],
                                                          n_p, t_p)
        ot = _prompt_attention(qt, k, vt, n_p, t_p)

        st = state_conv[l]
        bconv_s, q_s, qlat, ckvn_s, krn_s, u_s = _inproj_sample(
            h_s, tab_s, w1, wuq, wukt, qg, kvg, conv_w[l], st[:, 0], st[:, 1])
        olat = _sample_attention(page_table, qlat, q_s, ckvn_s, krn_s, cache_ckv[l],
                                 jnp.swapaxes(cache_krope[l], 1, 2))
        h1_s, route_s = _merge_sample(alpha, h_s, bconv_s, olat.reshape(n_s, N_HEADS * KV_LORA), wbd, wba, merge_w)
        h1t, route, counts = _merge_prompt(alpha, h_p, bconv, ot, h1_s, route_s, wba, merge_w)

        dest, blk_e, used, nvalid = _dispatch_plan(route, counts, n_blocks)
        asg = _slot_assignments(dest, n_blocks)
        y2 = _experts(blk_e, used, nvalid, asg, h1t, w_gate[l], w_up[l], w_down[l])
        ln2 = (row(ln2_g[l]), row(ln2_b[l]))
        h_p = _combine(alpha, 0, rows_p, TM, y2, h1t, route, *ln2)
        h_s = _combine(alpha, rows_p, n_s, n_s, y2, h1t, route, *ln2)

        ckv_p.append(ckv.reshape(n_p, t_p, KV_LORA))
        kr_p.append(kr.reshape(n_p, t_p, ROPE_DIM))
        cv_p.append(nconv)
        ckv_s.append(ckvn_s.reshape(n_s, 1, KV_LORA))
        kr_s.append(krn_s.reshape(n_s, 1, ROPE_DIM))
        cv_s.append(jnp.stack([st[:, 1], u_s], axis=1))
    return (h_p.reshape(n_p, t_p, D_MODEL), h_s.reshape(n_s, 1, D_MODEL), jnp.stack(ckv_p), jnp.stack(kr_p),
            jnp.stack(cv_p), jnp.stack(ckv_s), jnp.stack(kr_s), jnp.stack(cv_s))
```

```python
import functools

import jax
import jax.numpy as jnp
from jax import lax
from jax.experimental import pallas as pl
from jax.experimental.pallas import tpu as pltpu

F32 = jnp.float32
BF16 = jnp.bfloat16

D_MODEL = 1024
D_CONV = 512
CONV_W = 3
N_HEADS = 8
Q_LORA = 384
KV_LORA = 256
NOPE_DIM = 64
ROPE_DIM = 32
ROPE_HALF = ROPE_DIM // 2
V_DIM = 64
QK_DIM = NOPE_DIM + ROPE_DIM
ROPE_BASE = 10000.0
ATTN_SCALE = QK_DIM ** -0.5
LOG2E = 1.4426950408889634
Q_SCALE = ATTN_SCALE * LOG2E
N_GROUPS = 4
EXP_PER_GROUP = 8
N_EXPERTS = N_GROUPS * EXP_PER_GROUP
TOP_K = 2
D_EXPERT = 512
MOE_BLOCK = 128
PAGE_SIZE = 128
LN_EPS = 1e-5
RMS_EPS = 1e-6

LANES = 128
SUBLANES = 8
ROW_TILE = D_MODEL // LANES
assert ROW_TILE == SUBLANES
HEAD_W = LANES
QK_W = N_HEADS * HEAD_W
V_W = N_HEADS * V_DIM
_C_BG, _C_CG, _C_H = 0, D_CONV, 2 * D_CONV
_C_CQ = 3 * D_CONV
_C_CKV = _C_CQ + Q_LORA
_C_KR = _C_CKV + KV_LORA
W1_COLS = _C_KR + LANES
ROUTE_W = 2 * LANES
R_EID, R_WT, R_RANK = 0, 2, 4
NEG = -1e30
VMEM_LIMIT = 56 * 1024 * 1024

TM = 512
KEY_CHUNK = 2048
ATTN_HEADS_PER_LOOP = 4
TQ = 512
TM_COMBINE = 256
GATHER_RING = 3


def _cparams(*sem):
    return pltpu.CompilerParams(dimension_semantics=sem, vmem_limit_bytes=VMEM_LIMIT)


def _dot(a, b):
    return jnp.dot(a, b, preferred_element_type=F32)


def _dot_nt(a, b):
    return lax.dot_general(a, b, (((1,), (1,)), ((), ())), preferred_element_type=F32)


def _dot_tn(a, b):
    return lax.dot_general(a, b, (((0,), (0,)), ((), ())), preferred_element_type=F32)


def _rms(x, g):
    return x * lax.rsqrt(jnp.mean(x * x, axis=-1, keepdims=True) + RMS_EPS) * g


def _layernorm(x, g, b):
    mu = jnp.mean(x, axis=-1, keepdims=True)
    xc = x - mu
    var = jnp.mean(xc * xc, axis=-1, keepdims=True)
    return xc * lax.rsqrt(var + LN_EPS) * g + b


def _rope(xh, a, bm, cp):
    return xh * a + pltpu.roll(xh, LANES - ROPE_HALF, 1) * bm + pltpu.roll(xh, ROPE_HALF, 1) * cp


def _full(shape):
    nd = len(shape)
    return pl.BlockSpec(shape, lambda *_: (0,) * nd)


def _tile_rows(ref, c, n):
    return ref.at[pl.ds(c, n, stride=ROW_TILE), :]


def _inproj_common(x_ref, tab_ref, w1_ref, qg_ref, kvg_ref, ckv_ref, kr_ref):
    xb = x_ref[...].astype(BF16)

    def seg(lo, hi):
        return _dot(xb, w1_ref[:, lo:hi])

    b_g = seg(_C_BG, _C_CG)
    u = seg(_C_CG, _C_H) * seg(_C_H, _C_CQ)
    cqn = _rms(seg(_C_CQ, _C_CKV), qg_ref[...]).astype(BF16)
    ckvn = _rms(seg(_C_CKV, _C_KR), kvg_ref[...])
    ckv_ref[...] = ckvn
    krr = _rope(seg(_C_KR, W1_COLS), tab_ref[3], tab_ref[4], tab_ref[5])
    kr_ref[...] = krr[:, 0:ROPE_DIM]
    return b_g, u, cqn, ckvn.astype(BF16), krr


def _inproj_prompt_body(tiles_per_seq, x_ref, tab_ref, tabt_ref, w1_ref, wuqt_ref, wuk_ref, wuvt_ref, qg_ref, kvg_ref,
                        cw_ref, bconv_ref, qt_ref, k_ref, vt_ref, ckv_ref, kr_ref, nconv_ref, uext_ref):
    tm = x_ref.shape[0]
    b_g, u, cqn, cb, krr = _inproj_common(x_ref, tab_ref, w1_ref, qg_ref, kvg_ref, ckv_ref, kr_ref)
    qft = _dot_nt(wuqt_ref[...], cqn)
    cos, sin = tabt_ref[0], tabt_ref[1]
    for h in range(N_HEADS):
        r0 = h * HEAD_W
        x1 = qft[r0:r0 + ROPE_HALF, :]
        x2 = qft[r0 + ROPE_HALF:r0 + ROPE_DIM, :]
        qt_ref[r0:r0 + ROPE_HALF, :] = (x1 * cos - x2 * sin).astype(BF16)
        qt_ref[r0 + ROPE_HALF:r0 + ROPE_DIM, :] = (x1 * sin + x2 * cos).astype(BF16)
        qt_ref[r0 + ROPE_DIM:r0 + HEAD_W, :] = (qft[r0 + ROPE_DIM:r0 + HEAD_W, :] * Q_SCALE).astype(BF16)
    kn = _dot(cb, wuk_ref[...])
    for h in range(N_HEADS):
        sl = slice(h * HEAD_W, (h + 1) * HEAD_W)
        k_ref[:, sl] = (kn[:, sl] + krr).astype(BF16)
    vt_ref[...] = _dot_nt(wuvt_ref[...], cb).astype(BF16)

    first = (pl.program_id(0) % tiles_per_seq) == 0

    @pl.when(first)
    def _():
        uext_ref[0:8, :] = jnp.zeros((8, D_CONV), F32)

    @pl.when(jnp.logical_not(first))
    def _():
        uext_ref[0:8, :] = uext_ref[tm:tm + 8, :]

    uext_ref[8:8 + tm, :] = u
    conv = cw_ref[0:1, :] * uext_ref[6:6 + tm, :] + cw_ref[1:2, :] * uext_ref[7:7 + tm, :] + cw_ref[2:3, :] * u
    bconv_ref[...] = (b_g * conv).astype(BF16)
    nconv_ref[0] = u[tm - (CONV_W - 1):tm, :]


def _inproj_sample_body(x_ref, tab_ref, w1_ref, wuq_ref, wukt_ref, qg_ref, kvg_ref, cw_ref, s0_ref, s1_ref,
                        bconv_ref, q_ref, qlat_ref, ckv_ref, kr_ref, u_ref):
    b_g, u, cqn, _, _ = _inproj_common(x_ref, tab_ref, w1_ref, qg_ref, kvg_ref, ckv_ref, kr_ref)
    qf = _dot(cqn, wuq_ref[...])
    for h in range(N_HEADS):
        sl = slice(h * HEAD_W, (h + 1) * HEAD_W)
        qh = _rope(qf[:, sl], tab_ref[0], tab_ref[1], tab_ref[2]).astype(BF16)
        q_ref[:, sl] = qh
        qlat_ref[:, h * KV_LORA:(h + 1) * KV_LORA] = _dot(qh, wukt_ref[h]).astype(BF16)
    conv = cw_ref[0:1, :] * s0_ref[...] + cw_ref[1:2, :] * s1_ref[...] + cw_ref[2:3, :] * u
    bconv_ref[...] = (b_g * conv).astype(BF16)
    u_ref[...] = u


def _inproj_prompt(x, tab, tabt, w1, wuqt, wuk, wuvt, qg, kvg, cw, n_seq, seq):
    t = x.shape[0]
    tm = TM
    tiles_per_seq = seq // tm
    rows = lambda w: pl.BlockSpec((tm, w), lambda i: (i, 0))
    cols = lambda h: pl.BlockSpec((h, tm), lambda i: (0, i))
    return pl.pallas_call(
        functools.partial(_inproj_prompt_body, tiles_per_seq),
        grid=(t // tm,),
        in_specs=[rows(D_MODEL),
                  pl.BlockSpec((6, tm, LANES), lambda i: (0, i % tiles_per_seq, 0)),
                  pl.BlockSpec((2, ROPE_HALF, tm), lambda i: (0, 0, i % tiles_per_seq)),
                  _full(w1.shape), _full(wuqt.shape), _full(wuk.shape), _full(wuvt.shape),
                  _full(qg.shape), _full(kvg.shape), _full(cw.shape)],
        out_specs=[rows(D_CONV), cols(QK_W), rows(QK_W), cols(V_W), rows(KV_LORA), rows(ROPE_DIM),
                   pl.BlockSpec((1, CONV_W - 1, D_CONV), lambda i: (i // tiles_per_seq, 0, 0))],
        out_shape=[jax.ShapeDtypeStruct((t, D_CONV), BF16), jax.ShapeDtypeStruct((QK_W, t), BF16),
                   jax.ShapeDtypeStruct((t, QK_W), BF16), jax.ShapeDtypeStruct((V_W, t), BF16),
                   jax.ShapeDtypeStruct((t, KV_LORA), F32), jax.ShapeDtypeStruct((t, ROPE_DIM), F32),
                   jax.ShapeDtypeStruct((n_seq, CONV_W - 1, D_CONV), F32)],
        scratch_shapes=[pltpu.VMEM((tm + 8, D_CONV), F32)],
        compiler_params=_cparams("arbitrary"),
    )(x, tab, tabt, w1, wuqt, wuk, wuvt, qg, kvg, cw)


def _inproj_sample(x, tab, w1, wuq, wukt, qg, kvg, cw, s0, s1):
    n = x.shape[0]
    args = (x, tab, w1, wuq, wukt, qg, kvg, cw, s0, s1)
    out_shape = [jax.ShapeDtypeStruct((n, D_CONV), BF16), jax.ShapeDtypeStruct((n, QK_W), BF16),
                 jax.ShapeDtypeStruct((n, N_HEADS * KV_LORA), BF16), jax.ShapeDtypeStruct((n, KV_LORA), F32),
                 jax.ShapeDtypeStruct((n, ROPE_DIM), F32), jax.ShapeDtypeStruct((n, D_CONV), F32)]
    return pl.pallas_call(
        _inproj_sample_body,
        grid=(1,),
        in_specs=[_full(a.shape) for a in args],
        out_specs=[_full(s.shape) for s in out_shape],
        out_shape=out_shape,
        compiler_params=_cparams("arbitrary"),
    )(*args)


def _prompt_attention_body(qt_ref, k_ref, vt_ref, ot_ref, acc_ref, m_ref, l_ref, sa_ref, sb_ref):
    tq = qt_ref.shape[1]
    i = pl.program_id(1)

    for hg in range(N_HEADS // ATTN_HEADS_PER_LOOP):
        heads = tuple(range(hg * ATTN_HEADS_PER_LOOP, (hg + 1) * ATTN_HEADS_PER_LOOP))
        acc_ref[...] = jnp.zeros(acc_ref.shape, F32)
        m_ref[...] = jnp.full(m_ref.shape, NEG, F32)
        l_ref[...] = jnp.zeros(l_ref.shape, F32)

        def scores(j, s_ref):
            off = pl.multiple_of(j * tq, tq)
            for idx, h in enumerate(heads):
                s_ref[idx] = _dot(k_ref[pl.ds(off, tq), h * HEAD_W:(h + 1) * HEAD_W],
                                  qt_ref[h * HEAD_W:(h + 1) * HEAD_W, :])

        def consume(j, s_ref, masked):
            off = pl.multiple_of(j * tq, tq)
            for idx, h in enumerate(heads):
                s = s_ref[idx]
                if masked:
                    krow = lax.broadcasted_iota(jnp.int32, (tq, tq), 0)
                    qcol = lax.broadcasted_iota(jnp.int32, (tq, tq), 1)
                    s = jnp.where(krow <= qcol, s, NEG)
                m = m_ref[idx]
                m_new = jnp.maximum(m, jnp.max(s, axis=0, keepdims=True))
                alpha = jnp.exp2(m - m_new)
                p = jnp.exp2(s - m_new)
                m_ref[idx] = m_new
                l_ref[idx] = alpha * l_ref[idx] + jnp.sum(p, axis=0, keepdims=True)
                vblk = vt_ref[h * V_DIM:(h + 1) * V_DIM, pl.ds(off, tq)]
                acc_ref[idx] = alpha * acc_ref[idx] + _dot(vblk, p.astype(BF16))

        scores(0, sa_ref)

        def pair(pp, c):
            j = 2 * pp
            scores(j + 1, sb_ref)
            consume(j, sa_ref, False)
            scores(j + 2, sa_ref)
            consume(j + 1, sb_ref, False)
            return c
        lax.fori_loop(0, i // 2, pair, 0)

        @pl.when(i % 2 == 1)
        def _():
            scores(i, sb_ref)
            consume(i - 1, sa_ref, False)
            consume(i, sb_ref, True)

        @pl.when(i % 2 == 0)
        def _():
            consume(i, sa_ref, True)

        for idx, h in enumerate(heads):
            ot_ref[h * V_DIM:(h + 1) * V_DIM, :] = (acc_ref[idx] / l_ref[idx]).astype(BF16)


def _prompt_attention(qt, k, vt, n_seq, seq):
    t = k.shape[0]
    tq = TQ
    nq = seq // tq
    return pl.pallas_call(
        _prompt_attention_body,
        grid=(n_seq, nq),
        in_specs=[pl.BlockSpec((QK_W, tq), lambda b, i: (0, b * nq + i)),
                  pl.BlockSpec((seq, QK_W), lambda b, i: (b, 0)),
                  pl.BlockSpec((V_W, seq), lambda b, i: (0, b))],
        out_specs=pl.BlockSpec((V_W, tq), lambda b, i: (0, b * nq + i)),
        out_shape=jax.ShapeDtypeStruct((V_W, t), BF16),
        scratch_shapes=[pltpu.VMEM((ATTN_HEADS_PER_LOOP, V_DIM, tq), F32),
                        pltpu.VMEM((ATTN_HEADS_PER_LOOP, 1, tq), F32), pltpu.VMEM((ATTN_HEADS_PER_LOOP, 1, tq), F32),
                        pltpu.VMEM((ATTN_HEADS_PER_LOOP, tq, tq), F32), pltpu.VMEM((ATTN_HEADS_PER_LOOP, tq, tq), F32)],
        compiler_params=_cparams("arbitrary", "arbitrary"),
    )(qt, k, vt)


def _sample_attention_body(n_pages, pt_ref, qlat_ref, q_ref, ckvn_ref, krn_ref, cc_ref, cr_ref, olat_ref,
                           bufc, bufr, kcb, krb, sem):
    b = pl.program_id(0)
    nb = pl.num_programs(0)
    slot = b % 2
    past = n_pages * PAGE_SIZE

    def page_copies(bb, sl, p):
        pg = pt_ref[bb, p]
        pos = pl.ds(pl.multiple_of(p * PAGE_SIZE, PAGE_SIZE), PAGE_SIZE)
        return (pltpu.make_async_copy(cc_ref.at[pg], bufc.at[sl, pos], sem.at[0, sl]),
                pltpu.make_async_copy(cr_ref.at[pg], bufr.at[sl, :, pos], sem.at[1, sl]))

    def issue(bb, sl):
        for p in range(n_pages):
            for c in page_copies(bb, sl, p):
                c.start()

    def wait(sl):
        pltpu.make_async_copy(bufc.at[sl], bufc.at[sl], sem.at[0, sl]).wait()
        pltpu.make_async_copy(bufr.at[sl], bufr.at[sl], sem.at[1, sl]).wait()

    @pl.when(b == 0)
    def _():
        issue(0, 0)

    wait(slot)
    issue(jnp.minimum(b + 1, nb - 1), 1 - slot)

    ql = qlat_ref[0]
    qr = q_ref[0][:, 0:ROPE_DIM]
    cn = ckvn_ref[0]
    rn = krn_ref[0]
    chunk = min(KEY_CHUNK, past)
    scores = []
    for c in range(past // chunk):
        pos = slice(c * chunk, (c + 1) * chunk)
        kcb[pos, :] = bufc[slot, pos, :].astype(BF16)
        krb[:, pos] = bufr[slot, :, pos].astype(BF16)
        scores.append(_dot_nt(ql, kcb[pos, :]) + _dot(qr, krb[:, pos]))
    s_new = (jnp.sum(ql.astype(F32) * cn, axis=-1, keepdims=True)
             + jnp.sum(qr.astype(F32) * rn, axis=-1, keepdims=True))
    m = s_new
    for s in scores:
        m = jnp.maximum(m, jnp.max(s, axis=-1, keepdims=True))
    p_new = jnp.exp2(s_new - m)
    l = p_new
    o = p_new * cn
    for c, s in enumerate(scores):
        p = jnp.exp2(s - m)
        l = l + jnp.sum(p, axis=-1, keepdims=True)
        o = o + _dot(p.astype(BF16), kcb[c * chunk:(c + 1) * chunk, :])
    olat_ref[0] = o / l

    @pl.when(b == nb - 1)
    def _():
        wait(1 - slot)


def _sample_attention(page_table, qlat, q, ckvn, krn, cache_c, cache_rt):
    n, n_pages = page_table.shape
    past = n_pages * PAGE_SIZE
    blk = lambda d1, d2: pl.BlockSpec((1, d1, d2), lambda b, pt: (b, 0, 0))
    grid_spec = pltpu.PrefetchScalarGridSpec(
        num_scalar_prefetch=1,
        grid=(n,),
        in_specs=[blk(N_HEADS, KV_LORA), blk(N_HEADS, HEAD_W), blk(1, KV_LORA), blk(1, ROPE_DIM),
                  pl.BlockSpec(memory_space=pl.ANY), pl.BlockSpec(memory_space=pl.ANY)],
        out_specs=blk(N_HEADS, KV_LORA),
        scratch_shapes=[pltpu.VMEM((2, past, KV_LORA), F32), pltpu.VMEM((2, ROPE_DIM, past), F32),
                        pltpu.VMEM((past, KV_LORA), BF16), pltpu.VMEM((ROPE_DIM, past), BF16),
                        pltpu.SemaphoreType.DMA((2, 2))],
    )
    return pl.pallas_call(
        functools.partial(_sample_attention_body, n_pages),
        grid_spec=grid_spec,
        out_shape=jax.ShapeDtypeStruct((n, N_HEADS, KV_LORA), F32),
        compiler_params=_cparams("arbitrary"),
    )(page_table, qlat.reshape(n, N_HEADS, KV_LORA), q.reshape(n, N_HEADS, HEAD_W),
      ckvn.reshape(n, 1, KV_LORA), krn.reshape(n, 1, ROPE_DIM), cache_c, cache_rt)


def _route(logits):
    lane_i = lax.broadcasted_iota(jnp.int32, (logits.shape[0], LANES), 1)
    lane = lane_i.astype(F32)
    first_at = lambda hit: jnp.min(jnp.where(hit, lane, float(LANES)), axis=-1, keepdims=True)
    lg = jnp.where(lane_i < N_GROUPS, logits[:, 0:LANES], NEG)
    mg = jnp.max(lg, axis=-1, keepdims=True)
    p_grp = 1.0 / jnp.sum(jnp.exp(lg - mg), axis=-1, keepdims=True)
    grp = first_at(lg == mg)
    lane_grp = (lane_i // EXP_PER_GROUP).astype(F32)
    le = jnp.where(lane_grp == grp, logits[:, LANES:2 * LANES], NEG)
    top1 = jnp.max(le, axis=-1, keepdims=True)
    i1 = first_at(le == top1)
    le2 = jnp.where(lane == i1, NEG, le)
    top2 = jnp.max(le2, axis=-1, keepdims=True)
    i2 = first_at(le2 == top2)
    e2 = jnp.exp(top2 - top1)
    w1 = p_grp / (1.0 + e2)
    w2 = p_grp * e2 / (1.0 + e2)
    out = jnp.where(lane_i == R_EID, i1, 0.0)
    out = jnp.where(lane_i == R_EID + 1, i2, out)
    out = jnp.where(lane_i == R_WT, w1, out)
    return jnp.where(lane_i == R_WT + 1, w2, out)


def _add_ranks(route, ltri, count_ref):
    n = route.shape[0]
    lane_i = lax.broadcasted_iota(jnp.int32, (n, LANES), 1)
    lane = lane_i.astype(F32)
    hit1 = lane == route[:, R_EID:R_EID + 1]
    hit2 = lane == route[:, R_EID + 1:R_EID + 2]
    chosen = jnp.where(hit1, 1.0, jnp.where(hit2, 1.0, 0.0))
    before = _dot(ltri, chosen.astype(BF16)) + count_ref[...]
    r1 = jnp.sum(jnp.where(hit1, before, 0.0), axis=-1, keepdims=True)
    r2 = jnp.sum(jnp.where(hit2, before, 0.0), axis=-1, keepdims=True)
    count_ref[...] = count_ref[...] + jnp.sum(chosen, axis=0, keepdims=True)
    route = jnp.where(lane_i == R_RANK, r1, route)
    return jnp.where(lane_i == R_RANK + 1, r2, route)


def _merge_rows(alpha, x, bconv, y_attn, wg_ref, wbc_ref, wo_ref, lg_ref, lb_ref, wr_ref, br_ref):
    g = _dot(x.astype(BF16), wg_ref[...])
    y_conv = _dot(bconv, wbc_ref[...])
    m = jax.nn.sigmoid(g[:, 0:D_MODEL]) * y_conv + jax.nn.sigmoid(g[:, D_MODEL:]) * y_attn
    mix = _dot(m.astype(BF16), wo_ref[...])
    h1 = _layernorm(alpha * x + mix, lg_ref[...], lb_ref[...])
    return h1, _route(_dot(h1.astype(BF16), wr_ref[...]) + br_ref[...])


def _merge_sample_body(alpha, x_ref, bconv_ref, olat_ref, wbd_ref, wba_ref, *refs):
    w_refs, (h1_ref, route_ref) = refs[:-2], refs[-2:]
    o = _dot(olat_ref[...].astype(BF16), wbd_ref[...]).astype(BF16)
    h1_ref[...], route_ref[...] = _merge_rows(alpha, x_ref[...], bconv_ref[...], _dot(o, wba_ref[...]), *w_refs)


def _merge_prompt_body(alpha, n_tiles, x_ref, bconv_ref, ot_ref, h1s_ref, routes_ref, wba_ref, *refs):
    w_refs, (h1t_ref, route_ref, counts_ref, ltri_ref, count_ref) = refs[:-5], refs[-5:]
    i = pl.program_id(0)
    tm = x_ref.shape[0]

    @pl.when(i == 0)
    def _():
        r = lax.broadcasted_iota(jnp.int32, (tm, tm), 0)
        c = lax.broadcasted_iota(jnp.int32, (tm, tm), 1)
        ltri_ref[...] = jnp.where(c < r, 1.0, 0.0).astype(BF16)
        count_ref[...] = jnp.zeros(count_ref.shape, F32)

    @pl.when(i < n_tiles)
    def _():
        y_attn = _dot_tn(ot_ref[...], wba_ref[...])
        h1, route = _merge_rows(alpha, x_ref[...], bconv_ref[...], y_attn, *w_refs)
        for c in range(ROW_TILE):
            _tile_rows(h1t_ref, c, tm)[...] = h1[:, c * LANES:(c + 1) * LANES]
        route_ref[...] = _add_ranks(route, ltri_ref[...], count_ref)

    @pl.when(i == n_tiles)
    def _():
        n_s = h1s_ref.shape[0]
        h1t_ref[...] = jnp.zeros(h1t_ref.shape, F32)
        route_ref[...] = jnp.zeros(route_ref.shape, F32)
        for c in range(ROW_TILE):
            _tile_rows(h1t_ref, c, n_s)[...] = h1s_ref[:, c * LANES:(c + 1) * LANES]
        route_ref[0:n_s, :] = _add_ranks(routes_ref[...], ltri_ref[0:n_s, 0:n_s], count_ref)

    counts_ref[...] = jnp.broadcast_to(count_ref[...], counts_ref.shape)


def _merge_sample(alpha, x, bconv, olat, wbd, wba, ws):
    n = x.shape[0]
    args = (x, bconv, olat, wbd, wba) + tuple(ws)
    out_shape = [jax.ShapeDtypeStruct((n, D_MODEL), F32), jax.ShapeDtypeStruct((n, LANES), F32)]
    return pl.pallas_call(
        functools.partial(_merge_sample_body, alpha),
        grid=(1,),
        in_specs=[_full(a.shape) for a in args],
        out_specs=[_full(s.shape) for s in out_shape],
        out_shape=out_shape,
        compiler_params=_cparams("arbitrary"),
    )(*args)


def _merge_prompt(alpha, x, bconv, ot, h1_s, route_s, wba, ws):
    t = x.shape[0]
    tm = TM
    n_tiles = t // tm
    t_all = t + h1_s.shape[0]
    assert h1_s.shape[0] <= tm
    clamp = lambda i: jnp.minimum(i, n_tiles - 1)
    rows_in = lambda w: pl.BlockSpec((tm, w), lambda i: (clamp(i), 0))
    return pl.pallas_call(
        functools.partial(_merge_prompt_body, alpha, n_tiles),
        grid=(n_tiles + 1,),
        in_specs=[rows_in(D_MODEL), rows_in(D_CONV), pl.BlockSpec((V_W, tm), lambda i: (0, clamp(i))),
                  _full(h1_s.shape), _full(route_s.shape), _full(wba.shape)] + [_full(w.shape) for w in ws],
        out_specs=[pl.BlockSpec((tm * ROW_TILE, LANES), lambda i: (i, 0)), pl.BlockSpec((tm, LANES), lambda i: (i, 0)),
                   _full((SUBLANES, LANES))],
        out_shape=[jax.ShapeDtypeStruct((t_all * ROW_TILE, LANES), F32), jax.ShapeDtypeStruct((t_all, LANES), F32),
                   jax.ShapeDtypeStruct((SUBLANES, LANES), F32)],
        scratch_shapes=[pltpu.VMEM((tm, tm), BF16), pltpu.VMEM((1, LANES), F32)],
        compiler_params=_cparams("arbitrary"),
    )(x, bconv, ot, h1_s, route_s, wba, *ws)


SLOT_GROUP = 16


def _slot_assignments_body(rows_ref, dest_ref, asg_ref):
    def clear_row(j, c):
        for l in range(LANES):
            asg_ref[j * LANES + l] = 0
        return c
    lax.fori_loop(0, rows_ref[1], clear_row, 0)

    def fill_row(j, c):
        for g in range(0, LANES, SLOT_GROUP):
            ds = [dest_ref[j, g + l] for l in range(SLOT_GROUP)]
            for l, d in enumerate(ds):
                asg_ref[d] = j * LANES + g + l
        return c
    lax.fori_loop(0, rows_ref[0], fill_row, 0)


def _slot_assignments(dest, n_blocks):
    assert MOE_BLOCK == LANES and dest.shape[0] % LANES == 0
    dest = dest.reshape(-1, LANES)
    grid_spec = pltpu.PrefetchScalarGridSpec(
        num_scalar_prefetch=2,
        grid=(1,),
        in_specs=[],
        out_specs=pl.BlockSpec(memory_space=pltpu.SMEM),
    )
    return pl.pallas_call(
        _slot_assignments_body,
        grid_spec=grid_spec,
        out_shape=jax.ShapeDtypeStruct((n_blocks * MOE_BLOCK,), jnp.int32),
        compiler_params=_cparams("arbitrary"),
    )(jnp.array([dest.shape[0], n_blocks], jnp.int32), dest)


def _row_tile(i):
    return pl.ds(pl.multiple_of(i * ROW_TILE, ROW_TILE), ROW_TILE)


def _experts_body(blk_e_ref, used_ref, asg_ref, h1t_ref, wg_ref, wu_ref, wd_ref, ys_ref,
                  xbuf, xb, wgb, wub, wdb, sem_in):
    b = pl.program_id(0)
    ring = lambda bb: lax.rem(bb, GATHER_RING)
    used = used_ref[0]
    blk_rows = MOE_BLOCK * ROW_TILE
    changed = jnp.logical_or(b == 0, blk_e_ref[b] != blk_e_ref[jnp.maximum(b - 1, 0)])

    def in_copy(bb, sl, r):
        tok = lax.shift_right_logical(asg_ref[bb * MOE_BLOCK + r], TOP_K - 1)
        return pltpu.make_async_copy(h1t_ref.at[_row_tile(tok)], xbuf.at[sl, _row_tile(r)], sem_in.at[sl])

    def issue_in(bb, sl):
        for r in range(MOE_BLOCK):
            in_copy(bb, sl, r).start(priority=r % 2)

    def wait_in(sl):
        pltpu.make_async_copy(h1t_ref.at[pl.ds(0, blk_rows)], xbuf.at[sl], sem_in.at[sl]).wait()

    last = used - 1

    @pl.when(jnp.logical_and(b == 0, used > 0))
    def _():
        for a in range(GATHER_RING - 1):
            issue_in(jnp.minimum(a, last), a)

    @pl.when(jnp.logical_and(b < used, changed))
    def _():
        wgb[...] = wg_ref[0].astype(BF16)
        wub[...] = wu_ref[0].astype(BF16)
        wdb[...] = wd_ref[0].astype(BF16)

    @pl.when(b < used)
    def _():
        wait_in(ring(b))
        for c in range(ROW_TILE):
            xb[:, c * LANES:(c + 1) * LANES] = _tile_rows(xbuf.at[ring(b)], c, MOE_BLOCK)[...].astype(BF16)
        ahead = b + GATHER_RING - 1
        issue_in(jnp.minimum(ahead, last), ring(ahead))
        x = xb[...]
        g = _dot(x, wgb[...])
        u = _dot(x, wub[...])
        h = (g * jax.nn.sigmoid(g) * u).astype(BF16)
        y = _dot(h, wdb[...])
        for c in range(ROW_TILE):
            _tile_rows(ys_ref, c, MOE_BLOCK)[...] = y[:, c * LANES:(c + 1) * LANES]

        @pl.when(b == last)
        def _():
            for a in range(1, GATHER_RING):
                wait_in(ring(b + a))

    @pl.when(b >= used)
    def _():
        ys_ref[...] = jnp.zeros(ys_ref.shape, F32)


def _experts(blk_e, used, asg, h1t, wg, wu, wd):
    n_blocks = blk_e.shape[0]
    blk_rows = MOE_BLOCK * ROW_TILE
    by_expert = lambda b, e, u, a: (e[b], 0, 0)
    grid_spec = pltpu.PrefetchScalarGridSpec(
        num_scalar_prefetch=3,
        grid=(n_blocks,),
        in_specs=[pl.BlockSpec(memory_space=pl.ANY),
                  pl.BlockSpec((1, D_MODEL, D_EXPERT), by_expert), pl.BlockSpec((1, D_MODEL, D_EXPERT), by_expert),
                  pl.BlockSpec((1, D_EXPERT, D_MODEL), by_expert)],
        out_specs=pl.BlockSpec((blk_rows, LANES), lambda b, e, u, a: (b, 0)),
        scratch_shapes=[pltpu.VMEM((GATHER_RING, blk_rows, LANES), F32),
                        pltpu.VMEM((MOE_BLOCK, D_MODEL), BF16),
                        pltpu.VMEM((D_MODEL, D_EXPERT), BF16), pltpu.VMEM((D_MODEL, D_EXPERT), BF16),
                        pltpu.VMEM((D_EXPERT, D_MODEL), BF16),
                        pltpu.SemaphoreType.DMA((GATHER_RING,))],
    )
    return pl.pallas_call(
        _experts_body,
        grid_spec=grid_spec,
        out_shape=jax.ShapeDtypeStruct((n_blocks * blk_rows, LANES), F32),
        compiler_params=_cparams("arbitrary"),
    )(blk_e, used, asg, h1t, wg, wu, wd)


def _combine_body(alpha, tok0, dest_ref, ys_ref, h1t_ref, route_ref, lg_ref, lb_ref, out_ref, gbuf, sem):
    i = pl.program_id(0)
    nt = pl.num_programs(0)
    slot = i % 2
    tm = out_ref.shape[0]

    def issue(ii, sl):
        base = (tok0 + ii * tm) * TOP_K
        for r in range(tm):
            for k in range(TOP_K):
                d = dest_ref[base + r * TOP_K + k]
                pltpu.make_async_copy(ys_ref.at[_row_tile(d)], gbuf.at[sl, k, _row_tile(r)],
                                      sem.at[sl]).start(priority=k)

    def wait(sl):
        for k in range(TOP_K):
            pltpu.make_async_copy(ys_ref.at[pl.ds(0, tm * ROW_TILE)], gbuf.at[sl, k], sem.at[sl]).wait()

    @pl.when(i == 0)
    def _():
        issue(0, 0)

    wait(slot)
    issue(jnp.minimum(i + 1, nt - 1), 1 - slot)

    route = route_ref[...]
    w0 = jnp.broadcast_to(route[:, R_WT:R_WT + 1], (tm, LANES))
    w1 = jnp.broadcast_to(route[:, R_WT + 1:R_WT + 2], (tm, LANES))
    z = [alpha * _tile_rows(h1t_ref, c, tm)[...]
         + w0 * _tile_rows(gbuf.at[slot, 0], c, tm)[...] + w1 * _tile_rows(gbuf.at[slot, 1], c, tm)[...]
         for c in range(ROW_TILE)]
    mu = sum(jnp.sum(zc, axis=-1, keepdims=True) for zc in z) / D_MODEL
    zc = [v - mu for v in z]
    var = sum(jnp.sum(v * v, axis=-1, keepdims=True) for v in zc) / D_MODEL
    rstd = lax.rsqrt(var + LN_EPS)
    for c in range(ROW_TILE):
        sl = slice(c * LANES, (c + 1) * LANES)
        out_ref[:, sl] = zc[c] * rstd * lg_ref[:, sl] + lb_ref[:, sl]

    @pl.when(i == nt - 1)
    def _():
        wait(1 - slot)


def _combine(alpha, tok0, n_tok, tm, dest, ys, h1t, route, lg, lb):
    assert n_tok % tm == 0 and tok0 % tm == 0
    blk0 = tok0 // tm
    grid_spec = pltpu.PrefetchScalarGridSpec(
        num_scalar_prefetch=1,
        grid=(n_tok // tm,),
        in_specs=[pl.BlockSpec(memory_space=pl.ANY),
                  pl.BlockSpec((tm * ROW_TILE, LANES), lambda i, d: (blk0 + i, 0)),
                  pl.BlockSpec((tm, LANES), lambda i, d: (blk0 + i, 0)),
                  pl.BlockSpec(lg.shape, lambda i, d: (0, 0)), pl.BlockSpec(lb.shape, lambda i, d: (0, 0))],
        out_specs=pl.BlockSpec((tm, D_MODEL), lambda i, d: (i, 0)),
        scratch_shapes=[pltpu.VMEM((2, TOP_K, tm * ROW_TILE, LANES), F32), pltpu.SemaphoreType.DMA((2,))],
    )
    return pl.pallas_call(
        functools.partial(_combine_body, alpha, tok0),
        grid_spec=grid_spec,
        out_shape=jax.ShapeDtypeStruct((n_tok, D_MODEL), F32),
        compiler_params=_cparams("arbitrary"),
    )(dest, ys, h1t, route, lg, lb)


def _rope_tables(pos):
    inv = ROPE_BASE ** (-(jnp.arange(ROPE_HALF, dtype=F32) * 2.0 / ROPE_DIM))
    ang = pos.astype(F32)[:, None] * inv[None, :]
    cos, sin = jnp.cos(ang), jnp.sin(ang)
    n = pos.shape[0]
    a = jnp.concatenate([cos, cos, jnp.ones((n, LANES - ROPE_DIM), F32)], axis=1)
    bm = jnp.concatenate([-sin, jnp.zeros((n, LANES - ROPE_HALF), F32)], axis=1)
    cp = jnp.concatenate([jnp.zeros((n, ROPE_HALF), F32), sin, jnp.zeros((n, LANES - ROPE_DIM), F32)], axis=1)
    k_tabs = jnp.stack([a, bm, cp])
    return jnp.concatenate([k_tabs * Q_SCALE, k_tabs]), jnp.stack([cos.T, sin.T]) * Q_SCALE


def _head_blocks(w_rope, w_nope):
    k = w_nope.shape[0]
    pad = jnp.zeros((k, N_HEADS, HEAD_W - QK_DIM), w_nope.dtype)
    return jnp.concatenate([w_rope, w_nope, pad], axis=-1).reshape(k, QK_W)


def _pack_weights(w_in, w_uq, w_uk, w_uv, router_w_group, router_b_group, router_w_expert, router_b_expert):
    d = w_in.shape[0]
    c_kr = 3 * D_CONV + Q_LORA + KV_LORA
    w1 = jnp.concatenate([w_in[:, :c_kr], w_in[:, c_kr:c_kr + ROPE_DIM],
                          jnp.zeros((d, LANES - ROPE_DIM), w_in.dtype)], axis=1).astype(BF16)
    wg = w_in[:, c_kr + ROPE_DIM:].astype(BF16)
    uq = w_uq.reshape(Q_LORA, N_HEADS, QK_DIM)
    wuq = _head_blocks(uq[..., NOPE_DIM:], uq[..., :NOPE_DIM]).astype(BF16)
    wuk = _head_blocks(jnp.zeros((KV_LORA, N_HEADS, ROPE_DIM), w_uk.dtype), w_uk).astype(BF16)
    wukt = jnp.transpose(wuk.reshape(KV_LORA, N_HEADS, HEAD_W), (1, 2, 0))
    wuvt = w_uv.reshape(KV_LORA, V_W).T.astype(BF16)
    eye = jnp.eye(N_HEADS, dtype=w_uv.dtype)
    wbd = jnp.einsum('chd,hg->hcgd', w_uv, eye).reshape(N_HEADS * KV_LORA, V_W).astype(BF16)
    wr = jnp.zeros((d, ROUTE_W), F32)
    wr = wr.at[:, 0:N_GROUPS].set(router_w_group).at[:, LANES:LANES + N_EXPERTS].set(router_w_expert).astype(BF16)
    br = jnp.zeros((1, ROUTE_W), F32)
    br = br.at[0, 0:N_GROUPS].set(router_b_group).at[0, LANES:LANES + N_EXPERTS].set(router_b_expert)
    return w1, wg, wuq, wuq.T, wuk, wukt, wuvt, wbd, wr, br


def _dispatch_plan(route, counts, n_blocks):
    counts = counts[0, 0:N_EXPERTS].astype(jnp.int32)
    pcounts = (counts + MOE_BLOCK - 1) // MOE_BLOCK * MOE_BLOCK
    pend = jnp.cumsum(pcounts).astype(jnp.int32)
    pstart = (pend - pcounts).astype(F32)
    eid = route[:, R_EID:R_EID + TOP_K]
    rank = route[:, R_RANK:R_RANK + TOP_K]
    onehot = eid[:, :, None] == jnp.arange(N_EXPERTS, dtype=F32)[None, None, :]
    dest = (jnp.sum(jnp.where(onehot, pstart[None, None, :], 0.0), axis=-1) + rank).astype(jnp.int32).reshape(-1)
    blk_start = jnp.arange(n_blocks, dtype=jnp.int32) * MOE_BLOCK
    blk_e = jnp.sum((blk_start[:, None] >= pend[None, :]).astype(jnp.int32), axis=1)
    blk_e = jnp.minimum(blk_e, N_EXPERTS - 1).astype(jnp.int32)
    used = (pend[-1:] // MOE_BLOCK).astype(jnp.int32)
    return dest, blk_e, used


def kernel(x_prompt, x_sample, cache_ckv, cache_krope, state_conv, page_table, w_in, conv_w, q_norm_g, w_uq,
           kv_norm_g, w_uk, w_uv, w_br_conv, w_br_attn, w_o, ln1_g, ln1_b, router_w_group, router_b_group,
           router_w_expert, router_b_expert, w_gate, w_up, w_down, ln2_g, ln2_b):
    depth = w_in.shape[0]
    alpha = (2 * depth) ** 0.25
    n_p, t_p, _ = x_prompt.shape
    n_s, t_s, _ = x_sample.shape
    assert t_s == 1 and t_p % TM == 0 and n_s % MOE_BLOCK == 0
    rows_p = n_p * t_p
    rows_all = rows_p + n_s
    past = page_table.shape[1] * PAGE_SIZE
    tab_p, tabt_p = _rope_tables(jnp.arange(t_p))
    tab_s, _ = _rope_tables(jnp.full((n_s,), past, jnp.int32))
    n_blocks = -(-(rows_all * TOP_K) // MOE_BLOCK) + N_EXPERTS

    h_p = x_prompt.reshape(rows_p, D_MODEL)
    h_s = x_sample.reshape(n_s, D_MODEL)
    ckv_p, kr_p, cv_p, ckv_s, kr_s, cv_s = [], [], [], [], [], []
    row = lambda v: v.reshape(1, -1)
    for l in range(depth):
        w1, wg, wuq, wuqt, wuk, wukt, wuvt, wbd, wr, br = _pack_weights(
            w_in[l], w_uq[l], w_uk[l], w_uv[l], router_w_group[l], router_b_group[l],
            router_w_expert[l], router_b_expert[l])
        qg, kvg = row(q_norm_g[l]), row(kv_norm_g[l])
        wba = w_br_attn[l].astype(BF16)
        merge_w = (wg, w_br_conv[l].astype(BF16), w_o[l].astype(BF16), row(ln1_g[l]), row(ln1_b[l]), wr, br)

        bconv, qt, k, vt, ckv, kr, nconv = _inproj_prompt(h_p, tab_p, tabt_p, w1, wuqt, wuk, wuvt, qg, kvg, conv_w[l],
                                                          n_p, t_p)
        ot = _prompt_attention(qt, k, vt, n_p, t_p)

        st = state_conv[l]
        bconv_s, q_s, qlat, ckvn_s, krn_s, u_s = _inproj_sample(
            h_s, tab_s, w1, wuq, wukt, qg, kvg, conv_w[l], st[:, 0], st[:, 1])
        olat = _sample_attention(page_table, qlat, q_s, ckvn_s, krn_s, cache_ckv[l],
                                 jnp.swapaxes(cache_krope[l], 1, 2))
        h1_s, route_s = _merge_sample(alpha, h_s, bconv_s, olat.reshape(n_s, N_HEADS * KV_LORA), wbd, wba, merge_w)
        h1t, route, counts = _merge_prompt(alpha, h_p, bconv, ot, h1_s, route_s, wba, merge_w)

        dest, blk_e, used = _dispatch_plan(route, counts, n_blocks)
        asg = _slot_assignments(dest, n_blocks)
        ys = _experts(blk_e, used, asg, h1t, w_gate[l], w_up[l], w_down[l])
        ln2 = (row(ln2_g[l]), row(ln2_b[l]))
        h_p = _combine(alpha, 0, rows_p, TM_COMBINE, dest, ys, h1t, route, *ln2)
        h_s = _combine(alpha, rows_p, n_s, n_s, dest, ys, h1t, route, *ln2)

        ckv_p.append(ckv.reshape(n_p, t_p, KV_LORA))
        kr_p.append(kr.reshape(n_p, t_p, ROPE_DIM))
        cv_p.append(nconv)
        ckv_s.append(ckvn_s.reshape(n_s, 1, KV_LORA))
        kr_s.append(krn_s.reshape(n_s, 1, ROPE_DIM))
        cv_s.append(jnp.stack([st[:, 1], u_s], axis=1))
    return (h_p.reshape(n_p, t_p, D_MODEL), h_s.reshape(n_s, 1, D_MODEL), jnp.stack(ckv_p), jnp.stack(kr_p),
            jnp.stack(cv_p), jnp.stack(ckv_s), jnp.stack(kr_s), jnp.stack(cv_s))
```

```python
import functools

import jax
import jax.numpy as jnp
from jax import lax
from jax.experimental import pallas as pl
from jax.experimental.pallas import tpu as pltpu

F32 = jnp.float32
BF16 = jnp.bfloat16

D_MODEL = 1024
D_CONV = 512
CONV_W = 3
N_HEADS = 8
Q_LORA = 384
KV_LORA = 256
NOPE_DIM = 64
ROPE_DIM = 32
ROPE_HALF = ROPE_DIM // 2
V_DIM = 64
QK_DIM = NOPE_DIM + ROPE_DIM
ROPE_BASE = 10000.0
ATTN_SCALE = QK_DIM ** -0.5
LOG2E = 1.4426950408889634
Q_SCALE = ATTN_SCALE * LOG2E
N_GROUPS = 4
EXP_PER_GROUP = 8
N_EXPERTS = N_GROUPS * EXP_PER_GROUP
TOP_K = 2
D_EXPERT = 512
MOE_BLOCK = 256
PAGE_SIZE = 128
LN_EPS = 1e-5
RMS_EPS = 1e-6

LANES = 128
SUBLANES = 8
ROW_TILE = D_MODEL // LANES
assert ROW_TILE == SUBLANES
HEAD_W = LANES
QK_W = N_HEADS * HEAD_W
V_W = N_HEADS * V_DIM
_C_BG, _C_CG, _C_H = 0, D_CONV, 2 * D_CONV
_C_CQ = 3 * D_CONV
_C_CKV = _C_CQ + Q_LORA
_C_KR = _C_CKV + KV_LORA
W1_COLS = _C_KR + LANES
ROUTE_W = 2 * LANES
R_EID, R_WT, R_RANK = 0, 2, 4
NEG = -1e30
VMEM_LIMIT = 56 * 1024 * 1024

TM = 512
KEY_CHUNK = 2048
ATTN_HEADS_PER_LOOP = 4
TQ = 512
TM_COMBINE = 256
GATHER_RING = 3


def _cparams(*sem):
    return pltpu.CompilerParams(dimension_semantics=sem, vmem_limit_bytes=VMEM_LIMIT)


def _dot(a, b):
    return jnp.dot(a, b, preferred_element_type=F32)


def _dot_nt(a, b):
    return lax.dot_general(a, b, (((1,), (1,)), ((), ())), preferred_element_type=F32)


def _dot_tn(a, b):
    return lax.dot_general(a, b, (((0,), (0,)), ((), ())), preferred_element_type=F32)


def _rms(x, g):
    return x * lax.rsqrt(jnp.mean(x * x, axis=-1, keepdims=True) + RMS_EPS) * g


def _layernorm(x, g, b):
    mu = jnp.mean(x, axis=-1, keepdims=True)
    xc = x - mu
    var = jnp.mean(xc * xc, axis=-1, keepdims=True)
    return xc * lax.rsqrt(var + LN_EPS) * g + b


def _rope(xh, a, bm, cp):
    return xh * a + pltpu.roll(xh, LANES - ROPE_HALF, 1) * bm + pltpu.roll(xh, ROPE_HALF, 1) * cp


def _full(shape):
    nd = len(shape)
    return pl.BlockSpec(shape, lambda *_: (0,) * nd)


def _tile_rows(ref, c, n):
    return ref.at[pl.ds(c, n, stride=ROW_TILE), :]


def _inproj_common(x_ref, tab_ref, w1_ref, qg_ref, kvg_ref, ckv_ref, kr_ref):
    xb = x_ref[...].astype(BF16)

    def seg(lo, hi):
        return _dot(xb, w1_ref[:, lo:hi])

    b_g = seg(_C_BG, _C_CG)
    u = seg(_C_CG, _C_H) * seg(_C_H, _C_CQ)
    cqn = _rms(seg(_C_CQ, _C_CKV), qg_ref[...]).astype(BF16)
    ckvn = _rms(seg(_C_CKV, _C_KR), kvg_ref[...])
    ckv_ref[...] = ckvn
    krr = _rope(seg(_C_KR, W1_COLS), tab_ref[3], tab_ref[4], tab_ref[5])
    kr_ref[...] = krr[:, 0:ROPE_DIM]
    return b_g, u, cqn, ckvn.astype(BF16), krr


def _inproj_prompt_body(tiles_per_seq, x_ref, tab_ref, tabt_ref, w1_ref, wuqt_ref, wuk_ref, wuvt_ref, qg_ref, kvg_ref,
                        cw_ref, bconv_ref, qt_ref, k_ref, vt_ref, ckv_ref, kr_ref, nconv_ref, uext_ref):
    tm = x_ref.shape[0]
    b_g, u, cqn, cb, krr = _inproj_common(x_ref, tab_ref, w1_ref, qg_ref, kvg_ref, ckv_ref, kr_ref)
    qft = _dot_nt(wuqt_ref[...], cqn)
    cos, sin = tabt_ref[0], tabt_ref[1]
    for h in range(N_HEADS):
        r0 = h * HEAD_W
        x1 = qft[r0:r0 + ROPE_HALF, :]
        x2 = qft[r0 + ROPE_HALF:r0 + ROPE_DIM, :]
        qt_ref[r0:r0 + ROPE_HALF, :] = (x1 * cos - x2 * sin).astype(BF16)
        qt_ref[r0 + ROPE_HALF:r0 + ROPE_DIM, :] = (x1 * sin + x2 * cos).astype(BF16)
        qt_ref[r0 + ROPE_DIM:r0 + HEAD_W, :] = (qft[r0 + ROPE_DIM:r0 + HEAD_W, :] * Q_SCALE).astype(BF16)
    kn = _dot(cb, wuk_ref[...])
    for h in range(N_HEADS):
        sl = slice(h * HEAD_W, (h + 1) * HEAD_W)
        k_ref[:, sl] = (kn[:, sl] + krr).astype(BF16)
    vt_ref[...] = _dot_nt(wuvt_ref[...], cb).astype(BF16)

    first = (pl.program_id(0) % tiles_per_seq) == 0

    @pl.when(first)
    def _():
        uext_ref[0:8, :] = jnp.zeros((8, D_CONV), F32)

    @pl.when(jnp.logical_not(first))
    def _():
        uext_ref[0:8, :] = uext_ref[tm:tm + 8, :]

    uext_ref[8:8 + tm, :] = u
    conv = cw_ref[0:1, :] * uext_ref[6:6 + tm, :] + cw_ref[1:2, :] * uext_ref[7:7 + tm, :] + cw_ref[2:3, :] * u
    bconv_ref[...] = (b_g * conv).astype(BF16)
    nconv_ref[0] = u[tm - (CONV_W - 1):tm, :]


def _inproj_sample_body(x_ref, tab_ref, w1_ref, wuq_ref, wukt_ref, qg_ref, kvg_ref, cw_ref, s0_ref, s1_ref,
                        bconv_ref, q_ref, qlat_ref, ckv_ref, kr_ref, u_ref):
    b_g, u, cqn, _, _ = _inproj_common(x_ref, tab_ref, w1_ref, qg_ref, kvg_ref, ckv_ref, kr_ref)
    qf = _dot(cqn, wuq_ref[...])
    for h in range(N_HEADS):
        sl = slice(h * HEAD_W, (h + 1) * HEAD_W)
        qh = _rope(qf[:, sl], tab_ref[0], tab_ref[1], tab_ref[2]).astype(BF16)
        q_ref[:, sl] = qh
        qlat_ref[:, h * KV_LORA:(h + 1) * KV_LORA] = _dot(qh, wukt_ref[h]).astype(BF16)
    conv = cw_ref[0:1, :] * s0_ref[...] + cw_ref[1:2, :] * s1_ref[...] + cw_ref[2:3, :] * u
    bconv_ref[...] = (b_g * conv).astype(BF16)
    u_ref[...] = u


def _inproj_prompt(x, tab, tabt, w1, wuqt, wuk, wuvt, qg, kvg, cw, n_seq, seq):
    t = x.shape[0]
    tm = TM
    tiles_per_seq = seq // tm
    rows = lambda w: pl.BlockSpec((tm, w), lambda i: (i, 0))
    cols = lambda h: pl.BlockSpec((h, tm), lambda i: (0, i))
    return pl.pallas_call(
        functools.partial(_inproj_prompt_body, tiles_per_seq),
        grid=(t // tm,),
        in_specs=[rows(D_MODEL),
                  pl.BlockSpec((6, tm, LANES), lambda i: (0, i % tiles_per_seq, 0)),
                  pl.BlockSpec((2, ROPE_HALF, tm), lambda i: (0, 0, i % tiles_per_seq)),
                  _full(w1.shape), _full(wuqt.shape), _full(wuk.shape), _full(wuvt.shape),
                  _full(qg.shape), _full(kvg.shape), _full(cw.shape)],
        out_specs=[rows(D_CONV), cols(QK_W), rows(QK_W), cols(V_W), rows(KV_LORA), rows(ROPE_DIM),
                   pl.BlockSpec((1, CONV_W - 1, D_CONV), lambda i: (i // tiles_per_seq, 0, 0))],
        out_shape=[jax.ShapeDtypeStruct((t, D_CONV), BF16), jax.ShapeDtypeStruct((QK_W, t), BF16),
                   jax.ShapeDtypeStruct((t, QK_W), BF16), jax.ShapeDtypeStruct((V_W, t), BF16),
                   jax.ShapeDtypeStruct((t, KV_LORA), F32), jax.ShapeDtypeStruct((t, ROPE_DIM), F32),
                   jax.ShapeDtypeStruct((n_seq, CONV_W - 1, D_CONV), F32)],
        scratch_shapes=[pltpu.VMEM((tm + 8, D_CONV), F32)],
        compiler_params=_cparams("arbitrary"),
    )(x, tab, tabt, w1, wuqt, wuk, wuvt, qg, kvg, cw)


def _inproj_sample(x, tab, w1, wuq, wukt, qg, kvg, cw, s0, s1):
    n = x.shape[0]
    args = (x, tab, w1, wuq, wukt, qg, kvg, cw, s0, s1)
    out_shape = [jax.ShapeDtypeStruct((n, D_CONV), BF16), jax.ShapeDtypeStruct((n, QK_W), BF16),
                 jax.ShapeDtypeStruct((n, N_HEADS * KV_LORA), BF16), jax.ShapeDtypeStruct((n, KV_LORA), F32),
                 jax.ShapeDtypeStruct((n, ROPE_DIM), F32), jax.ShapeDtypeStruct((n, D_CONV), F32)]
    return pl.pallas_call(
        _inproj_sample_body,
        grid=(1,),
        in_specs=[_full(a.shape) for a in args],
        out_specs=[_full(s.shape) for s in out_shape],
        out_shape=out_shape,
        compiler_params=_cparams("arbitrary"),
    )(*args)


def _prompt_attention_body(qt_ref, k_ref, vt_ref, ot_ref, acc_ref, m_ref, l_ref, sa_ref, sb_ref):
    tq = qt_ref.shape[1]
    i = pl.program_id(1)

    for hg in range(N_HEADS // ATTN_HEADS_PER_LOOP):
        heads = tuple(range(hg * ATTN_HEADS_PER_LOOP, (hg + 1) * ATTN_HEADS_PER_LOOP))
        acc_ref[...] = jnp.zeros(acc_ref.shape, F32)
        m_ref[...] = jnp.full(m_ref.shape, NEG, F32)
        l_ref[...] = jnp.zeros(l_ref.shape, F32)

        def scores(j, s_ref):
            off = pl.multiple_of(j * tq, tq)
            for idx, h in enumerate(heads):
                s_ref[idx] = _dot(k_ref[pl.ds(off, tq), h * HEAD_W:(h + 1) * HEAD_W],
                                  qt_ref[h * HEAD_W:(h + 1) * HEAD_W, :])

        def consume(j, s_ref, masked):
            off = pl.multiple_of(j * tq, tq)
            for idx, h in enumerate(heads):
                s = s_ref[idx]
                if masked:
                    krow = lax.broadcasted_iota(jnp.int32, (tq, tq), 0)
                    qcol = lax.broadcasted_iota(jnp.int32, (tq, tq), 1)
                    s = jnp.where(krow <= qcol, s, NEG)
                m = m_ref[idx]
                m_new = jnp.maximum(m, jnp.max(s, axis=0, keepdims=True))
                alpha = jnp.exp2(m - m_new)
                p = jnp.exp2(s - m_new)
                m_ref[idx] = m_new
                l_ref[idx] = alpha * l_ref[idx] + jnp.sum(p, axis=0, keepdims=True)
                vblk = vt_ref[h * V_DIM:(h + 1) * V_DIM, pl.ds(off, tq)]
                acc_ref[idx] = alpha * acc_ref[idx] + _dot(vblk, p.astype(BF16))

        scores(0, sa_ref)

        def pair(pp, c):
            j = 2 * pp
            scores(j + 1, sb_ref)
            consume(j, sa_ref, False)
            scores(j + 2, sa_ref)
            consume(j + 1, sb_ref, False)
            return c
        lax.fori_loop(0, i // 2, pair, 0)

        @pl.when(i % 2 == 1)
        def _():
            scores(i, sb_ref)
            consume(i - 1, sa_ref, False)
            consume(i, sb_ref, True)

        @pl.when(i % 2 == 0)
        def _():
            consume(i, sa_ref, True)

        for idx, h in enumerate(heads):
            ot_ref[h * V_DIM:(h + 1) * V_DIM, :] = (acc_ref[idx] / l_ref[idx]).astype(BF16)


def _prompt_attention(qt, k, vt, n_seq, seq):
    t = k.shape[0]
    tq = TQ
    nq = seq // tq
    return pl.pallas_call(
        _prompt_attention_body,
        grid=(n_seq, nq),
        in_specs=[pl.BlockSpec((QK_W, tq), lambda b, i: (0, b * nq + i)),
                  pl.BlockSpec((seq, QK_W), lambda b, i: (b, 0)),
                  pl.BlockSpec((V_W, seq), lambda b, i: (0, b))],
        out_specs=pl.BlockSpec((V_W, tq), lambda b, i: (0, b * nq + i)),
        out_shape=jax.ShapeDtypeStruct((V_W, t), BF16),
        scratch_shapes=[pltpu.VMEM((ATTN_HEADS_PER_LOOP, V_DIM, tq), F32),
                        pltpu.VMEM((ATTN_HEADS_PER_LOOP, 1, tq), F32), pltpu.VMEM((ATTN_HEADS_PER_LOOP, 1, tq), F32),
                        pltpu.VMEM((ATTN_HEADS_PER_LOOP, tq, tq), F32), pltpu.VMEM((ATTN_HEADS_PER_LOOP, tq, tq), F32)],
        compiler_params=_cparams("arbitrary", "arbitrary"),
    )(qt, k, vt)


def _sample_attention_body(n_pages, pt_ref, qlat_ref, q_ref, ckvn_ref, krn_ref, cc_ref, cr_ref, olat_ref,
                           bufc, bufr, kcb, krb, sem):
    b = pl.program_id(0)
    nb = pl.num_programs(0)
    slot = b % 2
    past = n_pages * PAGE_SIZE

    def page_copies(bb, sl, p):
        pg = pt_ref[bb, p]
        pos = pl.ds(pl.multiple_of(p * PAGE_SIZE, PAGE_SIZE), PAGE_SIZE)
        return (pltpu.make_async_copy(cc_ref.at[pg], bufc.at[sl, pos], sem.at[0, sl]),
                pltpu.make_async_copy(cr_ref.at[pg], bufr.at[sl, :, pos], sem.at[1, sl]))

    def issue(bb, sl):
        for p in range(n_pages):
            for c in page_copies(bb, sl, p):
                c.start()

    def wait(sl):
        pltpu.make_async_copy(bufc.at[sl], bufc.at[sl], sem.at[0, sl]).wait()
        pltpu.make_async_copy(bufr.at[sl], bufr.at[sl], sem.at[1, sl]).wait()

    @pl.when(b == 0)
    def _():
        issue(0, 0)

    wait(slot)
    issue(jnp.minimum(b + 1, nb - 1), 1 - slot)

    ql = qlat_ref[0]
    qr = q_ref[0][:, 0:ROPE_DIM]
    cn = ckvn_ref[0]
    rn = krn_ref[0]
    chunk = min(KEY_CHUNK, past)
    scores = []
    for c in range(past // chunk):
        pos = slice(c * chunk, (c + 1) * chunk)
        kcb[pos, :] = bufc[slot, pos, :].astype(BF16)
        krb[:, pos] = bufr[slot, :, pos].astype(BF16)
        scores.append(_dot_nt(ql, kcb[pos, :]) + _dot(qr, krb[:, pos]))
    s_new = (jnp.sum(ql.astype(F32) * cn, axis=-1, keepdims=True)
             + jnp.sum(qr.astype(F32) * rn, axis=-1, keepdims=True))
    m = s_new
    for s in scores:
        m = jnp.maximum(m, jnp.max(s, axis=-1, keepdims=True))
    p_new = jnp.exp2(s_new - m)
    l = p_new
    o = p_new * cn
    for c, s in enumerate(scores):
        p = jnp.exp2(s - m)
        l = l + jnp.sum(p, axis=-1, keepdims=True)
        o = o + _dot(p.astype(BF16), kcb[c * chunk:(c + 1) * chunk, :])
    olat_ref[0] = o / l

    @pl.when(b == nb - 1)
    def _():
        wait(1 - slot)


def _sample_attention(page_table, qlat, q, ckvn, krn, cache_c, cache_rt):
    n, n_pages = page_table.shape
    past = n_pages * PAGE_SIZE
    blk = lambda d1, d2: pl.BlockSpec((1, d1, d2), lambda b, pt: (b, 0, 0))
    grid_spec = pltpu.PrefetchScalarGridSpec(
        num_scalar_prefetch=1,
        grid=(n,),
        in_specs=[blk(N_HEADS, KV_LORA), blk(N_HEADS, HEAD_W), blk(1, KV_LORA), blk(1, ROPE_DIM),
                  pl.BlockSpec(memory_space=pl.ANY), pl.BlockSpec(memory_space=pl.ANY)],
        out_specs=blk(N_HEADS, KV_LORA),
        scratch_shapes=[pltpu.VMEM((2, past, KV_LORA), F32), pltpu.VMEM((2, ROPE_DIM, past), F32),
                        pltpu.VMEM((past, KV_LORA), BF16), pltpu.VMEM((ROPE_DIM, past), BF16),
                        pltpu.SemaphoreType.DMA((2, 2))],
    )
    return pl.pallas_call(
        functools.partial(_sample_attention_body, n_pages),
        grid_spec=grid_spec,
        out_shape=jax.ShapeDtypeStruct((n, N_HEADS, KV_LORA), F32),
        compiler_params=_cparams("arbitrary"),
    )(page_table, qlat.reshape(n, N_HEADS, KV_LORA), q.reshape(n, N_HEADS, HEAD_W),
      ckvn.reshape(n, 1, KV_LORA), krn.reshape(n, 1, ROPE_DIM), cache_c, cache_rt)


def _route(logits):
    lane_i = lax.broadcasted_iota(jnp.int32, (logits.shape[0], LANES), 1)
    lane = lane_i.astype(F32)
    first_at = lambda hit: jnp.min(jnp.where(hit, lane, float(LANES)), axis=-1, keepdims=True)
    lg = jnp.where(lane_i < N_GROUPS, logits[:, 0:LANES], NEG)
    mg = jnp.max(lg, axis=-1, keepdims=True)
    p_grp = 1.0 / jnp.sum(jnp.exp(lg - mg), axis=-1, keepdims=True)
    grp = first_at(lg == mg)
    lane_grp = (lane_i // EXP_PER_GROUP).astype(F32)
    le = jnp.where(lane_grp == grp, logits[:, LANES:2 * LANES], NEG)
    top1 = jnp.max(le, axis=-1, keepdims=True)
    i1 = first_at(le == top1)
    le2 = jnp.where(lane == i1, NEG, le)
    top2 = jnp.max(le2, axis=-1, keepdims=True)
    i2 = first_at(le2 == top2)
    e2 = jnp.exp(top2 - top1)
    w1 = p_grp / (1.0 + e2)
    w2 = p_grp * e2 / (1.0 + e2)
    out = jnp.where(lane_i == R_EID, i1, 0.0)
    out = jnp.where(lane_i == R_EID + 1, i2, out)
    out = jnp.where(lane_i == R_WT, w1, out)
    return jnp.where(lane_i == R_WT + 1, w2, out)


def _add_ranks(route, ltri, count_ref):
    n = route.shape[0]
    lane_i = lax.broadcasted_iota(jnp.int32, (n, LANES), 1)
    lane = lane_i.astype(F32)
    hit1 = lane == route[:, R_EID:R_EID + 1]
    hit2 = lane == route[:, R_EID + 1:R_EID + 2]
    chosen = jnp.where(hit1, 1.0, jnp.where(hit2, 1.0, 0.0))
    before = _dot(ltri, chosen.astype(BF16)) + count_ref[...]
    r1 = jnp.sum(jnp.where(hit1, before, 0.0), axis=-1, keepdims=True)
    r2 = jnp.sum(jnp.where(hit2, before, 0.0), axis=-1, keepdims=True)
    count_ref[...] = count_ref[...] + jnp.sum(chosen, axis=0, keepdims=True)
    route = jnp.where(lane_i == R_RANK, r1, route)
    return jnp.where(lane_i == R_RANK + 1, r2, route)


def _merge_rows(alpha, x, bconv, y_attn, wg_ref, wbc_ref, wo_ref, lg_ref, lb_ref, wr_ref, br_ref):
    g = _dot(x.astype(BF16), wg_ref[...])
    y_conv = _dot(bconv, wbc_ref[...])
    m = jax.nn.sigmoid(g[:, 0:D_MODEL]) * y_conv + jax.nn.sigmoid(g[:, D_MODEL:]) * y_attn
    mix = _dot(m.astype(BF16), wo_ref[...])
    h1 = _layernorm(alpha * x + mix, lg_ref[...], lb_ref[...])
    return h1, _route(_dot(h1.astype(BF16), wr_ref[...]) + br_ref[...])


def _merge_sample_body(alpha, x_ref, bconv_ref, olat_ref, wbd_ref, wba_ref, *refs):
    w_refs, (h1_ref, route_ref) = refs[:-2], refs[-2:]
    o = _dot(olat_ref[...].astype(BF16), wbd_ref[...]).astype(BF16)
    h1_ref[...], route_ref[...] = _merge_rows(alpha, x_ref[...], bconv_ref[...], _dot(o, wba_ref[...]), *w_refs)


def _merge_prompt_body(alpha, n_tiles, x_ref, bconv_ref, ot_ref, h1s_ref, routes_ref, wba_ref, *refs):
    w_refs, (h1t_ref, route_ref, counts_ref, ltri_ref, count_ref) = refs[:-5], refs[-5:]
    i = pl.program_id(0)
    tm = x_ref.shape[0]

    @pl.when(i == 0)
    def _():
        r = lax.broadcasted_iota(jnp.int32, (tm, tm), 0)
        c = lax.broadcasted_iota(jnp.int32, (tm, tm), 1)
        ltri_ref[...] = jnp.where(c < r, 1.0, 0.0).astype(BF16)
        count_ref[...] = jnp.zeros(count_ref.shape, F32)

    @pl.when(i < n_tiles)
    def _():
        y_attn = _dot_tn(ot_ref[...], wba_ref[...])
        h1, route = _merge_rows(alpha, x_ref[...], bconv_ref[...], y_attn, *w_refs)
        for c in range(ROW_TILE):
            _tile_rows(h1t_ref, c, tm)[...] = h1[:, c * LANES:(c + 1) * LANES]
        route_ref[...] = _add_ranks(route, ltri_ref[...], count_ref)

    @pl.when(i == n_tiles)
    def _():
        n_s = h1s_ref.shape[0]
        h1t_ref[...] = jnp.zeros(h1t_ref.shape, F32)
        route_ref[...] = jnp.zeros(route_ref.shape, F32)
        for c in range(ROW_TILE):
            _tile_rows(h1t_ref, c, n_s)[...] = h1s_ref[:, c * LANES:(c + 1) * LANES]
        route_ref[0:n_s, :] = _add_ranks(routes_ref[...], ltri_ref[0:n_s, 0:n_s], count_ref)

    counts_ref[...] = jnp.broadcast_to(count_ref[...], counts_ref.shape)


def _merge_sample(alpha, x, bconv, olat, wbd, wba, ws):
    n = x.shape[0]
    args = (x, bconv, olat, wbd, wba) + tuple(ws)
    out_shape = [jax.ShapeDtypeStruct((n, D_MODEL), F32), jax.ShapeDtypeStruct((n, LANES), F32)]
    return pl.pallas_call(
        functools.partial(_merge_sample_body, alpha),
        grid=(1,),
        in_specs=[_full(a.shape) for a in args],
        out_specs=[_full(s.shape) for s in out_shape],
        out_shape=out_shape,
        compiler_params=_cparams("arbitrary"),
    )(*args)


def _merge_prompt(alpha, x, bconv, ot, h1_s, route_s, wba, ws):
    t = x.shape[0]
    tm = TM
    n_tiles = t // tm
    t_all = t + h1_s.shape[0]
    assert h1_s.shape[0] <= tm
    clamp = lambda i: jnp.minimum(i, n_tiles - 1)
    rows_in = lambda w: pl.BlockSpec((tm, w), lambda i: (clamp(i), 0))
    return pl.pallas_call(
        functools.partial(_merge_prompt_body, alpha, n_tiles),
        grid=(n_tiles + 1,),
        in_specs=[rows_in(D_MODEL), rows_in(D_CONV), pl.BlockSpec((V_W, tm), lambda i: (0, clamp(i))),
                  _full(h1_s.shape), _full(route_s.shape), _full(wba.shape)] + [_full(w.shape) for w in ws],
        out_specs=[pl.BlockSpec((tm * ROW_TILE, LANES), lambda i: (i, 0)), pl.BlockSpec((tm, LANES), lambda i: (i, 0)),
                   _full((SUBLANES, LANES))],
        out_shape=[jax.ShapeDtypeStruct((t_all * ROW_TILE, LANES), F32), jax.ShapeDtypeStruct((t_all, LANES), F32),
                   jax.ShapeDtypeStruct((SUBLANES, LANES), F32)],
        scratch_shapes=[pltpu.VMEM((tm, tm), BF16), pltpu.VMEM((1, LANES), F32)],
        compiler_params=_cparams("arbitrary"),
    )(x, bconv, ot, h1_s, route_s, wba, *ws)


SLOT_GROUP = 16


def _slot_assignments_body(rows_ref, dest_ref, asg_ref):
    def clear_row(j, c):
        for l in range(LANES):
            asg_ref[j * LANES + l] = 0
        return c
    lax.fori_loop(0, rows_ref[1], clear_row, 0)

    def fill_row(j, c):
        for g in range(0, LANES, SLOT_GROUP):
            ds = [dest_ref[j, g + l] for l in range(SLOT_GROUP)]
            for l, d in enumerate(ds):
                asg_ref[d] = j * LANES + g + l
        return c
    lax.fori_loop(0, rows_ref[0], fill_row, 0)


def _slot_assignments(dest, n_blocks):
    n_slots = n_blocks * MOE_BLOCK
    assert n_slots % LANES == 0 and dest.shape[0] % LANES == 0
    dest = dest.reshape(-1, LANES)
    grid_spec = pltpu.PrefetchScalarGridSpec(
        num_scalar_prefetch=2,
        grid=(1,),
        in_specs=[],
        out_specs=pl.BlockSpec(memory_space=pltpu.SMEM),
    )
    return pl.pallas_call(
        _slot_assignments_body,
        grid_spec=grid_spec,
        out_shape=jax.ShapeDtypeStruct((n_slots,), jnp.int32),
        compiler_params=_cparams("arbitrary"),
    )(jnp.array([dest.shape[0], n_slots // LANES], jnp.int32), dest)


def _row_tile(i):
    return pl.ds(pl.multiple_of(i * ROW_TILE, ROW_TILE), ROW_TILE)


def _experts_body(blk_e_ref, used_ref, asg_ref, h1t_ref, wg_ref, wu_ref, wd_ref, ys_ref,
                  xbuf, xb, wgb, wub, wdb, sem_in):
    b = pl.program_id(0)
    ring = lambda bb: lax.rem(bb, GATHER_RING)
    used = used_ref[0]
    blk_rows = MOE_BLOCK * ROW_TILE
    changed = jnp.logical_or(b == 0, blk_e_ref[b] != blk_e_ref[jnp.maximum(b - 1, 0)])

    def in_copy(bb, sl, r):
        tok = lax.shift_right_logical(asg_ref[bb * MOE_BLOCK + r], TOP_K - 1)
        return pltpu.make_async_copy(h1t_ref.at[_row_tile(tok)], xbuf.at[sl, _row_tile(r)], sem_in.at[sl])

    def issue_in(bb, sl):
        for r in range(MOE_BLOCK):
            in_copy(bb, sl, r).start(priority=r % 2)

    def wait_in(sl):
        pltpu.make_async_copy(h1t_ref.at[pl.ds(0, blk_rows)], xbuf.at[sl], sem_in.at[sl]).wait()

    last = used - 1

    @pl.when(jnp.logical_and(b == 0, used > 0))
    def _():
        for a in range(GATHER_RING - 1):
            issue_in(jnp.minimum(a, last), a)

    @pl.when(jnp.logical_and(b < used, changed))
    def _():
        wgb[...] = wg_ref[0].astype(BF16)
        wub[...] = wu_ref[0].astype(BF16)
        wdb[...] = wd_ref[0].astype(BF16)

    @pl.when(b < used)
    def _():
        wait_in(ring(b))
        for c in range(ROW_TILE):
            xb[:, c * LANES:(c + 1) * LANES] = _tile_rows(xbuf.at[ring(b)], c, MOE_BLOCK)[...].astype(BF16)
        ahead = b + GATHER_RING - 1
        issue_in(jnp.minimum(ahead, last), ring(ahead))
        x = xb[...]
        g = _dot(x, wgb[...])
        u = _dot(x, wub[...])
        h = (g * jax.nn.sigmoid(g) * u).astype(BF16)
        y = _dot(h, wdb[...])
        for c in range(ROW_TILE):
            _tile_rows(ys_ref, c, MOE_BLOCK)[...] = y[:, c * LANES:(c + 1) * LANES]

        @pl.when(b == last)
        def _():
            for a in range(1, GATHER_RING):
                wait_in(ring(b + a))

    @pl.when(b >= used)
    def _():
        ys_ref[...] = jnp.zeros(ys_ref.shape, F32)


def _experts(blk_e, used, asg, h1t, wg, wu, wd):
    n_blocks = blk_e.shape[0]
    blk_rows = MOE_BLOCK * ROW_TILE
    by_expert = lambda b, e, u, a: (e[b], 0, 0)
    grid_spec = pltpu.PrefetchScalarGridSpec(
        num_scalar_prefetch=3,
        grid=(n_blocks,),
        in_specs=[pl.BlockSpec(memory_space=pl.ANY),
                  pl.BlockSpec((1, D_MODEL, D_EXPERT), by_expert), pl.BlockSpec((1, D_MODEL, D_EXPERT), by_expert),
                  pl.BlockSpec((1, D_EXPERT, D_MODEL), by_expert)],
        out_specs=pl.BlockSpec((blk_rows, LANES), lambda b, e, u, a: (b, 0)),
        scratch_shapes=[pltpu.VMEM((GATHER_RING, blk_rows, LANES), F32),
                        pltpu.VMEM((MOE_BLOCK, D_MODEL), BF16),
                        pltpu.VMEM((D_MODEL, D_EXPERT), BF16), pltpu.VMEM((D_MODEL, D_EXPERT), BF16),
                        pltpu.VMEM((D_EXPERT, D_MODEL), BF16),
                        pltpu.SemaphoreType.DMA((GATHER_RING,))],
    )
    return pl.pallas_call(
        _experts_body,
        grid_spec=grid_spec,
        out_shape=jax.ShapeDtypeStruct((n_blocks * blk_rows, LANES), F32),
        compiler_params=_cparams("arbitrary"),
    )(blk_e, used, asg, h1t, wg, wu, wd)


def _combine_body(alpha, tok0, dest_ref, ys_ref, h1t_ref, route_ref, lg_ref, lb_ref, out_ref, gbuf, sem):
    i = pl.program_id(0)
    nt = pl.num_programs(0)
    slot = i % 2
    tm = out_ref.shape[0]

    def issue(ii, sl):
        base = (tok0 + ii * tm) * TOP_K
        for r in range(tm):
            for k in range(TOP_K):
                d = dest_ref[base + r * TOP_K + k]
                pltpu.make_async_copy(ys_ref.at[_row_tile(d)], gbuf.at[sl, k, _row_tile(r)],
                                      sem.at[sl]).start(priority=k)

    def wait(sl):
        for k in range(TOP_K):
            pltpu.make_async_copy(ys_ref.at[pl.ds(0, tm * ROW_TILE)], gbuf.at[sl, k], sem.at[sl]).wait()

    @pl.when(i == 0)
    def _():
        issue(0, 0)

    wait(slot)
    issue(jnp.minimum(i + 1, nt - 1), 1 - slot)

    route = route_ref[...]
    w0 = jnp.broadcast_to(route[:, R_WT:R_WT + 1], (tm, LANES))
    w1 = jnp.broadcast_to(route[:, R_WT + 1:R_WT + 2], (tm, LANES))
    z = [alpha * _tile_rows(h1t_ref, c, tm)[...]
         + w0 * _tile_rows(gbuf.at[slot, 0], c, tm)[...] + w1 * _tile_rows(gbuf.at[slot, 1], c, tm)[...]
         for c in range(ROW_TILE)]
    mu = sum(jnp.sum(zc, axis=-1, keepdims=True) for zc in z) / D_MODEL
    zc = [v - mu for v in z]
    var = sum(jnp.sum(v * v, axis=-1, keepdims=True) for v in zc) / D_MODEL
    rstd = lax.rsqrt(var + LN_EPS)
    for c in range(ROW_TILE):
        sl = slice(c * LANES, (c + 1) * LANES)
        out_ref[:, sl] = zc[c] * rstd * lg_ref[:, sl] + lb_ref[:, sl]

    @pl.when(i == nt - 1)
    def _():
        wait(1 - slot)


def _combine(alpha, tok0, n_tok, tm, dest, ys, h1t, route, lg, lb):
    assert n_tok % tm == 0 and tok0 % tm == 0
    blk0 = tok0 // tm
    grid_spec = pltpu.PrefetchScalarGridSpec(
        num_scalar_prefetch=1,
        grid=(n_tok // tm,),
        in_specs=[pl.BlockSpec(memory_space=pl.ANY),
                  pl.BlockSpec((tm * ROW_TILE, LANES), lambda i, d: (blk0 + i, 0)),
                  pl.BlockSpec((tm, LANES), lambda i, d: (blk0 + i, 0)),
                  pl.BlockSpec(lg.shape, lambda i, d: (0, 0)), pl.BlockSpec(lb.shape, lambda i, d: (0, 0))],
        out_specs=pl.BlockSpec((tm, D_MODEL), lambda i, d: (i, 0)),
        scratch_shapes=[pltpu.VMEM((2, TOP_K, tm * ROW_TILE, LANES), F32), pltpu.SemaphoreType.DMA((2,))],
    )
    return pl.pallas_call(
        functools.partial(_combine_body, alpha, tok0),
        grid_spec=grid_spec,
        out_shape=jax.ShapeDtypeStruct((n_tok, D_MODEL), F32),
        compiler_params=_cparams("arbitrary"),
    )(dest, ys, h1t, route, lg, lb)


def _rope_tables(pos):
    inv = ROPE_BASE ** (-(jnp.arange(ROPE_HALF, dtype=F32) * 2.0 / ROPE_DIM))
    ang = pos.astype(F32)[:, None] * inv[None, :]
    cos, sin = jnp.cos(ang), jnp.sin(ang)
    n = pos.shape[0]
    a = jnp.concatenate([cos, cos, jnp.ones((n, LANES - ROPE_DIM), F32)], axis=1)
    bm = jnp.concatenate([-sin, jnp.zeros((n, LANES - ROPE_HALF), F32)], axis=1)
    cp = jnp.concatenate([jnp.zeros((n, ROPE_HALF), F32), sin, jnp.zeros((n, LANES - ROPE_DIM), F32)], axis=1)
    k_tabs = jnp.stack([a, bm, cp])
    return jnp.concatenate([k_tabs * Q_SCALE, k_tabs]), jnp.stack([cos.T, sin.T]) * Q_SCALE


def _head_blocks(w_rope, w_nope):
    k = w_nope.shape[0]
    pad = jnp.zeros((k, N_HEADS, HEAD_W - QK_DIM), w_nope.dtype)
    return jnp.concatenate([w_rope, w_nope, pad], axis=-1).reshape(k, QK_W)


def _pack_weights(w_in, w_uq, w_uk, w_uv, router_w_group, router_b_group, router_w_expert, router_b_expert):
    d = w_in.shape[0]
    c_kr = 3 * D_CONV + Q_LORA + KV_LORA
    w1 = jnp.concatenate([w_in[:, :c_kr], w_in[:, c_kr:c_kr + ROPE_DIM],
                          jnp.zeros((d, LANES - ROPE_DIM), w_in.dtype)], axis=1).astype(BF16)
    wg = w_in[:, c_kr + ROPE_DIM:].astype(BF16)
    uq = w_uq.reshape(Q_LORA, N_HEADS, QK_DIM)
    wuq = _head_blocks(uq[..., NOPE_DIM:], uq[..., :NOPE_DIM]).astype(BF16)
    wuk = _head_blocks(jnp.zeros((KV_LORA, N_HEADS, ROPE_DIM), w_uk.dtype), w_uk).astype(BF16)
    wukt = jnp.transpose(wuk.reshape(KV_LORA, N_HEADS, HEAD_W), (1, 2, 0))
    wuvt = w_uv.reshape(KV_LORA, V_W).T.astype(BF16)
    eye = jnp.eye(N_HEADS, dtype=w_uv.dtype)
    wbd = jnp.einsum('chd,hg->hcgd', w_uv, eye).reshape(N_HEADS * KV_LORA, V_W).astype(BF16)
    wr = jnp.zeros((d, ROUTE_W), F32)
    wr = wr.at[:, 0:N_GROUPS].set(router_w_group).at[:, LANES:LANES + N_EXPERTS].set(router_w_expert).astype(BF16)
    br = jnp.zeros((1, ROUTE_W), F32)
    br = br.at[0, 0:N_GROUPS].set(router_b_group).at[0, LANES:LANES + N_EXPERTS].set(router_b_expert)
    return w1, wg, wuq, wuq.T, wuk, wukt, wuvt, wbd, wr, br


def _dispatch_plan(route, counts, n_blocks):
    counts = counts[0, 0:N_EXPERTS].astype(jnp.int32)
    pcounts = (counts + MOE_BLOCK - 1) // MOE_BLOCK * MOE_BLOCK
    pend = jnp.cumsum(pcounts).astype(jnp.int32)
    pstart = (pend - pcounts).astype(F32)
    eid = route[:, R_EID:R_EID + TOP_K]
    rank = route[:, R_RANK:R_RANK + TOP_K]
    onehot = eid[:, :, None] == jnp.arange(N_EXPERTS, dtype=F32)[None, None, :]
    dest = (jnp.sum(jnp.where(onehot, pstart[None, None, :], 0.0), axis=-1) + rank).astype(jnp.int32).reshape(-1)
    blk_start = jnp.arange(n_blocks, dtype=jnp.int32) * MOE_BLOCK
    blk_e = jnp.sum((blk_start[:, None] >= pend[None, :]).astype(jnp.int32), axis=1)
    blk_e = jnp.minimum(blk_e, N_EXPERTS - 1).astype(jnp.int32)
    used = (pend[-1:] // MOE_BLOCK).astype(jnp.int32)
    return dest, blk_e, used


def kernel(x_prompt, x_sample, cache_ckv, cache_krope, state_conv, page_table, w_in, conv_w, q_norm_g, w_uq,
           kv_norm_g, w_uk, w_uv, w_br_conv, w_br_attn, w_o, ln1_g, ln1_b, router_w_group, router_b_group,
           router_w_expert, router_b_expert, w_gate, w_up, w_down, ln2_g, ln2_b):
    depth = w_in.shape[0]
    alpha = (2 * depth) ** 0.25
    n_p, t_p, _ = x_prompt.shape
    n_s, t_s, _ = x_sample.shape
    assert t_s == 1 and t_p % TM == 0 and n_s % LANES == 0
    rows_p = n_p * t_p
    rows_all = rows_p + n_s
    past = page_table.shape[1] * PAGE_SIZE
    tab_p, tabt_p = _rope_tables(jnp.arange(t_p))
    tab_s, _ = _rope_tables(jnp.full((n_s,), past, jnp.int32))
    n_blocks = -(-(rows_all * TOP_K) // MOE_BLOCK) + N_EXPERTS

    h_p = x_prompt.reshape(rows_p, D_MODEL)
    h_s = x_sample.reshape(n_s, D_MODEL)
    ckv_p, kr_p, cv_p, ckv_s, kr_s, cv_s = [], [], [], [], [], []
    row = lambda v: v.reshape(1, -1)
    for l in range(depth):
        w1, wg, wuq, wuqt, wuk, wukt, wuvt, wbd, wr, br = _pack_weights(
            w_in[l], w_uq[l], w_uk[l], w_uv[l], router_w_group[l], router_b_group[l],
            router_w_expert[l], router_b_expert[l])
        qg, kvg = row(q_norm_g[l]), row(kv_norm_g[l])
        wba = w_br_attn[l].astype(BF16)
        merge_w = (wg, w_br_conv[l].astype(BF16), w_o[l].astype(BF16), row(ln1_g[l]), row(ln1_b[l]), wr, br)

        bconv, qt, k, vt, ckv, kr, nconv = _inproj_prompt(h_p, tab_p, tabt_p, w1, wuqt, wuk, wuvt, qg, kvg, conv_w[l],
                                                          n_p, t_p)
        ot = _prompt_attention(qt, k, vt, n_p, t_p)

        st = state_conv[l]
        bconv_s, q_s, qlat, ckvn_s, krn_s, u_s = _inproj_sample(
            h_s, tab_s, w1, wuq, wukt, qg, kvg, conv_w[l], st[:, 0], st[:, 1])
        olat = _sample_attention(page_table, qlat, q_s, ckvn_s, krn_s, cache_ckv[l],
                                 jnp.swapaxes(cache_krope[l], 1, 2))
        h1_s, route_s = _merge_sample(alpha, h_s, bconv_s, olat.reshape(n_s, N_HEADS * KV_LORA), wbd, wba, merge_w)
        h1t, route, counts = _merge_prompt(alpha, h_p, bconv, ot, h1_s, route_s, wba, merge_w)

        dest, blk_e, used = _dispatch_plan(route, counts, n_blocks)
        asg = _slot_assignments(dest, n_blocks)
        ys = _experts(blk_e, used, asg, h1t, w_gate[l], w_up[l], w_down[l])
        ln2 = (row(ln2_g[l]), row(ln2_b[l]))
        h_p = _combine(alpha, 0, rows_p, TM_COMBINE, dest, ys, h1t, route, *ln2)
        h_s = _combine(alpha, rows_p, n_s, n_s, dest, ys, h1t, route, *ln2)

        ckv_p.append(ckv.reshape(n_p, t_p, KV_LORA))
        kr_p.append(kr.reshape(n_p, t_p, ROPE_DIM))
        cv_p.append(nconv)
        ckv_s.append(ckvn_s.reshape(n_s, 1, KV_LORA))
        kr_s.append(krn_s.reshape(n_s, 1, ROPE_DIM))
        cv_s.append(jnp.stack([st[:, 1], u_s], axis=1))
    return (h_p.reshape(n_p, t_p, D_MODEL), h_s.reshape(n_s, 1, D_MODEL), jnp.stack(ckv_p), jnp.stack(kr_p),
            jnp.stack(cv_p), jnp.stack(ckv_s), jnp.stack(kr_s), jnp.stack(cv_s))
```

```python
import functools

import jax
import jax.numpy as jnp
from jax import lax
from jax.experimental import pallas as pl
from jax.experimental.pallas import tpu as pltpu

F32 = jnp.float32
BF16 = jnp.bfloat16

D_MODEL = 1024
D_CONV = 512
CONV_W = 3
N_HEADS = 8
Q_LORA = 384
KV_LORA = 256
NOPE_DIM = 64
ROPE_DIM = 32
ROPE_HALF = ROPE_DIM // 2
V_DIM = 64
QK_DIM = NOPE_DIM + ROPE_DIM
ROPE_BASE = 10000.0
ATTN_SCALE = QK_DIM ** -0.5
LOG2E = 1.4426950408889634
Q_SCALE = ATTN_SCALE * LOG2E
N_GROUPS = 4
EXP_PER_GROUP = 8
N_EXPERTS = N_GROUPS * EXP_PER_GROUP
TOP_K = 2
D_EXPERT = 512
MOE_BLOCK = 256
PAGE_SIZE = 128
LN_EPS = 1e-5
RMS_EPS = 1e-6

LANES = 128
SUBLANES = 8
ROW_TILE = D_MODEL // LANES
assert ROW_TILE == SUBLANES
HEAD_W = LANES
QK_W = N_HEADS * HEAD_W
V_W = N_HEADS * V_DIM
_C_BG, _C_CG, _C_H = 0, D_CONV, 2 * D_CONV
_C_CQ = 3 * D_CONV
_C_CKV = _C_CQ + Q_LORA
_C_KR = _C_CKV + KV_LORA
W1_COLS = _C_KR + LANES
ROUTE_W = 2 * LANES
R_EID, R_WT, R_RANK = 0, 2, 4
NEG = -1e30
VMEM_LIMIT = 56 * 1024 * 1024

TM = 512
KEY_CHUNK = 2048
ATTN_HEADS_PER_LOOP = 4
TQ = 512
TM_COMBINE = 256
GATHER_RING = 4


def _cparams(*sem):
    return pltpu.CompilerParams(dimension_semantics=sem, vmem_limit_bytes=VMEM_LIMIT)


def _dot(a, b):
    return jnp.dot(a, b, preferred_element_type=F32)


def _dot_nt(a, b):
    return lax.dot_general(a, b, (((1,), (1,)), ((), ())), preferred_element_type=F32)


def _dot_tn(a, b):
    return lax.dot_general(a, b, (((0,), (0,)), ((), ())), preferred_element_type=F32)


def _rms(x, g):
    return x * lax.rsqrt(jnp.mean(x * x, axis=-1, keepdims=True) + RMS_EPS) * g


def _layernorm(x, g, b):
    mu = jnp.mean(x, axis=-1, keepdims=True)
    xc = x - mu
    var = jnp.mean(xc * xc, axis=-1, keepdims=True)
    return xc * lax.rsqrt(var + LN_EPS) * g + b


def _rope(xh, a, bm, cp):
    return xh * a + pltpu.roll(xh, LANES - ROPE_HALF, 1) * bm + pltpu.roll(xh, ROPE_HALF, 1) * cp


def _full(shape):
    nd = len(shape)
    return pl.BlockSpec(shape, lambda *_: (0,) * nd)


def _tile_rows(ref, c, n):
    return ref.at[pl.ds(c, n, stride=ROW_TILE), :]


def _inproj_common(x_ref, tab_ref, w1_ref, qg_ref, kvg_ref, ckv_ref):
    xb = x_ref[...].astype(BF16)

    def seg(lo, hi):
        return _dot(xb, w1_ref[:, lo:hi])

    b_g = seg(_C_BG, _C_CG)
    u = seg(_C_CG, _C_H) * seg(_C_H, _C_CQ)
    cqn = _rms(seg(_C_CQ, _C_CKV), qg_ref[...]).astype(BF16)
    ckvn = _rms(seg(_C_CKV, _C_KR), kvg_ref[...])
    ckv_ref[...] = ckvn
    krr = _rope(seg(_C_KR, W1_COLS), tab_ref[3], tab_ref[4], tab_ref[5])
    return b_g, u, cqn, ckvn.astype(BF16), krr


def _inproj_prompt_body(tiles_per_seq, x_ref, tab_ref, tabt_ref, w1_ref, wuqt_ref, wuk_ref, wuvt_ref, qg_ref, kvg_ref,
                        cw_ref, bconv_ref, qt_ref, k_ref, vt_ref, ckv_ref, krt_ref, nconv_ref, uext_ref):
    tm = x_ref.shape[0]
    b_g, u, cqn, cb, krr = _inproj_common(x_ref, tab_ref, w1_ref, qg_ref, kvg_ref, ckv_ref)
    krt_ref[0] = krr.T[0:ROPE_DIM, :]
    qft = _dot_nt(wuqt_ref[...], cqn)
    cos, sin = tabt_ref[0], tabt_ref[1]
    for h in range(N_HEADS):
        r0 = h * HEAD_W
        x1 = qft[r0:r0 + ROPE_HALF, :]
        x2 = qft[r0 + ROPE_HALF:r0 + ROPE_DIM, :]
        qt_ref[r0:r0 + ROPE_HALF, :] = (x1 * cos - x2 * sin).astype(BF16)
        qt_ref[r0 + ROPE_HALF:r0 + ROPE_DIM, :] = (x1 * sin + x2 * cos).astype(BF16)
        qt_ref[r0 + ROPE_DIM:r0 + HEAD_W, :] = (qft[r0 + ROPE_DIM:r0 + HEAD_W, :] * Q_SCALE).astype(BF16)
    kn = _dot(cb, wuk_ref[...])
    for h in range(N_HEADS):
        sl = slice(h * HEAD_W, (h + 1) * HEAD_W)
        k_ref[:, sl] = (kn[:, sl] + krr).astype(BF16)
    vt_ref[...] = _dot_nt(wuvt_ref[...], cb).astype(BF16)

    first = (pl.program_id(0) % tiles_per_seq) == 0

    @pl.when(first)
    def _():
        uext_ref[0:8, :] = jnp.zeros((8, D_CONV), F32)

    @pl.when(jnp.logical_not(first))
    def _():
        uext_ref[0:8, :] = uext_ref[tm:tm + 8, :]

    uext_ref[8:8 + tm, :] = u
    conv = cw_ref[0:1, :] * uext_ref[6:6 + tm, :] + cw_ref[1:2, :] * uext_ref[7:7 + tm, :] + cw_ref[2:3, :] * u
    bconv_ref[...] = (b_g * conv).astype(BF16)
    nconv_ref[0] = u[tm - (CONV_W - 1):tm, :]


def _inproj_sample_body(x_ref, tab_ref, w1_ref, wuq_ref, wukt_ref, qg_ref, kvg_ref, cw_ref, s0_ref, s1_ref,
                        bconv_ref, q_ref, qlat_ref, ckv_ref, kr_ref, u_ref):
    b_g, u, cqn, _, krr = _inproj_common(x_ref, tab_ref, w1_ref, qg_ref, kvg_ref, ckv_ref)
    kr_ref[...] = krr[:, 0:ROPE_DIM]
    qf = _dot(cqn, wuq_ref[...])
    for h in range(N_HEADS):
        sl = slice(h * HEAD_W, (h + 1) * HEAD_W)
        qh = _rope(qf[:, sl], tab_ref[0], tab_ref[1], tab_ref[2]).astype(BF16)
        q_ref[:, sl] = qh
        qlat_ref[:, h * KV_LORA:(h + 1) * KV_LORA] = _dot(qh, wukt_ref[h]).astype(BF16)
    conv = cw_ref[0:1, :] * s0_ref[...] + cw_ref[1:2, :] * s1_ref[...] + cw_ref[2:3, :] * u
    bconv_ref[...] = (b_g * conv).astype(BF16)
    u_ref[...] = u


def _inproj_prompt(x, tab, tabt, w1, wuqt, wuk, wuvt, qg, kvg, cw, n_seq, seq):
    t = x.shape[0]
    tm = TM
    tiles_per_seq = seq // tm
    rows = lambda w: pl.BlockSpec((tm, w), lambda i: (i, 0))
    cols = lambda h: pl.BlockSpec((h, tm), lambda i: (0, i))
    return pl.pallas_call(
        functools.partial(_inproj_prompt_body, tiles_per_seq),
        grid=(t // tm,),
        in_specs=[rows(D_MODEL),
                  pl.BlockSpec((6, tm, LANES), lambda i: (0, i % tiles_per_seq, 0)),
                  pl.BlockSpec((2, ROPE_HALF, tm), lambda i: (0, 0, i % tiles_per_seq)),
                  _full(w1.shape), _full(wuqt.shape), _full(wuk.shape), _full(wuvt.shape),
                  _full(qg.shape), _full(kvg.shape), _full(cw.shape)],
        out_specs=[rows(D_CONV), cols(QK_W), rows(QK_W), cols(V_W), rows(KV_LORA),
                   pl.BlockSpec((1, ROPE_DIM, tm), lambda i: (i // tiles_per_seq, 0, i % tiles_per_seq)),
                   pl.BlockSpec((1, CONV_W - 1, D_CONV), lambda i: (i // tiles_per_seq, 0, 0))],
        out_shape=[jax.ShapeDtypeStruct((t, D_CONV), BF16), jax.ShapeDtypeStruct((QK_W, t), BF16),
                   jax.ShapeDtypeStruct((t, QK_W), BF16), jax.ShapeDtypeStruct((V_W, t), BF16),
                   jax.ShapeDtypeStruct((t, KV_LORA), F32), jax.ShapeDtypeStruct((n_seq, ROPE_DIM, seq), F32),
                   jax.ShapeDtypeStruct((n_seq, CONV_W - 1, D_CONV), F32)],
        scratch_shapes=[pltpu.VMEM((tm + 8, D_CONV), F32)],
        compiler_params=_cparams("arbitrary"),
    )(x, tab, tabt, w1, wuqt, wuk, wuvt, qg, kvg, cw)


def _inproj_sample(x, tab, w1, wuq, wukt, qg, kvg, cw, s0, s1):
    n = x.shape[0]
    args = (x, tab, w1, wuq, wukt, qg, kvg, cw, s0, s1)
    out_shape = [jax.ShapeDtypeStruct((n, D_CONV), BF16), jax.ShapeDtypeStruct((n, QK_W), BF16),
                 jax.ShapeDtypeStruct((n, N_HEADS * KV_LORA), BF16), jax.ShapeDtypeStruct((n, KV_LORA), F32),
                 jax.ShapeDtypeStruct((n, ROPE_DIM), F32), jax.ShapeDtypeStruct((n, D_CONV), F32)]
    return pl.pallas_call(
        _inproj_sample_body,
        grid=(1,),
        in_specs=[_full(a.shape) for a in args],
        out_specs=[_full(s.shape) for s in out_shape],
        out_shape=out_shape,
        compiler_params=_cparams("arbitrary"),
    )(*args)


def _prompt_attention_body(qt_ref, k_ref, vt_ref, ot_ref, acc_ref, m_ref, l_ref, sa_ref, sb_ref):
    tq = qt_ref.shape[1]
    i = pl.program_id(1)

    for hg in range(N_HEADS // ATTN_HEADS_PER_LOOP):
        heads = tuple(range(hg * ATTN_HEADS_PER_LOOP, (hg + 1) * ATTN_HEADS_PER_LOOP))
        acc_ref[...] = jnp.zeros(acc_ref.shape, F32)
        m_ref[...] = jnp.full(m_ref.shape, NEG, F32)
        l_ref[...] = jnp.zeros(l_ref.shape, F32)

        def scores(j, s_ref):
            off = pl.multiple_of(j * tq, tq)
            for idx, h in enumerate(heads):
                s_ref[idx] = _dot(k_ref[pl.ds(off, tq), h * HEAD_W:(h + 1) * HEAD_W],
                                  qt_ref[h * HEAD_W:(h + 1) * HEAD_W, :])

        def consume(j, s_ref, masked):
            off = pl.multiple_of(j * tq, tq)
            for idx, h in enumerate(heads):
                s = s_ref[idx]
                if masked:
                    krow = lax.broadcasted_iota(jnp.int32, (tq, tq), 0)
                    qcol = lax.broadcasted_iota(jnp.int32, (tq, tq), 1)
                    s = jnp.where(krow <= qcol, s, NEG)
                m = m_ref[idx]
                m_new = jnp.maximum(m, jnp.max(s, axis=0, keepdims=True))
                alpha = jnp.exp2(m - m_new)
                p = jnp.exp2(s - m_new)
                m_ref[idx] = m_new
                l_ref[idx] = alpha * l_ref[idx] + jnp.sum(p, axis=0, keepdims=True)
                vblk = vt_ref[h * V_DIM:(h + 1) * V_DIM, pl.ds(off, tq)]
                acc_ref[idx] = alpha * acc_ref[idx] + _dot(vblk, p.astype(BF16))

        scores(0, sa_ref)

        def pair(pp, c):
            j = 2 * pp
            scores(j + 1, sb_ref)
            consume(j, sa_ref, False)
            scores(j + 2, sa_ref)
            consume(j + 1, sb_ref, False)
            return c
        lax.fori_loop(0, i // 2, pair, 0)

        @pl.when(i % 2 == 1)
        def _():
            scores(i, sb_ref)
            consume(i - 1, sa_ref, False)
            consume(i, sb_ref, True)

        @pl.when(i % 2 == 0)
        def _():
            consume(i, sa_ref, True)

        for idx, h in enumerate(heads):
            ot_ref[h * V_DIM:(h + 1) * V_DIM, :] = (acc_ref[idx] / l_ref[idx]).astype(BF16)


def _prompt_attention(qt, k, vt, n_seq, seq):
    t = k.shape[0]
    tq = TQ
    nq = seq // tq
    return pl.pallas_call(
        _prompt_attention_body,
        grid=(n_seq, nq),
        in_specs=[pl.BlockSpec((QK_W, tq), lambda b, i: (0, b * nq + i)),
                  pl.BlockSpec((seq, QK_W), lambda b, i: (b, 0)),
                  pl.BlockSpec((V_W, seq), lambda b, i: (0, b))],
        out_specs=pl.BlockSpec((V_W, tq), lambda b, i: (0, b * nq + i)),
        out_shape=jax.ShapeDtypeStruct((V_W, t), BF16),
        scratch_shapes=[pltpu.VMEM((ATTN_HEADS_PER_LOOP, V_DIM, tq), F32),
                        pltpu.VMEM((ATTN_HEADS_PER_LOOP, 1, tq), F32), pltpu.VMEM((ATTN_HEADS_PER_LOOP, 1, tq), F32),
                        pltpu.VMEM((ATTN_HEADS_PER_LOOP, tq, tq), F32), pltpu.VMEM((ATTN_HEADS_PER_LOOP, tq, tq), F32)],
        compiler_params=_cparams("arbitrary", "arbitrary"),
    )(qt, k, vt)


def _sample_attention_body(n_pages, pt_ref, qlat_ref, q_ref, ckvn_ref, krn_ref, cc_ref, cr_ref, olat_ref,
                           bufc, bufr, kcb, krb, sem):
    b = pl.program_id(0)
    nb = pl.num_programs(0)
    slot = b % 2
    past = n_pages * PAGE_SIZE

    def page_copies(bb, sl, p):
        pg = pt_ref[bb, p]
        pos = pl.ds(pl.multiple_of(p * PAGE_SIZE, PAGE_SIZE), PAGE_SIZE)
        return (pltpu.make_async_copy(cc_ref.at[pg], bufc.at[sl, pos], sem.at[0, sl]),
                pltpu.make_async_copy(cr_ref.at[pg], bufr.at[sl, :, pos], sem.at[1, sl]))

    def issue(bb, sl):
        for p in range(n_pages):
            for c in page_copies(bb, sl, p):
                c.start()

    def wait(sl):
        pltpu.make_async_copy(bufc.at[sl], bufc.at[sl], sem.at[0, sl]).wait()
        pltpu.make_async_copy(bufr.at[sl], bufr.at[sl], sem.at[1, sl]).wait()

    @pl.when(b == 0)
    def _():
        issue(0, 0)

    wait(slot)
    issue(jnp.minimum(b + 1, nb - 1), 1 - slot)

    ql = qlat_ref[0]
    qr = q_ref[0][:, 0:ROPE_DIM]
    cn = ckvn_ref[0]
    rn = krn_ref[0]
    chunk = min(KEY_CHUNK, past)
    scores = []
    for c in range(past // chunk):
        pos = slice(c * chunk, (c + 1) * chunk)
        kcb[pos, :] = bufc[slot, pos, :].astype(BF16)
        krb[:, pos] = bufr[slot, :, pos].astype(BF16)
        scores.append(_dot_nt(ql, kcb[pos, :]) + _dot(qr, krb[:, pos]))
    s_new = (jnp.sum(ql.astype(F32) * cn, axis=-1, keepdims=True)
             + jnp.sum(qr.astype(F32) * rn, axis=-1, keepdims=True))
    m = s_new
    for s in scores:
        m = jnp.maximum(m, jnp.max(s, axis=-1, keepdims=True))
    p_new = jnp.exp2(s_new - m)
    l = p_new
    o = p_new * cn
    for c, s in enumerate(scores):
        p = jnp.exp2(s - m)
        l = l + jnp.sum(p, axis=-1, keepdims=True)
        o = o + _dot(p.astype(BF16), kcb[c * chunk:(c + 1) * chunk, :])
    olat_ref[0] = o / l

    @pl.when(b == nb - 1)
    def _():
        wait(1 - slot)


def _sample_attention(page_table, qlat, q, ckvn, krn, cache_c, cache_rt):
    n, n_pages = page_table.shape
    past = n_pages * PAGE_SIZE
    blk = lambda d1, d2: pl.BlockSpec((1, d1, d2), lambda b, pt: (b, 0, 0))
    grid_spec = pltpu.PrefetchScalarGridSpec(
        num_scalar_prefetch=1,
        grid=(n,),
        in_specs=[blk(N_HEADS, KV_LORA), blk(N_HEADS, HEAD_W), blk(1, KV_LORA), blk(1, ROPE_DIM),
                  pl.BlockSpec(memory_space=pl.ANY), pl.BlockSpec(memory_space=pl.ANY)],
        out_specs=blk(N_HEADS, KV_LORA),
        scratch_shapes=[pltpu.VMEM((2, past, KV_LORA), F32), pltpu.VMEM((2, ROPE_DIM, past), F32),
                        pltpu.VMEM((past, KV_LORA), BF16), pltpu.VMEM((ROPE_DIM, past), BF16),
                        pltpu.SemaphoreType.DMA((2, 2))],
    )
    return pl.pallas_call(
        functools.partial(_sample_attention_body, n_pages),
        grid_spec=grid_spec,
        out_shape=jax.ShapeDtypeStruct((n, N_HEADS, KV_LORA), F32),
        compiler_params=_cparams("arbitrary"),
    )(page_table, qlat.reshape(n, N_HEADS, KV_LORA), q.reshape(n, N_HEADS, HEAD_W),
      ckvn.reshape(n, 1, KV_LORA), krn.reshape(n, 1, ROPE_DIM), cache_c, cache_rt)


def _route(logits):
    lane_i = lax.broadcasted_iota(jnp.int32, (logits.shape[0], LANES), 1)
    lane = lane_i.astype(F32)
    first_at = lambda hit: jnp.min(jnp.where(hit, lane, float(LANES)), axis=-1, keepdims=True)
    lg = jnp.where(lane_i < N_GROUPS, logits[:, 0:LANES], NEG)
    mg = jnp.max(lg, axis=-1, keepdims=True)
    p_grp = 1.0 / jnp.sum(jnp.exp(lg - mg), axis=-1, keepdims=True)
    grp = first_at(lg == mg)
    lane_grp = (lane_i // EXP_PER_GROUP).astype(F32)
    le = jnp.where(lane_grp == grp, logits[:, LANES:2 * LANES], NEG)
    top1 = jnp.max(le, axis=-1, keepdims=True)
    i1 = first_at(le == top1)
    le2 = jnp.where(lane == i1, NEG, le)
    top2 = jnp.max(le2, axis=-1, keepdims=True)
    i2 = first_at(le2 == top2)
    e2 = jnp.exp(top2 - top1)
    w1 = p_grp / (1.0 + e2)
    w2 = p_grp * e2 / (1.0 + e2)
    out = jnp.where(lane_i == R_EID, i1, 0.0)
    out = jnp.where(lane_i == R_EID + 1, i2, out)
    out = jnp.where(lane_i == R_WT, w1, out)
    return jnp.where(lane_i == R_WT + 1, w2, out)


def _add_ranks(route, ltri, count_ref):
    n = route.shape[0]
    lane_i = lax.broadcasted_iota(jnp.int32, (n, LANES), 1)
    lane = lane_i.astype(F32)
    hit1 = lane == route[:, R_EID:R_EID + 1]
    hit2 = lane == route[:, R_EID + 1:R_EID + 2]
    chosen = jnp.where(hit1, 1.0, jnp.where(hit2, 1.0, 0.0))
    before = _dot(ltri, chosen.astype(BF16)) + count_ref[...]
    r1 = jnp.sum(jnp.where(hit1, before, 0.0), axis=-1, keepdims=True)
    r2 = jnp.sum(jnp.where(hit2, before, 0.0), axis=-1, keepdims=True)
    count_ref[...] = count_ref[...] + jnp.sum(chosen, axis=0, keepdims=True)
    route = jnp.where(lane_i == R_RANK, r1, route)
    return jnp.where(lane_i == R_RANK + 1, r2, route)


def _merge_rows(alpha, x, bconv, y_attn, wg_ref, wbc_ref, wo_ref, lg_ref, lb_ref, wr_ref, br_ref):
    g = _dot(x.astype(BF16), wg_ref[...])
    y_conv = _dot(bconv, wbc_ref[...])
    m = jax.nn.sigmoid(g[:, 0:D_MODEL]) * y_conv + jax.nn.sigmoid(g[:, D_MODEL:]) * y_attn
    mix = _dot(m.astype(BF16), wo_ref[...])
    h1 = _layernorm(alpha * x + mix, lg_ref[...], lb_ref[...])
    return h1, _route(_dot(h1.astype(BF16), wr_ref[...]) + br_ref[...])


def _merge_sample_body(alpha, x_ref, bconv_ref, olat_ref, wbd_ref, wba_ref, *refs):
    w_refs, (h1_ref, route_ref) = refs[:-2], refs[-2:]
    o = _dot(olat_ref[...].astype(BF16), wbd_ref[...]).astype(BF16)
    h1_ref[...], route_ref[...] = _merge_rows(alpha, x_ref[...], bconv_ref[...], _dot(o, wba_ref[...]), *w_refs)


def _merge_prompt_body(alpha, n_tiles, x_ref, bconv_ref, ot_ref, h1s_ref, routes_ref, wba_ref, *refs):
    w_refs, (h1t_ref, route_ref, routet_ref, counts_ref, ltri_ref, count_ref) = refs[:-6], refs[-6:]
    i = pl.program_id(0)
    tm = x_ref.shape[0]

    @pl.when(i == 0)
    def _():
        r = lax.broadcasted_iota(jnp.int32, (tm, tm), 0)
        c = lax.broadcasted_iota(jnp.int32, (tm, tm), 1)
        ltri_ref[...] = jnp.where(c < r, 1.0, 0.0).astype(BF16)
        count_ref[...] = jnp.zeros(count_ref.shape, F32)

    @pl.when(i < n_tiles)
    def _():
        y_attn = _dot_tn(ot_ref[...], wba_ref[...])
        h1, route = _merge_rows(alpha, x_ref[...], bconv_ref[...], y_attn, *w_refs)
        for c in range(ROW_TILE):
            _tile_rows(h1t_ref, c, tm)[...] = h1[:, c * LANES:(c + 1) * LANES]
        route_ref[...] = _add_ranks(route, ltri_ref[...], count_ref)

    @pl.when(i == n_tiles)
    def _():
        n_s = h1s_ref.shape[0]
        h1t_ref[...] = jnp.zeros(h1t_ref.shape, F32)
        route_ref[...] = jnp.zeros(route_ref.shape, F32)
        for c in range(ROW_TILE):
            _tile_rows(h1t_ref, c, n_s)[...] = h1s_ref[:, c * LANES:(c + 1) * LANES]
        route_ref[0:n_s, :] = _add_ranks(routes_ref[...], ltri_ref[0:n_s, 0:n_s], count_ref)

    counts_ref[...] = jnp.broadcast_to(count_ref[...], counts_ref.shape)
    routet_ref[...] = route_ref[...].T[0:SUBLANES, :]


def _merge_sample(alpha, x, bconv, olat, wbd, wba, ws):
    n = x.shape[0]
    args = (x, bconv, olat, wbd, wba) + tuple(ws)
    out_shape = [jax.ShapeDtypeStruct((n, D_MODEL), F32), jax.ShapeDtypeStruct((n, LANES), F32)]
    return pl.pallas_call(
        functools.partial(_merge_sample_body, alpha),
        grid=(1,),
        in_specs=[_full(a.shape) for a in args],
        out_specs=[_full(s.shape) for s in out_shape],
        out_shape=out_shape,
        compiler_params=_cparams("arbitrary"),
    )(*args)


def _merge_prompt(alpha, x, bconv, ot, h1_s, route_s, wba, ws):
    t = x.shape[0]
    tm = TM
    n_tiles = t // tm
    t_all = t + h1_s.shape[0]
    assert h1_s.shape[0] <= tm
    clamp = lambda i: jnp.minimum(i, n_tiles - 1)
    rows_in = lambda w: pl.BlockSpec((tm, w), lambda i: (clamp(i), 0))
    return pl.pallas_call(
        functools.partial(_merge_prompt_body, alpha, n_tiles),
        grid=(n_tiles + 1,),
        in_specs=[rows_in(D_MODEL), rows_in(D_CONV), pl.BlockSpec((V_W, tm), lambda i: (0, clamp(i))),
                  _full(h1_s.shape), _full(route_s.shape), _full(wba.shape)] + [_full(w.shape) for w in ws],
        out_specs=[pl.BlockSpec((tm * ROW_TILE, LANES), lambda i: (i, 0)), pl.BlockSpec((tm, LANES), lambda i: (i, 0)),
                   pl.BlockSpec((SUBLANES, tm), lambda i: (0, i)), _full((SUBLANES, LANES))],
        out_shape=[jax.ShapeDtypeStruct((t_all * ROW_TILE, LANES), F32), jax.ShapeDtypeStruct((t_all, LANES), F32),
                   jax.ShapeDtypeStruct((SUBLANES, t_all), F32), jax.ShapeDtypeStruct((SUBLANES, LANES), F32)],
        scratch_shapes=[pltpu.VMEM((tm, tm), BF16), pltpu.VMEM((1, LANES), F32)],
        compiler_params=_cparams("arbitrary"),
    )(x, bconv, ot, h1_s, route_s, wba, *ws)


SLOT_GROUP = 16


def _slot_tokens_body(rows_ref, dest_ref, tok_ref):
    def clear_row(j, c):
        for l in range(LANES):
            tok_ref[j * LANES + l] = 0
        return c
    lax.fori_loop(0, rows_ref[1], clear_row, 0)

    rows_k = rows_ref[0]
    for k in range(TOP_K):
        def fill_row(j, c):
            for g in range(0, LANES, SLOT_GROUP):
                ds = [dest_ref[k * rows_k + j, g + l] for l in range(SLOT_GROUP)]
                for l, d in enumerate(ds):
                    tok_ref[d] = j * LANES + g + l
            return c
        lax.fori_loop(0, rows_k, fill_row, 0)


def _slot_tokens(dest, n_blocks):
    n_slots = n_blocks * MOE_BLOCK
    n_tok = dest.shape[1]
    assert n_slots % LANES == 0 and n_tok % LANES == 0
    dest = dest.reshape(-1, LANES)
    grid_spec = pltpu.PrefetchScalarGridSpec(
        num_scalar_prefetch=2,
        grid=(1,),
        in_specs=[],
        out_specs=pl.BlockSpec(memory_space=pltpu.SMEM),
    )
    return pl.pallas_call(
        _slot_tokens_body,
        grid_spec=grid_spec,
        out_shape=jax.ShapeDtypeStruct((n_slots,), jnp.int32),
        compiler_params=_cparams("arbitrary"),
    )(jnp.array([n_tok // LANES, n_slots // LANES], jnp.int32), dest)


def _row_tile(i):
    return pl.ds(pl.multiple_of(i * ROW_TILE, ROW_TILE), ROW_TILE)


def _experts_body(blk_e_ref, used_ref, tok_ref, h1t_ref, wg_ref, wu_ref, wd_ref, ys_ref,
                  xbuf, xb, wgb, wub, wdb, sem_in):
    b = pl.program_id(0)
    ring = lambda bb: lax.rem(bb, GATHER_RING)
    used = used_ref[0]
    blk_rows = MOE_BLOCK * ROW_TILE
    changed = jnp.logical_or(b == 0, blk_e_ref[b] != blk_e_ref[jnp.maximum(b - 1, 0)])

    def in_copy(bb, sl, r):
        tok = tok_ref[bb * MOE_BLOCK + r]
        return pltpu.make_async_copy(h1t_ref.at[_row_tile(tok)], xbuf.at[sl, _row_tile(r)], sem_in.at[sl])

    def issue_in(bb, sl):
        for r in range(MOE_BLOCK):
            in_copy(bb, sl, r).start(priority=r % 2)

    def wait_in(sl):
        pltpu.make_async_copy(h1t_ref.at[pl.ds(0, blk_rows)], xbuf.at[sl], sem_in.at[sl]).wait()

    last = used - 1

    @pl.when(jnp.logical_and(b == 0, used > 0))
    def _():
        for a in range(GATHER_RING - 1):
            issue_in(jnp.minimum(a, last), a)

    @pl.when(jnp.logical_and(b < used, changed))
    def _():
        wgb[...] = wg_ref[0].astype(BF16)
        wub[...] = wu_ref[0].astype(BF16)
        wdb[...] = wd_ref[0].astype(BF16)

    @pl.when(b < used)
    def _():
        wait_in(ring(b))
        for c in range(ROW_TILE):
            xb[:, c * LANES:(c + 1) * LANES] = _tile_rows(xbuf.at[ring(b)], c, MOE_BLOCK)[...].astype(BF16)
        ahead = b + GATHER_RING - 1
        issue_in(jnp.minimum(ahead, last), ring(ahead))
        x = xb[...]
        g = _dot(x, wgb[...])
        u = _dot(x, wub[...])
        h = (g * jax.nn.sigmoid(g) * u).astype(BF16)
        y = _dot(h, wdb[...])
        for c in range(ROW_TILE):
            _tile_rows(ys_ref, c, MOE_BLOCK)[...] = y[:, c * LANES:(c + 1) * LANES]

        @pl.when(b == last)
        def _():
            for a in range(1, GATHER_RING):
                wait_in(ring(b + a))

    @pl.when(b >= used)
    def _():
        ys_ref[...] = jnp.zeros(ys_ref.shape, F32)


def _experts(blk_e, used, slot_tok, h1t, wg, wu, wd):
    n_blocks = blk_e.shape[0]
    blk_rows = MOE_BLOCK * ROW_TILE
    by_expert = lambda b, e, u, a: (e[b], 0, 0)
    grid_spec = pltpu.PrefetchScalarGridSpec(
        num_scalar_prefetch=3,
        grid=(n_blocks,),
        in_specs=[pl.BlockSpec(memory_space=pl.ANY),
                  pl.BlockSpec((1, D_MODEL, D_EXPERT), by_expert), pl.BlockSpec((1, D_MODEL, D_EXPERT), by_expert),
                  pl.BlockSpec((1, D_EXPERT, D_MODEL), by_expert)],
        out_specs=pl.BlockSpec((blk_rows, LANES), lambda b, e, u, a: (b, 0)),
        scratch_shapes=[pltpu.VMEM((GATHER_RING, blk_rows, LANES), F32),
                        pltpu.VMEM((MOE_BLOCK, D_MODEL), BF16),
                        pltpu.VMEM((D_MODEL, D_EXPERT), BF16), pltpu.VMEM((D_MODEL, D_EXPERT), BF16),
                        pltpu.VMEM((D_EXPERT, D_MODEL), BF16),
                        pltpu.SemaphoreType.DMA((GATHER_RING,))],
    )
    return pl.pallas_call(
        _experts_body,
        grid_spec=grid_spec,
        out_shape=jax.ShapeDtypeStruct((n_blocks * blk_rows, LANES), F32),
        compiler_params=_cparams("arbitrary"),
    )(blk_e, used, slot_tok, h1t, wg, wu, wd)


def _combine_body(alpha, tok0, n_all, dest_ref, ys_ref, h1t_ref, route_ref, lg_ref, lb_ref, out_ref, gbuf, sem):
    i = pl.program_id(0)
    nt = pl.num_programs(0)
    slot = i % 2
    tm = out_ref.shape[0]

    def issue(ii, sl):
        base = tok0 + ii * tm
        for r in range(tm):
            for k in range(TOP_K):
                d = dest_ref[base + k * n_all + r]
                pltpu.make_async_copy(ys_ref.at[_row_tile(d)], gbuf.at[sl, k, _row_tile(r)],
                                      sem.at[sl]).start(priority=k)

    def wait(sl):
        for k in range(TOP_K):
            pltpu.make_async_copy(ys_ref.at[pl.ds(0, tm * ROW_TILE)], gbuf.at[sl, k], sem.at[sl]).wait()

    @pl.when(i == 0)
    def _():
        issue(0, 0)

    wait(slot)
    issue(jnp.minimum(i + 1, nt - 1), 1 - slot)

    route = route_ref[...]
    w0 = jnp.broadcast_to(route[:, R_WT:R_WT + 1], (tm, LANES))
    w1 = jnp.broadcast_to(route[:, R_WT + 1:R_WT + 2], (tm, LANES))
    z = [alpha * _tile_rows(h1t_ref, c, tm)[...]
         + w0 * _tile_rows(gbuf.at[slot, 0], c, tm)[...] + w1 * _tile_rows(gbuf.at[slot, 1], c, tm)[...]
         for c in range(ROW_TILE)]
    mu = sum(jnp.sum(zc, axis=-1, keepdims=True) for zc in z) / D_MODEL
    zc = [v - mu for v in z]
    var = sum(jnp.sum(v * v, axis=-1, keepdims=True) for v in zc) / D_MODEL
    rstd = lax.rsqrt(var + LN_EPS)
    for c in range(ROW_TILE):
        sl = slice(c * LANES, (c + 1) * LANES)
        out_ref[:, sl] = zc[c] * rstd * lg_ref[:, sl] + lb_ref[:, sl]

    @pl.when(i == nt - 1)
    def _():
        wait(1 - slot)


def _combine(alpha, tok0, n_tok, tm, dest, ys, h1t, route, lg, lb):
    assert n_tok % tm == 0 and tok0 % tm == 0
    blk0 = tok0 // tm
    grid_spec = pltpu.PrefetchScalarGridSpec(
        num_scalar_prefetch=1,
        grid=(n_tok // tm,),
        in_specs=[pl.BlockSpec(memory_space=pl.ANY),
                  pl.BlockSpec((tm * ROW_TILE, LANES), lambda i, d: (blk0 + i, 0)),
                  pl.BlockSpec((tm, LANES), lambda i, d: (blk0 + i, 0)),
                  pl.BlockSpec(lg.shape, lambda i, d: (0, 0)), pl.BlockSpec(lb.shape, lambda i, d: (0, 0))],
        out_specs=pl.BlockSpec((tm, D_MODEL), lambda i, d: (i, 0)),
        scratch_shapes=[pltpu.VMEM((2, TOP_K, tm * ROW_TILE, LANES), F32), pltpu.SemaphoreType.DMA((2,))],
    )
    return pl.pallas_call(
        functools.partial(_combine_body, alpha, tok0, route.shape[0]),
        grid_spec=grid_spec,
        out_shape=jax.ShapeDtypeStruct((n_tok, D_MODEL), F32),
        compiler_params=_cparams("arbitrary"),
    )(dest, ys, h1t, route, lg, lb)


def _rope_tables(pos):
    inv = ROPE_BASE ** (-(jnp.arange(ROPE_HALF, dtype=F32) * 2.0 / ROPE_DIM))
    ang = pos.astype(F32)[:, None] * inv[None, :]
    cos, sin = jnp.cos(ang), jnp.sin(ang)
    n = pos.shape[0]
    a = jnp.concatenate([cos, cos, jnp.ones((n, LANES - ROPE_DIM), F32)], axis=1)
    bm = jnp.concatenate([-sin, jnp.zeros((n, LANES - ROPE_HALF), F32)], axis=1)
    cp = jnp.concatenate([jnp.zeros((n, ROPE_HALF), F32), sin, jnp.zeros((n, LANES - ROPE_DIM), F32)], axis=1)
    k_tabs = jnp.stack([a, bm, cp])
    return jnp.concatenate([k_tabs * Q_SCALE, k_tabs]), jnp.stack([cos.T, sin.T]) * Q_SCALE


def _head_blocks(w_rope, w_nope):
    k = w_nope.shape[0]
    pad = jnp.zeros((k, N_HEADS, HEAD_W - QK_DIM), w_nope.dtype)
    return jnp.concatenate([w_rope, w_nope, pad], axis=-1).reshape(k, QK_W)


def _pack_weights(w_in, w_uq, w_uk, w_uv, router_w_group, router_b_group, router_w_expert, router_b_expert):
    d = w_in.shape[0]
    c_kr = 3 * D_CONV + Q_LORA + KV_LORA
    w1 = jnp.concatenate([w_in[:, :c_kr], w_in[:, c_kr:c_kr + ROPE_DIM],
                          jnp.zeros((d, LANES - ROPE_DIM), w_in.dtype)], axis=1).astype(BF16)
    wg = w_in[:, c_kr + ROPE_DIM:].astype(BF16)
    uq = w_uq.reshape(Q_LORA, N_HEADS, QK_DIM)
    wuq = _head_blocks(uq[..., NOPE_DIM:], uq[..., :NOPE_DIM]).astype(BF16)
    wuk = _head_blocks(jnp.zeros((KV_LORA, N_HEADS, ROPE_DIM), w_uk.dtype), w_uk).astype(BF16)
    wukt = jnp.transpose(wuk.reshape(KV_LORA, N_HEADS, HEAD_W), (1, 2, 0))
    wuvt = w_uv.reshape(KV_LORA, V_W).T.astype(BF16)
    eye = jnp.eye(N_HEADS, dtype=w_uv.dtype)
    wbd = jnp.einsum('chd,hg->hcgd', w_uv, eye).reshape(N_HEADS * KV_LORA, V_W).astype(BF16)
    wr = jnp.zeros((d, ROUTE_W), F32)
    wr = wr.at[:, 0:N_GROUPS].set(router_w_group).at[:, LANES:LANES + N_EXPERTS].set(router_w_expert).astype(BF16)
    br = jnp.zeros((1, ROUTE_W), F32)
    br = br.at[0, 0:N_GROUPS].set(router_b_group).at[0, LANES:LANES + N_EXPERTS].set(router_b_expert)
    return w1, wg, wuq, wuq.T, wuk, wukt, wuvt, wbd, wr, br


def _dispatch_plan(route_t, counts, n_blocks):
    counts = counts[0, 0:N_EXPERTS].astype(jnp.int32)
    pcounts = (counts + MOE_BLOCK - 1) // MOE_BLOCK * MOE_BLOCK
    pend = jnp.cumsum(pcounts).astype(jnp.int32)
    pstart = (pend - pcounts).astype(F32)
    eid = route_t[R_EID:R_EID + TOP_K]
    rank = route_t[R_RANK:R_RANK + TOP_K]
    base = jnp.zeros_like(eid)
    for e in range(N_EXPERTS):
        base = jnp.where(eid == float(e), pstart[e], base)
    dest = (base + rank).astype(jnp.int32)
    blk_start = jnp.arange(n_blocks, dtype=jnp.int32) * MOE_BLOCK
    blk_e = jnp.sum((blk_start[:, None] >= pend[None, :]).astype(jnp.int32), axis=1)
    blk_e = jnp.minimum(blk_e, N_EXPERTS - 1).astype(jnp.int32)
    used = (pend[-1:] // MOE_BLOCK).astype(jnp.int32)
    return dest, blk_e, used


def kernel(x_prompt, x_sample, cache_ckv, cache_krope, state_conv, page_table, w_in, conv_w, q_norm_g, w_uq,
           kv_norm_g, w_uk, w_uv, w_br_conv, w_br_attn, w_o, ln1_g, ln1_b, router_w_group, router_b_group,
           router_w_expert, router_b_expert, w_gate, w_up, w_down, ln2_g, ln2_b):
    depth = w_in.shape[0]
    alpha = (2 * depth) ** 0.25
    n_p, t_p, _ = x_prompt.shape
    n_s, t_s, _ = x_sample.shape
    assert t_s == 1 and t_p % TM == 0 and n_s % LANES == 0
    rows_p = n_p * t_p
    rows_all = rows_p + n_s
    past = page_table.shape[1] * PAGE_SIZE
    tab_p, tabt_p = _rope_tables(jnp.arange(t_p))
    tab_s, _ = _rope_tables(jnp.full((n_s,), past, jnp.int32))
    n_blocks = -(-(rows_all * TOP_K) // MOE_BLOCK) + N_EXPERTS

    h_p = x_prompt.reshape(rows_p, D_MODEL)
    h_s = x_sample.reshape(n_s, D_MODEL)
    ckv_p, kr_p, cv_p, ckv_s, kr_s, cv_s = [], [], [], [], [], []
    row = lambda v: v.reshape(1, -1)
    for l in range(depth):
        w1, wg, wuq, wuqt, wuk, wukt, wuvt, wbd, wr, br = _pack_weights(
            w_in[l], w_uq[l], w_uk[l], w_uv[l], router_w_group[l], router_b_group[l],
            router_w_expert[l], router_b_expert[l])
        qg, kvg = row(q_norm_g[l]), row(kv_norm_g[l])
        wba = w_br_attn[l].astype(BF16)
        merge_w = (wg, w_br_conv[l].astype(BF16), w_o[l].astype(BF16), row(ln1_g[l]), row(ln1_b[l]), wr, br)

        bconv, qt, k, vt, ckv, krt, nconv = _inproj_prompt(h_p, tab_p, tabt_p, w1, wuqt, wuk, wuvt, qg, kvg, conv_w[l],
                                                          n_p, t_p)
        ot = _prompt_attention(qt, k, vt, n_p, t_p)

        st = state_conv[l]
        bconv_s, q_s, qlat, ckvn_s, krn_s, u_s = _inproj_sample(
            h_s, tab_s, w1, wuq, wukt, qg, kvg, conv_w[l], st[:, 0], st[:, 1])
        olat = _sample_attention(page_table, qlat, q_s, ckvn_s, krn_s, cache_ckv[l],
                                 jnp.swapaxes(cache_krope[l], 1, 2))
        h1_s, route_s = _merge_sample(alpha, h_s, bconv_s, olat.reshape(n_s, N_HEADS * KV_LORA), wbd, wba, merge_w)
        h1t, route, route_t, counts = _merge_prompt(alpha, h_p, bconv, ot, h1_s, route_s, wba, merge_w)

        dest, blk_e, used = _dispatch_plan(route_t, counts, n_blocks)
        slot_tok = _slot_tokens(dest, n_blocks)
        ys = _experts(blk_e, used, slot_tok, h1t, w_gate[l], w_up[l], w_down[l])
        dest = dest.reshape(-1)
        ln2 = (row(ln2_g[l]), row(ln2_b[l]))
        h_p = _combine(alpha, 0, rows_p, TM_COMBINE, dest, ys, h1t, route, *ln2)
        h_s = _combine(alpha, rows_p, n_s, n_s, dest, ys, h1t, route, *ln2)

        ckv_p.append(ckv.reshape(n_p, t_p, KV_LORA))
        kr_p.append(jnp.swapaxes(krt, 1, 2))
        cv_p.append(nconv)
        ckv_s.append(ckvn_s.reshape(n_s, 1, KV_LORA))
        kr_s.append(krn_s.reshape(n_s, 1, ROPE_DIM))
        cv_s.append(jnp.stack([st[:, 1], u_s], axis=1))
    return (h_p.reshape(n_p, t_p, D_MODEL), h_s.reshape(n_s, 1, D_MODEL), jnp.stack(ckv_p), jnp.stack(kr_p),
            jnp.stack(cv_p), jnp.stack(ckv_s), jnp.stack(kr_s), jnp.stack(cv_s))
```

```python
import functools

import jax
import jax.numpy as jnp
from jax import lax
from jax.experimental import pallas as pl
from jax.experimental.pallas import tpu as pltpu

F32 = jnp.float32
BF16 = jnp.bfloat16

D_MODEL = 1024
D_CONV = 512
CONV_W = 3
N_HEADS = 8
Q_LORA = 384
KV_LORA = 256
NOPE_DIM = 64
ROPE_DIM = 32
ROPE_HALF = ROPE_DIM // 2
V_DIM = 64
QK_DIM = NOPE_DIM + ROPE_DIM
ROPE_BASE = 10000.0
ATTN_SCALE = QK_DIM ** -0.5
LOG2E = 1.4426950408889634
Q_SCALE = ATTN_SCALE * LOG2E
N_GROUPS = 4
EXP_PER_GROUP = 8
N_EXPERTS = N_GROUPS * EXP_PER_GROUP
TOP_K = 2
D_EXPERT = 512
MOE_BLOCK = 256
PAGE_SIZE = 128
LN_EPS = 1e-5
RMS_EPS = 1e-6

LANES = 128
SUBLANES = 8
ROW_TILE = D_MODEL // LANES
assert ROW_TILE == SUBLANES
HEAD_W = LANES
QK_W = N_HEADS * HEAD_W
V_W = N_HEADS * V_DIM
_C_BG, _C_CG, _C_H = 0, D_CONV, 2 * D_CONV
_C_CQ = 3 * D_CONV
_C_CKV = _C_CQ + Q_LORA
_C_KR = _C_CKV + KV_LORA
W1_COLS = _C_KR + LANES
ROUTE_W = 2 * LANES
R_EID, R_WT, R_RANK = 0, 2, 4
NEG = -1e30
VMEM_LIMIT = 56 * 1024 * 1024

TM = 512
KEY_CHUNK = 2048
ATTN_HEADS_PER_LOOP = 4
TQ = 512
TM_COMBINE = 256


def _cparams(*sem):
    return pltpu.CompilerParams(dimension_semantics=sem, vmem_limit_bytes=VMEM_LIMIT)


def _dot(a, b):
    return jnp.dot(a, b, preferred_element_type=F32)


def _dot_nt(a, b):
    return lax.dot_general(a, b, (((1,), (1,)), ((), ())), preferred_element_type=F32)


def _dot_tn(a, b):
    return lax.dot_general(a, b, (((0,), (0,)), ((), ())), preferred_element_type=F32)


def _rms(x, g):
    return x * lax.rsqrt(jnp.mean(x * x, axis=-1, keepdims=True) + RMS_EPS) * g


def _layernorm(x, g, b):
    mu = jnp.mean(x, axis=-1, keepdims=True)
    xc = x - mu
    var = jnp.mean(xc * xc, axis=-1, keepdims=True)
    return xc * lax.rsqrt(var + LN_EPS) * g + b


def _rope(xh, a, bm, cp):
    return xh * a + pltpu.roll(xh, LANES - ROPE_HALF, 1) * bm + pltpu.roll(xh, ROPE_HALF, 1) * cp


def _full(shape):
    nd = len(shape)
    return pl.BlockSpec(shape, lambda *_: (0,) * nd)


def _tile_rows(ref, c, n):
    return ref.at[pl.ds(c, n, stride=ROW_TILE), :]


def _inproj_common(x_ref, tab_ref, w1_ref, qg_ref, kvg_ref, ckv_ref):
    xb = x_ref[...].astype(BF16)

    def seg(lo, hi):
        return _dot(xb, w1_ref[:, lo:hi])

    b_g = seg(_C_BG, _C_CG)
    u = seg(_C_CG, _C_H) * seg(_C_H, _C_CQ)
    cqn = _rms(seg(_C_CQ, _C_CKV), qg_ref[...]).astype(BF16)
    ckvn = _rms(seg(_C_CKV, _C_KR), kvg_ref[...])
    ckv_ref[...] = ckvn
    krr = _rope(seg(_C_KR, W1_COLS), tab_ref[3], tab_ref[4], tab_ref[5])
    return b_g, u, cqn, ckvn.astype(BF16), krr


def _inproj_prompt_body(tiles_per_seq, x_ref, tab_ref, tabt_ref, w1_ref, wuqt_ref, wuk_ref, wuvt_ref, qg_ref, kvg_ref,
                        cw_ref, bconv_ref, qt_ref, k_ref, vt_ref, ckv_ref, krt_ref, nconv_ref, uext_ref):
    tm = x_ref.shape[0]
    b_g, u, cqn, cb, krr = _inproj_common(x_ref, tab_ref, w1_ref, qg_ref, kvg_ref, ckv_ref)
    krt_ref[0] = krr.T[0:ROPE_DIM, :]
    qft = _dot_nt(wuqt_ref[...], cqn)
    cos, sin = tabt_ref[0], tabt_ref[1]
    for h in range(N_HEADS):
        r0 = h * HEAD_W
        x1 = qft[r0:r0 + ROPE_HALF, :]
        x2 = qft[r0 + ROPE_HALF:r0 + ROPE_DIM, :]
        qt_ref[r0:r0 + ROPE_HALF, :] = (x1 * cos - x2 * sin).astype(BF16)
        qt_ref[r0 + ROPE_HALF:r0 + ROPE_DIM, :] = (x1 * sin + x2 * cos).astype(BF16)
        qt_ref[r0 + ROPE_DIM:r0 + HEAD_W, :] = (qft[r0 + ROPE_DIM:r0 + HEAD_W, :] * Q_SCALE).astype(BF16)
    kn = _dot(cb, wuk_ref[...])
    for h in range(N_HEADS):
        sl = slice(h * HEAD_W, (h + 1) * HEAD_W)
        k_ref[:, sl] = (kn[:, sl] + krr).astype(BF16)
    vt_ref[...] = _dot_nt(wuvt_ref[...], cb).astype(BF16)

    first = (pl.program_id(0) % tiles_per_seq) == 0

    @pl.when(first)
    def _():
        uext_ref[0:8, :] = jnp.zeros((8, D_CONV), F32)

    @pl.when(jnp.logical_not(first))
    def _():
        uext_ref[0:8, :] = uext_ref[tm:tm + 8, :]

    uext_ref[8:8 + tm, :] = u
    conv = cw_ref[0:1, :] * uext_ref[6:6 + tm, :] + cw_ref[1:2, :] * uext_ref[7:7 + tm, :] + cw_ref[2:3, :] * u
    bconv_ref[...] = (b_g * conv).astype(BF16)
    nconv_ref[0] = u[tm - (CONV_W - 1):tm, :]


def _inproj_sample_body(x_ref, tab_ref, w1_ref, wuq_ref, wukt_ref, qg_ref, kvg_ref, cw_ref, s0_ref, s1_ref,
                        bconv_ref, q_ref, qlat_ref, ckv_ref, kr_ref, u_ref):
    b_g, u, cqn, _, krr = _inproj_common(x_ref, tab_ref, w1_ref, qg_ref, kvg_ref, ckv_ref)
    kr_ref[...] = krr[:, 0:ROPE_DIM]
    qf = _dot(cqn, wuq_ref[...])
    for h in range(N_HEADS):
        sl = slice(h * HEAD_W, (h + 1) * HEAD_W)
        qh = _rope(qf[:, sl], tab_ref[0], tab_ref[1], tab_ref[2]).astype(BF16)
        q_ref[:, sl] = qh
        qlat_ref[:, h * KV_LORA:(h + 1) * KV_LORA] = _dot(qh, wukt_ref[h]).astype(BF16)
    conv = cw_ref[0:1, :] * s0_ref[...] + cw_ref[1:2, :] * s1_ref[...] + cw_ref[2:3, :] * u
    bconv_ref[...] = (b_g * conv).astype(BF16)
    u_ref[...] = u


def _inproj_prompt(x, tab, tabt, w1, wuqt, wuk, wuvt, qg, kvg, cw, n_seq, seq):
    t = x.shape[0]
    tm = TM
    tiles_per_seq = seq // tm
    rows = lambda w: pl.BlockSpec((tm, w), lambda i: (i, 0))
    cols = lambda h: pl.BlockSpec((h, tm), lambda i: (0, i))
    return pl.pallas_call(
        functools.partial(_inproj_prompt_body, tiles_per_seq),
        grid=(t // tm,),
        in_specs=[rows(D_MODEL),
                  pl.BlockSpec((6, tm, LANES), lambda i: (0, i % tiles_per_seq, 0)),
                  pl.BlockSpec((2, ROPE_HALF, tm), lambda i: (0, 0, i % tiles_per_seq)),
                  _full(w1.shape), _full(wuqt.shape), _full(wuk.shape), _full(wuvt.shape),
                  _full(qg.shape), _full(kvg.shape), _full(cw.shape)],
        out_specs=[rows(D_CONV), cols(QK_W), rows(QK_W), cols(V_W), rows(KV_LORA),
                   pl.BlockSpec((1, ROPE_DIM, tm), lambda i: (i // tiles_per_seq, 0, i % tiles_per_seq)),
                   pl.BlockSpec((1, CONV_W - 1, D_CONV), lambda i: (i // tiles_per_seq, 0, 0))],
        out_shape=[jax.ShapeDtypeStruct((t, D_CONV), BF16), jax.ShapeDtypeStruct((QK_W, t), BF16),
                   jax.ShapeDtypeStruct((t, QK_W), BF16), jax.ShapeDtypeStruct((V_W, t), BF16),
                   jax.ShapeDtypeStruct((t, KV_LORA), F32), jax.ShapeDtypeStruct((n_seq, ROPE_DIM, seq), F32),
                   jax.ShapeDtypeStruct((n_seq, CONV_W - 1, D_CONV), F32)],
        scratch_shapes=[pltpu.VMEM((tm + 8, D_CONV), F32)],
        compiler_params=_cparams("arbitrary"),
    )(x, tab, tabt, w1, wuqt, wuk, wuvt, qg, kvg, cw)


def _inproj_sample(x, tab, w1, wuq, wukt, qg, kvg, cw, s0, s1):
    n = x.shape[0]
    args = (x, tab, w1, wuq, wukt, qg, kvg, cw, s0, s1)
    out_shape = [jax.ShapeDtypeStruct((n, D_CONV), BF16), jax.ShapeDtypeStruct((n, QK_W), BF16),
                 jax.ShapeDtypeStruct((n, N_HEADS * KV_LORA), BF16), jax.ShapeDtypeStruct((n, KV_LORA), F32),
                 jax.ShapeDtypeStruct((n, ROPE_DIM), F32), jax.ShapeDtypeStruct((n, D_CONV), F32)]
    return pl.pallas_call(
        _inproj_sample_body,
        grid=(1,),
        in_specs=[_full(a.shape) for a in args],
        out_specs=[_full(s.shape) for s in out_shape],
        out_shape=out_shape,
        compiler_params=_cparams("arbitrary"),
    )(*args)


def _prompt_attention_body(qt_ref, k_ref, vt_ref, ot_ref, acc_ref, m_ref, l_ref, sa_ref, sb_ref):
    tq = qt_ref.shape[1]
    i = pl.program_id(1)

    for hg in range(N_HEADS // ATTN_HEADS_PER_LOOP):
        heads = tuple(range(hg * ATTN_HEADS_PER_LOOP, (hg + 1) * ATTN_HEADS_PER_LOOP))
        acc_ref[...] = jnp.zeros(acc_ref.shape, F32)
        m_ref[...] = jnp.full(m_ref.shape, NEG, F32)
        l_ref[...] = jnp.zeros(l_ref.shape, F32)

        def scores(j, s_ref):
            off = pl.multiple_of(j * tq, tq)
            for idx, h in enumerate(heads):
                s_ref[idx] = _dot(k_ref[pl.ds(off, tq), h * HEAD_W:(h + 1) * HEAD_W],
                                  qt_ref[h * HEAD_W:(h + 1) * HEAD_W, :])

        def consume(j, s_ref, masked):
            off = pl.multiple_of(j * tq, tq)
            for idx, h in enumerate(heads):
                s = s_ref[idx]
                if masked:
                    krow = lax.broadcasted_iota(jnp.int32, (tq, tq), 0)
                    qcol = lax.broadcasted_iota(jnp.int32, (tq, tq), 1)
                    s = jnp.where(krow <= qcol, s, NEG)
                m = m_ref[idx]
                m_new = jnp.maximum(m, jnp.max(s, axis=0, keepdims=True))
                alpha = jnp.exp2(m - m_new)
                p = jnp.exp2(s - m_new)
                m_ref[idx] = m_new
                l_ref[idx] = alpha * l_ref[idx] + jnp.sum(p, axis=0, keepdims=True)
                vblk = vt_ref[h * V_DIM:(h + 1) * V_DIM, pl.ds(off, tq)]
                acc_ref[idx] = alpha * acc_ref[idx] + _dot(vblk, p.astype(BF16))

        scores(0, sa_ref)

        def pair(pp, c):
            j = 2 * pp
            scores(j + 1, sb_ref)
            consume(j, sa_ref, False)
            scores(j + 2, sa_ref)
            consume(j + 1, sb_ref, False)
            return c
        lax.fori_loop(0, i // 2, pair, 0)

        @pl.when(i % 2 == 1)
        def _():
            scores(i, sb_ref)
            consume(i - 1, sa_ref, False)
            consume(i, sb_ref, True)

        @pl.when(i % 2 == 0)
        def _():
            consume(i, sa_ref, True)

        for idx, h in enumerate(heads):
            ot_ref[h * V_DIM:(h + 1) * V_DIM, :] = (acc_ref[idx] / l_ref[idx]).astype(BF16)


def _prompt_attention(qt, k, vt, n_seq, seq):
    t = k.shape[0]
    tq = TQ
    nq = seq // tq
    return pl.pallas_call(
        _prompt_attention_body,
        grid=(n_seq, nq),
        in_specs=[pl.BlockSpec((QK_W, tq), lambda b, i: (0, b * nq + i)),
                  pl.BlockSpec((seq, QK_W), lambda b, i: (b, 0)),
                  pl.BlockSpec((V_W, seq), lambda b, i: (0, b))],
        out_specs=pl.BlockSpec((V_W, tq), lambda b, i: (0, b * nq + i)),
        out_shape=jax.ShapeDtypeStruct((V_W, t), BF16),
        scratch_shapes=[pltpu.VMEM((ATTN_HEADS_PER_LOOP, V_DIM, tq), F32),
                        pltpu.VMEM((ATTN_HEADS_PER_LOOP, 1, tq), F32), pltpu.VMEM((ATTN_HEADS_PER_LOOP, 1, tq), F32),
                        pltpu.VMEM((ATTN_HEADS_PER_LOOP, tq, tq), F32), pltpu.VMEM((ATTN_HEADS_PER_LOOP, tq, tq), F32)],
        compiler_params=_cparams("arbitrary", "arbitrary"),
    )(qt, k, vt)


def _sample_attention_body(n_pages, pt_ref, qlat_ref, q_ref, ckvn_ref, krn_ref, cc_ref, cr_ref, olat_ref,
                           bufc, bufr, kcb, krb, sem):
    b = pl.program_id(0)
    nb = pl.num_programs(0)
    slot = b % 2
    past = n_pages * PAGE_SIZE

    def page_copies(bb, sl, p):
        pg = pt_ref[bb, p]
        pos = pl.ds(pl.multiple_of(p * PAGE_SIZE, PAGE_SIZE), PAGE_SIZE)
        return (pltpu.make_async_copy(cc_ref.at[pg], bufc.at[sl, pos], sem.at[0, sl]),
                pltpu.make_async_copy(cr_ref.at[pg], bufr.at[sl, :, pos], sem.at[1, sl]))

    def issue(bb, sl):
        for p in range(n_pages):
            for c in page_copies(bb, sl, p):
                c.start()

    def wait(sl):
        pltpu.make_async_copy(bufc.at[sl], bufc.at[sl], sem.at[0, sl]).wait()
        pltpu.make_async_copy(bufr.at[sl], bufr.at[sl], sem.at[1, sl]).wait()

    @pl.when(b == 0)
    def _():
        issue(0, 0)

    wait(slot)
    issue(jnp.minimum(b + 1, nb - 1), 1 - slot)

    ql = qlat_ref[0]
    qr = q_ref[0][:, 0:ROPE_DIM]
    cn = ckvn_ref[0]
    rn = krn_ref[0]
    chunk = min(KEY_CHUNK, past)
    scores = []
    for c in range(past // chunk):
        pos = slice(c * chunk, (c + 1) * chunk)
        kcb[pos, :] = bufc[slot, pos, :].astype(BF16)
        krb[:, pos] = bufr[slot, :, pos].astype(BF16)
        scores.append(_dot_nt(ql, kcb[pos, :]) + _dot(qr, krb[:, pos]))
    s_new = (jnp.sum(ql.astype(F32) * cn, axis=-1, keepdims=True)
             + jnp.sum(qr.astype(F32) * rn, axis=-1, keepdims=True))
    m = s_new
    for s in scores:
        m = jnp.maximum(m, jnp.max(s, axis=-1, keepdims=True))
    p_new = jnp.exp2(s_new - m)
    l = p_new
    o = p_new * cn
    for c, s in enumerate(scores):
        p = jnp.exp2(s - m)
        l = l + jnp.sum(p, axis=-1, keepdims=True)
        o = o + _dot(p.astype(BF16), kcb[c * chunk:(c + 1) * chunk, :])
    olat_ref[0] = o / l

    @pl.when(b == nb - 1)
    def _():
        wait(1 - slot)


def _sample_attention(page_table, qlat, q, ckvn, krn, cache_c, cache_rt):
    n, n_pages = page_table.shape
    past = n_pages * PAGE_SIZE
    blk = lambda d1, d2: pl.BlockSpec((1, d1, d2), lambda b, pt: (b, 0, 0))
    grid_spec = pltpu.PrefetchScalarGridSpec(
        num_scalar_prefetch=1,
        grid=(n,),
        in_specs=[blk(N_HEADS, KV_LORA), blk(N_HEADS, HEAD_W), blk(1, KV_LORA), blk(1, ROPE_DIM),
                  pl.BlockSpec(memory_space=pl.ANY), pl.BlockSpec(memory_space=pl.ANY)],
        out_specs=blk(N_HEADS, KV_LORA),
        scratch_shapes=[pltpu.VMEM((2, past, KV_LORA), F32), pltpu.VMEM((2, ROPE_DIM, past), F32),
                        pltpu.VMEM((past, KV_LORA), BF16), pltpu.VMEM((ROPE_DIM, past), BF16),
                        pltpu.SemaphoreType.DMA((2, 2))],
    )
    return pl.pallas_call(
        functools.partial(_sample_attention_body, n_pages),
        grid_spec=grid_spec,
        out_shape=jax.ShapeDtypeStruct((n, N_HEADS, KV_LORA), F32),
        compiler_params=_cparams("arbitrary"),
    )(page_table, qlat.reshape(n, N_HEADS, KV_LORA), q.reshape(n, N_HEADS, HEAD_W),
      ckvn.reshape(n, 1, KV_LORA), krn.reshape(n, 1, ROPE_DIM), cache_c, cache_rt)


def _route(logits):
    lane_i = lax.broadcasted_iota(jnp.int32, (logits.shape[0], LANES), 1)
    lane = lane_i.astype(F32)
    first_at = lambda hit: jnp.min(jnp.where(hit, lane, float(LANES)), axis=-1, keepdims=True)
    lg = jnp.where(lane_i < N_GROUPS, logits[:, 0:LANES], NEG)
    mg = jnp.max(lg, axis=-1, keepdims=True)
    p_grp = 1.0 / jnp.sum(jnp.exp(lg - mg), axis=-1, keepdims=True)
    grp = first_at(lg == mg)
    lane_grp = (lane_i // EXP_PER_GROUP).astype(F32)
    le = jnp.where(lane_grp == grp, logits[:, LANES:2 * LANES], NEG)
    top1 = jnp.max(le, axis=-1, keepdims=True)
    i1 = first_at(le == top1)
    le2 = jnp.where(lane == i1, NEG, le)
    top2 = jnp.max(le2, axis=-1, keepdims=True)
    i2 = first_at(le2 == top2)
    e2 = jnp.exp(top2 - top1)
    w1 = p_grp / (1.0 + e2)
    w2 = p_grp * e2 / (1.0 + e2)
    out = jnp.where(lane_i == R_EID, i1, 0.0)
    out = jnp.where(lane_i == R_EID + 1, i2, out)
    out = jnp.where(lane_i == R_WT, w1, out)
    return jnp.where(lane_i == R_WT + 1, w2, out)


def _add_ranks(route, ltri, count_ref):
    n = route.shape[0]
    lane_i = lax.broadcasted_iota(jnp.int32, (n, LANES), 1)
    lane = lane_i.astype(F32)
    hit1 = lane == route[:, R_EID:R_EID + 1]
    hit2 = lane == route[:, R_EID + 1:R_EID + 2]
    chosen = jnp.where(hit1, 1.0, jnp.where(hit2, 1.0, 0.0))
    before = _dot(ltri, chosen.astype(BF16)) + count_ref[...]
    r1 = jnp.sum(jnp.where(hit1, before, 0.0), axis=-1, keepdims=True)
    r2 = jnp.sum(jnp.where(hit2, before, 0.0), axis=-1, keepdims=True)
    count_ref[...] = count_ref[...] + jnp.sum(chosen, axis=0, keepdims=True)
    route = jnp.where(lane_i == R_RANK, r1, route)
    return jnp.where(lane_i == R_RANK + 1, r2, route)


def _merge_rows(alpha, x, bconv, y_attn, wg_ref, wbc_ref, wo_ref, lg_ref, lb_ref, wr_ref, br_ref):
    g = _dot(x.astype(BF16), wg_ref[...])
    y_conv = _dot(bconv, wbc_ref[...])
    m = jax.nn.sigmoid(g[:, 0:D_MODEL]) * y_conv + jax.nn.sigmoid(g[:, D_MODEL:]) * y_attn
    mix = _dot(m.astype(BF16), wo_ref[...])
    h1 = _layernorm(alpha * x + mix, lg_ref[...], lb_ref[...])
    return h1, _route(_dot(h1.astype(BF16), wr_ref[...]) + br_ref[...])


def _merge_sample_body(alpha, x_ref, bconv_ref, olat_ref, wbd_ref, wba_ref, *refs):
    w_refs, (h1_ref, route_ref) = refs[:-2], refs[-2:]
    o = _dot(olat_ref[...].astype(BF16), wbd_ref[...]).astype(BF16)
    h1_ref[...], route_ref[...] = _merge_rows(alpha, x_ref[...], bconv_ref[...], _dot(o, wba_ref[...]), *w_refs)


def _merge_prompt_body(alpha, n_tiles, x_ref, bconv_ref, ot_ref, h1s_ref, routes_ref, wba_ref, *refs):
    w_refs, (h1t_ref, route_ref, routet_ref, counts_ref, ltri_ref, count_ref) = refs[:-6], refs[-6:]
    i = pl.program_id(0)
    tm = x_ref.shape[0]

    @pl.when(i == 0)
    def _():
        r = lax.broadcasted_iota(jnp.int32, (tm, tm), 0)
        c = lax.broadcasted_iota(jnp.int32, (tm, tm), 1)
        ltri_ref[...] = jnp.where(c < r, 1.0, 0.0).astype(BF16)
        count_ref[...] = jnp.zeros(count_ref.shape, F32)

    @pl.when(i < n_tiles)
    def _():
        y_attn = _dot_tn(ot_ref[...], wba_ref[...])
        h1, route = _merge_rows(alpha, x_ref[...], bconv_ref[...], y_attn, *w_refs)
        for c in range(ROW_TILE):
            _tile_rows(h1t_ref, c, tm)[...] = h1[:, c * LANES:(c + 1) * LANES]
        route_ref[...] = _add_ranks(route, ltri_ref[...], count_ref)

    @pl.when(i == n_tiles)
    def _():
        n_s = h1s_ref.shape[0]
        h1t_ref[...] = jnp.zeros(h1t_ref.shape, F32)
        route_ref[...] = jnp.zeros(route_ref.shape, F32)
        for c in range(ROW_TILE):
            _tile_rows(h1t_ref, c, n_s)[...] = h1s_ref[:, c * LANES:(c + 1) * LANES]
        route_ref[0:n_s, :] = _add_ranks(routes_ref[...], ltri_ref[0:n_s, 0:n_s], count_ref)

    counts_ref[...] = jnp.broadcast_to(count_ref[...], counts_ref.shape)
    routet_ref[...] = route_ref[...].T[0:SUBLANES, :]


def _merge_sample(alpha, x, bconv, olat, wbd, wba, ws):
    n = x.shape[0]
    args = (x, bconv, olat, wbd, wba) + tuple(ws)
    out_shape = [jax.ShapeDtypeStruct((n, D_MODEL), F32), jax.ShapeDtypeStruct((n, LANES), F32)]
    return pl.pallas_call(
        functools.partial(_merge_sample_body, alpha),
        grid=(1,),
        in_specs=[_full(a.shape) for a in args],
        out_specs=[_full(s.shape) for s in out_shape],
        out_shape=out_shape,
        compiler_params=_cparams("arbitrary"),
    )(*args)


def _merge_prompt(alpha, x, bconv, ot, h1_s, route_s, wba, ws):
    t = x.shape[0]
    tm = TM
    n_tiles = t // tm
    t_all = t + h1_s.shape[0]
    assert h1_s.shape[0] <= tm
    clamp = lambda i: jnp.minimum(i, n_tiles - 1)
    rows_in = lambda w: pl.BlockSpec((tm, w), lambda i: (clamp(i), 0))
    return pl.pallas_call(
        functools.partial(_merge_prompt_body, alpha, n_tiles),
        grid=(n_tiles + 1,),
        in_specs=[rows_in(D_MODEL), rows_in(D_CONV), pl.BlockSpec((V_W, tm), lambda i: (0, clamp(i))),
                  _full(h1_s.shape), _full(route_s.shape), _full(wba.shape)] + [_full(w.shape) for w in ws],
        out_specs=[pl.BlockSpec((tm * ROW_TILE, LANES), lambda i: (i, 0)), pl.BlockSpec((tm, LANES), lambda i: (i, 0)),
                   pl.BlockSpec((SUBLANES, tm), lambda i: (0, i)), _full((SUBLANES, LANES))],
        out_shape=[jax.ShapeDtypeStruct((t_all * ROW_TILE, LANES), F32), jax.ShapeDtypeStruct((t_all, LANES), F32),
                   jax.ShapeDtypeStruct((SUBLANES, t_all), F32), jax.ShapeDtypeStruct((SUBLANES, LANES), F32)],
        scratch_shapes=[pltpu.VMEM((tm, tm), BF16), pltpu.VMEM((1, LANES), F32)],
        compiler_params=_cparams("arbitrary"),
    )(x, bconv, ot, h1_s, route_s, wba, *ws)


def _row_tile(i):
    return pl.ds(pl.multiple_of(i * ROW_TILE, ROW_TILE), ROW_TILE)


def _dispatch_body(chunk, n_tok, dest_ref, pend_ref, h1t_ref, xs_ref, zero_ref, sem, zsem):
    i = pl.program_id(0)
    nchunks = pl.num_programs(0)
    blk_rows = MOE_BLOCK * ROW_TILE
    n_blocks = xs_ref.shape[0] // blk_rows
    tail = n_tok - (nchunks - 1) * chunk

    def block_zero(blk):
        start = pl.multiple_of(blk * blk_rows, blk_rows)
        return pltpu.make_async_copy(zero_ref, xs_ref.at[pl.ds(start, blk_rows)], zsem)

    def last_block(e):
        return pend_ref[e] // MOE_BLOCK - 1

    def has_block(e):
        return pend_ref[e] > jnp.where(e == 0, 0, pend_ref[jnp.maximum(e - 1, 0)])

    @pl.when(i == 0)
    def _():
        zero_ref[...] = jnp.zeros(zero_ref.shape, F32)
        used = pend_ref[N_EXPERTS - 1] // MOE_BLOCK
        for e in range(N_EXPERTS):
            @pl.when(has_block(e))
            def _():
                block_zero(last_block(e)).start()
        lax.fori_loop(used, n_blocks, lambda blk, c: (block_zero(blk).start(), c)[1], 0)
        for e in range(N_EXPERTS):
            @pl.when(has_block(e))
            def _():
                block_zero(last_block(e)).wait()
        lax.fori_loop(used, n_blocks, lambda blk, c: (block_zero(blk).wait(), c)[1], 0)

    def issue(n):
        def body(r, c):
            t = i * chunk + r
            for k in range(TOP_K):
                d = dest_ref[k * n_tok + t]
                pltpu.make_async_copy(h1t_ref.at[_row_tile(t)], xs_ref.at[_row_tile(d)],
                                      sem.at[i % 2]).start(priority=k)
            return c
        lax.fori_loop(0, n, body, 0, unroll=8)

    def wait(ii, n):
        for k in range(TOP_K):
            pltpu.make_async_copy(h1t_ref.at[pl.ds(0, n * ROW_TILE)], xs_ref.at[pl.ds(0, n * ROW_TILE)],
                                  sem.at[ii % 2]).wait()

    @pl.when(i < nchunks - 1)
    def _():
        issue(chunk)

    @pl.when(i == nchunks - 1)
    def _():
        issue(tail)

    @pl.when(i > 0)
    def _():
        wait(i - 1, chunk)

    @pl.when(i == nchunks - 1)
    def _():
        wait(i, tail)


def _dispatch(dest, pend, h1t, n_blocks):
    n_tok = h1t.shape[0] // ROW_TILE
    chunk = TM
    blk_rows = MOE_BLOCK * ROW_TILE
    grid_spec = pltpu.PrefetchScalarGridSpec(
        num_scalar_prefetch=2,
        grid=(-(-n_tok // chunk),),
        in_specs=[pl.BlockSpec(memory_space=pl.ANY)],
        out_specs=pl.BlockSpec(memory_space=pl.ANY),
        scratch_shapes=[pltpu.VMEM((blk_rows, LANES), F32), pltpu.SemaphoreType.DMA((2,)),
                        pltpu.SemaphoreType.DMA],
    )
    return pl.pallas_call(
        functools.partial(_dispatch_body, chunk, n_tok),
        grid_spec=grid_spec,
        out_shape=jax.ShapeDtypeStruct((n_blocks * blk_rows, LANES), F32),
        compiler_params=_cparams("arbitrary"),
    )(dest, pend, h1t)


def _experts_body(blk_e_ref, used_ref, xs_ref, wg_ref, wu_ref, wd_ref, ys_ref, xb, wgb, wub, wdb):
    b = pl.program_id(0)
    used = used_ref[0]
    changed = jnp.logical_or(b == 0, blk_e_ref[b] != blk_e_ref[jnp.maximum(b - 1, 0)])

    @pl.when(jnp.logical_and(b < used, changed))
    def _():
        wgb[...] = wg_ref[0].astype(BF16)
        wub[...] = wu_ref[0].astype(BF16)
        wdb[...] = wd_ref[0].astype(BF16)

    @pl.when(b < used)
    def _():
        for c in range(ROW_TILE):
            xb[:, c * LANES:(c + 1) * LANES] = _tile_rows(xs_ref, c, MOE_BLOCK)[...].astype(BF16)
        x = xb[...]
        g = _dot(x, wgb[...])
        u = _dot(x, wub[...])
        h = (g * jax.nn.sigmoid(g) * u).astype(BF16)
        y = _dot(h, wdb[...])
        for c in range(ROW_TILE):
            _tile_rows(ys_ref, c, MOE_BLOCK)[...] = y[:, c * LANES:(c + 1) * LANES]

    @pl.when(b >= used)
    def _():
        ys_ref[...] = jnp.zeros(ys_ref.shape, F32)


def _experts(blk_e, used, xs, wg, wu, wd):
    n_blocks = blk_e.shape[0]
    blk_rows = MOE_BLOCK * ROW_TILE
    by_expert = lambda b, e, u: (e[b], 0, 0)
    grid_spec = pltpu.PrefetchScalarGridSpec(
        num_scalar_prefetch=2,
        grid=(n_blocks,),
        in_specs=[pl.BlockSpec((blk_rows, LANES), lambda b, e, u: (jnp.minimum(b, jnp.maximum(u[0] - 1, 0)), 0)),
                  pl.BlockSpec((1, D_MODEL, D_EXPERT), by_expert), pl.BlockSpec((1, D_MODEL, D_EXPERT), by_expert),
                  pl.BlockSpec((1, D_EXPERT, D_MODEL), by_expert)],
        out_specs=pl.BlockSpec((blk_rows, LANES), lambda b, e, u: (b, 0)),
        scratch_shapes=[pltpu.VMEM((MOE_BLOCK, D_MODEL), BF16),
                        pltpu.VMEM((D_MODEL, D_EXPERT), BF16), pltpu.VMEM((D_MODEL, D_EXPERT), BF16),
                        pltpu.VMEM((D_EXPERT, D_MODEL), BF16)],
    )
    return pl.pallas_call(
        _experts_body,
        grid_spec=grid_spec,
        out_shape=jax.ShapeDtypeStruct((n_blocks * blk_rows, LANES), F32),
        compiler_params=_cparams("arbitrary"),
    )(blk_e, used, xs, wg, wu, wd)


def _combine_body(alpha, tok0, n_all, dest_ref, ys_ref, h1t_ref, route_ref, lg_ref, lb_ref, out_ref, gbuf, sem):
    i = pl.program_id(0)
    nt = pl.num_programs(0)
    slot = i % 2
    tm = out_ref.shape[0]

    def issue(ii, sl):
        base = tok0 + ii * tm
        for r in range(tm):
            for k in range(TOP_K):
                d = dest_ref[base + k * n_all + r]
                pltpu.make_async_copy(ys_ref.at[_row_tile(d)], gbuf.at[sl, k, _row_tile(r)],
                                      sem.at[sl]).start(priority=k)

    def wait(sl):
        for k in range(TOP_K):
            pltpu.make_async_copy(ys_ref.at[pl.ds(0, tm * ROW_TILE)], gbuf.at[sl, k], sem.at[sl]).wait()

    @pl.when(i == 0)
    def _():
        issue(0, 0)

    wait(slot)
    issue(jnp.minimum(i + 1, nt - 1), 1 - slot)

    route = route_ref[...]
    w0 = jnp.broadcast_to(route[:, R_WT:R_WT + 1], (tm, LANES))
    w1 = jnp.broadcast_to(route[:, R_WT + 1:R_WT + 2], (tm, LANES))
    z = [alpha * _tile_rows(h1t_ref, c, tm)[...]
         + w0 * _tile_rows(gbuf.at[slot, 0], c, tm)[...] + w1 * _tile_rows(gbuf.at[slot, 1], c, tm)[...]
         for c in range(ROW_TILE)]
    mu = sum(jnp.sum(zc, axis=-1, keepdims=True) for zc in z) / D_MODEL
    zc = [v - mu for v in z]
    var = sum(jnp.sum(v * v, axis=-1, keepdims=True) for v in zc) / D_MODEL
    rstd = lax.rsqrt(var + LN_EPS)
    for c in range(ROW_TILE):
        sl = slice(c * LANES, (c + 1) * LANES)
        out_ref[:, sl] = zc[c] * rstd * lg_ref[:, sl] + lb_ref[:, sl]

    @pl.when(i == nt - 1)
    def _():
        wait(1 - slot)


def _combine(alpha, tok0, n_tok, tm, dest, ys, h1t, route, lg, lb):
    assert n_tok % tm == 0 and tok0 % tm == 0
    blk0 = tok0 // tm
    grid_spec = pltpu.PrefetchScalarGridSpec(
        num_scalar_prefetch=1,
        grid=(n_tok // tm,),
        in_specs=[pl.BlockSpec(memory_space=pl.ANY),
                  pl.BlockSpec((tm * ROW_TILE, LANES), lambda i, d: (blk0 + i, 0)),
                  pl.BlockSpec((tm, LANES), lambda i, d: (blk0 + i, 0)),
                  pl.BlockSpec(lg.shape, lambda i, d: (0, 0)), pl.BlockSpec(lb.shape, lambda i, d: (0, 0))],
        out_specs=pl.BlockSpec((tm, D_MODEL), lambda i, d: (i, 0)),
        scratch_shapes=[pltpu.VMEM((2, TOP_K, tm * ROW_TILE, LANES), F32), pltpu.SemaphoreType.DMA((2,))],
    )
    return pl.pallas_call(
        functools.partial(_combine_body, alpha, tok0, route.shape[0]),
        grid_spec=grid_spec,
        out_shape=jax.ShapeDtypeStruct((n_tok, D_MODEL), F32),
        compiler_params=_cparams("arbitrary"),
    )(dest, ys, h1t, route, lg, lb)


def _rope_tables(pos):
    inv = ROPE_BASE ** (-(jnp.arange(ROPE_HALF, dtype=F32) * 2.0 / ROPE_DIM))
    ang = pos.astype(F32)[:, None] * inv[None, :]
    cos, sin = jnp.cos(ang), jnp.sin(ang)
    n = pos.shape[0]
    a = jnp.concatenate([cos, cos, jnp.ones((n, LANES - ROPE_DIM), F32)], axis=1)
    bm = jnp.concatenate([-sin, jnp.zeros((n, LANES - ROPE_HALF), F32)], axis=1)
    cp = jnp.concatenate([jnp.zeros((n, ROPE_HALF), F32), sin, jnp.zeros((n, LANES - ROPE_DIM), F32)], axis=1)
    k_tabs = jnp.stack([a, bm, cp])
    return jnp.concatenate([k_tabs * Q_SCALE, k_tabs]), jnp.stack([cos.T, sin.T]) * Q_SCALE


def _head_blocks(w_rope, w_nope):
    k = w_nope.shape[0]
    pad = jnp.zeros((k, N_HEADS, HEAD_W - QK_DIM), w_nope.dtype)
    return jnp.concatenate([w_rope, w_nope, pad], axis=-1).reshape(k, QK_W)


def _pack_weights(w_in, w_uq, w_uk, w_uv, router_w_group, router_b_group, router_w_expert, router_b_expert):
    d = w_in.shape[0]
    c_kr = 3 * D_CONV + Q_LORA + KV_LORA
    w1 = jnp.concatenate([w_in[:, :c_kr], w_in[:, c_kr:c_kr + ROPE_DIM],
                          jnp.zeros((d, LANES - ROPE_DIM), w_in.dtype)], axis=1).astype(BF16)
    wg = w_in[:, c_kr + ROPE_DIM:].astype(BF16)
    uq = w_uq.reshape(Q_LORA, N_HEADS, QK_DIM)
    wuq = _head_blocks(uq[..., NOPE_DIM:], uq[..., :NOPE_DIM]).astype(BF16)
    wuk = _head_blocks(jnp.zeros((KV_LORA, N_HEADS, ROPE_DIM), w_uk.dtype), w_uk).astype(BF16)
    wukt = jnp.transpose(wuk.reshape(KV_LORA, N_HEADS, HEAD_W), (1, 2, 0))
    wuvt = w_uv.reshape(KV_LORA, V_W).T.astype(BF16)
    eye = jnp.eye(N_HEADS, dtype=w_uv.dtype)
    wbd = jnp.einsum('chd,hg->hcgd', w_uv, eye).reshape(N_HEADS * KV_LORA, V_W).astype(BF16)
    wr = jnp.zeros((d, ROUTE_W), F32)
    wr = wr.at[:, 0:N_GROUPS].set(router_w_group).at[:, LANES:LANES + N_EXPERTS].set(router_w_expert).astype(BF16)
    br = jnp.zeros((1, ROUTE_W), F32)
    br = br.at[0, 0:N_GROUPS].set(router_b_group).at[0, LANES:LANES + N_EXPERTS].set(router_b_expert)
    return w1, wg, wuq, wuq.T, wuk, wukt, wuvt, wbd, wr, br


def _dispatch_plan(route_t, counts, n_blocks):
    counts = counts[0, 0:N_EXPERTS].astype(jnp.int32)
    pcounts = (counts + MOE_BLOCK - 1) // MOE_BLOCK * MOE_BLOCK
    pend = jnp.cumsum(pcounts).astype(jnp.int32)
    pstart = (pend - pcounts).astype(F32)
    eid = route_t[R_EID:R_EID + TOP_K]
    rank = route_t[R_RANK:R_RANK + TOP_K]
    base = jnp.zeros_like(eid)
    for e in range(N_EXPERTS):
        base = jnp.where(eid == float(e), pstart[e], base)
    dest = (base + rank).astype(jnp.int32)
    blk_start = jnp.arange(n_blocks, dtype=jnp.int32) * MOE_BLOCK
    blk_e = jnp.sum((blk_start[:, None] >= pend[None, :]).astype(jnp.int32), axis=1)
    blk_e = jnp.minimum(blk_e, N_EXPERTS - 1).astype(jnp.int32)
    used = (pend[-1:] // MOE_BLOCK).astype(jnp.int32)
    return dest.reshape(-1), pend, blk_e, used


def kernel(x_prompt, x_sample, cache_ckv, cache_krope, state_conv, page_table, w_in, conv_w, q_norm_g, w_uq,
           kv_norm_g, w_uk, w_uv, w_br_conv, w_br_attn, w_o, ln1_g, ln1_b, router_w_group, router_b_group,
           router_w_expert, router_b_expert, w_gate, w_up, w_down, ln2_g, ln2_b):
    depth = w_in.shape[0]
    alpha = (2 * depth) ** 0.25
    n_p, t_p, _ = x_prompt.shape
    n_s, t_s, _ = x_sample.shape
    assert t_s == 1 and t_p % TM == 0 and n_s % LANES == 0
    rows_p = n_p * t_p
    rows_all = rows_p + n_s
    past = page_table.shape[1] * PAGE_SIZE
    tab_p, tabt_p = _rope_tables(jnp.arange(t_p))
    tab_s, _ = _rope_tables(jnp.full((n_s,), past, jnp.int32))
    n_blocks = -(-(rows_all * TOP_K) // MOE_BLOCK) + N_EXPERTS

    h_p = x_prompt.reshape(rows_p, D_MODEL)
    h_s = x_sample.reshape(n_s, D_MODEL)
    ckv_p, kr_p, cv_p, ckv_s, kr_s, cv_s = [], [], [], [], [], []
    row = lambda v: v.reshape(1, -1)
    for l in range(depth):
        w1, wg, wuq, wuqt, wuk, wukt, wuvt, wbd, wr, br = _pack_weights(
            w_in[l], w_uq[l], w_uk[l], w_uv[l], router_w_group[l], router_b_group[l],
            router_w_expert[l], router_b_expert[l])
        qg, kvg = row(q_norm_g[l]), row(kv_norm_g[l])
        wba = w_br_attn[l].astype(BF16)
        merge_w = (wg, w_br_conv[l].astype(BF16), w_o[l].astype(BF16), row(ln1_g[l]), row(ln1_b[l]), wr, br)

        bconv, qt, k, vt, ckv, krt, nconv = _inproj_prompt(h_p, tab_p, tabt_p, w1, wuqt, wuk, wuvt, qg, kvg, conv_w[l],
                                                          n_p, t_p)
        ot = _prompt_attention(qt, k, vt, n_p, t_p)

        st = state_conv[l]
        bconv_s, q_s, qlat, ckvn_s, krn_s, u_s = _inproj_sample(
            h_s, tab_s, w1, wuq, wukt, qg, kvg, conv_w[l], st[:, 0], st[:, 1])
        olat = _sample_attention(page_table, qlat, q_s, ckvn_s, krn_s, cache_ckv[l],
                                 jnp.swapaxes(cache_krope[l], 1, 2))
        h1_s, route_s = _merge_sample(alpha, h_s, bconv_s, olat.reshape(n_s, N_HEADS * KV_LORA), wbd, wba, merge_w)
        h1t, route, route_t, counts = _merge_prompt(alpha, h_p, bconv, ot, h1_s, route_s, wba, merge_w)

        dest, pend, blk_e, used = _dispatch_plan(route_t, counts, n_blocks)
        xs = _dispatch(dest, pend, h1t, n_blocks)
        ys = _experts(blk_e, used, xs, w_gate[l], w_up[l], w_down[l])
        ln2 = (row(ln2_g[l]), row(ln2_b[l]))
        h_p = _combine(alpha, 0, rows_p, TM_COMBINE, dest, ys, h1t, route, *ln2)
        h_s = _combine(alpha, rows_p, n_s, n_s, dest, ys, h1t, route, *ln2)

        ckv_p.append(ckv.reshape(n_p, t_p, KV_LORA))
        kr_p.append(jnp.swapaxes(krt, 1, 2))
        cv_p.append(nconv)
        ckv_s.append(ckvn_s.reshape(n_s, 1, KV_LORA))
        kr_s.append(krn_s.reshape(n_s, 1, ROPE_DIM))
        cv_s.append(jnp.stack([st[:, 1], u_s], axis=1))
    return (h_p.reshape(n_p, t_p, D_MODEL), h_s.reshape(n_s, 1, D_MODEL), jnp.stack(ckv_p), jnp.stack(kr_p),
            jnp.stack(cv_p), jnp.stack(ckv_s), jnp.stack(kr_s), jnp.stack(cv_s))
```

```python
import functools

import jax
import jax.numpy as jnp
from jax import lax
from jax.experimental import pallas as pl
from jax.experimental.pallas import tpu as pltpu

F32 = jnp.float32
BF16 = jnp.bfloat16

D_MODEL = 1024
D_CONV = 512
CONV_W = 3
N_HEADS = 8
Q_LORA = 384
KV_LORA = 256
NOPE_DIM = 64
ROPE_DIM = 32
ROPE_HALF = ROPE_DIM // 2
V_DIM = 64
QK_DIM = NOPE_DIM + ROPE_DIM
ROPE_BASE = 10000.0
ATTN_SCALE = QK_DIM ** -0.5
LOG2E = 1.4426950408889634
Q_SCALE = ATTN_SCALE * LOG2E
N_GROUPS = 4
EXP_PER_GROUP = 8
N_EXPERTS = N_GROUPS * EXP_PER_GROUP
TOP_K = 2
D_EXPERT = 512
MOE_BLOCK = 512
PAGE_SIZE = 128
LN_EPS = 1e-5
RMS_EPS = 1e-6

LANES = 128
SUBLANES = 8
ROW_TILE = D_MODEL // LANES
assert ROW_TILE == SUBLANES
HEAD_W = LANES
QK_W = N_HEADS * HEAD_W
V_W = N_HEADS * V_DIM
_C_BG, _C_CG, _C_H = 0, D_CONV, 2 * D_CONV
_C_CQ = 3 * D_CONV
_C_CKV = _C_CQ + Q_LORA
_C_KR = _C_CKV + KV_LORA
W1_COLS = _C_KR + LANES
ROUTE_W = 2 * LANES
R_EID, R_WT, R_RANK = 0, 2, 4
NEG = -1e30
VMEM_LIMIT = 56 * 1024 * 1024

TM = 512
KEY_CHUNK = 2048
ATTN_HEADS_PER_LOOP = 4
TQ = 512
TM_COMBINE = 512
GATHER_RING = 3


def _cparams(*sem):
    return pltpu.CompilerParams(dimension_semantics=sem, vmem_limit_bytes=VMEM_LIMIT)


def _dot(a, b):
    return jnp.dot(a, b, preferred_element_type=F32)


def _dot_nt(a, b):
    return lax.dot_general(a, b, (((1,), (1,)), ((), ())), preferred_element_type=F32)


def _dot_tn(a, b):
    return lax.dot_general(a, b, (((0,), (0,)), ((), ())), preferred_element_type=F32)


def _rms(x, g):
    return x * lax.rsqrt(jnp.mean(x * x, axis=-1, keepdims=True) + RMS_EPS) * g


def _layernorm(x, g, b):
    mu = jnp.mean(x, axis=-1, keepdims=True)
    xc = x - mu
    var = jnp.mean(xc * xc, axis=-1, keepdims=True)
    return xc * lax.rsqrt(var + LN_EPS) * g + b


def _rope(xh, a, bm, cp):
    return xh * a + pltpu.roll(xh, LANES - ROPE_HALF, 1) * bm + pltpu.roll(xh, ROPE_HALF, 1) * cp


def _full(shape):
    nd = len(shape)
    return pl.BlockSpec(shape, lambda *_: (0,) * nd)


def _tile_rows(ref, c, n):
    return ref.at[pl.ds(c, n, stride=ROW_TILE), :]


def _inproj_common(x_ref, tab_ref, w1_ref, qg_ref, kvg_ref, ckv_ref):
    xb = x_ref[...].astype(BF16)

    def seg(lo, hi):
        return _dot(xb, w1_ref[:, lo:hi])

    b_g = seg(_C_BG, _C_CG)
    u = seg(_C_CG, _C_H) * seg(_C_H, _C_CQ)
    cqn = _rms(seg(_C_CQ, _C_CKV), qg_ref[...]).astype(BF16)
    ckvn = _rms(seg(_C_CKV, _C_KR), kvg_ref[...])
    ckv_ref[...] = ckvn
    krr = _rope(seg(_C_KR, W1_COLS), tab_ref[3], tab_ref[4], tab_ref[5])
    return b_g, u, cqn, ckvn.astype(BF16), krr


def _inproj_prompt_body(tiles_per_seq, x_ref, tab_ref, tabt_ref, w1_ref, wuqt_ref, wuk_ref, wuvt_ref, qg_ref, kvg_ref,
                        cw_ref, bconv_ref, qt_ref, k_ref, vt_ref, ckv_ref, krt_ref, nconv_ref, uext_ref):
    tm = x_ref.shape[0]
    b_g, u, cqn, cb, krr = _inproj_common(x_ref, tab_ref, w1_ref, qg_ref, kvg_ref, ckv_ref)
    krt_ref[0] = krr.T[0:ROPE_DIM, :]
    qft = _dot_nt(wuqt_ref[...], cqn)
    cos, sin = tabt_ref[0], tabt_ref[1]
    for h in range(N_HEADS):
        r0 = h * HEAD_W
        x1 = qft[r0:r0 + ROPE_HALF, :]
        x2 = qft[r0 + ROPE_HALF:r0 + ROPE_DIM, :]
        qt_ref[r0:r0 + ROPE_HALF, :] = (x1 * cos - x2 * sin).astype(BF16)
        qt_ref[r0 + ROPE_HALF:r0 + ROPE_DIM, :] = (x1 * sin + x2 * cos).astype(BF16)
        qt_ref[r0 + ROPE_DIM:r0 + HEAD_W, :] = (qft[r0 + ROPE_DIM:r0 + HEAD_W, :] * Q_SCALE).astype(BF16)
    kn = _dot(cb, wuk_ref[...])
    for h in range(N_HEADS):
        sl = slice(h * HEAD_W, (h + 1) * HEAD_W)
        k_ref[:, sl] = (kn[:, sl] + krr).astype(BF16)
    vt_ref[...] = _dot_nt(wuvt_ref[...], cb).astype(BF16)

    first = (pl.program_id(0) % tiles_per_seq) == 0

    @pl.when(first)
    def _():
        uext_ref[0:8, :] = jnp.zeros((8, D_CONV), F32)

    @pl.when(jnp.logical_not(first))
    def _():
        uext_ref[0:8, :] = uext_ref[tm:tm + 8, :]

    uext_ref[8:8 + tm, :] = u
    conv = cw_ref[0:1, :] * uext_ref[6:6 + tm, :] + cw_ref[1:2, :] * uext_ref[7:7 + tm, :] + cw_ref[2:3, :] * u
    bconv_ref[...] = (b_g * conv).astype(BF16)
    nconv_ref[0] = u[tm - (CONV_W - 1):tm, :]


def _inproj_sample_body(x_ref, tab_ref, w1_ref, wuq_ref, wukt_ref, qg_ref, kvg_ref, cw_ref, s0_ref, s1_ref,
                        bconv_ref, q_ref, qlat_ref, ckv_ref, kr_ref, u_ref):
    b_g, u, cqn, _, krr = _inproj_common(x_ref, tab_ref, w1_ref, qg_ref, kvg_ref, ckv_ref)
    kr_ref[...] = krr[:, 0:ROPE_DIM]
    qf = _dot(cqn, wuq_ref[...])
    for h in range(N_HEADS):
        sl = slice(h * HEAD_W, (h + 1) * HEAD_W)
        qh = _rope(qf[:, sl], tab_ref[0], tab_ref[1], tab_ref[2]).astype(BF16)
        q_ref[:, sl] = qh
        qlat_ref[:, h * KV_LORA:(h + 1) * KV_LORA] = _dot(qh, wukt_ref[h]).astype(BF16)
    conv = cw_ref[0:1, :] * s0_ref[...] + cw_ref[1:2, :] * s1_ref[...] + cw_ref[2:3, :] * u
    bconv_ref[...] = (b_g * conv).astype(BF16)
    u_ref[...] = u


def _inproj_prompt(x, tab, tabt, w1, wuqt, wuk, wuvt, qg, kvg, cw, n_seq, seq):
    t = x.shape[0]
    tm = TM
    tiles_per_seq = seq // tm
    rows = lambda w: pl.BlockSpec((tm, w), lambda i: (i, 0))
    cols = lambda h: pl.BlockSpec((h, tm), lambda i: (0, i))
    return pl.pallas_call(
        functools.partial(_inproj_prompt_body, tiles_per_seq),
        grid=(t // tm,),
        in_specs=[rows(D_MODEL),
                  pl.BlockSpec((6, tm, LANES), lambda i: (0, i % tiles_per_seq, 0)),
                  pl.BlockSpec((2, ROPE_HALF, tm), lambda i: (0, 0, i % tiles_per_seq)),
                  _full(w1.shape), _full(wuqt.shape), _full(wuk.shape), _full(wuvt.shape),
                  _full(qg.shape), _full(kvg.shape), _full(cw.shape)],
        out_specs=[rows(D_CONV), cols(QK_W), rows(QK_W), cols(V_W), rows(KV_LORA),
                   pl.BlockSpec((1, ROPE_DIM, tm), lambda i: (i // tiles_per_seq, 0, i % tiles_per_seq)),
                   pl.BlockSpec((1, CONV_W - 1, D_CONV), lambda i: (i // tiles_per_seq, 0, 0))],
        out_shape=[jax.ShapeDtypeStruct((t, D_CONV), BF16), jax.ShapeDtypeStruct((QK_W, t), BF16),
                   jax.ShapeDtypeStruct((t, QK_W), BF16), jax.ShapeDtypeStruct((V_W, t), BF16),
                   jax.ShapeDtypeStruct((t, KV_LORA), F32), jax.ShapeDtypeStruct((n_seq, ROPE_DIM, seq), F32),
                   jax.ShapeDtypeStruct((n_seq, CONV_W - 1, D_CONV), F32)],
        scratch_shapes=[pltpu.VMEM((tm + 8, D_CONV), F32)],
        compiler_params=_cparams("arbitrary"),
    )(x, tab, tabt, w1, wuqt, wuk, wuvt, qg, kvg, cw)


def _inproj_sample(x, tab, w1, wuq, wukt, qg, kvg, cw, s0, s1):
    n = x.shape[0]
    args = (x, tab, w1, wuq, wukt, qg, kvg, cw, s0, s1)
    out_shape = [jax.ShapeDtypeStruct((n, D_CONV), BF16), jax.ShapeDtypeStruct((n, QK_W), BF16),
                 jax.ShapeDtypeStruct((n, N_HEADS * KV_LORA), BF16), jax.ShapeDtypeStruct((n, KV_LORA), F32),
                 jax.ShapeDtypeStruct((n, ROPE_DIM), F32), jax.ShapeDtypeStruct((n, D_CONV), F32)]
    return pl.pallas_call(
        _inproj_sample_body,
        grid=(1,),
        in_specs=[_full(a.shape) for a in args],
        out_specs=[_full(s.shape) for s in out_shape],
        out_shape=out_shape,
        compiler_params=_cparams("arbitrary"),
    )(*args)


def _prompt_attention_body(qt_ref, k_ref, vt_ref, ot_ref, acc_ref, m_ref, l_ref, sa_ref, sb_ref):
    tq = qt_ref.shape[1]
    i = pl.program_id(1)

    for hg in range(N_HEADS // ATTN_HEADS_PER_LOOP):
        heads = tuple(range(hg * ATTN_HEADS_PER_LOOP, (hg + 1) * ATTN_HEADS_PER_LOOP))
        acc_ref[...] = jnp.zeros(acc_ref.shape, F32)
        m_ref[...] = jnp.full(m_ref.shape, NEG, F32)
        l_ref[...] = jnp.zeros(l_ref.shape, F32)

        def scores(j, s_ref):
            off = pl.multiple_of(j * tq, tq)
            for idx, h in enumerate(heads):
                s_ref[idx] = _dot(k_ref[pl.ds(off, tq), h * HEAD_W:(h + 1) * HEAD_W],
                                  qt_ref[h * HEAD_W:(h + 1) * HEAD_W, :])

        def consume(j, s_ref, masked):
            off = pl.multiple_of(j * tq, tq)
            for idx, h in enumerate(heads):
                s = s_ref[idx]
                if masked:
                    krow = lax.broadcasted_iota(jnp.int32, (tq, tq), 0)
                    qcol = lax.broadcasted_iota(jnp.int32, (tq, tq), 1)
                    s = jnp.where(krow <= qcol, s, NEG)
                m = m_ref[idx]
                m_new = jnp.maximum(m, jnp.max(s, axis=0, keepdims=True))
                alpha = jnp.exp2(m - m_new)
                p = jnp.exp2(s - m_new)
                m_ref[idx] = m_new
                l_ref[idx] = alpha * l_ref[idx] + jnp.sum(p, axis=0, keepdims=True)
                vblk = vt_ref[h * V_DIM:(h + 1) * V_DIM, pl.ds(off, tq)]
                acc_ref[idx] = alpha * acc_ref[idx] + _dot(vblk, p.astype(BF16))

        scores(0, sa_ref)

        def pair(pp, c):
            j = 2 * pp
            scores(j + 1, sb_ref)
            consume(j, sa_ref, False)
            scores(j + 2, sa_ref)
            consume(j + 1, sb_ref, False)
            return c
        lax.fori_loop(0, i // 2, pair, 0)

        @pl.when(i % 2 == 1)
        def _():
            scores(i, sb_ref)
            consume(i - 1, sa_ref, False)
            consume(i, sb_ref, True)

        @pl.when(i % 2 == 0)
        def _():
            consume(i, sa_ref, True)

        for idx, h in enumerate(heads):
            ot_ref[h * V_DIM:(h + 1) * V_DIM, :] = (acc_ref[idx] / l_ref[idx]).astype(BF16)


def _prompt_attention(qt, k, vt, n_seq, seq):
    t = k.shape[0]
    tq = TQ
    nq = seq // tq
    return pl.pallas_call(
        _prompt_attention_body,
        grid=(n_seq, nq),
        in_specs=[pl.BlockSpec((QK_W, tq), lambda b, i: (0, b * nq + i)),
                  pl.BlockSpec((seq, QK_W), lambda b, i: (b, 0)),
                  pl.BlockSpec((V_W, seq), lambda b, i: (0, b))],
        out_specs=pl.BlockSpec((V_W, tq), lambda b, i: (0, b * nq + i)),
        out_shape=jax.ShapeDtypeStruct((V_W, t), BF16),
        scratch_shapes=[pltpu.VMEM((ATTN_HEADS_PER_LOOP, V_DIM, tq), F32),
                        pltpu.VMEM((ATTN_HEADS_PER_LOOP, 1, tq), F32), pltpu.VMEM((ATTN_HEADS_PER_LOOP, 1, tq), F32),
                        pltpu.VMEM((ATTN_HEADS_PER_LOOP, tq, tq), F32), pltpu.VMEM((ATTN_HEADS_PER_LOOP, tq, tq), F32)],
        compiler_params=_cparams("arbitrary", "arbitrary"),
    )(qt, k, vt)


def _sample_attention_body(n_pages, pt_ref, qlat_ref, q_ref, ckvn_ref, krn_ref, cc_ref, cr_ref, olat_ref,
                           bufc, bufr, kcb, krb, sem):
    b = pl.program_id(0)
    nb = pl.num_programs(0)
    slot = b % 2
    past = n_pages * PAGE_SIZE

    def page_copies(bb, sl, p):
        pg = pt_ref[bb, p]
        pos = pl.ds(pl.multiple_of(p * PAGE_SIZE, PAGE_SIZE), PAGE_SIZE)
        return (pltpu.make_async_copy(cc_ref.at[pg], bufc.at[sl, pos], sem.at[0, sl]),
                pltpu.make_async_copy(cr_ref.at[pg], bufr.at[sl, :, pos], sem.at[1, sl]))

    def issue(bb, sl):
        for p in range(n_pages):
            for c in page_copies(bb, sl, p):
                c.start()

    def wait(sl):
        pltpu.make_async_copy(bufc.at[sl], bufc.at[sl], sem.at[0, sl]).wait()
        pltpu.make_async_copy(bufr.at[sl], bufr.at[sl], sem.at[1, sl]).wait()

    @pl.when(b == 0)
    def _():
        issue(0, 0)

    wait(slot)
    issue(jnp.minimum(b + 1, nb - 1), 1 - slot)

    ql = qlat_ref[0]
    qr = q_ref[0][:, 0:ROPE_DIM]
    cn = ckvn_ref[0]
    rn = krn_ref[0]
    chunk = min(KEY_CHUNK, past)
    scores = []
    for c in range(past // chunk):
        pos = slice(c * chunk, (c + 1) * chunk)
        kcb[pos, :] = bufc[slot, pos, :].astype(BF16)
        krb[:, pos] = bufr[slot, :, pos].astype(BF16)
        scores.append(_dot_nt(ql, kcb[pos, :]) + _dot(qr, krb[:, pos]))
    s_new = (jnp.sum(ql.astype(F32) * cn, axis=-1, keepdims=True)
             + jnp.sum(qr.astype(F32) * rn, axis=-1, keepdims=True))
    m = s_new
    for s in scores:
        m = jnp.maximum(m, jnp.max(s, axis=-1, keepdims=True))
    p_new = jnp.exp2(s_new - m)
    l = p_new
    o = p_new * cn
    for c, s in enumerate(scores):
        p = jnp.exp2(s - m)
        l = l + jnp.sum(p, axis=-1, keepdims=True)
        o = o + _dot(p.astype(BF16), kcb[c * chunk:(c + 1) * chunk, :])
    olat_ref[0] = o / l

    @pl.when(b == nb - 1)
    def _():
        wait(1 - slot)


def _sample_attention(page_table, qlat, q, ckvn, krn, cache_c, cache_rt):
    n, n_pages = page_table.shape
    past = n_pages * PAGE_SIZE
    blk = lambda d1, d2: pl.BlockSpec((1, d1, d2), lambda b, pt: (b, 0, 0))
    grid_spec = pltpu.PrefetchScalarGridSpec(
        num_scalar_prefetch=1,
        grid=(n,),
        in_specs=[blk(N_HEADS, KV_LORA), blk(N_HEADS, HEAD_W), blk(1, KV_LORA), blk(1, ROPE_DIM),
                  pl.BlockSpec(memory_space=pl.ANY), pl.BlockSpec(memory_space=pl.ANY)],
        out_specs=blk(N_HEADS, KV_LORA),
        scratch_shapes=[pltpu.VMEM((2, past, KV_LORA), F32), pltpu.VMEM((2, ROPE_DIM, past), F32),
                        pltpu.VMEM((past, KV_LORA), BF16), pltpu.VMEM((ROPE_DIM, past), BF16),
                        pltpu.SemaphoreType.DMA((2, 2))],
    )
    return pl.pallas_call(
        functools.partial(_sample_attention_body, n_pages),
        grid_spec=grid_spec,
        out_shape=jax.ShapeDtypeStruct((n, N_HEADS, KV_LORA), F32),
        compiler_params=_cparams("arbitrary"),
    )(page_table, qlat.reshape(n, N_HEADS, KV_LORA), q.reshape(n, N_HEADS, HEAD_W),
      ckvn.reshape(n, 1, KV_LORA), krn.reshape(n, 1, ROPE_DIM), cache_c, cache_rt)


def _route(logits):
    lane_i = lax.broadcasted_iota(jnp.int32, (logits.shape[0], LANES), 1)
    lane = lane_i.astype(F32)
    first_at = lambda hit: jnp.min(jnp.where(hit, lane, float(LANES)), axis=-1, keepdims=True)
    lg = jnp.where(lane_i < N_GROUPS, logits[:, 0:LANES], NEG)
    mg = jnp.max(lg, axis=-1, keepdims=True)
    p_grp = 1.0 / jnp.sum(jnp.exp(lg - mg), axis=-1, keepdims=True)
    grp = first_at(lg == mg)
    lane_grp = (lane_i // EXP_PER_GROUP).astype(F32)
    le = jnp.where(lane_grp == grp, logits[:, LANES:2 * LANES], NEG)
    top1 = jnp.max(le, axis=-1, keepdims=True)
    i1 = first_at(le == top1)
    le2 = jnp.where(lane == i1, NEG, le)
    top2 = jnp.max(le2, axis=-1, keepdims=True)
    i2 = first_at(le2 == top2)
    e2 = jnp.exp(top2 - top1)
    w1 = p_grp / (1.0 + e2)
    w2 = p_grp * e2 / (1.0 + e2)
    out = jnp.where(lane_i == R_EID, i1, 0.0)
    out = jnp.where(lane_i == R_EID + 1, i2, out)
    out = jnp.where(lane_i == R_WT, w1, out)
    return jnp.where(lane_i == R_WT + 1, w2, out)


def _add_ranks(route, ltri, count_ref):
    n = route.shape[0]
    lane_i = lax.broadcasted_iota(jnp.int32, (n, LANES), 1)
    lane = lane_i.astype(F32)
    hit1 = lane == route[:, R_EID:R_EID + 1]
    hit2 = lane == route[:, R_EID + 1:R_EID + 2]
    chosen = jnp.where(hit1, 1.0, jnp.where(hit2, 1.0, 0.0))
    before = _dot(ltri, chosen.astype(BF16)) + count_ref[...]
    r1 = jnp.sum(jnp.where(hit1, before, 0.0), axis=-1, keepdims=True)
    r2 = jnp.sum(jnp.where(hit2, before, 0.0), axis=-1, keepdims=True)
    count_ref[...] = count_ref[...] + jnp.sum(chosen, axis=0, keepdims=True)
    route = jnp.where(lane_i == R_RANK, r1, route)
    return jnp.where(lane_i == R_RANK + 1, r2, route)


def _merge_rows(alpha, x, bconv, y_attn, wg_ref, wbc_ref, wo_ref, lg_ref, lb_ref, wr_ref, br_ref):
    g = _dot(x.astype(BF16), wg_ref[...])
    y_conv = _dot(bconv, wbc_ref[...])
    m = jax.nn.sigmoid(g[:, 0:D_MODEL]) * y_conv + jax.nn.sigmoid(g[:, D_MODEL:]) * y_attn
    mix = _dot(m.astype(BF16), wo_ref[...])
    h1 = _layernorm(alpha * x + mix, lg_ref[...], lb_ref[...])
    return h1, _route(_dot(h1.astype(BF16), wr_ref[...]) + br_ref[...])


def _merge_sample_body(alpha, x_ref, bconv_ref, olat_ref, wbd_ref, wba_ref, *refs):
    w_refs, (h1_ref, route_ref) = refs[:-2], refs[-2:]
    o = _dot(olat_ref[...].astype(BF16), wbd_ref[...]).astype(BF16)
    h1_ref[...], route_ref[...] = _merge_rows(alpha, x_ref[...], bconv_ref[...], _dot(o, wba_ref[...]), *w_refs)


def _merge_prompt_body(alpha, n_tiles, x_ref, bconv_ref, ot_ref, h1s_ref, routes_ref, wba_ref, *refs):
    w_refs, (h1t_ref, route_ref, routet_ref, counts_ref, ltri_ref, count_ref) = refs[:-6], refs[-6:]
    i = pl.program_id(0)
    tm = x_ref.shape[0]

    @pl.when(i == 0)
    def _():
        r = lax.broadcasted_iota(jnp.int32, (tm, tm), 0)
        c = lax.broadcasted_iota(jnp.int32, (tm, tm), 1)
        ltri_ref[...] = jnp.where(c < r, 1.0, 0.0).astype(BF16)
        count_ref[...] = jnp.zeros(count_ref.shape, F32)

    @pl.when(i < n_tiles)
    def _():
        y_attn = _dot_tn(ot_ref[...], wba_ref[...])
        h1, route = _merge_rows(alpha, x_ref[...], bconv_ref[...], y_attn, *w_refs)
        for c in range(ROW_TILE):
            _tile_rows(h1t_ref, c, tm)[...] = h1[:, c * LANES:(c + 1) * LANES]
        route_ref[...] = _add_ranks(route, ltri_ref[...], count_ref)

    @pl.when(i == n_tiles)
    def _():
        n_s = h1s_ref.shape[0]
        h1t_ref[...] = jnp.zeros(h1t_ref.shape, F32)
        route_ref[...] = jnp.zeros(route_ref.shape, F32)
        for c in range(ROW_TILE):
            _tile_rows(h1t_ref, c, n_s)[...] = h1s_ref[:, c * LANES:(c + 1) * LANES]
        route_ref[0:n_s, :] = _add_ranks(routes_ref[...], ltri_ref[0:n_s, 0:n_s], count_ref)

    counts_ref[...] = jnp.broadcast_to(count_ref[...], counts_ref.shape)
    routet_ref[...] = route_ref[...].T[0:SUBLANES, :]


def _merge_sample(alpha, x, bconv, olat, wbd, wba, ws):
    n = x.shape[0]
    args = (x, bconv, olat, wbd, wba) + tuple(ws)
    out_shape = [jax.ShapeDtypeStruct((n, D_MODEL), F32), jax.ShapeDtypeStruct((n, LANES), F32)]
    return pl.pallas_call(
        functools.partial(_merge_sample_body, alpha),
        grid=(1,),
        in_specs=[_full(a.shape) for a in args],
        out_specs=[_full(s.shape) for s in out_shape],
        out_shape=out_shape,
        compiler_params=_cparams("arbitrary"),
    )(*args)


def _merge_prompt(alpha, x, bconv, ot, h1_s, route_s, wba, ws):
    t = x.shape[0]
    tm = TM
    n_tiles = t // tm
    t_all = t + h1_s.shape[0]
    assert h1_s.shape[0] <= tm
    clamp = lambda i: jnp.minimum(i, n_tiles - 1)
    rows_in = lambda w: pl.BlockSpec((tm, w), lambda i: (clamp(i), 0))
    return pl.pallas_call(
        functools.partial(_merge_prompt_body, alpha, n_tiles),
        grid=(n_tiles + 1,),
        in_specs=[rows_in(D_MODEL), rows_in(D_CONV), pl.BlockSpec((V_W, tm), lambda i: (0, clamp(i))),
                  _full(h1_s.shape), _full(route_s.shape), _full(wba.shape)] + [_full(w.shape) for w in ws],
        out_specs=[pl.BlockSpec((tm * ROW_TILE, LANES), lambda i: (i, 0)), pl.BlockSpec((tm, LANES), lambda i: (i, 0)),
                   pl.BlockSpec((SUBLANES, tm), lambda i: (0, i)), _full((SUBLANES, LANES))],
        out_shape=[jax.ShapeDtypeStruct((t_all * ROW_TILE, LANES), F32), jax.ShapeDtypeStruct((t_all, LANES), F32),
                   jax.ShapeDtypeStruct((SUBLANES, t_all), F32), jax.ShapeDtypeStruct((SUBLANES, LANES), F32)],
        scratch_shapes=[pltpu.VMEM((tm, tm), BF16), pltpu.VMEM((1, LANES), F32)],
        compiler_params=_cparams("arbitrary"),
    )(x, bconv, ot, h1_s, route_s, wba, *ws)


SLOT_GROUP = 16


def _slot_tokens_body(rows_ref, dest_ref, tok_ref):
    def clear_row(j, c):
        for l in range(LANES):
            tok_ref[j * LANES + l] = 0
        return c
    lax.fori_loop(0, rows_ref[1], clear_row, 0)

    rows_k = rows_ref[0]
    for k in range(TOP_K):
        def fill_row(j, c):
            for g in range(0, LANES, SLOT_GROUP):
                ds = [dest_ref[k * rows_k + j, g + l] for l in range(SLOT_GROUP)]
                for l, d in enumerate(ds):
                    tok_ref[d] = j * LANES + g + l
            return c
        lax.fori_loop(0, rows_k, fill_row, 0)


def _slot_tokens(dest, n_blocks):
    n_slots = n_blocks * MOE_BLOCK
    n_tok = dest.shape[1]
    assert n_slots % LANES == 0 and n_tok % LANES == 0
    dest = dest.reshape(-1, LANES)
    grid_spec = pltpu.PrefetchScalarGridSpec(
        num_scalar_prefetch=2,
        grid=(1,),
        in_specs=[],
        out_specs=pl.BlockSpec(memory_space=pltpu.SMEM),
    )
    return pl.pallas_call(
        _slot_tokens_body,
        grid_spec=grid_spec,
        out_shape=jax.ShapeDtypeStruct((n_slots,), jnp.int32),
        compiler_params=_cparams("arbitrary"),
    )(jnp.array([n_tok // LANES, n_slots // LANES], jnp.int32), dest)


def _row_tile(i):
    return pl.ds(pl.multiple_of(i * ROW_TILE, ROW_TILE), ROW_TILE)


def _experts_body(blk_e_ref, used_ref, tok_ref, h1t_ref, wg_ref, wu_ref, wd_ref, ys_ref,
                  xbuf, xb, wgb, wub, wdb, sem_in):
    b = pl.program_id(0)
    ring = lambda bb: lax.rem(bb, GATHER_RING)
    used = used_ref[0]
    blk_rows = MOE_BLOCK * ROW_TILE
    changed = jnp.logical_or(b == 0, blk_e_ref[b] != blk_e_ref[jnp.maximum(b - 1, 0)])

    def in_copy(bb, sl, r):
        tok = tok_ref[bb * MOE_BLOCK + r]
        return pltpu.make_async_copy(h1t_ref.at[_row_tile(tok)], xbuf.at[sl, _row_tile(r)], sem_in.at[sl])

    def issue_in(bb, sl):
        for r in range(MOE_BLOCK):
            in_copy(bb, sl, r).start(priority=r % 2)

    def wait_in(sl):
        pltpu.make_async_copy(h1t_ref.at[pl.ds(0, blk_rows)], xbuf.at[sl], sem_in.at[sl]).wait()

    last = used - 1

    @pl.when(jnp.logical_and(b == 0, used > 0))
    def _():
        for a in range(GATHER_RING - 1):
            issue_in(jnp.minimum(a, last), a)

    @pl.when(jnp.logical_and(b < used, changed))
    def _():
        wgb[...] = wg_ref[0].astype(BF16)
        wub[...] = wu_ref[0].astype(BF16)
        wdb[...] = wd_ref[0].astype(BF16)

    @pl.when(b < used)
    def _():
        wait_in(ring(b))
        for c in range(ROW_TILE):
            xb[:, c * LANES:(c + 1) * LANES] = _tile_rows(xbuf.at[ring(b)], c, MOE_BLOCK)[...].astype(BF16)
        ahead = b + GATHER_RING - 1
        issue_in(jnp.minimum(ahead, last), ring(ahead))
        x = xb[...]
        g = _dot(x, wgb[...])
        u = _dot(x, wub[...])
        h = (g * jax.nn.sigmoid(g) * u).astype(BF16)
        y = _dot(h, wdb[...])
        for c in range(ROW_TILE):
            _tile_rows(ys_ref, c, MOE_BLOCK)[...] = y[:, c * LANES:(c + 1) * LANES]

        @pl.when(b == last)
        def _():
            for a in range(1, GATHER_RING):
                wait_in(ring(b + a))

    @pl.when(b >= used)
    def _():
        ys_ref[...] = jnp.zeros(ys_ref.shape, F32)


def _experts(blk_e, used, slot_tok, h1t, wg, wu, wd):
    n_blocks = blk_e.shape[0]
    blk_rows = MOE_BLOCK * ROW_TILE
    by_expert = lambda b, e, u, a: (e[b], 0, 0)
    grid_spec = pltpu.PrefetchScalarGridSpec(
        num_scalar_prefetch=3,
        grid=(n_blocks,),
        in_specs=[pl.BlockSpec(memory_space=pl.ANY),
                  pl.BlockSpec((1, D_MODEL, D_EXPERT), by_expert), pl.BlockSpec((1, D_MODEL, D_EXPERT), by_expert),
                  pl.BlockSpec((1, D_EXPERT, D_MODEL), by_expert)],
        out_specs=pl.BlockSpec((blk_rows, LANES), lambda b, e, u, a: (b, 0)),
        scratch_shapes=[pltpu.VMEM((GATHER_RING, blk_rows, LANES), F32),
                        pltpu.VMEM((MOE_BLOCK, D_MODEL), BF16),
                        pltpu.VMEM((D_MODEL, D_EXPERT), BF16), pltpu.VMEM((D_MODEL, D_EXPERT), BF16),
                        pltpu.VMEM((D_EXPERT, D_MODEL), BF16),
                        pltpu.SemaphoreType.DMA((GATHER_RING,))],
    )
    return pl.pallas_call(
        _experts_body,
        grid_spec=grid_spec,
        out_shape=jax.ShapeDtypeStruct((n_blocks * blk_rows, LANES), F32),
        compiler_params=_cparams("arbitrary"),
    )(blk_e, used, slot_tok, h1t, wg, wu, wd)


def _combine_body(alpha, tok0, n_all, dest_ref, ys_ref, h1t_ref, route_ref, lg_ref, lb_ref, out_ref, gbuf, sem):
    i = pl.program_id(0)
    nt = pl.num_programs(0)
    slot = i % 2
    tm = out_ref.shape[0]

    def issue(ii, sl):
        base = tok0 + ii * tm
        for r in range(tm):
            for k in range(TOP_K):
                d = dest_ref[base + k * n_all + r]
                pltpu.make_async_copy(ys_ref.at[_row_tile(d)], gbuf.at[sl, k, _row_tile(r)],
                                      sem.at[sl]).start(priority=k)

    def wait(sl):
        for k in range(TOP_K):
            pltpu.make_async_copy(ys_ref.at[pl.ds(0, tm * ROW_TILE)], gbuf.at[sl, k], sem.at[sl]).wait()

    @pl.when(i == 0)
    def _():
        issue(0, 0)

    wait(slot)
    issue(jnp.minimum(i + 1, nt - 1), 1 - slot)

    route = route_ref[...]
    w0 = jnp.broadcast_to(route[:, R_WT:R_WT + 1], (tm, LANES))
    w1 = jnp.broadcast_to(route[:, R_WT + 1:R_WT + 2], (tm, LANES))
    z = [alpha * _tile_rows(h1t_ref, c, tm)[...]
         + w0 * _tile_rows(gbuf.at[slot, 0], c, tm)[...] + w1 * _tile_rows(gbuf.at[slot, 1], c, tm)[...]
         for c in range(ROW_TILE)]
    mu = sum(jnp.sum(zc, axis=-1, keepdims=True) for zc in z) / D_MODEL
    zc = [v - mu for v in z]
    var = sum(jnp.sum(v * v, axis=-1, keepdims=True) for v in zc) / D_MODEL
    rstd = lax.rsqrt(var + LN_EPS)
    for c in range(ROW_TILE):
        sl = slice(c * LANES, (c + 1) * LANES)
        out_ref[:, sl] = zc[c] * rstd * lg_ref[:, sl] + lb_ref[:, sl]

    @pl.when(i == nt - 1)
    def _():
        wait(1 - slot)


def _combine(alpha, tok0, n_tok, tm, dest, ys, h1t, route, lg, lb):
    assert n_tok % tm == 0 and tok0 % tm == 0
    blk0 = tok0 // tm
    grid_spec = pltpu.PrefetchScalarGridSpec(
        num_scalar_prefetch=1,
        grid=(n_tok // tm,),
        in_specs=[pl.BlockSpec(memory_space=pl.ANY),
                  pl.BlockSpec((tm * ROW_TILE, LANES), lambda i, d: (blk0 + i, 0)),
                  pl.BlockSpec((tm, LANES), lambda i, d: (blk0 + i, 0)),
                  pl.BlockSpec(lg.shape, lambda i, d: (0, 0)), pl.BlockSpec(lb.shape, lambda i, d: (0, 0))],
        out_specs=pl.BlockSpec((tm, D_MODEL), lambda i, d: (i, 0)),
        scratch_shapes=[pltpu.VMEM((2, TOP_K, tm * ROW_TILE, LANES), F32), pltpu.SemaphoreType.DMA((2,))],
    )
    return pl.pallas_call(
        functools.partial(_combine_body, alpha, tok0, route.shape[0]),
        grid_spec=grid_spec,
        out_shape=jax.ShapeDtypeStruct((n_tok, D_MODEL), F32),
        compiler_params=_cparams("arbitrary"),
    )(dest, ys, h1t, route, lg, lb)


def _rope_tables(pos):
    inv = ROPE_BASE ** (-(jnp.arange(ROPE_HALF, dtype=F32) * 2.0 / ROPE_DIM))
    ang = pos.astype(F32)[:, None] * inv[None, :]
    cos, sin = jnp.cos(ang), jnp.sin(ang)
    n = pos.shape[0]
    a = jnp.concatenate([cos, cos, jnp.ones((n, LANES - ROPE_DIM), F32)], axis=1)
    bm = jnp.concatenate([-sin, jnp.zeros((n, LANES - ROPE_HALF), F32)], axis=1)
    cp = jnp.concatenate([jnp.zeros((n, ROPE_HALF), F32), sin, jnp.zeros((n, LANES - ROPE_DIM), F32)], axis=1)
    k_tabs = jnp.stack([a, bm, cp])
    return jnp.concatenate([k_tabs * Q_SCALE, k_tabs]), jnp.stack([cos.T, sin.T]) * Q_SCALE


def _head_blocks(w_rope, w_nope):
    k = w_nope.shape[0]
    pad = jnp.zeros((k, N_HEADS, HEAD_W - QK_DIM), w_nope.dtype)
    return jnp.concatenate([w_rope, w_nope, pad], axis=-1).reshape(k, QK_W)


def _pack_weights(w_in, w_uq, w_uk, w_uv, router_w_group, router_b_group, router_w_expert, router_b_expert):
    d = w_in.shape[0]
    c_kr = 3 * D_CONV + Q_LORA + KV_LORA
    w1 = jnp.concatenate([w_in[:, :c_kr], w_in[:, c_kr:c_kr + ROPE_DIM],
                          jnp.zeros((d, LANES - ROPE_DIM), w_in.dtype)], axis=1).astype(BF16)
    wg = w_in[:, c_kr + ROPE_DIM:].astype(BF16)
    uq = w_uq.reshape(Q_LORA, N_HEADS, QK_DIM)
    wuq = _head_blocks(uq[..., NOPE_DIM:], uq[..., :NOPE_DIM]).astype(BF16)
    wuk = _head_blocks(jnp.zeros((KV_LORA, N_HEADS, ROPE_DIM), w_uk.dtype), w_uk).astype(BF16)
    wukt = jnp.transpose(wuk.reshape(KV_LORA, N_HEADS, HEAD_W), (1, 2, 0))
    wuvt = w_uv.reshape(KV_LORA, V_W).T.astype(BF16)
    eye = jnp.eye(N_HEADS, dtype=w_uv.dtype)
    wbd = jnp.einsum('chd,hg->hcgd', w_uv, eye).reshape(N_HEADS * KV_LORA, V_W).astype(BF16)
    wr = jnp.zeros((d, ROUTE_W), F32)
    wr = wr.at[:, 0:N_GROUPS].set(router_w_group).at[:, LANES:LANES + N_EXPERTS].set(router_w_expert).astype(BF16)
    br = jnp.zeros((1, ROUTE_W), F32)
    br = br.at[0, 0:N_GROUPS].set(router_b_group).at[0, LANES:LANES + N_EXPERTS].set(router_b_expert)
    return w1, wg, wuq, wuq.T, wuk, wukt, wuvt, wbd, wr, br


def _dispatch_plan(route_t, counts, n_blocks):
    counts = counts[0, 0:N_EXPERTS].astype(jnp.int32)
    pcounts = (counts + MOE_BLOCK - 1) // MOE_BLOCK * MOE_BLOCK
    pend = jnp.cumsum(pcounts).astype(jnp.int32)
    pstart = (pend - pcounts).astype(F32)
    eid = route_t[R_EID:R_EID + TOP_K]
    rank = route_t[R_RANK:R_RANK + TOP_K]
    base = jnp.zeros_like(eid)
    for e in range(N_EXPERTS):
        base = jnp.where(eid == float(e), pstart[e], base)
    dest = (base + rank).astype(jnp.int32)
    blk_start = jnp.arange(n_blocks, dtype=jnp.int32) * MOE_BLOCK
    blk_e = jnp.sum((blk_start[:, None] >= pend[None, :]).astype(jnp.int32), axis=1)
    blk_e = jnp.minimum(blk_e, N_EXPERTS - 1).astype(jnp.int32)
    used = (pend[-1:] // MOE_BLOCK).astype(jnp.int32)
    return dest, blk_e, used


def kernel(x_prompt, x_sample, cache_ckv, cache_krope, state_conv, page_table, w_in, conv_w, q_norm_g, w_uq,
           kv_norm_g, w_uk, w_uv, w_br_conv, w_br_attn, w_o, ln1_g, ln1_b, router_w_group, router_b_group,
           router_w_expert, router_b_expert, w_gate, w_up, w_down, ln2_g, ln2_b):
    depth = w_in.shape[0]
    alpha = (2 * depth) ** 0.25
    n_p, t_p, _ = x_prompt.shape
    n_s, t_s, _ = x_sample.shape
    assert t_s == 1 and t_p % TM == 0 and n_s % LANES == 0
    rows_p = n_p * t_p
    rows_all = rows_p + n_s
    past = page_table.shape[1] * PAGE_SIZE
    tab_p, tabt_p = _rope_tables(jnp.arange(t_p))
    tab_s, _ = _rope_tables(jnp.full((n_s,), past, jnp.int32))
    n_blocks = -(-(rows_all * TOP_K) // MOE_BLOCK) + N_EXPERTS

    h_p = x_prompt.reshape(rows_p, D_MODEL)
    h_s = x_sample.reshape(n_s, D_MODEL)
    ckv_p, kr_p, cv_p, ckv_s, kr_s, cv_s = [], [], [], [], [], []
    row = lambda v: v.reshape(1, -1)
    for l in range(depth):
        w1, wg, wuq, wuqt, wuk, wukt, wuvt, wbd, wr, br = _pack_weights(
            w_in[l], w_uq[l], w_uk[l], w_uv[l], router_w_group[l], router_b_group[l],
            router_w_expert[l], router_b_expert[l])
        qg, kvg = row(q_norm_g[l]), row(kv_norm_g[l])
        wba = w_br_attn[l].astype(BF16)
        merge_w = (wg, w_br_conv[l].astype(BF16), w_o[l].astype(BF16), row(ln1_g[l]), row(ln1_b[l]), wr, br)

        bconv, qt, k, vt, ckv, krt, nconv = _inproj_prompt(h_p, tab_p, tabt_p, w1, wuqt, wuk, wuvt, qg, kvg, conv_w[l],
                                                          n_p, t_p)
        ot = _prompt_attention(qt, k, vt, n_p, t_p)

        st = state_conv[l]
        bconv_s, q_s, qlat, ckvn_s, krn_s, u_s = _inproj_sample(
            h_s, tab_s, w1, wuq, wukt, qg, kvg, conv_w[l], st[:, 0], st[:, 1])
        olat = _sample_attention(page_table, qlat, q_s, ckvn_s, krn_s, cache_ckv[l],
                                 jnp.swapaxes(cache_krope[l], 1, 2))
        h1_s, route_s = _merge_sample(alpha, h_s, bconv_s, olat.reshape(n_s, N_HEADS * KV_LORA), wbd, wba, merge_w)
        h1t, route, route_t, counts = _merge_prompt(alpha, h_p, bconv, ot, h1_s, route_s, wba, merge_w)

        dest, blk_e, used = _dispatch_plan(route_t, counts, n_blocks)
        slot_tok = _slot_tokens(dest, n_blocks)
        ys = _experts(blk_e, used, slot_tok, h1t, w_gate[l], w_up[l], w_down[l])
        dest = dest.reshape(-1)
        ln2 = (row(ln2_g[l]), row(ln2_b[l]))
        h_p = _combine(alpha, 0, rows_p, TM_COMBINE, dest, ys, h1t, route, *ln2)
        h_s = _combine(alpha, rows_p, n_s, n_s, dest, ys, h1t, route, *ln2)

        ckv_p.append(ckv.reshape(n_p, t_p, KV_LORA))
        kr_p.append(jnp.swapaxes(krt, 1, 2))
        cv_p.append(nconv)
        ckv_s.append(ckvn_s.reshape(n_s, 1, KV_LORA))
        kr_s.append(krn_s.reshape(n_s, 1, ROPE_DIM))
        cv_s.append(jnp.stack([st[:, 1], u_s], axis=1))
    return (h_p.reshape(n_p, t_p, D_MODEL), h_s.reshape(n_s, 1, D_MODEL), jnp.stack(ckv_p), jnp.stack(kr_p),
            jnp.stack(cv_p), jnp.stack(ckv_s), jnp.stack(kr_s), jnp.stack(cv_s))
```

```python
import functools

import jax
import jax.numpy as jnp
from jax import lax
from jax.experimental import pallas as pl
from jax.experimental.pallas import tpu as pltpu

F32 = jnp.float32
BF16 = jnp.bfloat16

D_MODEL = 1024
D_CONV = 512
CONV_W = 3
N_HEADS = 8
Q_LORA = 384
KV_LORA = 256
NOPE_DIM = 64
ROPE_DIM = 32
ROPE_HALF = ROPE_DIM // 2
V_DIM = 64
QK_DIM = NOPE_DIM + ROPE_DIM
ROPE_BASE = 10000.0
ATTN_SCALE = QK_DIM ** -0.5
LOG2E = 1.4426950408889634
Q_SCALE = ATTN_SCALE * LOG2E
N_GROUPS = 4
EXP_PER_GROUP = 8
N_EXPERTS = N_GROUPS * EXP_PER_GROUP
TOP_K = 2
D_EXPERT = 512
MOE_BLOCK = 256
PAGE_SIZE = 128
LN_EPS = 1e-5
RMS_EPS = 1e-6

LANES = 128
SUBLANES = 8
ROW_TILE = D_MODEL // LANES
assert ROW_TILE == SUBLANES
HEAD_W = LANES
QK_W = N_HEADS * HEAD_W
V_W = N_HEADS * V_DIM
_C_BG, _C_CG, _C_H = 0, D_CONV, 2 * D_CONV
_C_CQ = 3 * D_CONV
_C_CKV = _C_CQ + Q_LORA
_C_KR = _C_CKV + KV_LORA
W1_COLS = _C_KR + LANES
ROUTE_W = 2 * LANES
R_EID, R_WT, R_RANK = 0, 2, 4
NEG = -1e30
VMEM_LIMIT = 56 * 1024 * 1024

TM = 512
KEY_CHUNK = 2048
ATTN_HEADS_PER_LOOP = 4
TQ = 512
TM_COMBINE = 256
GATHER_RING = 4


def _cparams(*sem):
    return pltpu.CompilerParams(dimension_semantics=sem, vmem_limit_bytes=VMEM_LIMIT)


def _dot(a, b):
    return jnp.dot(a, b, preferred_element_type=F32)


def _dot_nt(a, b):
    return lax.dot_general(a, b, (((1,), (1,)), ((), ())), preferred_element_type=F32)


def _dot_tn(a, b):
    return lax.dot_general(a, b, (((0,), (0,)), ((), ())), preferred_element_type=F32)


def _rms(x, g):
    return x * lax.rsqrt(jnp.mean(x * x, axis=-1, keepdims=True) + RMS_EPS) * g


def _layernorm(x, g, b):
    mu = jnp.mean(x, axis=-1, keepdims=True)
    xc = x - mu
    var = jnp.mean(xc * xc, axis=-1, keepdims=True)
    return xc * lax.rsqrt(var + LN_EPS) * g + b


def _rope(xh, a, bm, cp):
    return xh * a + pltpu.roll(xh, LANES - ROPE_HALF, 1) * bm + pltpu.roll(xh, ROPE_HALF, 1) * cp


def _full(shape):
    nd = len(shape)
    return pl.BlockSpec(shape, lambda *_: (0,) * nd)


def _tile_rows(ref, c, n):
    return ref.at[pl.ds(c, n, stride=ROW_TILE), :]


def _inproj_common(x_ref, tab_ref, w1_ref, qg_ref, kvg_ref, ckv_ref):
    xb = x_ref[...].astype(BF16)

    def seg(lo, hi):
        return _dot(xb, w1_ref[:, lo:hi])

    b_g = seg(_C_BG, _C_CG)
    u = seg(_C_CG, _C_H) * seg(_C_H, _C_CQ)
    cqn = _rms(seg(_C_CQ, _C_CKV), qg_ref[...]).astype(BF16)
    ckvn = _rms(seg(_C_CKV, _C_KR), kvg_ref[...])
    ckv_ref[...] = ckvn
    krr = _rope(seg(_C_KR, W1_COLS), tab_ref[3], tab_ref[4], tab_ref[5])
    return b_g, u, cqn, ckvn.astype(BF16), krr


def _inproj_prompt_body(tiles_per_seq, x_ref, tab_ref, tabt_ref, w1_ref, wuqt_ref, wuk_ref, wuvt_ref, qg_ref, kvg_ref,
                        cw_ref, bconv_ref, qt_ref, k_ref, vt_ref, ckv_ref, krt_ref, nconv_ref, uext_ref):
    tm = x_ref.shape[0]
    b_g, u, cqn, cb, krr = _inproj_common(x_ref, tab_ref, w1_ref, qg_ref, kvg_ref, ckv_ref)
    krt_ref[0] = krr.T[0:ROPE_DIM, :]
    qft = _dot_nt(wuqt_ref[...], cqn)
    cos, sin = tabt_ref[0], tabt_ref[1]
    for h in range(N_HEADS):
        r0 = h * HEAD_W
        x1 = qft[r0:r0 + ROPE_HALF, :]
        x2 = qft[r0 + ROPE_HALF:r0 + ROPE_DIM, :]
        qt_ref[r0:r0 + ROPE_HALF, :] = (x1 * cos - x2 * sin).astype(BF16)
        qt_ref[r0 + ROPE_HALF:r0 + ROPE_DIM, :] = (x1 * sin + x2 * cos).astype(BF16)
        qt_ref[r0 + ROPE_DIM:r0 + HEAD_W, :] = (qft[r0 + ROPE_DIM:r0 + HEAD_W, :] * Q_SCALE).astype(BF16)
    kn = _dot(cb, wuk_ref[...])
    for h in range(N_HEADS):
        sl = slice(h * HEAD_W, (h + 1) * HEAD_W)
        k_ref[:, sl] = (kn[:, sl] + krr).astype(BF16)
    vt_ref[...] = _dot_nt(wuvt_ref[...], cb).astype(BF16)

    first = (pl.program_id(0) % tiles_per_seq) == 0

    @pl.when(first)
    def _():
        uext_ref[0:8, :] = jnp.zeros((8, D_CONV), F32)

    @pl.when(jnp.logical_not(first))
    def _():
        uext_ref[0:8, :] = uext_ref[tm:tm + 8, :]

    uext_ref[8:8 + tm, :] = u
    conv = cw_ref[0:1, :] * uext_ref[6:6 + tm, :] + cw_ref[1:2, :] * uext_ref[7:7 + tm, :] + cw_ref[2:3, :] * u
    bconv_ref[...] = (b_g * conv).astype(BF16)
    nconv_ref[0] = u[tm - (CONV_W - 1):tm, :]


def _inproj_sample_body(x_ref, tab_ref, w1_ref, wuq_ref, wukt_ref, qg_ref, kvg_ref, cw_ref, s0_ref, s1_ref,
                        bconv_ref, q_ref, qlat_ref, ckv_ref, kr_ref, u_ref):
    b_g, u, cqn, _, krr = _inproj_common(x_ref, tab_ref, w1_ref, qg_ref, kvg_ref, ckv_ref)
    kr_ref[...] = krr[:, 0:ROPE_DIM]
    qf = _dot(cqn, wuq_ref[...])
    for h in range(N_HEADS):
        sl = slice(h * HEAD_W, (h + 1) * HEAD_W)
        qh = _rope(qf[:, sl], tab_ref[0], tab_ref[1], tab_ref[2]).astype(BF16)
        q_ref[:, sl] = qh
        qlat_ref[:, h * KV_LORA:(h + 1) * KV_LORA] = _dot(qh, wukt_ref[h]).astype(BF16)
    conv = cw_ref[0:1, :] * s0_ref[...] + cw_ref[1:2, :] * s1_ref[...] + cw_ref[2:3, :] * u
    bconv_ref[...] = (b_g * conv).astype(BF16)
    u_ref[...] = u


def _inproj_prompt(x, tab, tabt, w1, wuqt, wuk, wuvt, qg, kvg, cw, n_seq, seq):
    t = x.shape[0]
    tm = TM
    tiles_per_seq = seq // tm
    rows = lambda w: pl.BlockSpec((tm, w), lambda i: (i, 0))
    cols = lambda h: pl.BlockSpec((h, tm), lambda i: (0, i))
    return pl.pallas_call(
        functools.partial(_inproj_prompt_body, tiles_per_seq),
        grid=(t // tm,),
        in_specs=[rows(D_MODEL),
                  pl.BlockSpec((6, tm, LANES), lambda i: (0, i % tiles_per_seq, 0)),
                  pl.BlockSpec((2, ROPE_HALF, tm), lambda i: (0, 0, i % tiles_per_seq)),
                  _full(w1.shape), _full(wuqt.shape), _full(wuk.shape), _full(wuvt.shape),
                  _full(qg.shape), _full(kvg.shape), _full(cw.shape)],
        out_specs=[rows(D_CONV), cols(QK_W), rows(QK_W), cols(V_W), rows(KV_LORA),
                   pl.BlockSpec((1, ROPE_DIM, tm), lambda i: (i // tiles_per_seq, 0, i % tiles_per_seq)),
                   pl.BlockSpec((1, CONV_W - 1, D_CONV), lambda i: (i // tiles_per_seq, 0, 0))],
        out_shape=[jax.ShapeDtypeStruct((t, D_CONV), BF16), jax.ShapeDtypeStruct((QK_W, t), BF16),
                   jax.ShapeDtypeStruct((t, QK_W), BF16), jax.ShapeDtypeStruct((V_W, t), BF16),
                   jax.ShapeDtypeStruct((t, KV_LORA), F32), jax.ShapeDtypeStruct((n_seq, ROPE_DIM, seq), F32),
                   jax.ShapeDtypeStruct((n_seq, CONV_W - 1, D_CONV), F32)],
        scratch_shapes=[pltpu.VMEM((tm + 8, D_CONV), F32)],
        compiler_params=_cparams("arbitrary"),
    )(x, tab, tabt, w1, wuqt, wuk, wuvt, qg, kvg, cw)


def _inproj_sample(x, tab, w1, wuq, wukt, qg, kvg, cw, s0, s1):
    n = x.shape[0]
    args = (x, tab, w1, wuq, wukt, qg, kvg, cw, s0, s1)
    out_shape = [jax.ShapeDtypeStruct((n, D_CONV), BF16), jax.ShapeDtypeStruct((n, QK_W), BF16),
                 jax.ShapeDtypeStruct((n, N_HEADS * KV_LORA), BF16), jax.ShapeDtypeStruct((n, KV_LORA), F32),
                 jax.ShapeDtypeStruct((n, ROPE_DIM), F32), jax.ShapeDtypeStruct((n, D_CONV), F32)]
    return pl.pallas_call(
        _inproj_sample_body,
        grid=(1,),
        in_specs=[_full(a.shape) for a in args],
        out_specs=[_full(s.shape) for s in out_shape],
        out_shape=out_shape,
        compiler_params=_cparams("arbitrary"),
    )(*args)


def _prompt_attention_body(qt_ref, k_ref, vt_ref, ot_ref, acc_ref, m_ref, l_ref, sa_ref, sb_ref):
    tq = qt_ref.shape[1]
    i = pl.program_id(1)

    for hg in range(N_HEADS // ATTN_HEADS_PER_LOOP):
        heads = tuple(range(hg * ATTN_HEADS_PER_LOOP, (hg + 1) * ATTN_HEADS_PER_LOOP))
        acc_ref[...] = jnp.zeros(acc_ref.shape, F32)
        m_ref[...] = jnp.full(m_ref.shape, NEG, F32)
        l_ref[...] = jnp.zeros(l_ref.shape, F32)

        def scores(j, s_ref):
            off = pl.multiple_of(j * tq, tq)
            for idx, h in enumerate(heads):
                s_ref[idx] = _dot(k_ref[pl.ds(off, tq), h * HEAD_W:(h + 1) * HEAD_W],
                                  qt_ref[h * HEAD_W:(h + 1) * HEAD_W, :])

        def consume(j, s_ref, masked):
            off = pl.multiple_of(j * tq, tq)
            for idx, h in enumerate(heads):
                s = s_ref[idx]
                if masked:
                    krow = lax.broadcasted_iota(jnp.int32, (tq, tq), 0)
                    qcol = lax.broadcasted_iota(jnp.int32, (tq, tq), 1)
                    s = jnp.where(krow <= qcol, s, NEG)
                m = m_ref[idx]
                m_new = jnp.maximum(m, jnp.max(s, axis=0, keepdims=True))
                alpha = jnp.exp2(m - m_new)
                p = jnp.exp2(s - m_new)
                m_ref[idx] = m_new
                l_ref[idx] = alpha * l_ref[idx] + jnp.sum(p, axis=0, keepdims=True)
                vblk = vt_ref[h * V_DIM:(h + 1) * V_DIM, pl.ds(off, tq)]
                acc_ref[idx] = alpha * acc_ref[idx] + _dot(vblk, p.astype(BF16))

        scores(0, sa_ref)

        def pair(pp, c):
            j = 2 * pp
            scores(j + 1, sb_ref)
            consume(j, sa_ref, False)
            scores(j + 2, sa_ref)
            consume(j + 1, sb_ref, False)
            return c
        lax.fori_loop(0, i // 2, pair, 0)

        @pl.when(i % 2 == 1)
        def _():
            scores(i, sb_ref)
            consume(i - 1, sa_ref, False)
            consume(i, sb_ref, True)

        @pl.when(i % 2 == 0)
        def _():
            consume(i, sa_ref, True)

        for idx, h in enumerate(heads):
            ot_ref[h * V_DIM:(h + 1) * V_DIM, :] = (acc_ref[idx] / l_ref[idx]).astype(BF16)


def _prompt_attention(qt, k, vt, n_seq, seq):
    t = k.shape[0]
    tq = TQ
    nq = seq // tq
    return pl.pallas_call(
        _prompt_attention_body,
        grid=(n_seq, nq),
        in_specs=[pl.BlockSpec((QK_W, tq), lambda b, i: (0, b * nq + i)),
                  pl.BlockSpec((seq, QK_W), lambda b, i: (b, 0)),
                  pl.BlockSpec((V_W, seq), lambda b, i: (0, b))],
        out_specs=pl.BlockSpec((V_W, tq), lambda b, i: (0, b * nq + i)),
        out_shape=jax.ShapeDtypeStruct((V_W, t), BF16),
        scratch_shapes=[pltpu.VMEM((ATTN_HEADS_PER_LOOP, V_DIM, tq), F32),
                        pltpu.VMEM((ATTN_HEADS_PER_LOOP, 1, tq), F32), pltpu.VMEM((ATTN_HEADS_PER_LOOP, 1, tq), F32),
                        pltpu.VMEM((ATTN_HEADS_PER_LOOP, tq, tq), F32), pltpu.VMEM((ATTN_HEADS_PER_LOOP, tq, tq), F32)],
        compiler_params=_cparams("arbitrary", "arbitrary"),
    )(qt, k, vt)


def _sample_attention_body(n_pages, pt_ref, qlat_ref, q_ref, ckvn_ref, krn_ref, cc_ref, cr_ref, olat_ref,
                           bufc, bufr, kcb, krb, sem):
    b = pl.program_id(0)
    nb = pl.num_programs(0)
    slot = b % 2
    past = n_pages * PAGE_SIZE

    def page_copies(bb, sl, p):
        pg = pt_ref[bb, p]
        pos = pl.ds(pl.multiple_of(p * PAGE_SIZE, PAGE_SIZE), PAGE_SIZE)
        return (pltpu.make_async_copy(cc_ref.at[pg], bufc.at[sl, pos], sem.at[0, sl]),
                pltpu.make_async_copy(cr_ref.at[pg], bufr.at[sl, :, pos], sem.at[1, sl]))

    def issue(bb, sl):
        for p in range(n_pages):
            for c in page_copies(bb, sl, p):
                c.start()

    def wait(sl):
        pltpu.make_async_copy(bufc.at[sl], bufc.at[sl], sem.at[0, sl]).wait()
        pltpu.make_async_copy(bufr.at[sl], bufr.at[sl], sem.at[1, sl]).wait()

    @pl.when(b == 0)
    def _():
        issue(0, 0)

    wait(slot)
    issue(jnp.minimum(b + 1, nb - 1), 1 - slot)

    ql = qlat_ref[0]
    qr = q_ref[0][:, 0:ROPE_DIM]
    cn = ckvn_ref[0]
    rn = krn_ref[0]
    chunk = min(KEY_CHUNK, past)
    scores = []
    for c in range(past // chunk):
        pos = slice(c * chunk, (c + 1) * chunk)
        kcb[pos, :] = bufc[slot, pos, :].astype(BF16)
        krb[:, pos] = bufr[slot, :, pos].astype(BF16)
        scores.append(_dot_nt(ql, kcb[pos, :]) + _dot(qr, krb[:, pos]))
    s_new = (jnp.sum(ql.astype(F32) * cn, axis=-1, keepdims=True)
             + jnp.sum(qr.astype(F32) * rn, axis=-1, keepdims=True))
    m = s_new
    for s in scores:
        m = jnp.maximum(m, jnp.max(s, axis=-1, keepdims=True))
    p_new = jnp.exp2(s_new - m)
    l = p_new
    o = p_new * cn
    for c, s in enumerate(scores):
        p = jnp.exp2(s - m)
        l = l + jnp.sum(p, axis=-1, keepdims=True)
        o = o + _dot(p.astype(BF16), kcb[c * chunk:(c + 1) * chunk, :])
    olat_ref[0] = o / l

    @pl.when(b == nb - 1)
    def _():
        wait(1 - slot)


def _sample_attention(page_table, qlat, q, ckvn, krn, cache_c, cache_rt):
    n, n_pages = page_table.shape
    past = n_pages * PAGE_SIZE
    blk = lambda d1, d2: pl.BlockSpec((1, d1, d2), lambda b, pt: (b, 0, 0))
    grid_spec = pltpu.PrefetchScalarGridSpec(
        num_scalar_prefetch=1,
        grid=(n,),
        in_specs=[blk(N_HEADS, KV_LORA), blk(N_HEADS, HEAD_W), blk(1, KV_LORA), blk(1, ROPE_DIM),
                  pl.BlockSpec(memory_space=pl.ANY), pl.BlockSpec(memory_space=pl.ANY)],
        out_specs=blk(N_HEADS, KV_LORA),
        scratch_shapes=[pltpu.VMEM((2, past, KV_LORA), F32), pltpu.VMEM((2, ROPE_DIM, past), F32),
                        pltpu.VMEM((past, KV_LORA), BF16), pltpu.VMEM((ROPE_DIM, past), BF16),
                        pltpu.SemaphoreType.DMA((2, 2))],
    )
    return pl.pallas_call(
        functools.partial(_sample_attention_body, n_pages),
        grid_spec=grid_spec,
        out_shape=jax.ShapeDtypeStruct((n, N_HEADS, KV_LORA), F32),
        compiler_params=_cparams("arbitrary"),
    )(page_table, qlat.reshape(n, N_HEADS, KV_LORA), q.reshape(n, N_HEADS, HEAD_W),
      ckvn.reshape(n, 1, KV_LORA), krn.reshape(n, 1, ROPE_DIM), cache_c, cache_rt)


def _route(logits):
    lane_i = lax.broadcasted_iota(jnp.int32, (logits.shape[0], LANES), 1)
    lane = lane_i.astype(F32)
    first_at = lambda hit: jnp.min(jnp.where(hit, lane, float(LANES)), axis=-1, keepdims=True)
    lg = jnp.where(lane_i < N_GROUPS, logits[:, 0:LANES], NEG)
    mg = jnp.max(lg, axis=-1, keepdims=True)
    p_grp = 1.0 / jnp.sum(jnp.exp(lg - mg), axis=-1, keepdims=True)
    grp = first_at(lg == mg)
    lane_grp = (lane_i // EXP_PER_GROUP).astype(F32)
    le = jnp.where(lane_grp == grp, logits[:, LANES:2 * LANES], NEG)
    top1 = jnp.max(le, axis=-1, keepdims=True)
    i1 = first_at(le == top1)
    le2 = jnp.where(lane == i1, NEG, le)
    top2 = jnp.max(le2, axis=-1, keepdims=True)
    i2 = first_at(le2 == top2)
    e2 = jnp.exp(top2 - top1)
    w1 = p_grp / (1.0 + e2)
    w2 = p_grp * e2 / (1.0 + e2)
    out = jnp.where(lane_i == R_EID, i1, 0.0)
    out = jnp.where(lane_i == R_EID + 1, i2, out)
    out = jnp.where(lane_i == R_WT, w1, out)
    return jnp.where(lane_i == R_WT + 1, w2, out)


def _add_ranks(route, ltri, count_ref):
    n = route.shape[0]
    lane_i = lax.broadcasted_iota(jnp.int32, (n, LANES), 1)
    lane = lane_i.astype(F32)
    hit1 = lane == route[:, R_EID:R_EID + 1]
    hit2 = lane == route[:, R_EID + 1:R_EID + 2]
    chosen = jnp.where(hit1, 1.0, jnp.where(hit2, 1.0, 0.0))
    before = _dot(ltri, chosen.astype(BF16)) + count_ref[...]
    r1 = jnp.sum(jnp.where(hit1, before, 0.0), axis=-1, keepdims=True)
    r2 = jnp.sum(jnp.where(hit2, before, 0.0), axis=-1, keepdims=True)
    count_ref[...] = count_ref[...] + jnp.sum(chosen, axis=0, keepdims=True)
    route = jnp.where(lane_i == R_RANK, r1, route)
    return jnp.where(lane_i == R_RANK + 1, r2, route)


def _merge_rows(alpha, x, bconv, y_attn, wg_ref, wbc_ref, wo_ref, lg_ref, lb_ref, wr_ref, br_ref):
    g = _dot(x.astype(BF16), wg_ref[...])
    y_conv = _dot(bconv, wbc_ref[...])
    m = jax.nn.sigmoid(g[:, 0:D_MODEL]) * y_conv + jax.nn.sigmoid(g[:, D_MODEL:]) * y_attn
    mix = _dot(m.astype(BF16), wo_ref[...])
    h1 = _layernorm(alpha * x + mix, lg_ref[...], lb_ref[...])
    return h1, _route(_dot(h1.astype(BF16), wr_ref[...]) + br_ref[...])


def _merge_sample_body(alpha, x_ref, bconv_ref, olat_ref, wbd_ref, wba_ref, *refs):
    w_refs, (h1_ref, route_ref) = refs[:-2], refs[-2:]
    o = _dot(olat_ref[...].astype(BF16), wbd_ref[...]).astype(BF16)
    h1_ref[...], route_ref[...] = _merge_rows(alpha, x_ref[...], bconv_ref[...], _dot(o, wba_ref[...]), *w_refs)


def _merge_prompt_body(alpha, n_tiles, x_ref, bconv_ref, ot_ref, h1s_ref, routes_ref, wba_ref, *refs):
    w_refs, (h1t_ref, route_ref, routet_ref, counts_ref, ltri_ref, count_ref) = refs[:-6], refs[-6:]
    i = pl.program_id(0)
    tm = x_ref.shape[0]

    @pl.when(i == 0)
    def _():
        r = lax.broadcasted_iota(jnp.int32, (tm, tm), 0)
        c = lax.broadcasted_iota(jnp.int32, (tm, tm), 1)
        ltri_ref[...] = jnp.where(c < r, 1.0, 0.0).astype(BF16)
        count_ref[...] = jnp.zeros(count_ref.shape, F32)

    @pl.when(i < n_tiles)
    def _():
        y_attn = _dot_tn(ot_ref[...], wba_ref[...])
        h1, route = _merge_rows(alpha, x_ref[...], bconv_ref[...], y_attn, *w_refs)
        for c in range(ROW_TILE):
            _tile_rows(h1t_ref, c, tm)[...] = h1[:, c * LANES:(c + 1) * LANES]
        route_ref[...] = _add_ranks(route, ltri_ref[...], count_ref)

    @pl.when(i == n_tiles)
    def _():
        n_s = h1s_ref.shape[0]
        h1t_ref[...] = jnp.zeros(h1t_ref.shape, F32)
        route_ref[...] = jnp.zeros(route_ref.shape, F32)
        for c in range(ROW_TILE):
            _tile_rows(h1t_ref, c, n_s)[...] = h1s_ref[:, c * LANES:(c + 1) * LANES]
        route_ref[0:n_s, :] = _add_ranks(routes_ref[...], ltri_ref[0:n_s, 0:n_s], count_ref)

    counts_ref[...] = jnp.broadcast_to(count_ref[...], counts_ref.shape)
    routet_ref[...] = route_ref[...].T[0:SUBLANES, :]


def _merge_sample(alpha, x, bconv, olat, wbd, wba, ws):
    n = x.shape[0]
    args = (x, bconv, olat, wbd, wba) + tuple(ws)
    out_shape = [jax.ShapeDtypeStruct((n, D_MODEL), F32), jax.ShapeDtypeStruct((n, LANES), F32)]
    return pl.pallas_call(
        functools.partial(_merge_sample_body, alpha),
        grid=(1,),
        in_specs=[_full(a.shape) for a in args],
        out_specs=[_full(s.shape) for s in out_shape],
        out_shape=out_shape,
        compiler_params=_cparams("arbitrary"),
    )(*args)


def _merge_prompt(alpha, x, bconv, ot, h1_s, route_s, wba, ws):
    t = x.shape[0]
    tm = TM
    n_tiles = t // tm
    t_all = t + h1_s.shape[0]
    assert h1_s.shape[0] <= tm
    clamp = lambda i: jnp.minimum(i, n_tiles - 1)
    rows_in = lambda w: pl.BlockSpec((tm, w), lambda i: (clamp(i), 0))
    return pl.pallas_call(
        functools.partial(_merge_prompt_body, alpha, n_tiles),
        grid=(n_tiles + 1,),
        in_specs=[rows_in(D_MODEL), rows_in(D_CONV), pl.BlockSpec((V_W, tm), lambda i: (0, clamp(i))),
                  _full(h1_s.shape), _full(route_s.shape), _full(wba.shape)] + [_full(w.shape) for w in ws],
        out_specs=[pl.BlockSpec((tm * ROW_TILE, LANES), lambda i: (i, 0)), pl.BlockSpec((tm, LANES), lambda i: (i, 0)),
                   pl.BlockSpec((SUBLANES, tm), lambda i: (0, i)), _full((SUBLANES, LANES))],
        out_shape=[jax.ShapeDtypeStruct((t_all * ROW_TILE, LANES), F32), jax.ShapeDtypeStruct((t_all, LANES), F32),
                   jax.ShapeDtypeStruct((SUBLANES, t_all), F32), jax.ShapeDtypeStruct((SUBLANES, LANES), F32)],
        scratch_shapes=[pltpu.VMEM((tm, tm), BF16), pltpu.VMEM((1, LANES), F32)],
        compiler_params=_cparams("arbitrary"),
    )(x, bconv, ot, h1_s, route_s, wba, *ws)


SLOT_GROUP = 16


def _slot_tokens_body(rows_ref, dest_ref, tok_ref):
    def clear_row(j, c):
        for l in range(LANES):
            tok_ref[j * LANES + l] = 0
        return c
    lax.fori_loop(0, rows_ref[1], clear_row, 0)

    rows_k = rows_ref[0]
    for k in range(TOP_K):
        def fill_row(j, c):
            for g in range(0, LANES, SLOT_GROUP):
                ds = [dest_ref[k * rows_k + j, g + l] for l in range(SLOT_GROUP)]
                for l, d in enumerate(ds):
                    tok_ref[d] = j * LANES + g + l
            return c
        lax.fori_loop(0, rows_k, fill_row, 0)


def _slot_tokens(dest, n_blocks):
    n_slots = n_blocks * MOE_BLOCK
    n_tok = dest.shape[1]
    assert n_slots % LANES == 0 and n_tok % LANES == 0
    dest = dest.reshape(-1, LANES)
    grid_spec = pltpu.PrefetchScalarGridSpec(
        num_scalar_prefetch=2,
        grid=(1,),
        in_specs=[],
        out_specs=pl.BlockSpec(memory_space=pltpu.SMEM),
    )
    return pl.pallas_call(
        _slot_tokens_body,
        grid_spec=grid_spec,
        out_shape=jax.ShapeDtypeStruct((n_slots,), jnp.int32),
        compiler_params=_cparams("arbitrary"),
    )(jnp.array([n_tok // LANES, n_slots // LANES], jnp.int32), dest)


def _row_tile(i):
    return pl.ds(pl.multiple_of(i * ROW_TILE, ROW_TILE), ROW_TILE)


def _experts_body(blk_e_ref, used_ref, nvalid_ref, tok_ref, h1t_ref, wg_ref, wu_ref, wd_ref, ys_ref,
                  xbuf, xb, wgb, wub, wdb, sem_in):
    b = pl.program_id(0)
    ring = lambda bb: lax.rem(bb, GATHER_RING)
    used = used_ref[0]
    blk_rows = MOE_BLOCK * ROW_TILE
    changed = jnp.logical_or(b == 0, blk_e_ref[b] != blk_e_ref[jnp.maximum(b - 1, 0)])
    last = used - 1
    clamp = lambda v: jnp.minimum(v, last)

    def in_copy(bb, sl, r):
        tok = tok_ref[bb * MOE_BLOCK + r]
        return pltpu.make_async_copy(h1t_ref.at[_row_tile(tok)], xbuf.at[sl, _row_tile(r)], sem_in.at[sl])

    def issue_full(v):
        for r in range(MOE_BLOCK):
            in_copy(clamp(v), ring(v), r).start(priority=r % 2)

    def issue_partial(v):
        lax.fori_loop(0, nvalid_ref[clamp(v)], lambda r, c: (in_copy(clamp(v), ring(v), r).start(), c)[1], 0)

    def is_full(v):
        return nvalid_ref[clamp(v)] == MOE_BLOCK

    def wait_in(v):
        sl = ring(v)

        @pl.when(is_full(v))
        def _():
            pltpu.make_async_copy(h1t_ref.at[pl.ds(0, blk_rows)], xbuf.at[sl], sem_in.at[sl]).wait()

        @pl.when(jnp.logical_not(is_full(v)))
        def _():
            lax.fori_loop(0, nvalid_ref[clamp(v)], lambda r, c: (in_copy(clamp(v), sl, r).wait(), c)[1], 0)

    @pl.when(jnp.logical_and(b == 0, used > 0))
    def _():
        xbuf[...] = jnp.zeros(xbuf.shape, F32)
        for a in range(GATHER_RING - 1):
            @pl.when(is_full(a))
            def _():
                issue_full(a)

            @pl.when(jnp.logical_not(is_full(a)))
            def _():
                issue_partial(a)

    @pl.when(jnp.logical_and(b < used, changed))
    def _():
        wgb[...] = wg_ref[0].astype(BF16)
        wub[...] = wu_ref[0].astype(BF16)
        wdb[...] = wd_ref[0].astype(BF16)

    ahead = b + GATHER_RING - 1

    def block(issue_ahead):
        for c in range(ROW_TILE):
            xb[:, c * LANES:(c + 1) * LANES] = _tile_rows(xbuf.at[ring(b)], c, MOE_BLOCK)[...].astype(BF16)
        if issue_ahead:
            issue_full(ahead)
        x = xb[...]
        g = _dot(x, wgb[...])
        u = _dot(x, wub[...])
        h = (g * jax.nn.sigmoid(g) * u).astype(BF16)
        y = _dot(h, wdb[...])
        for c in range(ROW_TILE):
            _tile_rows(ys_ref, c, MOE_BLOCK)[...] = y[:, c * LANES:(c + 1) * LANES]

    @pl.when(b < used)
    def _():
        wait_in(b)

    @pl.when(jnp.logical_and(b < used, is_full(ahead)))
    def _():
        block(True)

    @pl.when(jnp.logical_and(b < used, jnp.logical_not(is_full(ahead))))
    def _():
        block(False)
        issue_partial(ahead)

    @pl.when(b == last)
    def _():
        for a in range(1, GATHER_RING):
            wait_in(b + a)

    @pl.when(b >= used)
    def _():
        ys_ref[...] = jnp.zeros(ys_ref.shape, F32)


def _experts(blk_e, used, nvalid, slot_tok, h1t, wg, wu, wd):
    n_blocks = blk_e.shape[0]
    blk_rows = MOE_BLOCK * ROW_TILE
    by_expert = lambda b, e, u, n, a: (e[b], 0, 0)
    grid_spec = pltpu.PrefetchScalarGridSpec(
        num_scalar_prefetch=4,
        grid=(n_blocks,),
        in_specs=[pl.BlockSpec(memory_space=pl.ANY),
                  pl.BlockSpec((1, D_MODEL, D_EXPERT), by_expert), pl.BlockSpec((1, D_MODEL, D_EXPERT), by_expert),
                  pl.BlockSpec((1, D_EXPERT, D_MODEL), by_expert)],
        out_specs=pl.BlockSpec((blk_rows, LANES), lambda b, e, u, n, a: (b, 0)),
        scratch_shapes=[pltpu.VMEM((GATHER_RING, blk_rows, LANES), F32),
                        pltpu.VMEM((MOE_BLOCK, D_MODEL), BF16),
                        pltpu.VMEM((D_MODEL, D_EXPERT), BF16), pltpu.VMEM((D_MODEL, D_EXPERT), BF16),
                        pltpu.VMEM((D_EXPERT, D_MODEL), BF16),
                        pltpu.SemaphoreType.DMA((GATHER_RING,))],
    )
    return pl.pallas_call(
        _experts_body,
        grid_spec=grid_spec,
        out_shape=jax.ShapeDtypeStruct((n_blocks * blk_rows, LANES), F32),
        compiler_params=_cparams("arbitrary"),
    )(blk_e, used, nvalid, slot_tok, h1t, wg, wu, wd)


def _combine_body(alpha, tok0, n_all, dest_ref, ys_ref, h1t_ref, route_ref, lg_ref, lb_ref, out_ref, gbuf, sem):
    i = pl.program_id(0)
    nt = pl.num_programs(0)
    slot = i % 2
    tm = out_ref.shape[0]

    def issue(ii, sl):
        base = tok0 + ii * tm
        for r in range(tm):
            for k in range(TOP_K):
                d = dest_ref[base + k * n_all + r]
                pltpu.make_async_copy(ys_ref.at[_row_tile(d)], gbuf.at[sl, k, _row_tile(r)],
                                      sem.at[sl]).start(priority=k)

    def wait(sl):
        for k in range(TOP_K):
            pltpu.make_async_copy(ys_ref.at[pl.ds(0, tm * ROW_TILE)], gbuf.at[sl, k], sem.at[sl]).wait()

    @pl.when(i == 0)
    def _():
        issue(0, 0)

    wait(slot)
    issue(jnp.minimum(i + 1, nt - 1), 1 - slot)

    route = route_ref[...]
    w0 = jnp.broadcast_to(route[:, R_WT:R_WT + 1], (tm, LANES))
    w1 = jnp.broadcast_to(route[:, R_WT + 1:R_WT + 2], (tm, LANES))
    z = [alpha * _tile_rows(h1t_ref, c, tm)[...]
         + w0 * _tile_rows(gbuf.at[slot, 0], c, tm)[...] + w1 * _tile_rows(gbuf.at[slot, 1], c, tm)[...]
         for c in range(ROW_TILE)]
    mu = sum(jnp.sum(zc, axis=-1, keepdims=True) for zc in z) / D_MODEL
    zc = [v - mu for v in z]
    var = sum(jnp.sum(v * v, axis=-1, keepdims=True) for v in zc) / D_MODEL
    rstd = lax.rsqrt(var + LN_EPS)
    for c in range(ROW_TILE):
        sl = slice(c * LANES, (c + 1) * LANES)
        out_ref[:, sl] = zc[c] * rstd * lg_ref[:, sl] + lb_ref[:, sl]

    @pl.when(i == nt - 1)
    def _():
        wait(1 - slot)


def _combine(alpha, tok0, n_tok, tm, dest, ys, h1t, route, lg, lb):
    assert n_tok % tm == 0 and tok0 % tm == 0
    blk0 = tok0 // tm
    grid_spec = pltpu.PrefetchScalarGridSpec(
        num_scalar_prefetch=1,
        grid=(n_tok // tm,),
        in_specs=[pl.BlockSpec(memory_space=pl.ANY),
                  pl.BlockSpec((tm * ROW_TILE, LANES), lambda i, d: (blk0 + i, 0)),
                  pl.BlockSpec((tm, LANES), lambda i, d: (blk0 + i, 0)),
                  pl.BlockSpec(lg.shape, lambda i, d: (0, 0)), pl.BlockSpec(lb.shape, lambda i, d: (0, 0))],
        out_specs=pl.BlockSpec((tm, D_MODEL), lambda i, d: (i, 0)),
        scratch_shapes=[pltpu.VMEM((2, TOP_K, tm * ROW_TILE, LANES), F32), pltpu.SemaphoreType.DMA((2,))],
    )
    return pl.pallas_call(
        functools.partial(_combine_body, alpha, tok0, route.shape[0]),
        grid_spec=grid_spec,
        out_shape=jax.ShapeDtypeStruct((n_tok, D_MODEL), F32),
        compiler_params=_cparams("arbitrary"),
    )(dest, ys, h1t, route, lg, lb)


def _rope_tables(pos):
    inv = ROPE_BASE ** (-(jnp.arange(ROPE_HALF, dtype=F32) * 2.0 / ROPE_DIM))
    ang = pos.astype(F32)[:, None] * inv[None, :]
    cos, sin = jnp.cos(ang), jnp.sin(ang)
    n = pos.shape[0]
    a = jnp.concatenate([cos, cos, jnp.ones((n, LANES - ROPE_DIM), F32)], axis=1)
    bm = jnp.concatenate([-sin, jnp.zeros((n, LANES - ROPE_HALF), F32)], axis=1)
    cp = jnp.concatenate([jnp.zeros((n, ROPE_HALF), F32), sin, jnp.zeros((n, LANES - ROPE_DIM), F32)], axis=1)
    k_tabs = jnp.stack([a, bm, cp])
    return jnp.concatenate([k_tabs * Q_SCALE, k_tabs]), jnp.stack([cos.T, sin.T]) * Q_SCALE


def _head_blocks(w_rope, w_nope):
    k = w_nope.shape[0]
    pad = jnp.zeros((k, N_HEADS, HEAD_W - QK_DIM), w_nope.dtype)
    return jnp.concatenate([w_rope, w_nope, pad], axis=-1).reshape(k, QK_W)


def _pack_weights(w_in, w_uq, w_uk, w_uv, router_w_group, router_b_group, router_w_expert, router_b_expert):
    d = w_in.shape[0]
    c_kr = 3 * D_CONV + Q_LORA + KV_LORA
    w1 = jnp.concatenate([w_in[:, :c_kr], w_in[:, c_kr:c_kr + ROPE_DIM],
                          jnp.zeros((d, LANES - ROPE_DIM), w_in.dtype)], axis=1).astype(BF16)
    wg = w_in[:, c_kr + ROPE_DIM:].astype(BF16)
    uq = w_uq.reshape(Q_LORA, N_HEADS, QK_DIM)
    wuq = _head_blocks(uq[..., NOPE_DIM:], uq[..., :NOPE_DIM]).astype(BF16)
    wuk = _head_blocks(jnp.zeros((KV_LORA, N_HEADS, ROPE_DIM), w_uk.dtype), w_uk).astype(BF16)
    wukt = jnp.transpose(wuk.reshape(KV_LORA, N_HEADS, HEAD_W), (1, 2, 0))
    wuvt = w_uv.reshape(KV_LORA, V_W).T.astype(BF16)
    eye = jnp.eye(N_HEADS, dtype=w_uv.dtype)
    wbd = jnp.einsum('chd,hg->hcgd', w_uv, eye).reshape(N_HEADS * KV_LORA, V_W).astype(BF16)
    wr = jnp.zeros((d, ROUTE_W), F32)
    wr = wr.at[:, 0:N_GROUPS].set(router_w_group).at[:, LANES:LANES + N_EXPERTS].set(router_w_expert).astype(BF16)
    br = jnp.zeros((1, ROUTE_W), F32)
    br = br.at[0, 0:N_GROUPS].set(router_b_group).at[0, LANES:LANES + N_EXPERTS].set(router_b_expert)
    return w1, wg, wuq, wuq.T, wuk, wukt, wuvt, wbd, wr, br


def _dispatch_plan(route_t, counts, n_blocks):
    counts = counts[0, 0:N_EXPERTS].astype(jnp.int32)
    pcounts = (counts + MOE_BLOCK - 1) // MOE_BLOCK * MOE_BLOCK
    pend = jnp.cumsum(pcounts).astype(jnp.int32)
    pstart = (pend - pcounts).astype(F32)
    eid = route_t[R_EID:R_EID + TOP_K]
    rank = route_t[R_RANK:R_RANK + TOP_K]
    base = jnp.zeros_like(eid)
    for e in range(N_EXPERTS):
        base = jnp.where(eid == float(e), pstart[e], base)
    dest = (base + rank).astype(jnp.int32)
    blk_start = jnp.arange(n_blocks, dtype=jnp.int32) * MOE_BLOCK
    blk_e = jnp.sum((blk_start[:, None] >= pend[None, :]).astype(jnp.int32), axis=1)
    blk_e = jnp.minimum(blk_e, N_EXPERTS - 1).astype(jnp.int32)
    used = (pend[-1:] // MOE_BLOCK).astype(jnp.int32)
    filled_end = (pend - pcounts + counts)[blk_e]
    nvalid = jnp.where(blk_start < pend[-1], jnp.clip(filled_end - blk_start, 0, MOE_BLOCK), 0).astype(jnp.int32)
    return dest, blk_e, used, nvalid


def kernel(x_prompt, x_sample, cache_ckv, cache_krope, state_conv, page_table, w_in, conv_w, q_norm_g, w_uq,
           kv_norm_g, w_uk, w_uv, w_br_conv, w_br_attn, w_o, ln1_g, ln1_b, router_w_group, router_b_group,
           router_w_expert, router_b_expert, w_gate, w_up, w_down, ln2_g, ln2_b):
    depth = w_in.shape[0]
    alpha = (2 * depth) ** 0.25
    n_p, t_p, _ = x_prompt.shape
    n_s, t_s, _ = x_sample.shape
    assert t_s == 1 and t_p % TM == 0 and n_s % LANES == 0
    rows_p = n_p * t_p
    rows_all = rows_p + n_s
    past = page_table.shape[1] * PAGE_SIZE
    tab_p, tabt_p = _rope_tables(jnp.arange(t_p))
    tab_s, _ = _rope_tables(jnp.full((n_s,), past, jnp.int32))
    n_blocks = -(-(rows_all * TOP_K) // MOE_BLOCK) + N_EXPERTS

    h_p = x_prompt.reshape(rows_p, D_MODEL)
    h_s = x_sample.reshape(n_s, D_MODEL)
    ckv_p, kr_p, cv_p, ckv_s, kr_s, cv_s = [], [], [], [], [], []
    row = lambda v: v.reshape(1, -1)
    for l in range(depth):
        w1, wg, wuq, wuqt, wuk, wukt, wuvt, wbd, wr, br = _pack_weights(
            w_in[l], w_uq[l], w_uk[l], w_uv[l], router_w_group[l], router_b_group[l],
            router_w_expert[l], router_b_expert[l])
        qg, kvg = row(q_norm_g[l]), row(kv_norm_g[l])
        wba = w_br_attn[l].astype(BF16)
        merge_w = (wg, w_br_conv[l].astype(BF16), w_o[l].astype(BF16), row(ln1_g[l]), row(ln1_b[l]), wr, br)

        bconv, qt, k, vt, ckv, krt, nconv = _inproj_prompt(h_p, tab_p, tabt_p, w1, wuqt, wuk, wuvt, qg, kvg, conv_w[l],
                                                          n_p, t_p)
        ot = _prompt_attention(qt, k, vt, n_p, t_p)

        st = state_conv[l]
        bconv_s, q_s, qlat, ckvn_s, krn_s, u_s = _inproj_sample(
            h_s, tab_s, w1, wuq, wukt, qg, kvg, conv_w[l], st[:, 0], st[:, 1])
        olat = _sample_attention(page_table, qlat, q_s, ckvn_s, krn_s, cache_ckv[l],
                                 jnp.swapaxes(cache_krope[l], 1, 2))
        h1_s, route_s = _merge_sample(alpha, h_s, bconv_s, olat.reshape(n_s, N_HEADS * KV_LORA), wbd, wba, merge_w)
        h1t, route, route_t, counts = _merge_prompt(alpha, h_p, bconv, ot, h1_s, route_s, wba, merge_w)

        dest, blk_e, used, nvalid = _dispatch_plan(route_t, counts, n_blocks)
        slot_tok = _slot_tokens(dest, n_blocks)
        ys = _experts(blk_e, used, nvalid, slot_tok, h1t, w_gate[l], w_up[l], w_down[l])
        dest = dest.reshape(-1)
        ln2 = (row(ln2_g[l]), row(ln2_b[l]))
        h_p = _combine(alpha, 0, rows_p, TM_COMBINE, dest, ys, h1t, route, *ln2)
        h_s = _combine(alpha, rows_p, n_s, n_s, dest, ys, h1t, route, *ln2)

        ckv_p.append(ckv.reshape(n_p, t_p, KV_LORA))
        kr_p.append(jnp.swapaxes(krt, 1, 2))
        cv_p.append(nconv)
        ckv_s.append(ckvn_s.reshape(n_s, 1, KV_LORA))
        kr_s.append(krn_s.reshape(n_s, 1, ROPE_DIM))
        cv_s.append(jnp.stack([st[:, 1], u_s], axis=1))
    return (h_p.reshape(n_p, t_p, D_MODEL), h_s.reshape(n_s, 1, D_MODEL), jnp.stack(ckv_p), jnp.stack(kr_p),
            jnp.stack(cv_p), jnp.stack(ckv_s), jnp.stack(kr_s), jnp.stack(cv_s))
```

```python
import functools

import jax
import jax.numpy as jnp
from jax import lax
from jax.experimental import pallas as pl
from jax.experimental.pallas import tpu as pltpu

F32 = jnp.float32
BF16 = jnp.bfloat16

D_MODEL = 1024
D_CONV = 512
CONV_W = 3
N_HEADS = 8
Q_LORA = 384
KV_LORA = 256
NOPE_DIM = 64
ROPE_DIM = 32
ROPE_HALF = ROPE_DIM // 2
V_DIM = 64
QK_DIM = NOPE_DIM + ROPE_DIM
ROPE_BASE = 10000.0
ATTN_SCALE = QK_DIM ** -0.5
LOG2E = 1.4426950408889634
Q_SCALE = ATTN_SCALE * LOG2E
N_GROUPS = 4
EXP_PER_GROUP = 8
N_EXPERTS = N_GROUPS * EXP_PER_GROUP
TOP_K = 2
D_EXPERT = 512
MOE_BLOCK = 256
PAGE_SIZE = 128
LN_EPS = 1e-5
RMS_EPS = 1e-6

LANES = 128
SUBLANES = 8
ROW_TILE = D_MODEL // LANES
assert ROW_TILE == SUBLANES
HEAD_W = LANES
QK_W = N_HEADS * HEAD_W
V_W = N_HEADS * V_DIM
_C_BG, _C_CG, _C_H = 0, D_CONV, 2 * D_CONV
_C_CQ = 3 * D_CONV
_C_CKV = _C_CQ + Q_LORA
_C_KR = _C_CKV + KV_LORA
W1_COLS = _C_KR + LANES
ROUTE_W = 2 * LANES
R_EID, R_WT, R_RANK = 0, 2, 4
NEG = -1e30
VMEM_LIMIT = 56 * 1024 * 1024

TM = 512
KEY_CHUNK = 2048
ATTN_HEADS_PER_LOOP = 8
TQ = 512
TM_COMBINE = 256
GATHER_RING = 4


def _cparams(*sem):
    return pltpu.CompilerParams(dimension_semantics=sem, vmem_limit_bytes=VMEM_LIMIT)


def _dot(a, b):
    return jnp.dot(a, b, preferred_element_type=F32)


def _dot_nt(a, b):
    return lax.dot_general(a, b, (((1,), (1,)), ((), ())), preferred_element_type=F32)


def _dot_tn(a, b):
    return lax.dot_general(a, b, (((0,), (0,)), ((), ())), preferred_element_type=F32)


def _rms(x, g):
    return x * lax.rsqrt(jnp.mean(x * x, axis=-1, keepdims=True) + RMS_EPS) * g


def _layernorm(x, g, b):
    mu = jnp.mean(x, axis=-1, keepdims=True)
    xc = x - mu
    var = jnp.mean(xc * xc, axis=-1, keepdims=True)
    return xc * lax.rsqrt(var + LN_EPS) * g + b


def _rope(xh, a, bm, cp):
    return xh * a + pltpu.roll(xh, LANES - ROPE_HALF, 1) * bm + pltpu.roll(xh, ROPE_HALF, 1) * cp


def _full(shape):
    nd = len(shape)
    return pl.BlockSpec(shape, lambda *_: (0,) * nd)


def _tile_rows(ref, c, n):
    return ref.at[pl.ds(c, n, stride=ROW_TILE), :]


def _inproj_common(x_ref, tab_ref, w1_ref, qg_ref, kvg_ref, ckv_ref):
    xb = x_ref[...].astype(BF16)

    def seg(lo, hi):
        return _dot(xb, w1_ref[:, lo:hi])

    b_g = seg(_C_BG, _C_CG)
    u = seg(_C_CG, _C_H) * seg(_C_H, _C_CQ)
    cqn = _rms(seg(_C_CQ, _C_CKV), qg_ref[...]).astype(BF16)
    ckvn = _rms(seg(_C_CKV, _C_KR), kvg_ref[...])
    ckv_ref[...] = ckvn
    krr = _rope(seg(_C_KR, W1_COLS), tab_ref[3], tab_ref[4], tab_ref[5])
    return b_g, u, cqn, ckvn.astype(BF16), krr


def _inproj_prompt_body(tiles_per_seq, x_ref, tab_ref, tabt_ref, w1_ref, wuqt_ref, wuk_ref, wuvt_ref, qg_ref, kvg_ref,
                        cw_ref, bconv_ref, qt_ref, k_ref, vt_ref, ckv_ref, krt_ref, nconv_ref, uext_ref):
    tm = x_ref.shape[0]
    b_g, u, cqn, cb, krr = _inproj_common(x_ref, tab_ref, w1_ref, qg_ref, kvg_ref, ckv_ref)
    krt_ref[0] = krr.T[0:ROPE_DIM, :]
    qft = _dot_nt(wuqt_ref[...], cqn)
    cos, sin = tabt_ref[0], tabt_ref[1]
    for h in range(N_HEADS):
        r0 = h * HEAD_W
        x1 = qft[r0:r0 + ROPE_HALF, :]
        x2 = qft[r0 + ROPE_HALF:r0 + ROPE_DIM, :]
        qt_ref[r0:r0 + ROPE_HALF, :] = (x1 * cos - x2 * sin).astype(BF16)
        qt_ref[r0 + ROPE_HALF:r0 + ROPE_DIM, :] = (x1 * sin + x2 * cos).astype(BF16)
        qt_ref[r0 + ROPE_DIM:r0 + HEAD_W, :] = (qft[r0 + ROPE_DIM:r0 + HEAD_W, :] * Q_SCALE).astype(BF16)
    kn = _dot(cb, wuk_ref[...])
    for h in range(N_HEADS):
        sl = slice(h * HEAD_W, (h + 1) * HEAD_W)
        k_ref[:, sl] = (kn[:, sl] + krr).astype(BF16)
    vt_ref[...] = _dot_nt(wuvt_ref[...], cb).astype(BF16)

    first = (pl.program_id(0) % tiles_per_seq) == 0

    @pl.when(first)
    def _():
        uext_ref[0:8, :] = jnp.zeros((8, D_CONV), F32)

    @pl.when(jnp.logical_not(first))
    def _():
        uext_ref[0:8, :] = uext_ref[tm:tm + 8, :]

    uext_ref[8:8 + tm, :] = u
    conv = cw_ref[0:1, :] * uext_ref[6:6 + tm, :] + cw_ref[1:2, :] * uext_ref[7:7 + tm, :] + cw_ref[2:3, :] * u
    bconv_ref[...] = (b_g * conv).astype(BF16)
    nconv_ref[0] = u[tm - (CONV_W - 1):tm, :]


def _inproj_sample_body(x_ref, tab_ref, w1_ref, wuq_ref, wukt_ref, qg_ref, kvg_ref, cw_ref, s0_ref, s1_ref,
                        bconv_ref, q_ref, qlat_ref, ckv_ref, kr_ref, u_ref):
    b_g, u, cqn, _, krr = _inproj_common(x_ref, tab_ref, w1_ref, qg_ref, kvg_ref, ckv_ref)
    kr_ref[...] = krr[:, 0:ROPE_DIM]
    qf = _dot(cqn, wuq_ref[...])
    for h in range(N_HEADS):
        sl = slice(h * HEAD_W, (h + 1) * HEAD_W)
        qh = _rope(qf[:, sl], tab_ref[0], tab_ref[1], tab_ref[2]).astype(BF16)
        q_ref[:, sl] = qh
        qlat_ref[:, h * KV_LORA:(h + 1) * KV_LORA] = _dot(qh, wukt_ref[h]).astype(BF16)
    conv = cw_ref[0:1, :] * s0_ref[...] + cw_ref[1:2, :] * s1_ref[...] + cw_ref[2:3, :] * u
    bconv_ref[...] = (b_g * conv).astype(BF16)
    u_ref[...] = u


def _inproj_prompt(x, tab, tabt, w1, wuqt, wuk, wuvt, qg, kvg, cw, n_seq, seq):
    t = x.shape[0]
    tm = TM
    tiles_per_seq = seq // tm
    rows = lambda w: pl.BlockSpec((tm, w), lambda i: (i, 0))
    cols = lambda h: pl.BlockSpec((h, tm), lambda i: (0, i))
    return pl.pallas_call(
        functools.partial(_inproj_prompt_body, tiles_per_seq),
        grid=(t // tm,),
        in_specs=[rows(D_MODEL),
                  pl.BlockSpec((6, tm, LANES), lambda i: (0, i % tiles_per_seq, 0)),
                  pl.BlockSpec((2, ROPE_HALF, tm), lambda i: (0, 0, i % tiles_per_seq)),
                  _full(w1.shape), _full(wuqt.shape), _full(wuk.shape), _full(wuvt.shape),
                  _full(qg.shape), _full(kvg.shape), _full(cw.shape)],
        out_specs=[rows(D_CONV), cols(QK_W), rows(QK_W), cols(V_W), rows(KV_LORA),
                   pl.BlockSpec((1, ROPE_DIM, tm), lambda i: (i // tiles_per_seq, 0, i % tiles_per_seq)),
                   pl.BlockSpec((1, CONV_W - 1, D_CONV), lambda i: (i // tiles_per_seq, 0, 0))],
        out_shape=[jax.ShapeDtypeStruct((t, D_CONV), BF16), jax.ShapeDtypeStruct((QK_W, t), BF16),
                   jax.ShapeDtypeStruct((t, QK_W), BF16), jax.ShapeDtypeStruct((V_W, t), BF16),
                   jax.ShapeDtypeStruct((t, KV_LORA), F32), jax.ShapeDtypeStruct((n_seq, ROPE_DIM, seq), F32),
                   jax.ShapeDtypeStruct((n_seq, CONV_W - 1, D_CONV), F32)],
        scratch_shapes=[pltpu.VMEM((tm + 8, D_CONV), F32)],
        compiler_params=_cparams("arbitrary"),
    )(x, tab, tabt, w1, wuqt, wuk, wuvt, qg, kvg, cw)


def _inproj_sample(x, tab, w1, wuq, wukt, qg, kvg, cw, s0, s1):
    n = x.shape[0]
    args = (x, tab, w1, wuq, wukt, qg, kvg, cw, s0, s1)
    out_shape = [jax.ShapeDtypeStruct((n, D_CONV), BF16), jax.ShapeDtypeStruct((n, QK_W), BF16),
                 jax.ShapeDtypeStruct((n, N_HEADS * KV_LORA), BF16), jax.ShapeDtypeStruct((n, KV_LORA), F32),
                 jax.ShapeDtypeStruct((n, ROPE_DIM), F32), jax.ShapeDtypeStruct((n, D_CONV), F32)]
    return pl.pallas_call(
        _inproj_sample_body,
        grid=(1,),
        in_specs=[_full(a.shape) for a in args],
        out_specs=[_full(s.shape) for s in out_shape],
        out_shape=out_shape,
        compiler_params=_cparams("arbitrary"),
    )(*args)


def _prompt_attention_body(qt_ref, k_ref, vt_ref, ot_ref, acc_ref, m_ref, l_ref, sa_ref, sb_ref):
    tq = qt_ref.shape[1]
    i = pl.program_id(1)

    for hg in range(N_HEADS // ATTN_HEADS_PER_LOOP):
        heads = tuple(range(hg * ATTN_HEADS_PER_LOOP, (hg + 1) * ATTN_HEADS_PER_LOOP))
        acc_ref[...] = jnp.zeros(acc_ref.shape, F32)
        m_ref[...] = jnp.full(m_ref.shape, NEG, F32)
        l_ref[...] = jnp.zeros(l_ref.shape, F32)

        def scores(j, s_ref):
            off = pl.multiple_of(j * tq, tq)
            for idx, h in enumerate(heads):
                s_ref[idx] = _dot(k_ref[pl.ds(off, tq), h * HEAD_W:(h + 1) * HEAD_W],
                                  qt_ref[h * HEAD_W:(h + 1) * HEAD_W, :])

        def consume(j, s_ref, masked):
            off = pl.multiple_of(j * tq, tq)
            for idx, h in enumerate(heads):
                s = s_ref[idx]
                if masked:
                    krow = lax.broadcasted_iota(jnp.int32, (tq, tq), 0)
                    qcol = lax.broadcasted_iota(jnp.int32, (tq, tq), 1)
                    s = jnp.where(krow <= qcol, s, NEG)
                m = m_ref[idx]
                m_new = jnp.maximum(m, jnp.max(s, axis=0, keepdims=True))
                alpha = jnp.exp2(m - m_new)
                p = jnp.exp2(s - m_new)
                m_ref[idx] = m_new
                l_ref[idx] = alpha * l_ref[idx] + jnp.sum(p, axis=0, keepdims=True)
                vblk = vt_ref[h * V_DIM:(h + 1) * V_DIM, pl.ds(off, tq)]
                acc_ref[idx] = alpha * acc_ref[idx] + _dot(vblk, p.astype(BF16))

        scores(0, sa_ref)

        def pair(pp, c):
            j = 2 * pp
            scores(j + 1, sb_ref)
            consume(j, sa_ref, False)
            scores(j + 2, sa_ref)
            consume(j + 1, sb_ref, False)
            return c
        lax.fori_loop(0, i // 2, pair, 0)

        @pl.when(i % 2 == 1)
        def _():
            scores(i, sb_ref)
            consume(i - 1, sa_ref, False)
            consume(i, sb_ref, True)

        @pl.when(i % 2 == 0)
        def _():
            consume(i, sa_ref, True)

        for idx, h in enumerate(heads):
            ot_ref[h * V_DIM:(h + 1) * V_DIM, :] = (acc_ref[idx] / l_ref[idx]).astype(BF16)


def _prompt_attention(qt, k, vt, n_seq, seq):
    t = k.shape[0]
    tq = TQ
    nq = seq // tq
    return pl.pallas_call(
        _prompt_attention_body,
        grid=(n_seq, nq),
        in_specs=[pl.BlockSpec((QK_W, tq), lambda b, i: (0, b * nq + i)),
                  pl.BlockSpec((seq, QK_W), lambda b, i: (b, 0)),
                  pl.BlockSpec((V_W, seq), lambda b, i: (0, b))],
        out_specs=pl.BlockSpec((V_W, tq), lambda b, i: (0, b * nq + i)),
        out_shape=jax.ShapeDtypeStruct((V_W, t), BF16),
        scratch_shapes=[pltpu.VMEM((ATTN_HEADS_PER_LOOP, V_DIM, tq), F32),
                        pltpu.VMEM((ATTN_HEADS_PER_LOOP, 1, tq), F32), pltpu.VMEM((ATTN_HEADS_PER_LOOP, 1, tq), F32),
                        pltpu.VMEM((ATTN_HEADS_PER_LOOP, tq, tq), F32), pltpu.VMEM((ATTN_HEADS_PER_LOOP, tq, tq), F32)],
        compiler_params=_cparams("arbitrary", "arbitrary"),
    )(qt, k, vt)


def _sample_attention_body(n_pages, pt_ref, qlat_ref, q_ref, ckvn_ref, krn_ref, cc_ref, cr_ref, olat_ref,
                           bufc, bufr, kcb, krb, sem):
    b = pl.program_id(0)
    nb = pl.num_programs(0)
    slot = b % 2
    past = n_pages * PAGE_SIZE

    def page_copies(bb, sl, p):
        pg = pt_ref[bb, p]
        pos = pl.ds(pl.multiple_of(p * PAGE_SIZE, PAGE_SIZE), PAGE_SIZE)
        return (pltpu.make_async_copy(cc_ref.at[pg], bufc.at[sl, pos], sem.at[0, sl]),
                pltpu.make_async_copy(cr_ref.at[pg], bufr.at[sl, :, pos], sem.at[1, sl]))

    def issue(bb, sl):
        for p in range(n_pages):
            for c in page_copies(bb, sl, p):
                c.start(priority=p % 2)

    def wait(sl):
        pltpu.make_async_copy(bufc.at[sl], bufc.at[sl], sem.at[0, sl]).wait()
        pltpu.make_async_copy(bufr.at[sl], bufr.at[sl], sem.at[1, sl]).wait()

    @pl.when(b == 0)
    def _():
        issue(0, 0)

    wait(slot)
    issue(jnp.minimum(b + 1, nb - 1), 1 - slot)

    ql = qlat_ref[0]
    qr = q_ref[0][:, 0:ROPE_DIM]
    cn = ckvn_ref[0]
    rn = krn_ref[0]
    chunk = min(KEY_CHUNK, past)
    scores = []
    for c in range(past // chunk):
        pos = slice(c * chunk, (c + 1) * chunk)
        kc = bufc[slot, pos, :]
        kcb[pos, :] = kc.astype(BF16)
        krb[:, pos] = bufr[slot, :, pos].astype(BF16)
        scores.append(_dot(ql, kc.T.astype(BF16)) + _dot(qr, krb[:, pos]))
    s_new = (jnp.sum(ql.astype(F32) * cn, axis=-1, keepdims=True)
             + jnp.sum(qr.astype(F32) * rn, axis=-1, keepdims=True))
    m = s_new
    for s in scores:
        m = jnp.maximum(m, jnp.max(s, axis=-1, keepdims=True))
    p_new = jnp.exp2(s_new - m)
    l = p_new
    o = p_new * cn
    for c, s in enumerate(scores):
        p = jnp.exp2(s - m)
        l = l + jnp.sum(p, axis=-1, keepdims=True)
        o = o + _dot(p.astype(BF16), kcb[c * chunk:(c + 1) * chunk, :])
    olat_ref[0] = o / l

    @pl.when(b == nb - 1)
    def _():
        wait(1 - slot)


def _sample_attention(page_table, qlat, q, ckvn, krn, cache_c, cache_rt):
    n, n_pages = page_table.shape
    past = n_pages * PAGE_SIZE
    blk = lambda d1, d2: pl.BlockSpec((1, d1, d2), lambda b, pt: (b, 0, 0))
    grid_spec = pltpu.PrefetchScalarGridSpec(
        num_scalar_prefetch=1,
        grid=(n,),
        in_specs=[blk(N_HEADS, KV_LORA), blk(N_HEADS, HEAD_W), blk(1, KV_LORA), blk(1, ROPE_DIM),
                  pl.BlockSpec(memory_space=pl.ANY), pl.BlockSpec(memory_space=pl.ANY)],
        out_specs=blk(N_HEADS, KV_LORA),
        scratch_shapes=[pltpu.VMEM((2, past, KV_LORA), F32), pltpu.VMEM((2, ROPE_DIM, past), F32),
                        pltpu.VMEM((past, KV_LORA), BF16), pltpu.VMEM((ROPE_DIM, past), BF16),
                        pltpu.SemaphoreType.DMA((2, 2))],
    )
    return pl.pallas_call(
        functools.partial(_sample_attention_body, n_pages),
        grid_spec=grid_spec,
        out_shape=jax.ShapeDtypeStruct((n, N_HEADS, KV_LORA), F32),
        compiler_params=_cparams("arbitrary"),
    )(page_table, qlat.reshape(n, N_HEADS, KV_LORA), q.reshape(n, N_HEADS, HEAD_W),
      ckvn.reshape(n, 1, KV_LORA), krn.reshape(n, 1, ROPE_DIM), cache_c, cache_rt)


def _route(logits):
    lane_i = lax.broadcasted_iota(jnp.int32, (logits.shape[0], LANES), 1)
    lane = lane_i.astype(F32)
    first_at = lambda hit: jnp.min(jnp.where(hit, lane, float(LANES)), axis=-1, keepdims=True)
    lg = jnp.where(lane_i < N_GROUPS, logits[:, 0:LANES], NEG)
    mg = jnp.max(lg, axis=-1, keepdims=True)
    p_grp = 1.0 / jnp.sum(jnp.exp(lg - mg), axis=-1, keepdims=True)
    grp = first_at(lg == mg)
    lane_grp = (lane_i // EXP_PER_GROUP).astype(F32)
    le = jnp.where(lane_grp == grp, logits[:, LANES:2 * LANES], NEG)
    top1 = jnp.max(le, axis=-1, keepdims=True)
    i1 = first_at(le == top1)
    le2 = jnp.where(lane == i1, NEG, le)
    top2 = jnp.max(le2, axis=-1, keepdims=True)
    i2 = first_at(le2 == top2)
    e2 = jnp.exp(top2 - top1)
    w1 = p_grp / (1.0 + e2)
    w2 = p_grp * e2 / (1.0 + e2)
    out = jnp.where(lane_i == R_EID, i1, 0.0)
    out = jnp.where(lane_i == R_EID + 1, i2, out)
    out = jnp.where(lane_i == R_WT, w1, out)
    return jnp.where(lane_i == R_WT + 1, w2, out)


def _add_ranks(route, ltri, count_ref):
    n = route.shape[0]
    lane_i = lax.broadcasted_iota(jnp.int32, (n, LANES), 1)
    lane = lane_i.astype(F32)
    hit1 = lane == route[:, R_EID:R_EID + 1]
    hit2 = lane == route[:, R_EID + 1:R_EID + 2]
    chosen = jnp.where(hit1, 1.0, jnp.where(hit2, 1.0, 0.0))
    before = _dot(ltri, chosen.astype(BF16)) + count_ref[...]
    r1 = jnp.sum(jnp.where(hit1, before, 0.0), axis=-1, keepdims=True)
    r2 = jnp.sum(jnp.where(hit2, before, 0.0), axis=-1, keepdims=True)
    count_ref[...] = count_ref[...] + jnp.sum(chosen, axis=0, keepdims=True)
    route = jnp.where(lane_i == R_RANK, r1, route)
    return jnp.where(lane_i == R_RANK + 1, r2, route)


def _merge_rows(alpha, x, bconv, y_attn, wg_ref, wbc_ref, wo_ref, lg_ref, lb_ref, wr_ref, br_ref):
    g = _dot(x.astype(BF16), wg_ref[...])
    y_conv = _dot(bconv, wbc_ref[...])
    m = jax.nn.sigmoid(g[:, 0:D_MODEL]) * y_conv + jax.nn.sigmoid(g[:, D_MODEL:]) * y_attn
    mix = _dot(m.astype(BF16), wo_ref[...])
    h1 = _layernorm(alpha * x + mix, lg_ref[...], lb_ref[...])
    return h1, _route(_dot(h1.astype(BF16), wr_ref[...]) + br_ref[...])


def _merge_sample_body(alpha, x_ref, bconv_ref, olat_ref, wbd_ref, wba_ref, *refs):
    w_refs, (h1_ref, route_ref) = refs[:-2], refs[-2:]
    o = _dot(olat_ref[...].astype(BF16), wbd_ref[...]).astype(BF16)
    h1_ref[...], route_ref[...] = _merge_rows(alpha, x_ref[...], bconv_ref[...], _dot(o, wba_ref[...]), *w_refs)


def _merge_prompt_body(alpha, n_tiles, x_ref, bconv_ref, ot_ref, h1s_ref, routes_ref, wba_ref, *refs):
    w_refs, (h1t_ref, route_ref, routet_ref, counts_ref, ltri_ref, count_ref) = refs[:-6], refs[-6:]
    i = pl.program_id(0)
    tm = x_ref.shape[0]

    @pl.when(i == 0)
    def _():
        r = lax.broadcasted_iota(jnp.int32, (tm, tm), 0)
        c = lax.broadcasted_iota(jnp.int32, (tm, tm), 1)
        ltri_ref[...] = jnp.where(c < r, 1.0, 0.0).astype(BF16)
        count_ref[...] = jnp.zeros(count_ref.shape, F32)

    @pl.when(i < n_tiles)
    def _():
        y_attn = _dot_tn(ot_ref[...], wba_ref[...])
        h1, route = _merge_rows(alpha, x_ref[...], bconv_ref[...], y_attn, *w_refs)
        for c in range(ROW_TILE):
            _tile_rows(h1t_ref, c, tm)[...] = h1[:, c * LANES:(c + 1) * LANES]
        route_ref[...] = _add_ranks(route, ltri_ref[...], count_ref)

    @pl.when(i == n_tiles)
    def _():
        n_s = h1s_ref.shape[0]
        h1t_ref[...] = jnp.zeros(h1t_ref.shape, F32)
        route_ref[...] = jnp.zeros(route_ref.shape, F32)
        for c in range(ROW_TILE):
            _tile_rows(h1t_ref, c, n_s)[...] = h1s_ref[:, c * LANES:(c + 1) * LANES]
        route_ref[0:n_s, :] = _add_ranks(routes_ref[...], ltri_ref[0:n_s, 0:n_s], count_ref)

    counts_ref[...] = jnp.broadcast_to(count_ref[...], counts_ref.shape)
    routet_ref[...] = route_ref[...].T[0:SUBLANES, :]


def _merge_sample(alpha, x, bconv, olat, wbd, wba, ws):
    n = x.shape[0]
    args = (x, bconv, olat, wbd, wba) + tuple(ws)
    out_shape = [jax.ShapeDtypeStruct((n, D_MODEL), F32), jax.ShapeDtypeStruct((n, LANES), F32)]
    return pl.pallas_call(
        functools.partial(_merge_sample_body, alpha),
        grid=(1,),
        in_specs=[_full(a.shape) for a in args],
        out_specs=[_full(s.shape) for s in out_shape],
        out_shape=out_shape,
        compiler_params=_cparams("arbitrary"),
    )(*args)


def _merge_prompt(alpha, x, bconv, ot, h1_s, route_s, wba, ws):
    t = x.shape[0]
    tm = TM
    n_tiles = t // tm
    t_all = t + h1_s.shape[0]
    assert h1_s.shape[0] <= tm
    clamp = lambda i: jnp.minimum(i, n_tiles - 1)
    rows_in = lambda w: pl.BlockSpec((tm, w), lambda i: (clamp(i), 0))
    return pl.pallas_call(
        functools.partial(_merge_prompt_body, alpha, n_tiles),
        grid=(n_tiles + 1,),
        in_specs=[rows_in(D_MODEL), rows_in(D_CONV), pl.BlockSpec((V_W, tm), lambda i: (0, clamp(i))),
                  _full(h1_s.shape), _full(route_s.shape), _full(wba.shape)] + [_full(w.shape) for w in ws],
        out_specs=[pl.BlockSpec((tm * ROW_TILE, LANES), lambda i: (i, 0)), pl.BlockSpec((tm, LANES), lambda i: (i, 0)),
                   pl.BlockSpec((SUBLANES, tm), lambda i: (0, i)), _full((SUBLANES, LANES))],
        out_shape=[jax.ShapeDtypeStruct((t_all * ROW_TILE, LANES), F32), jax.ShapeDtypeStruct((t_all, LANES), F32),
                   jax.ShapeDtypeStruct((SUBLANES, t_all), F32), jax.ShapeDtypeStruct((SUBLANES, LANES), F32)],
        scratch_shapes=[pltpu.VMEM((tm, tm), BF16), pltpu.VMEM((1, LANES), F32)],
        compiler_params=_cparams("arbitrary"),
    )(x, bconv, ot, h1_s, route_s, wba, *ws)


SLOT_GROUP = 16


def _slot_tokens_body(rows_ref, dest_ref, tok_ref):
    def clear_row(j, c):
        for l in range(LANES):
            tok_ref[j * LANES + l] = 0
        return c
    lax.fori_loop(0, rows_ref[1], clear_row, 0)

    rows_k = rows_ref[0]
    for k in range(TOP_K):
        def fill_row(j, c):
            for g in range(0, LANES, SLOT_GROUP):
                ds = [dest_ref[k * rows_k + j, g + l] for l in range(SLOT_GROUP)]
                for l, d in enumerate(ds):
                    tok_ref[d] = j * LANES + g + l
            return c
        lax.fori_loop(0, rows_k, fill_row, 0)


def _slot_tokens(dest, n_blocks):
    n_slots = n_blocks * MOE_BLOCK
    n_tok = dest.shape[1]
    assert n_slots % LANES == 0 and n_tok % LANES == 0
    dest = dest.reshape(-1, LANES)
    grid_spec = pltpu.PrefetchScalarGridSpec(
        num_scalar_prefetch=2,
        grid=(1,),
        in_specs=[],
        out_specs=pl.BlockSpec(memory_space=pltpu.SMEM),
    )
    return pl.pallas_call(
        _slot_tokens_body,
        grid_spec=grid_spec,
        out_shape=jax.ShapeDtypeStruct((n_slots,), jnp.int32),
        compiler_params=_cparams("arbitrary"),
    )(jnp.array([n_tok // LANES, n_slots // LANES], jnp.int32), dest)


def _row_tile(i):
    return pl.ds(pl.multiple_of(i * ROW_TILE, ROW_TILE), ROW_TILE)


def _experts_body(blk_e_ref, used_ref, nvalid_ref, tok_ref, h1t_ref, wg_ref, wu_ref, wd_ref, ys_ref,
                  xbuf, xb, wgb, wub, wdb, sem_in):
    b = pl.program_id(0)
    ring = lambda bb: lax.rem(bb, GATHER_RING)
    used = used_ref[0]
    blk_rows = MOE_BLOCK * ROW_TILE
    changed = jnp.logical_or(b == 0, blk_e_ref[b] != blk_e_ref[jnp.maximum(b - 1, 0)])
    last = used - 1
    clamp = lambda v: jnp.minimum(v, last)

    def in_copy(bb, sl, r):
        tok = tok_ref[bb * MOE_BLOCK + r]
        return pltpu.make_async_copy(h1t_ref.at[_row_tile(tok)], xbuf.at[sl, _row_tile(r)], sem_in.at[sl])

    def issue_full(v):
        for r in range(MOE_BLOCK):
            in_copy(clamp(v), ring(v), r).start(priority=1)

    def issue_partial(v):
        lax.fori_loop(0, nvalid_ref[clamp(v)], lambda r, c: (in_copy(clamp(v), ring(v), r).start(), c)[1], 0)

    def is_full(v):
        return nvalid_ref[clamp(v)] == MOE_BLOCK

    def wait_in(v):
        sl = ring(v)

        @pl.when(is_full(v))
        def _():
            pltpu.make_async_copy(h1t_ref.at[pl.ds(0, blk_rows)], xbuf.at[sl], sem_in.at[sl]).wait()

        @pl.when(jnp.logical_not(is_full(v)))
        def _():
            lax.fori_loop(0, nvalid_ref[clamp(v)], lambda r, c: (in_copy(clamp(v), sl, r).wait(), c)[1], 0)

    @pl.when(jnp.logical_and(b == 0, used > 0))
    def _():
        xbuf[...] = jnp.zeros(xbuf.shape, F32)
        for a in range(GATHER_RING - 1):
            @pl.when(is_full(a))
            def _():
                issue_full(a)

            @pl.when(jnp.logical_not(is_full(a)))
            def _():
                issue_partial(a)

    @pl.when(jnp.logical_and(b < used, changed))
    def _():
        wgb[...] = wg_ref[0].astype(BF16)
        wub[...] = wu_ref[0].astype(BF16)
        wdb[...] = wd_ref[0].astype(BF16)

    ahead = b + GATHER_RING - 1

    def block(issue_ahead):
        for c in range(ROW_TILE):
            xb[:, c * LANES:(c + 1) * LANES] = _tile_rows(xbuf.at[ring(b)], c, MOE_BLOCK)[...].astype(BF16)
        if issue_ahead:
            issue_full(ahead)
        x = xb[...]
        g = _dot(x, wgb[...])
        u = _dot(x, wub[...])
        h = (g * jax.nn.sigmoid(g) * u).astype(BF16)
        y = _dot(h, wdb[...])
        for c in range(ROW_TILE):
            _tile_rows(ys_ref, c, MOE_BLOCK)[...] = y[:, c * LANES:(c + 1) * LANES]

    @pl.when(b < used)
    def _():
        wait_in(b)

    @pl.when(jnp.logical_and(b < used, is_full(ahead)))
    def _():
        block(True)

    @pl.when(jnp.logical_and(b < used, jnp.logical_not(is_full(ahead))))
    def _():
        block(False)
        issue_partial(ahead)

    @pl.when(b == last)
    def _():
        for a in range(1, GATHER_RING):
            wait_in(b + a)

    @pl.when(b >= used)
    def _():
        ys_ref[...] = jnp.zeros(ys_ref.shape, F32)


def _experts(blk_e, used, nvalid, slot_tok, h1t, wg, wu, wd):
    n_blocks = blk_e.shape[0]
    blk_rows = MOE_BLOCK * ROW_TILE
    by_expert = lambda b, e, u, n, a: (e[b], 0, 0)
    grid_spec = pltpu.PrefetchScalarGridSpec(
        num_scalar_prefetch=4,
        grid=(n_blocks,),
        in_specs=[pl.BlockSpec(memory_space=pl.ANY),
                  pl.BlockSpec((1, D_MODEL, D_EXPERT), by_expert), pl.BlockSpec((1, D_MODEL, D_EXPERT), by_expert),
                  pl.BlockSpec((1, D_EXPERT, D_MODEL), by_expert)],
        out_specs=pl.BlockSpec((blk_rows, LANES), lambda b, e, u, n, a: (b, 0)),
        scratch_shapes=[pltpu.VMEM((GATHER_RING, blk_rows, LANES), F32),
                        pltpu.VMEM((MOE_BLOCK, D_MODEL), BF16),
                        pltpu.VMEM((D_MODEL, D_EXPERT), BF16), pltpu.VMEM((D_MODEL, D_EXPERT), BF16),
                        pltpu.VMEM((D_EXPERT, D_MODEL), BF16),
                        pltpu.SemaphoreType.DMA((GATHER_RING,))],
    )
    return pl.pallas_call(
        _experts_body,
        grid_spec=grid_spec,
        out_shape=jax.ShapeDtypeStruct((n_blocks * blk_rows, LANES), F32),
        compiler_params=_cparams("arbitrary"),
    )(blk_e, used, nvalid, slot_tok, h1t, wg, wu, wd)


def _combine_body(alpha, tok0, n_all, dest_ref, ys_ref, h1t_ref, route_ref, lg_ref, lb_ref, out_ref, gbuf, sem):
    i = pl.program_id(0)
    nt = pl.num_programs(0)
    slot = i % 2
    tm = out_ref.shape[0]

    def issue(ii, sl):
        base = tok0 + ii * tm
        for r in range(tm):
            for k in range(TOP_K):
                d = dest_ref[base + k * n_all + r]
                pltpu.make_async_copy(ys_ref.at[_row_tile(d)], gbuf.at[sl, k, _row_tile(r)],
                                      sem.at[sl]).start(priority=k)

    def wait(sl):
        for k in range(TOP_K):
            pltpu.make_async_copy(ys_ref.at[pl.ds(0, tm * ROW_TILE)], gbuf.at[sl, k], sem.at[sl]).wait()

    @pl.when(i == 0)
    def _():
        issue(0, 0)

    wait(slot)
    issue(jnp.minimum(i + 1, nt - 1), 1 - slot)

    route = route_ref[...]
    w0 = jnp.broadcast_to(route[:, R_WT:R_WT + 1], (tm, LANES))
    w1 = jnp.broadcast_to(route[:, R_WT + 1:R_WT + 2], (tm, LANES))
    z = [alpha * _tile_rows(h1t_ref, c, tm)[...]
         + w0 * _tile_rows(gbuf.at[slot, 0], c, tm)[...] + w1 * _tile_rows(gbuf.at[slot, 1], c, tm)[...]
         for c in range(ROW_TILE)]
    mu = sum(jnp.sum(zc, axis=-1, keepdims=True) for zc in z) / D_MODEL
    zc = [v - mu for v in z]
    var = sum(jnp.sum(v * v, axis=-1, keepdims=True) for v in zc) / D_MODEL
    rstd = lax.rsqrt(var + LN_EPS)
    for c in range(ROW_TILE):
        sl = slice(c * LANES, (c + 1) * LANES)
        out_ref[:, sl] = zc[c] * rstd * lg_ref[:, sl] + lb_ref[:, sl]

    @pl.when(i == nt - 1)
    def _():
        wait(1 - slot)


def _combine(alpha, tok0, n_tok, tm, dest, ys, h1t, route, lg, lb):
    assert n_tok % tm == 0 and tok0 % tm == 0
    blk0 = tok0 // tm
    grid_spec = pltpu.PrefetchScalarGridSpec(
        num_scalar_prefetch=1,
        grid=(n_tok // tm,),
        in_specs=[pl.BlockSpec(memory_space=pl.ANY),
                  pl.BlockSpec((tm * ROW_TILE, LANES), lambda i, d: (blk0 + i, 0)),
                  pl.BlockSpec((tm, LANES), lambda i, d: (blk0 + i, 0)),
                  pl.BlockSpec(lg.shape, lambda i, d: (0, 0)), pl.BlockSpec(lb.shape, lambda i, d: (0, 0))],
        out_specs=pl.BlockSpec((tm, D_MODEL), lambda i, d: (i, 0)),
        scratch_shapes=[pltpu.VMEM((2, TOP_K, tm * ROW_TILE, LANES), F32), pltpu.SemaphoreType.DMA((2,))],
    )
    return pl.pallas_call(
        functools.partial(_combine_body, alpha, tok0, route.shape[0]),
        grid_spec=grid_spec,
        out_shape=jax.ShapeDtypeStruct((n_tok, D_MODEL), F32),
        compiler_params=_cparams("arbitrary"),
    )(dest, ys, h1t, route, lg, lb)


def _rope_tables(pos):
    inv = ROPE_BASE ** (-(jnp.arange(ROPE_HALF, dtype=F32) * 2.0 / ROPE_DIM))
    ang = pos.astype(F32)[:, None] * inv[None, :]
    cos, sin = jnp.cos(ang), jnp.sin(ang)
    n = pos.shape[0]
    a = jnp.concatenate([cos, cos, jnp.ones((n, LANES - ROPE_DIM), F32)], axis=1)
    bm = jnp.concatenate([-sin, jnp.zeros((n, LANES - ROPE_HALF), F32)], axis=1)
    cp = jnp.concatenate([jnp.zeros((n, ROPE_HALF), F32), sin, jnp.zeros((n, LANES - ROPE_DIM), F32)], axis=1)
    k_tabs = jnp.stack([a, bm, cp])
    return jnp.concatenate([k_tabs * Q_SCALE, k_tabs]), jnp.stack([cos.T, sin.T]) * Q_SCALE


def _head_blocks(w_rope, w_nope):
    k = w_nope.shape[0]
    pad = jnp.zeros((k, N_HEADS, HEAD_W - QK_DIM), w_nope.dtype)
    return jnp.concatenate([w_rope, w_nope, pad], axis=-1).reshape(k, QK_W)


def _pack_weights(w_in, w_uq, w_uk, w_uv, router_w_group, router_b_group, router_w_expert, router_b_expert):
    d = w_in.shape[0]
    c_kr = 3 * D_CONV + Q_LORA + KV_LORA
    w1 = jnp.concatenate([w_in[:, :c_kr], w_in[:, c_kr:c_kr + ROPE_DIM],
                          jnp.zeros((d, LANES - ROPE_DIM), w_in.dtype)], axis=1).astype(BF16)
    wg = w_in[:, c_kr + ROPE_DIM:].astype(BF16)
    uq = w_uq.reshape(Q_LORA, N_HEADS, QK_DIM)
    wuq = _head_blocks(uq[..., NOPE_DIM:], uq[..., :NOPE_DIM]).astype(BF16)
    wuk = _head_blocks(jnp.zeros((KV_LORA, N_HEADS, ROPE_DIM), w_uk.dtype), w_uk).astype(BF16)
    wukt = jnp.transpose(wuk.reshape(KV_LORA, N_HEADS, HEAD_W), (1, 2, 0))
    wuvt = w_uv.reshape(KV_LORA, V_W).T.astype(BF16)
    eye = jnp.eye(N_HEADS, dtype=w_uv.dtype)
    wbd = jnp.einsum('chd,hg->hcgd', w_uv, eye).reshape(N_HEADS * KV_LORA, V_W).astype(BF16)
    wr = jnp.zeros((d, ROUTE_W), F32)
    wr = wr.at[:, 0:N_GROUPS].set(router_w_group).at[:, LANES:LANES + N_EXPERTS].set(router_w_expert).astype(BF16)
    br = jnp.zeros((1, ROUTE_W), F32)
    br = br.at[0, 0:N_GROUPS].set(router_b_group).at[0, LANES:LANES + N_EXPERTS].set(router_b_expert)
    return w1, wg, wuq, wuq.T, wuk, wukt, wuvt, wbd, wr, br


def _dispatch_plan(route_t, counts, n_blocks):
    counts = counts[0, 0:N_EXPERTS].astype(jnp.int32)
    pcounts = (counts + MOE_BLOCK - 1) // MOE_BLOCK * MOE_BLOCK
    pend = jnp.cumsum(pcounts).astype(jnp.int32)
    pstart = (pend - pcounts).astype(F32)
    eid = route_t[R_EID:R_EID + TOP_K]
    rank = route_t[R_RANK:R_RANK + TOP_K]
    base = jnp.zeros_like(eid)
    for e in range(N_EXPERTS):
        base = jnp.where(eid == float(e), pstart[e], base)
    dest = (base + rank).astype(jnp.int32)
    blk_start = jnp.arange(n_blocks, dtype=jnp.int32) * MOE_BLOCK
    blk_e = jnp.sum((blk_start[:, None] >= pend[None, :]).astype(jnp.int32), axis=1)
    blk_e = jnp.minimum(blk_e, N_EXPERTS - 1).astype(jnp.int32)
    used = (pend[-1:] // MOE_BLOCK).astype(jnp.int32)
    filled_end = (pend - pcounts + counts)[blk_e]
    nvalid = jnp.where(blk_start < pend[-1], jnp.clip(filled_end - blk_start, 0, MOE_BLOCK), 0).astype(jnp.int32)
    return dest, blk_e, used, nvalid


def kernel(x_prompt, x_sample, cache_ckv, cache_krope, state_conv, page_table, w_in, conv_w, q_norm_g, w_uq,
           kv_norm_g, w_uk, w_uv, w_br_conv, w_br_attn, w_o, ln1_g, ln1_b, router_w_group, router_b_group,
           router_w_expert, router_b_expert, w_gate, w_up, w_down, ln2_g, ln2_b):
    depth = w_in.shape[0]
    alpha = (2 * depth) ** 0.25
    n_p, t_p, _ = x_prompt.shape
    n_s, t_s, _ = x_sample.shape
    assert t_s == 1 and t_p % TM == 0 and n_s % LANES == 0
    rows_p = n_p * t_p
    rows_all = rows_p + n_s
    past = page_table.shape[1] * PAGE_SIZE
    tab_p, tabt_p = _rope_tables(jnp.arange(t_p))
    tab_s, _ = _rope_tables(jnp.full((n_s,), past, jnp.int32))
    n_blocks = -(-(rows_all * TOP_K) // MOE_BLOCK) + N_EXPERTS

    h_p = x_prompt.reshape(rows_p, D_MODEL)
    h_s = x_sample.reshape(n_s, D_MODEL)
    ckv_p, kr_p, cv_p, ckv_s, kr_s, cv_s = [], [], [], [], [], []
    row = lambda v: v.reshape(1, -1)
    for l in range(depth):
        w1, wg, wuq, wuqt, wuk, wukt, wuvt, wbd, wr, br = _pack_weights(
            w_in[l], w_uq[l], w_uk[l], w_uv[l], router_w_group[l], router_b_group[l],
            router_w_expert[l], router_b_expert[l])
        qg, kvg = row(q_norm_g[l]), row(kv_norm_g[l])
        wba = w_br_attn[l].astype(BF16)
        merge_w = (wg, w_br_conv[l].astype(BF16), w_o[l].astype(BF16), row(ln1_g[l]), row(ln1_b[l]), wr, br)

        bconv, qt, k, vt, ckv, krt, nconv = _inproj_prompt(h_p, tab_p, tabt_p, w1, wuqt, wuk, wuvt, qg, kvg, conv_w[l],
                                                          n_p, t_p)
        ot = _prompt_attention(qt, k, vt, n_p, t_p)

        st = state_conv[l]
        bconv_s, q_s, qlat, ckvn_s, krn_s, u_s = _inproj_sample(
            h_s, tab_s, w1, wuq, wukt, qg, kvg, conv_w[l], st[:, 0], st[:, 1])
        olat = _sample_attention(page_table, qlat, q_s, ckvn_s, krn_s, cache_ckv[l],
                                 jnp.swapaxes(cache_krope[l], 1, 2))
        h1_s, route_s = _merge_sample(alpha, h_s, bconv_s, olat.reshape(n_s, N_HEADS * KV_LORA), wbd, wba, merge_w)
        h1t, route, route_t, counts = _merge_prompt(alpha, h_p, bconv, ot, h1_s, route_s, wba, merge_w)

        dest, blk_e, used, nvalid = _dispatch_plan(route_t, counts, n_blocks)
        slot_tok = _slot_tokens(dest, n_blocks)
        ys = _experts(blk_e, used, nvalid, slot_tok, h1t, w_gate[l], w_up[l], w_down[l])
        dest = dest.reshape(-1)
        ln2 = (row(ln2_g[l]), row(ln2_b[l]))
        h_p = _combine(alpha, 0, rows_p, TM_COMBINE, dest, ys, h1t, route, *ln2)
        h_s = _combine(alpha, rows_p, n_s, n_s, dest, ys, h1t, route, *ln2)

        ckv_p.append(ckv.reshape(n_p, t_p, KV_LORA))
        kr_p.append(jnp.swapaxes(krt, 1, 2))
        cv_p.append(nconv)
        ckv_s.append(ckvn_s.reshape(n_s, 1, KV_LORA))
        kr_s.append(krn_s.reshape(n_s, 1, ROPE_DIM))
        cv_s.append(jnp.stack([st[:, 1], u_s], axis=1))
    return (h_p.reshape(n_p, t_p, D_MODEL), h_s.reshape(n_s, 1, D_MODEL), jnp.stack(ckv_p), jnp.stack(kr_p),
            jnp.stack(cv_p), jnp.stack(ckv_s), jnp.stack(kr_s), jnp.stack(cv_s))
```

```python
import functools

import jax
import jax.numpy as jnp
from jax import lax
from jax.experimental import pallas as pl
from jax.experimental.pallas import tpu as pltpu

F32 = jnp.float32
BF16 = jnp.bfloat16

D_MODEL = 1024
D_CONV = 512
CONV_W = 3
N_HEADS = 8
Q_LORA = 384
KV_LORA = 256
NOPE_DIM = 64
ROPE_DIM = 32
ROPE_HALF = ROPE_DIM // 2
V_DIM = 64
QK_DIM = NOPE_DIM + ROPE_DIM
ROPE_BASE = 10000.0
ATTN_SCALE = QK_DIM ** -0.5
LOG2E = 1.4426950408889634
Q_SCALE = ATTN_SCALE * LOG2E
N_GROUPS = 4
EXP_PER_GROUP = 8
N_EXPERTS = N_GROUPS * EXP_PER_GROUP
TOP_K = 2
D_EXPERT = 512
MOE_BLOCK = 256
PAGE_SIZE = 128
LN_EPS = 1e-5
RMS_EPS = 1e-6

LANES = 128
SUBLANES = 8
ROW_TILE = D_MODEL // LANES
assert ROW_TILE == SUBLANES
HEAD_W = LANES
QK_W = N_HEADS * HEAD_W
V_W = N_HEADS * V_DIM
_C_BG, _C_CG, _C_H = 0, D_CONV, 2 * D_CONV
_C_CQ = 3 * D_CONV
_C_CKV = _C_CQ + Q_LORA
_C_KR = _C_CKV + KV_LORA
W1_COLS = _C_KR + LANES
ROUTE_W = 2 * LANES
R_EID, R_WT, R_RANK = 0, 2, 4
NEG = -1e30
VMEM_LIMIT = 56 * 1024 * 1024

TM = 512
TM_INPROJ = 1024
KEY_CHUNK = 2048
ATTN_HEADS_PER_LOOP = 8
TQ = 512
TM_COMBINE = 256
GATHER_RING = 4


def _cparams(*sem):
    return pltpu.CompilerParams(dimension_semantics=sem, vmem_limit_bytes=VMEM_LIMIT)


def _dot(a, b):
    return jnp.dot(a, b, preferred_element_type=F32)


def _dot_nt(a, b):
    return lax.dot_general(a, b, (((1,), (1,)), ((), ())), preferred_element_type=F32)


def _dot_tn(a, b):
    return lax.dot_general(a, b, (((0,), (0,)), ((), ())), preferred_element_type=F32)


def _rms(x, g):
    return x * lax.rsqrt(jnp.mean(x * x, axis=-1, keepdims=True) + RMS_EPS) * g


def _layernorm(x, g, b):
    mu = jnp.mean(x, axis=-1, keepdims=True)
    xc = x - mu
    var = jnp.mean(xc * xc, axis=-1, keepdims=True)
    return xc * lax.rsqrt(var + LN_EPS) * g + b


def _rope(xh, a, bm, cp):
    return xh * a + pltpu.roll(xh, LANES - ROPE_HALF, 1) * bm + pltpu.roll(xh, ROPE_HALF, 1) * cp


def _full(shape):
    nd = len(shape)
    return pl.BlockSpec(shape, lambda *_: (0,) * nd)


def _tile_rows(ref, c, n):
    return ref.at[pl.ds(c, n, stride=ROW_TILE), :]


def _inproj_common(x_ref, tab_ref, w1_ref, qg_ref, kvg_ref, ckv_ref):
    xb = x_ref[...].astype(BF16)

    def seg(lo, hi):
        return _dot(xb, w1_ref[:, lo:hi])

    b_g = seg(_C_BG, _C_CG)
    u = seg(_C_CG, _C_H) * seg(_C_H, _C_CQ)
    cqn = _rms(seg(_C_CQ, _C_CKV), qg_ref[...]).astype(BF16)
    ckvn = _rms(seg(_C_CKV, _C_KR), kvg_ref[...])
    ckv_ref[...] = ckvn
    krr = _rope(seg(_C_KR, W1_COLS), tab_ref[3], tab_ref[4], tab_ref[5])
    return b_g, u, cqn, ckvn.astype(BF16), krr


def _inproj_prompt_body(tiles_per_seq, x_ref, tab_ref, tabt_ref, w1_ref, wuqt_ref, wuk_ref, wuvt_ref, qg_ref, kvg_ref,
                        cw_ref, bconv_ref, qt_ref, k_ref, vt_ref, ckv_ref, krt_ref, nconv_ref, uext_ref):
    tm = x_ref.shape[0]
    b_g, u, cqn, cb, krr = _inproj_common(x_ref, tab_ref, w1_ref, qg_ref, kvg_ref, ckv_ref)
    krt_ref[0] = krr.T[0:ROPE_DIM, :]
    qft = _dot_nt(wuqt_ref[...], cqn)
    cos, sin = tabt_ref[0], tabt_ref[1]
    for h in range(N_HEADS):
        r0 = h * HEAD_W
        x1 = qft[r0:r0 + ROPE_HALF, :]
        x2 = qft[r0 + ROPE_HALF:r0 + ROPE_DIM, :]
        qt_ref[r0:r0 + ROPE_HALF, :] = (x1 * cos - x2 * sin).astype(BF16)
        qt_ref[r0 + ROPE_HALF:r0 + ROPE_DIM, :] = (x1 * sin + x2 * cos).astype(BF16)
        qt_ref[r0 + ROPE_DIM:r0 + HEAD_W, :] = (qft[r0 + ROPE_DIM:r0 + HEAD_W, :] * Q_SCALE).astype(BF16)
    kn = _dot(cb, wuk_ref[...])
    for h in range(N_HEADS):
        sl = slice(h * HEAD_W, (h + 1) * HEAD_W)
        k_ref[:, sl] = (kn[:, sl] + krr).astype(BF16)
    vt_ref[...] = _dot_nt(wuvt_ref[...], cb).astype(BF16)

    first = (pl.program_id(0) % tiles_per_seq) == 0

    @pl.when(first)
    def _():
        uext_ref[0:8, :] = jnp.zeros((8, D_CONV), F32)

    @pl.when(jnp.logical_not(first))
    def _():
        uext_ref[0:8, :] = uext_ref[tm:tm + 8, :]

    uext_ref[8:8 + tm, :] = u
    conv = cw_ref[0:1, :] * uext_ref[6:6 + tm, :] + cw_ref[1:2, :] * uext_ref[7:7 + tm, :] + cw_ref[2:3, :] * u
    bconv_ref[...] = (b_g * conv).astype(BF16)
    nconv_ref[0] = u[tm - (CONV_W - 1):tm, :]


def _inproj_sample_body(x_ref, tab_ref, w1_ref, wuq_ref, wukt_ref, qg_ref, kvg_ref, cw_ref, s0_ref, s1_ref,
                        bconv_ref, q_ref, qlat_ref, ckv_ref, kr_ref, u_ref):
    b_g, u, cqn, _, krr = _inproj_common(x_ref, tab_ref, w1_ref, qg_ref, kvg_ref, ckv_ref)
    kr_ref[...] = krr[:, 0:ROPE_DIM]
    qf = _dot(cqn, wuq_ref[...])
    for h in range(N_HEADS):
        sl = slice(h * HEAD_W, (h + 1) * HEAD_W)
        qh = _rope(qf[:, sl], tab_ref[0], tab_ref[1], tab_ref[2]).astype(BF16)
        q_ref[:, sl] = qh
        qlat_ref[:, h * KV_LORA:(h + 1) * KV_LORA] = _dot(qh, wukt_ref[h]).astype(BF16)
    conv = cw_ref[0:1, :] * s0_ref[...] + cw_ref[1:2, :] * s1_ref[...] + cw_ref[2:3, :] * u
    bconv_ref[...] = (b_g * conv).astype(BF16)
    u_ref[...] = u


def _inproj_prompt(x, tab, tabt, w1, wuqt, wuk, wuvt, qg, kvg, cw, n_seq, seq):
    t = x.shape[0]
    tm = TM_INPROJ
    assert seq % tm == 0
    tiles_per_seq = seq // tm
    rows = lambda w: pl.BlockSpec((tm, w), lambda i: (i, 0))
    cols = lambda h: pl.BlockSpec((h, tm), lambda i: (0, i))
    return pl.pallas_call(
        functools.partial(_inproj_prompt_body, tiles_per_seq),
        grid=(t // tm,),
        in_specs=[rows(D_MODEL),
                  pl.BlockSpec((6, tm, LANES), lambda i: (0, i % tiles_per_seq, 0)),
                  pl.BlockSpec((2, ROPE_HALF, tm), lambda i: (0, 0, i % tiles_per_seq)),
                  _full(w1.shape), _full(wuqt.shape), _full(wuk.shape), _full(wuvt.shape),
                  _full(qg.shape), _full(kvg.shape), _full(cw.shape)],
        out_specs=[rows(D_CONV), cols(QK_W), rows(QK_W), cols(V_W), rows(KV_LORA),
                   pl.BlockSpec((1, ROPE_DIM, tm), lambda i: (i // tiles_per_seq, 0, i % tiles_per_seq)),
                   pl.BlockSpec((1, CONV_W - 1, D_CONV), lambda i: (i // tiles_per_seq, 0, 0))],
        out_shape=[jax.ShapeDtypeStruct((t, D_CONV), BF16), jax.ShapeDtypeStruct((QK_W, t), BF16),
                   jax.ShapeDtypeStruct((t, QK_W), BF16), jax.ShapeDtypeStruct((V_W, t), BF16),
                   jax.ShapeDtypeStruct((t, KV_LORA), F32), jax.ShapeDtypeStruct((n_seq, ROPE_DIM, seq), F32),
                   jax.ShapeDtypeStruct((n_seq, CONV_W - 1, D_CONV), F32)],
        scratch_shapes=[pltpu.VMEM((tm + 8, D_CONV), F32)],
        compiler_params=_cparams("arbitrary"),
    )(x, tab, tabt, w1, wuqt, wuk, wuvt, qg, kvg, cw)


def _inproj_sample(x, tab, w1, wuq, wukt, qg, kvg, cw, s0, s1):
    n = x.shape[0]
    args = (x, tab, w1, wuq, wukt, qg, kvg, cw, s0, s1)
    out_shape = [jax.ShapeDtypeStruct((n, D_CONV), BF16), jax.ShapeDtypeStruct((n, QK_W), BF16),
                 jax.ShapeDtypeStruct((n, N_HEADS * KV_LORA), BF16), jax.ShapeDtypeStruct((n, KV_LORA), F32),
                 jax.ShapeDtypeStruct((n, ROPE_DIM), F32), jax.ShapeDtypeStruct((n, D_CONV), F32)]
    return pl.pallas_call(
        _inproj_sample_body,
        grid=(1,),
        in_specs=[_full(a.shape) for a in args],
        out_specs=[_full(s.shape) for s in out_shape],
        out_shape=out_shape,
        compiler_params=_cparams("arbitrary"),
    )(*args)


def _prompt_attention_body(qt_ref, k_ref, vt_ref, ot_ref, acc_ref, m_ref, l_ref, sa_ref, sb_ref):
    tq = qt_ref.shape[1]
    i = pl.program_id(1)

    for hg in range(N_HEADS // ATTN_HEADS_PER_LOOP):
        heads = tuple(range(hg * ATTN_HEADS_PER_LOOP, (hg + 1) * ATTN_HEADS_PER_LOOP))
        acc_ref[...] = jnp.zeros(acc_ref.shape, F32)
        m_ref[...] = jnp.full(m_ref.shape, NEG, F32)
        l_ref[...] = jnp.zeros(l_ref.shape, F32)

        def scores(j, s_ref):
            off = pl.multiple_of(j * tq, tq)
            for idx, h in enumerate(heads):
                s_ref[idx] = _dot(k_ref[pl.ds(off, tq), h * HEAD_W:(h + 1) * HEAD_W],
                                  qt_ref[h * HEAD_W:(h + 1) * HEAD_W, :])

        def consume(j, s_ref, masked):
            off = pl.multiple_of(j * tq, tq)
            for idx, h in enumerate(heads):
                s = s_ref[idx]
                if masked:
                    krow = lax.broadcasted_iota(jnp.int32, (tq, tq), 0)
                    qcol = lax.broadcasted_iota(jnp.int32, (tq, tq), 1)
                    s = jnp.where(krow <= qcol, s, NEG)
                m = m_ref[idx]
                m_new = jnp.maximum(m, jnp.max(s, axis=0, keepdims=True))
                alpha = jnp.exp2(m - m_new)
                p = jnp.exp2(s - m_new)
                m_ref[idx] = m_new
                l_ref[idx] = alpha * l_ref[idx] + jnp.sum(p, axis=0, keepdims=True)
                vblk = vt_ref[h * V_DIM:(h + 1) * V_DIM, pl.ds(off, tq)]
                acc_ref[idx] = alpha * acc_ref[idx] + _dot(vblk, p.astype(BF16))

        scores(0, sa_ref)

        def pair(pp, c):
            j = 2 * pp
            scores(j + 1, sb_ref)
            consume(j, sa_ref, False)
            scores(j + 2, sa_ref)
            consume(j + 1, sb_ref, False)
            return c
        lax.fori_loop(0, i // 2, pair, 0)

        @pl.when(i % 2 == 1)
        def _():
            scores(i, sb_ref)
            consume(i - 1, sa_ref, False)
            consume(i, sb_ref, True)

        @pl.when(i % 2 == 0)
        def _():
            consume(i, sa_ref, True)

        for idx, h in enumerate(heads):
            ot_ref[h * V_DIM:(h + 1) * V_DIM, :] = (acc_ref[idx] / l_ref[idx]).astype(BF16)


def _prompt_attention(qt, k, vt, n_seq, seq):
    t = k.shape[0]
    tq = TQ
    nq = seq // tq
    return pl.pallas_call(
        _prompt_attention_body,
        grid=(n_seq, nq),
        in_specs=[pl.BlockSpec((QK_W, tq), lambda b, i: (0, b * nq + i)),
                  pl.BlockSpec((seq, QK_W), lambda b, i: (b, 0)),
                  pl.BlockSpec((V_W, seq), lambda b, i: (0, b))],
        out_specs=pl.BlockSpec((V_W, tq), lambda b, i: (0, b * nq + i)),
        out_shape=jax.ShapeDtypeStruct((V_W, t), BF16),
        scratch_shapes=[pltpu.VMEM((ATTN_HEADS_PER_LOOP, V_DIM, tq), F32),
                        pltpu.VMEM((ATTN_HEADS_PER_LOOP, 1, tq), F32), pltpu.VMEM((ATTN_HEADS_PER_LOOP, 1, tq), F32),
                        pltpu.VMEM((ATTN_HEADS_PER_LOOP, tq, tq), F32), pltpu.VMEM((ATTN_HEADS_PER_LOOP, tq, tq), F32)],
        compiler_params=_cparams("arbitrary", "arbitrary"),
    )(qt, k, vt)


def _sample_attention_body(n_pages, pt_ref, qlat_ref, q_ref, ckvn_ref, krn_ref, cc_ref, cr_ref, olat_ref,
                           bufc, bufr, kcb, krb, sem):
    b = pl.program_id(0)
    nb = pl.num_programs(0)
    slot = b % 2
    past = n_pages * PAGE_SIZE

    def page_copies(bb, sl, p):
        pg = pt_ref[bb, p]
        pos = pl.ds(pl.multiple_of(p * PAGE_SIZE, PAGE_SIZE), PAGE_SIZE)
        return (pltpu.make_async_copy(cc_ref.at[pg], bufc.at[sl, pos], sem.at[0, sl]),
                pltpu.make_async_copy(cr_ref.at[pg], bufr.at[sl, :, pos], sem.at[1, sl]))

    def issue(bb, sl):
        for p in range(n_pages):
            for c in page_copies(bb, sl, p):
                c.start(priority=p % 2)

    def wait(sl):
        pltpu.make_async_copy(bufc.at[sl], bufc.at[sl], sem.at[0, sl]).wait()
        pltpu.make_async_copy(bufr.at[sl], bufr.at[sl], sem.at[1, sl]).wait()

    @pl.when(b == 0)
    def _():
        issue(0, 0)

    wait(slot)
    issue(jnp.minimum(b + 1, nb - 1), 1 - slot)

    ql = qlat_ref[0]
    qr = q_ref[0][:, 0:ROPE_DIM]
    cn = ckvn_ref[0]
    rn = krn_ref[0]
    chunk = min(KEY_CHUNK, past)
    scores = []
    for c in range(past // chunk):
        pos = slice(c * chunk, (c + 1) * chunk)
        kc = bufc[slot, pos, :]
        kcb[pos, :] = kc.astype(BF16)
        krb[:, pos] = bufr[slot, :, pos].astype(BF16)
        scores.append(_dot(ql, kc.T.astype(BF16)) + _dot(qr, krb[:, pos]))
    s_new = (jnp.sum(ql.astype(F32) * cn, axis=-1, keepdims=True)
             + jnp.sum(qr.astype(F32) * rn, axis=-1, keepdims=True))
    m = s_new
    for s in scores:
        m = jnp.maximum(m, jnp.max(s, axis=-1, keepdims=True))
    p_new = jnp.exp2(s_new - m)
    l = p_new
    o = p_new * cn
    for c, s in enumerate(scores):
        p = jnp.exp2(s - m)
        l = l + jnp.sum(p, axis=-1, keepdims=True)
        o = o + _dot(p.astype(BF16), kcb[c * chunk:(c + 1) * chunk, :])
    olat_ref[0] = o / l

    @pl.when(b == nb - 1)
    def _():
        wait(1 - slot)


def _sample_attention(page_table, qlat, q, ckvn, krn, cache_c, cache_rt):
    n, n_pages = page_table.shape
    past = n_pages * PAGE_SIZE
    blk = lambda d1, d2: pl.BlockSpec((1, d1, d2), lambda b, pt: (b, 0, 0))
    grid_spec = pltpu.PrefetchScalarGridSpec(
        num_scalar_prefetch=1,
        grid=(n,),
        in_specs=[blk(N_HEADS, KV_LORA), blk(N_HEADS, HEAD_W), blk(1, KV_LORA), blk(1, ROPE_DIM),
                  pl.BlockSpec(memory_space=pl.ANY), pl.BlockSpec(memory_space=pl.ANY)],
        out_specs=blk(N_HEADS, KV_LORA),
        scratch_shapes=[pltpu.VMEM((2, past, KV_LORA), F32), pltpu.VMEM((2, ROPE_DIM, past), F32),
                        pltpu.VMEM((past, KV_LORA), BF16), pltpu.VMEM((ROPE_DIM, past), BF16),
                        pltpu.SemaphoreType.DMA((2, 2))],
    )
    return pl.pallas_call(
        functools.partial(_sample_attention_body, n_pages),
        grid_spec=grid_spec,
        out_shape=jax.ShapeDtypeStruct((n, N_HEADS, KV_LORA), F32),
        compiler_params=_cparams("arbitrary"),
    )(page_table, qlat.reshape(n, N_HEADS, KV_LORA), q.reshape(n, N_HEADS, HEAD_W),
      ckvn.reshape(n, 1, KV_LORA), krn.reshape(n, 1, ROPE_DIM), cache_c, cache_rt)


def _route(logits):
    lane_i = lax.broadcasted_iota(jnp.int32, (logits.shape[0], LANES), 1)
    lane = lane_i.astype(F32)
    first_at = lambda hit: jnp.min(jnp.where(hit, lane, float(LANES)), axis=-1, keepdims=True)
    lg = jnp.where(lane_i < N_GROUPS, logits[:, 0:LANES], NEG)
    mg = jnp.max(lg, axis=-1, keepdims=True)
    p_grp = 1.0 / jnp.sum(jnp.exp(lg - mg), axis=-1, keepdims=True)
    grp = first_at(lg == mg)
    lane_grp = (lane_i // EXP_PER_GROUP).astype(F32)
    le = jnp.where(lane_grp == grp, logits[:, LANES:2 * LANES], NEG)
    top1 = jnp.max(le, axis=-1, keepdims=True)
    i1 = first_at(le == top1)
    le2 = jnp.where(lane == i1, NEG, le)
    top2 = jnp.max(le2, axis=-1, keepdims=True)
    i2 = first_at(le2 == top2)
    e2 = jnp.exp(top2 - top1)
    w1 = p_grp / (1.0 + e2)
    w2 = p_grp * e2 / (1.0 + e2)
    out = jnp.where(lane_i == R_EID, i1, 0.0)
    out = jnp.where(lane_i == R_EID + 1, i2, out)
    out = jnp.where(lane_i == R_WT, w1, out)
    return jnp.where(lane_i == R_WT + 1, w2, out)


def _add_ranks(route, ltri, count_ref):
    n = route.shape[0]
    lane_i = lax.broadcasted_iota(jnp.int32, (n, LANES), 1)
    lane = lane_i.astype(F32)
    hit1 = lane == route[:, R_EID:R_EID + 1]
    hit2 = lane == route[:, R_EID + 1:R_EID + 2]
    chosen = jnp.where(hit1, 1.0, jnp.where(hit2, 1.0, 0.0))
    before = _dot(ltri, chosen.astype(BF16)) + count_ref[...]
    r1 = jnp.sum(jnp.where(hit1, before, 0.0), axis=-1, keepdims=True)
    r2 = jnp.sum(jnp.where(hit2, before, 0.0), axis=-1, keepdims=True)
    count_ref[...] = count_ref[...] + jnp.sum(chosen, axis=0, keepdims=True)
    route = jnp.where(lane_i == R_RANK, r1, route)
    return jnp.where(lane_i == R_RANK + 1, r2, route)


def _merge_rows(alpha, x, bconv, y_attn, wg_ref, wbc_ref, wo_ref, lg_ref, lb_ref, wr_ref, br_ref):
    g = _dot(x.astype(BF16), wg_ref[...])
    y_conv = _dot(bconv, wbc_ref[...])
    m = jax.nn.sigmoid(g[:, 0:D_MODEL]) * y_conv + jax.nn.sigmoid(g[:, D_MODEL:]) * y_attn
    mix = _dot(m.astype(BF16), wo_ref[...])
    h1 = _layernorm(alpha * x + mix, lg_ref[...], lb_ref[...])
    return h1, _route(_dot(h1.astype(BF16), wr_ref[...]) + br_ref[...])


def _merge_sample_body(alpha, x_ref, bconv_ref, olat_ref, wbd_ref, wba_ref, *refs):
    w_refs, (h1_ref, route_ref) = refs[:-2], refs[-2:]
    o = _dot(olat_ref[...].astype(BF16), wbd_ref[...]).astype(BF16)
    h1_ref[...], route_ref[...] = _merge_rows(alpha, x_ref[...], bconv_ref[...], _dot(o, wba_ref[...]), *w_refs)


def _merge_prompt_body(alpha, n_tiles, x_ref, bconv_ref, ot_ref, h1s_ref, routes_ref, wba_ref, *refs):
    w_refs, (h1t_ref, route_ref, routet_ref, counts_ref, ltri_ref, count_ref) = refs[:-6], refs[-6:]
    i = pl.program_id(0)
    tm = x_ref.shape[0]

    @pl.when(i == 0)
    def _():
        r = lax.broadcasted_iota(jnp.int32, (tm, tm), 0)
        c = lax.broadcasted_iota(jnp.int32, (tm, tm), 1)
        ltri_ref[...] = jnp.where(c < r, 1.0, 0.0).astype(BF16)
        count_ref[...] = jnp.zeros(count_ref.shape, F32)

    @pl.when(i < n_tiles)
    def _():
        y_attn = _dot_tn(ot_ref[...], wba_ref[...])
        h1, route = _merge_rows(alpha, x_ref[...], bconv_ref[...], y_attn, *w_refs)
        for c in range(ROW_TILE):
            _tile_rows(h1t_ref, c, tm)[...] = h1[:, c * LANES:(c + 1) * LANES]
        route_ref[...] = _add_ranks(route, ltri_ref[...], count_ref)

    @pl.when(i == n_tiles)
    def _():
        n_s = h1s_ref.shape[0]
        h1t_ref[...] = jnp.zeros(h1t_ref.shape, F32)
        route_ref[...] = jnp.zeros(route_ref.shape, F32)
        for c in range(ROW_TILE):
            _tile_rows(h1t_ref, c, n_s)[...] = h1s_ref[:, c * LANES:(c + 1) * LANES]
        route_ref[0:n_s, :] = _add_ranks(routes_ref[...], ltri_ref[0:n_s, 0:n_s], count_ref)

    counts_ref[...] = jnp.broadcast_to(count_ref[...], counts_ref.shape)
    routet_ref[...] = route_ref[...].T[0:SUBLANES, :]


def _merge_sample(alpha, x, bconv, olat, wbd, wba, ws):
    n = x.shape[0]
    args = (x, bconv, olat, wbd, wba) + tuple(ws)
    out_shape = [jax.ShapeDtypeStruct((n, D_MODEL), F32), jax.ShapeDtypeStruct((n, LANES), F32)]
    return pl.pallas_call(
        functools.partial(_merge_sample_body, alpha),
        grid=(1,),
        in_specs=[_full(a.shape) for a in args],
        out_specs=[_full(s.shape) for s in out_shape],
        out_shape=out_shape,
        compiler_params=_cparams("arbitrary"),
    )(*args)


def _merge_prompt(alpha, x, bconv, ot, h1_s, route_s, wba, ws):
    t = x.shape[0]
    tm = TM
    n_tiles = t // tm
    t_all = t + h1_s.shape[0]
    assert h1_s.shape[0] <= tm
    clamp = lambda i: jnp.minimum(i, n_tiles - 1)
    rows_in = lambda w: pl.BlockSpec((tm, w), lambda i: (clamp(i), 0))
    return pl.pallas_call(
        functools.partial(_merge_prompt_body, alpha, n_tiles),
        grid=(n_tiles + 1,),
        in_specs=[rows_in(D_MODEL), rows_in(D_CONV), pl.BlockSpec((V_W, tm), lambda i: (0, clamp(i))),
                  _full(h1_s.shape), _full(route_s.shape), _full(wba.shape)] + [_full(w.shape) for w in ws],
        out_specs=[pl.BlockSpec((tm * ROW_TILE, LANES), lambda i: (i, 0)), pl.BlockSpec((tm, LANES), lambda i: (i, 0)),
                   pl.BlockSpec((SUBLANES, tm), lambda i: (0, i)), _full((SUBLANES, LANES))],
        out_shape=[jax.ShapeDtypeStruct((t_all * ROW_TILE, LANES), F32), jax.ShapeDtypeStruct((t_all, LANES), F32),
                   jax.ShapeDtypeStruct((SUBLANES, t_all), F32), jax.ShapeDtypeStruct((SUBLANES, LANES), F32)],
        scratch_shapes=[pltpu.VMEM((tm, tm), BF16), pltpu.VMEM((1, LANES), F32)],
        compiler_params=_cparams("arbitrary"),
    )(x, bconv, ot, h1_s, route_s, wba, *ws)


SLOT_GROUP = 16


def _slot_tokens_body(rows_ref, dest_ref, tok_ref):
    def clear_row(j, c):
        for l in range(LANES):
            tok_ref[j * LANES + l] = 0
        return c
    lax.fori_loop(0, rows_ref[1], clear_row, 0)

    rows_k = rows_ref[0]
    for k in range(TOP_K):
        def fill_row(j, c):
            for g in range(0, LANES, SLOT_GROUP):
                ds = [dest_ref[k * rows_k + j, g + l] for l in range(SLOT_GROUP)]
                for l, d in enumerate(ds):
                    tok_ref[d] = j * LANES + g + l
            return c
        lax.fori_loop(0, rows_k, fill_row, 0)


def _slot_tokens(dest, n_blocks):
    n_slots = n_blocks * MOE_BLOCK
    n_tok = dest.shape[1]
    assert n_slots % LANES == 0 and n_tok % LANES == 0
    dest = dest.reshape(-1, LANES)
    grid_spec = pltpu.PrefetchScalarGridSpec(
        num_scalar_prefetch=2,
        grid=(1,),
        in_specs=[],
        out_specs=pl.BlockSpec(memory_space=pltpu.SMEM),
    )
    return pl.pallas_call(
        _slot_tokens_body,
        grid_spec=grid_spec,
        out_shape=jax.ShapeDtypeStruct((n_slots,), jnp.int32),
        compiler_params=_cparams("arbitrary"),
    )(jnp.array([n_tok // LANES, n_slots // LANES], jnp.int32), dest)


def _row_tile(i):
    return pl.ds(pl.multiple_of(i * ROW_TILE, ROW_TILE), ROW_TILE)


def _experts_body(blk_e_ref, used_ref, nvalid_ref, tok_ref, h1t_ref, wg_ref, wu_ref, wd_ref, ys_ref,
                  xbuf, xb, wgb, wub, wdb, sem_in):
    b = pl.program_id(0)
    ring = lambda bb: lax.rem(bb, GATHER_RING)
    used = used_ref[0]
    blk_rows = MOE_BLOCK * ROW_TILE
    changed = jnp.logical_or(b == 0, blk_e_ref[b] != blk_e_ref[jnp.maximum(b - 1, 0)])
    last = used - 1
    clamp = lambda v: jnp.minimum(v, last)

    def in_copy(bb, sl, r):
        tok = tok_ref[bb * MOE_BLOCK + r]
        return pltpu.make_async_copy(h1t_ref.at[_row_tile(tok)], xbuf.at[sl, _row_tile(r)], sem_in.at[sl])

    def issue_full(v):
        bits = MOE_BLOCK.bit_length() - 1
        for i in range(MOE_BLOCK):
            r = int(format(i, f"0{bits}b")[::-1], 2)
            in_copy(clamp(v), ring(v), r).start(priority=1)

    def issue_partial(v):
        lax.fori_loop(0, nvalid_ref[clamp(v)], lambda r, c: (in_copy(clamp(v), ring(v), r).start(), c)[1], 0)

    def is_full(v):
        return nvalid_ref[clamp(v)] == MOE_BLOCK

    def wait_in(v):
        sl = ring(v)

        @pl.when(is_full(v))
        def _():
            pltpu.make_async_copy(h1t_ref.at[pl.ds(0, blk_rows)], xbuf.at[sl], sem_in.at[sl]).wait()

        @pl.when(jnp.logical_not(is_full(v)))
        def _():
            lax.fori_loop(0, nvalid_ref[clamp(v)], lambda r, c: (in_copy(clamp(v), sl, r).wait(), c)[1], 0)

    @pl.when(jnp.logical_and(b == 0, used > 0))
    def _():
        xbuf[...] = jnp.zeros(xbuf.shape, F32)
        for a in range(GATHER_RING - 1):
            @pl.when(is_full(a))
            def _():
                issue_full(a)

            @pl.when(jnp.logical_not(is_full(a)))
            def _():
                issue_partial(a)

    @pl.when(jnp.logical_and(b < used, changed))
    def _():
        wgb[...] = wg_ref[0].astype(BF16)
        wub[...] = wu_ref[0].astype(BF16)
        wdb[...] = wd_ref[0].astype(BF16)

    ahead = b + GATHER_RING - 1

    def block(issue_ahead):
        for c in range(ROW_TILE):
            xb[:, c * LANES:(c + 1) * LANES] = _tile_rows(xbuf.at[ring(b)], c, MOE_BLOCK)[...].astype(BF16)
        if issue_ahead:
            issue_full(ahead)
        x = xb[...]
        g = _dot(x, wgb[...])
        u = _dot(x, wub[...])
        h = (g * jax.nn.sigmoid(g) * u).astype(BF16)
        y = _dot(h, wdb[...])
        for c in range(ROW_TILE):
            _tile_rows(ys_ref, c, MOE_BLOCK)[...] = y[:, c * LANES:(c + 1) * LANES]

    @pl.when(b < used)
    def _():
        wait_in(b)

    @pl.when(jnp.logical_and(b < used, is_full(ahead)))
    def _():
        block(True)

    @pl.when(jnp.logical_and(b < used, jnp.logical_not(is_full(ahead))))
    def _():
        block(False)
        issue_partial(ahead)

    @pl.when(b == last)
    def _():
        for a in range(1, GATHER_RING):
            wait_in(b + a)

    @pl.when(b >= used)
    def _():
        ys_ref[...] = jnp.zeros(ys_ref.shape, F32)


def _experts(blk_e, used, nvalid, slot_tok, h1t, wg, wu, wd):
    n_blocks = blk_e.shape[0]
    blk_rows = MOE_BLOCK * ROW_TILE
    by_expert = lambda b, e, u, n, a: (e[b], 0, 0)
    grid_spec = pltpu.PrefetchScalarGridSpec(
        num_scalar_prefetch=4,
        grid=(n_blocks,),
        in_specs=[pl.BlockSpec(memory_space=pl.ANY),
                  pl.BlockSpec((1, D_MODEL, D_EXPERT), by_expert), pl.BlockSpec((1, D_MODEL, D_EXPERT), by_expert),
                  pl.BlockSpec((1, D_EXPERT, D_MODEL), by_expert)],
        out_specs=pl.BlockSpec((blk_rows, LANES), lambda b, e, u, n, a: (b, 0)),
        scratch_shapes=[pltpu.VMEM((GATHER_RING, blk_rows, LANES), F32),
                        pltpu.VMEM((MOE_BLOCK, D_MODEL), BF16),
                        pltpu.VMEM((D_MODEL, D_EXPERT), BF16), pltpu.VMEM((D_MODEL, D_EXPERT), BF16),
                        pltpu.VMEM((D_EXPERT, D_MODEL), BF16),
                        pltpu.SemaphoreType.DMA((GATHER_RING,))],
    )
    return pl.pallas_call(
        _experts_body,
        grid_spec=grid_spec,
        out_shape=jax.ShapeDtypeStruct((n_blocks * blk_rows, LANES), F32),
        compiler_params=_cparams("arbitrary"),
    )(blk_e, used, nvalid, slot_tok, h1t, wg, wu, wd)


def _combine_body(alpha, tok0, n_all, dest_ref, ys_ref, h1t_ref, route_ref, lg_ref, lb_ref, out_ref, gbuf, sem):
    i = pl.program_id(0)
    nt = pl.num_programs(0)
    slot = i % 2
    tm = out_ref.shape[0]

    def issue(ii, sl):
        base = tok0 + ii * tm
        for r in range(tm):
            for k in range(TOP_K):
                d = dest_ref[base + k * n_all + r]
                pltpu.make_async_copy(ys_ref.at[_row_tile(d)], gbuf.at[sl, k, _row_tile(r)],
                                      sem.at[sl]).start(priority=k)

    def wait(sl):
        for k in range(TOP_K):
            pltpu.make_async_copy(ys_ref.at[pl.ds(0, tm * ROW_TILE)], gbuf.at[sl, k], sem.at[sl]).wait()

    @pl.when(i == 0)
    def _():
        issue(0, 0)

    wait(slot)
    issue(jnp.minimum(i + 1, nt - 1), 1 - slot)

    route = route_ref[...]
    w0 = jnp.broadcast_to(route[:, R_WT:R_WT + 1], (tm, LANES))
    w1 = jnp.broadcast_to(route[:, R_WT + 1:R_WT + 2], (tm, LANES))
    z = [alpha * _tile_rows(h1t_ref, c, tm)[...]
         + w0 * _tile_rows(gbuf.at[slot, 0], c, tm)[...] + w1 * _tile_rows(gbuf.at[slot, 1], c, tm)[...]
         for c in range(ROW_TILE)]
    mu = sum(jnp.sum(zc, axis=-1, keepdims=True) for zc in z) / D_MODEL
    zc = [v - mu for v in z]
    var = sum(jnp.sum(v * v, axis=-1, keepdims=True) for v in zc) / D_MODEL
    rstd = lax.rsqrt(var + LN_EPS)
    for c in range(ROW_TILE):
        sl = slice(c * LANES, (c + 1) * LANES)
        out_ref[:, sl] = zc[c] * rstd * lg_ref[:, sl] + lb_ref[:, sl]

    @pl.when(i == nt - 1)
    def _():
        wait(1 - slot)


def _combine(alpha, tok0, n_tok, tm, dest, ys, h1t, route, lg, lb):
    assert n_tok % tm == 0 and tok0 % tm == 0
    blk0 = tok0 // tm
    grid_spec = pltpu.PrefetchScalarGridSpec(
        num_scalar_prefetch=1,
        grid=(n_tok // tm,),
        in_specs=[pl.BlockSpec(memory_space=pl.ANY),
                  pl.BlockSpec((tm * ROW_TILE, LANES), lambda i, d: (blk0 + i, 0)),
                  pl.BlockSpec((tm, LANES), lambda i, d: (blk0 + i, 0)),
                  pl.BlockSpec(lg.shape, lambda i, d: (0, 0)), pl.BlockSpec(lb.shape, lambda i, d: (0, 0))],
        out_specs=pl.BlockSpec((tm, D_MODEL), lambda i, d: (i, 0)),
        scratch_shapes=[pltpu.VMEM((2, TOP_K, tm * ROW_TILE, LANES), F32), pltpu.SemaphoreType.DMA((2,))],
    )
    return pl.pallas_call(
        functools.partial(_combine_body, alpha, tok0, route.shape[0]),
        grid_spec=grid_spec,
        out_shape=jax.ShapeDtypeStruct((n_tok, D_MODEL), F32),
        compiler_params=_cparams("arbitrary"),
    )(dest, ys, h1t, route, lg, lb)


def _rope_tables(pos):
    inv = ROPE_BASE ** (-(jnp.arange(ROPE_HALF, dtype=F32) * 2.0 / ROPE_DIM))
    ang = pos.astype(F32)[:, None] * inv[None, :]
    cos, sin = jnp.cos(ang), jnp.sin(ang)
    n = pos.shape[0]
    a = jnp.concatenate([cos, cos, jnp.ones((n, LANES - ROPE_DIM), F32)], axis=1)
    bm = jnp.concatenate([-sin, jnp.zeros((n, LANES - ROPE_HALF), F32)], axis=1)
    cp = jnp.concatenate([jnp.zeros((n, ROPE_HALF), F32), sin, jnp.zeros((n, LANES - ROPE_DIM), F32)], axis=1)
    k_tabs = jnp.stack([a, bm, cp])
    return jnp.concatenate([k_tabs * Q_SCALE, k_tabs]), jnp.stack([cos.T, sin.T]) * Q_SCALE


def _head_blocks(w_rope, w_nope):
    k = w_nope.shape[0]
    pad = jnp.zeros((k, N_HEADS, HEAD_W - QK_DIM), w_nope.dtype)
    return jnp.concatenate([w_rope, w_nope, pad], axis=-1).reshape(k, QK_W)


def _pack_weights(w_in, w_uq, w_uk, w_uv, router_w_group, router_b_group, router_w_expert, router_b_expert):
    d = w_in.shape[0]
    c_kr = 3 * D_CONV + Q_LORA + KV_LORA
    w1 = jnp.concatenate([w_in[:, :c_kr], w_in[:, c_kr:c_kr + ROPE_DIM],
                          jnp.zeros((d, LANES - ROPE_DIM), w_in.dtype)], axis=1).astype(BF16)
    wg = w_in[:, c_kr + ROPE_DIM:].astype(BF16)
    uq = w_uq.reshape(Q_LORA, N_HEADS, QK_DIM)
    wuq = _head_blocks(uq[..., NOPE_DIM:], uq[..., :NOPE_DIM]).astype(BF16)
    wuk = _head_blocks(jnp.zeros((KV_LORA, N_HEADS, ROPE_DIM), w_uk.dtype), w_uk).astype(BF16)
    wukt = jnp.transpose(wuk.reshape(KV_LORA, N_HEADS, HEAD_W), (1, 2, 0))
    wuvt = w_uv.reshape(KV_LORA, V_W).T.astype(BF16)
    eye = jnp.eye(N_HEADS, dtype=w_uv.dtype)
    wbd = jnp.einsum('chd,hg->hcgd', w_uv, eye).reshape(N_HEADS * KV_LORA, V_W).astype(BF16)
    wr = jnp.zeros((d, ROUTE_W), F32)
    wr = wr.at[:, 0:N_GROUPS].set(router_w_group).at[:, LANES:LANES + N_EXPERTS].set(router_w_expert).astype(BF16)
    br = jnp.zeros((1, ROUTE_W), F32)
    br = br.at[0, 0:N_GROUPS].set(router_b_group).at[0, LANES:LANES + N_EXPERTS].set(router_b_expert)
    return w1, wg, wuq, wuq.T, wuk, wukt, wuvt, wbd, wr, br


def _dispatch_plan(route_t, counts, n_blocks):
    counts = counts[0, 0:N_EXPERTS].astype(jnp.int32)
    pcounts = (counts + MOE_BLOCK - 1) // MOE_BLOCK * MOE_BLOCK
    pend = jnp.cumsum(pcounts).astype(jnp.int32)
    pstart = (pend - pcounts).astype(F32)
    eid = route_t[R_EID:R_EID + TOP_K]
    rank = route_t[R_RANK:R_RANK + TOP_K]
    base = jnp.zeros_like(eid)
    for e in range(N_EXPERTS):
        base = jnp.where(eid == float(e), pstart[e], base)
    dest = (base + rank).astype(jnp.int32)
    blk_start = jnp.arange(n_blocks, dtype=jnp.int32) * MOE_BLOCK
    blk_e = jnp.sum((blk_start[:, None] >= pend[None, :]).astype(jnp.int32), axis=1)
    blk_e = jnp.minimum(blk_e, N_EXPERTS - 1).astype(jnp.int32)
    used = (pend[-1:] // MOE_BLOCK).astype(jnp.int32)
    filled_end = (pend - pcounts + counts)[blk_e]
    nvalid = jnp.where(blk_start < pend[-1], jnp.clip(filled_end - blk_start, 0, MOE_BLOCK), 0).astype(jnp.int32)
    return dest, blk_e, used, nvalid


def kernel(x_prompt, x_sample, cache_ckv, cache_krope, state_conv, page_table, w_in, conv_w, q_norm_g, w_uq,
           kv_norm_g, w_uk, w_uv, w_br_conv, w_br_attn, w_o, ln1_g, ln1_b, router_w_group, router_b_group,
           router_w_expert, router_b_expert, w_gate, w_up, w_down, ln2_g, ln2_b):
    depth = w_in.shape[0]
    alpha = (2 * depth) ** 0.25
    n_p, t_p, _ = x_prompt.shape
    n_s, t_s, _ = x_sample.shape
    assert t_s == 1 and t_p % TM == 0 and n_s % LANES == 0
    rows_p = n_p * t_p
    rows_all = rows_p + n_s
    past = page_table.shape[1] * PAGE_SIZE
    tab_p, tabt_p = _rope_tables(jnp.arange(t_p))
    tab_s, _ = _rope_tables(jnp.full((n_s,), past, jnp.int32))
    n_blocks = -(-(rows_all * TOP_K) // MOE_BLOCK) + N_EXPERTS

    h_p = x_prompt.reshape(rows_p, D_MODEL)
    h_s = x_sample.reshape(n_s, D_MODEL)
    ckv_p, kr_p, cv_p, ckv_s, kr_s, cv_s = [], [], [], [], [], []
    row = lambda v: v.reshape(1, -1)
    for l in range(depth):
        w1, wg, wuq, wuqt, wuk, wukt, wuvt, wbd, wr, br = _pack_weights(
            w_in[l], w_uq[l], w_uk[l], w_uv[l], router_w_group[l], router_b_group[l],
            router_w_expert[l], router_b_expert[l])
        qg, kvg = row(q_norm_g[l]), row(kv_norm_g[l])
        wba = w_br_attn[l].astype(BF16)
        merge_w = (wg, w_br_conv[l].astype(BF16), w_o[l].astype(BF16), row(ln1_g[l]), row(ln1_b[l]), wr, br)

        bconv, qt, k, vt, ckv, krt, nconv = _inproj_prompt(h_p, tab_p, tabt_p, w1, wuqt, wuk, wuvt, qg, kvg, conv_w[l],
                                                          n_p, t_p)
        ot = _prompt_attention(qt, k, vt, n_p, t_p)

        st = state_conv[l]
        bconv_s, q_s, qlat, ckvn_s, krn_s, u_s = _inproj_sample(
            h_s, tab_s, w1, wuq, wukt, qg, kvg, conv_w[l], st[:, 0], st[:, 1])
        olat = _sample_attention(page_table, qlat, q_s, ckvn_s, krn_s, cache_ckv[l],
                                 jnp.swapaxes(cache_krope[l], 1, 2))
        h1_s, route_s = _merge_sample(alpha, h_s, bconv_s, olat.reshape(n_s, N_HEADS * KV_LORA), wbd, wba, merge_w)
        h1t, route, route_t, counts = _merge_prompt(alpha, h_p, bconv, ot, h1_s, route_s, wba, merge_w)

        dest, blk_e, used, nvalid = _dispatch_plan(route_t, counts, n_blocks)
        slot_tok = _slot_tokens(dest, n_blocks)
        ys = _experts(blk_e, used, nvalid, slot_tok, h1t, w_gate[l], w_up[l], w_down[l])
        dest = dest.reshape(-1)
        ln2 = (row(ln2_g[l]), row(ln2_b[l]))
        h_p = _combine(alpha, 0, rows_p, TM_COMBINE, dest, ys, h1t, route, *ln2)
        h_s = _combine(alpha, rows_p, n_s, n_s, dest, ys, h1t, route, *ln2)

        ckv_p.append(ckv.reshape(n_p, t_p, KV_LORA))
        kr_p.append(jnp.swapaxes(krt, 1, 2))
        cv_p.append(nconv)
        ckv_s.append(ckvn_s.reshape(n_s, 1, KV_LORA))
        kr_s.append(krn_s.reshape(n_s, 1, ROPE_DIM))
        cv_s.append(jnp.stack([st[:, 1], u_s], axis=1))
    return (h_p.reshape(n_p, t_p, D_MODEL), h_s.reshape(n_s, 1, D_MODEL), jnp.stack(ckv_p), jnp.stack(kr_p),
            jnp.stack(cv_p), jnp.stack(ckv_s), jnp.stack(kr_s), jnp.stack(cv_s))
```

```python
import functools

import jax
import jax.numpy as jnp
from jax import lax
from jax.experimental import pallas as pl
from jax.experimental.pallas import tpu as pltpu

F32 = jnp.float32
BF16 = jnp.bfloat16

D_MODEL = 1024
D_CONV = 512
CONV_W = 3
N_HEADS = 8
Q_LORA = 384
KV_LORA = 256
NOPE_DIM = 64
ROPE_DIM = 32
ROPE_HALF = ROPE_DIM // 2
V_DIM = 64
QK_DIM = NOPE_DIM + ROPE_DIM
ROPE_BASE = 10000.0
ATTN_SCALE = QK_DIM ** -0.5
LOG2E = 1.4426950408889634
Q_SCALE = ATTN_SCALE * LOG2E
N_GROUPS = 4
EXP_PER_GROUP = 8
N_EXPERTS = N_GROUPS * EXP_PER_GROUP
TOP_K = 2
D_EXPERT = 512
MOE_BLOCK = 256
PAGE_SIZE = 128
LN_EPS = 1e-5
RMS_EPS = 1e-6

LANES = 128
SUBLANES = 8
ROW_TILE = D_MODEL // LANES
assert ROW_TILE == SUBLANES
HEAD_W = LANES
QK_W = N_HEADS * HEAD_W
V_W = N_HEADS * V_DIM
_C_BG, _C_CG, _C_H = 0, D_CONV, 2 * D_CONV
_C_CQ = 3 * D_CONV
_C_CKV = _C_CQ + Q_LORA
_C_KR = _C_CKV + KV_LORA
W1_COLS = _C_KR + LANES
ROUTE_W = 2 * LANES
R_EID, R_WT, R_RANK = 0, 2, 4
NEG = -1e30
VMEM_LIMIT = 56 * 1024 * 1024

TM = 512
TM_INPROJ = 1024
KEY_CHUNK = 2048
ATTN_HEADS_PER_LOOP = 8
TQ = 512
TM_COMBINE = 256
GATHER_RING = 4


def _cparams(*sem):
    return pltpu.CompilerParams(dimension_semantics=sem, vmem_limit_bytes=VMEM_LIMIT)


def _dot(a, b):
    return jnp.dot(a, b, preferred_element_type=F32)


def _dot_nt(a, b):
    return lax.dot_general(a, b, (((1,), (1,)), ((), ())), preferred_element_type=F32)


def _dot_tn(a, b):
    return lax.dot_general(a, b, (((0,), (0,)), ((), ())), preferred_element_type=F32)


def _rms(x, g):
    return x * lax.rsqrt(jnp.mean(x * x, axis=-1, keepdims=True) + RMS_EPS) * g


def _layernorm(x, g, b):
    mu = jnp.mean(x, axis=-1, keepdims=True)
    xc = x - mu
    var = jnp.mean(xc * xc, axis=-1, keepdims=True)
    return xc * lax.rsqrt(var + LN_EPS) * g + b


def _rope(xh, a, bm, cp):
    return xh * a + pltpu.roll(xh, LANES - ROPE_HALF, 1) * bm + pltpu.roll(xh, ROPE_HALF, 1) * cp


def _full(shape):
    nd = len(shape)
    return pl.BlockSpec(shape, lambda *_: (0,) * nd)


def _tile_rows(ref, c, n):
    return ref.at[pl.ds(c, n, stride=ROW_TILE), :]


def _inproj_common(x_ref, tab_ref, w1_ref, qg_ref, kvg_ref, ckv_ref):
    xb = x_ref[...].astype(BF16)

    def seg(lo, hi):
        return _dot(xb, w1_ref[:, lo:hi])

    b_g = seg(_C_BG, _C_CG)
    u = seg(_C_CG, _C_H) * seg(_C_H, _C_CQ)
    cqn = _rms(seg(_C_CQ, _C_CKV), qg_ref[...]).astype(BF16)
    ckvn = _rms(seg(_C_CKV, _C_KR), kvg_ref[...])
    ckv_ref[...] = ckvn
    krr = _rope(seg(_C_KR, W1_COLS), tab_ref[3], tab_ref[4], tab_ref[5])
    return b_g, u, cqn, ckvn.astype(BF16), krr


def _inproj_prompt_body(tiles_per_seq, x_ref, tab_ref, tabt_ref, w1_ref, wuqt_ref, wuk_ref, wuvt_ref, qg_ref, kvg_ref,
                        cw_ref, bconv_ref, qt_ref, k_ref, vt_ref, ckv_ref, krt_ref, nconv_ref, uext_ref):
    tm = x_ref.shape[0]
    b_g, u, cqn, cb, krr = _inproj_common(x_ref, tab_ref, w1_ref, qg_ref, kvg_ref, ckv_ref)
    krt_ref[0] = krr.T[0:ROPE_DIM, :]
    qft = _dot_nt(wuqt_ref[...], cqn)
    cos, sin = tabt_ref[0], tabt_ref[1]
    for h in range(N_HEADS):
        r0 = h * HEAD_W
        x1 = qft[r0:r0 + ROPE_HALF, :]
        x2 = qft[r0 + ROPE_HALF:r0 + ROPE_DIM, :]
        qt_ref[r0:r0 + ROPE_HALF, :] = (x1 * cos - x2 * sin).astype(BF16)
        qt_ref[r0 + ROPE_HALF:r0 + ROPE_DIM, :] = (x1 * sin + x2 * cos).astype(BF16)
        qt_ref[r0 + ROPE_DIM:r0 + HEAD_W, :] = (qft[r0 + ROPE_DIM:r0 + HEAD_W, :] * Q_SCALE).astype(BF16)
    kn = _dot(cb, wuk_ref[...])
    for h in range(N_HEADS):
        sl = slice(h * HEAD_W, (h + 1) * HEAD_W)
        k_ref[:, sl] = (kn[:, sl] + krr).astype(BF16)
    vt_ref[...] = _dot_nt(wuvt_ref[...], cb).astype(BF16)

    first = (pl.program_id(0) % tiles_per_seq) == 0

    @pl.when(first)
    def _():
        uext_ref[0:8, :] = jnp.zeros((8, D_CONV), F32)

    @pl.when(jnp.logical_not(first))
    def _():
        uext_ref[0:8, :] = uext_ref[tm:tm + 8, :]

    uext_ref[8:8 + tm, :] = u
    conv = cw_ref[0:1, :] * uext_ref[6:6 + tm, :] + cw_ref[1:2, :] * uext_ref[7:7 + tm, :] + cw_ref[2:3, :] * u
    bconv_ref[...] = (b_g * conv).astype(BF16)
    nconv_ref[0] = u[tm - (CONV_W - 1):tm, :]


def _inproj_sample_body(x_ref, tab_ref, w1_ref, wuq_ref, wukt_ref, qg_ref, kvg_ref, cw_ref, s0_ref, s1_ref,
                        bconv_ref, q_ref, qlat_ref, ckv_ref, kr_ref, u_ref):
    b_g, u, cqn, _, krr = _inproj_common(x_ref, tab_ref, w1_ref, qg_ref, kvg_ref, ckv_ref)
    kr_ref[...] = krr[:, 0:ROPE_DIM]
    qf = _dot(cqn, wuq_ref[...])
    for h in range(N_HEADS):
        sl = slice(h * HEAD_W, (h + 1) * HEAD_W)
        qh = _rope(qf[:, sl], tab_ref[0], tab_ref[1], tab_ref[2]).astype(BF16)
        q_ref[:, sl] = qh
        qlat_ref[:, h * KV_LORA:(h + 1) * KV_LORA] = _dot(qh, wukt_ref[h]).astype(BF16)
    conv = cw_ref[0:1, :] * s0_ref[...] + cw_ref[1:2, :] * s1_ref[...] + cw_ref[2:3, :] * u
    bconv_ref[...] = (b_g * conv).astype(BF16)
    u_ref[...] = u


def _inproj_prompt(x, tab, tabt, w1, wuqt, wuk, wuvt, qg, kvg, cw, n_seq, seq):
    t = x.shape[0]
    tm = TM_INPROJ
    assert seq % tm == 0
    tiles_per_seq = seq // tm
    rows = lambda w: pl.BlockSpec((tm, w), lambda i: (i, 0))
    cols = lambda h: pl.BlockSpec((h, tm), lambda i: (0, i))
    return pl.pallas_call(
        functools.partial(_inproj_prompt_body, tiles_per_seq),
        grid=(t // tm,),
        in_specs=[rows(D_MODEL),
                  pl.BlockSpec((6, tm, LANES), lambda i: (0, i % tiles_per_seq, 0)),
                  pl.BlockSpec((2, ROPE_HALF, tm), lambda i: (0, 0, i % tiles_per_seq)),
                  _full(w1.shape), _full(wuqt.shape), _full(wuk.shape), _full(wuvt.shape),
                  _full(qg.shape), _full(kvg.shape), _full(cw.shape)],
        out_specs=[rows(D_CONV), cols(QK_W), rows(QK_W), cols(V_W), rows(KV_LORA),
                   pl.BlockSpec((1, ROPE_DIM, tm), lambda i: (i // tiles_per_seq, 0, i % tiles_per_seq)),
                   pl.BlockSpec((1, CONV_W - 1, D_CONV), lambda i: (i // tiles_per_seq, 0, 0))],
        out_shape=[jax.ShapeDtypeStruct((t, D_CONV), BF16), jax.ShapeDtypeStruct((QK_W, t), BF16),
                   jax.ShapeDtypeStruct((t, QK_W), BF16), jax.ShapeDtypeStruct((V_W, t), BF16),
                   jax.ShapeDtypeStruct((t, KV_LORA), F32), jax.ShapeDtypeStruct((n_seq, ROPE_DIM, seq), F32),
                   jax.ShapeDtypeStruct((n_seq, CONV_W - 1, D_CONV), F32)],
        scratch_shapes=[pltpu.VMEM((tm + 8, D_CONV), F32)],
        compiler_params=_cparams("arbitrary"),
    )(x, tab, tabt, w1, wuqt, wuk, wuvt, qg, kvg, cw)


def _inproj_sample(x, tab, w1, wuq, wukt, qg, kvg, cw, s0, s1):
    n = x.shape[0]
    args = (x, tab, w1, wuq, wukt, qg, kvg, cw, s0, s1)
    out_shape = [jax.ShapeDtypeStruct((n, D_CONV), BF16), jax.ShapeDtypeStruct((n, QK_W), BF16),
                 jax.ShapeDtypeStruct((n, N_HEADS * KV_LORA), BF16), jax.ShapeDtypeStruct((n, KV_LORA), F32),
                 jax.ShapeDtypeStruct((n, ROPE_DIM), F32), jax.ShapeDtypeStruct((n, D_CONV), F32)]
    return pl.pallas_call(
        _inproj_sample_body,
        grid=(1,),
        in_specs=[_full(a.shape) for a in args],
        out_specs=[_full(s.shape) for s in out_shape],
        out_shape=out_shape,
        compiler_params=_cparams("arbitrary"),
    )(*args)


def _prompt_attention_body(qt_ref, k_ref, vt_ref, ot_ref, acc_ref, m_ref, l_ref, sa_ref, sb_ref):
    tq = qt_ref.shape[1]
    i = pl.program_id(1)

    for hg in range(N_HEADS // ATTN_HEADS_PER_LOOP):
        heads = tuple(range(hg * ATTN_HEADS_PER_LOOP, (hg + 1) * ATTN_HEADS_PER_LOOP))
        acc_ref[...] = jnp.zeros(acc_ref.shape, F32)
        m_ref[...] = jnp.full(m_ref.shape, NEG, F32)
        l_ref[...] = jnp.zeros(l_ref.shape, F32)

        def scores(j, s_ref):
            off = pl.multiple_of(j * tq, tq)
            for idx, h in enumerate(heads):
                s_ref[idx] = _dot(k_ref[pl.ds(off, tq), h * HEAD_W:(h + 1) * HEAD_W],
                                  qt_ref[h * HEAD_W:(h + 1) * HEAD_W, :])

        def consume(j, s_ref, masked):
            off = pl.multiple_of(j * tq, tq)
            for idx, h in enumerate(heads):
                s = s_ref[idx]
                if masked:
                    krow = lax.broadcasted_iota(jnp.int32, (tq, tq), 0)
                    qcol = lax.broadcasted_iota(jnp.int32, (tq, tq), 1)
                    s = jnp.where(krow <= qcol, s, NEG)
                m = m_ref[idx]
                m_new = jnp.maximum(m, jnp.max(s, axis=0, keepdims=True))
                alpha = jnp.exp2(m - m_new)
                p = jnp.exp2(s - m_new)
                m_ref[idx] = m_new
                l_ref[idx] = alpha * l_ref[idx] + jnp.sum(p, axis=0, keepdims=True)
                vblk = vt_ref[h * V_DIM:(h + 1) * V_DIM, pl.ds(off, tq)]
                acc_ref[idx] = alpha * acc_ref[idx] + _dot(vblk, p.astype(BF16))

        scores(0, sa_ref)

        def pair(pp, c):
            j = 2 * pp
            scores(j + 1, sb_ref)
            consume(j, sa_ref, False)
            scores(j + 2, sa_ref)
            consume(j + 1, sb_ref, False)
            return c
        lax.fori_loop(0, i // 2, pair, 0)

        @pl.when(i % 2 == 1)
        def _():
            scores(i, sb_ref)
            consume(i - 1, sa_ref, False)
            consume(i, sb_ref, True)

        @pl.when(i % 2 == 0)
        def _():
            consume(i, sa_ref, True)

        for idx, h in enumerate(heads):
            ot_ref[h * V_DIM:(h + 1) * V_DIM, :] = (acc_ref[idx] / l_ref[idx]).astype(BF16)


def _prompt_attention(qt, k, vt, n_seq, seq):
    t = k.shape[0]
    tq = TQ
    nq = seq // tq
    return pl.pallas_call(
        _prompt_attention_body,
        grid=(n_seq, nq),
        in_specs=[pl.BlockSpec((QK_W, tq), lambda b, i: (0, b * nq + i)),
                  pl.BlockSpec((seq, QK_W), lambda b, i: (b, 0)),
                  pl.BlockSpec((V_W, seq), lambda b, i: (0, b))],
        out_specs=pl.BlockSpec((V_W, tq), lambda b, i: (0, b * nq + i)),
        out_shape=jax.ShapeDtypeStruct((V_W, t), BF16),
        scratch_shapes=[pltpu.VMEM((ATTN_HEADS_PER_LOOP, V_DIM, tq), F32),
                        pltpu.VMEM((ATTN_HEADS_PER_LOOP, 1, tq), F32), pltpu.VMEM((ATTN_HEADS_PER_LOOP, 1, tq), F32),
                        pltpu.VMEM((ATTN_HEADS_PER_LOOP, tq, tq), F32), pltpu.VMEM((ATTN_HEADS_PER_LOOP, tq, tq), F32)],
        compiler_params=_cparams("arbitrary", "arbitrary"),
    )(qt, k, vt)


def _sample_attention_body(n_pages, pt_ref, qlat_ref, q_ref, ckvn_ref, krn_ref, cc_ref, cr_ref, olat_ref,
                           bufc, bufr, kcb, krb, sem):
    b = pl.program_id(0)
    nb = pl.num_programs(0)
    slot = b % 2
    past = n_pages * PAGE_SIZE

    def page_copies(bb, sl, p):
        pg = pt_ref[bb, p]
        pos = pl.ds(pl.multiple_of(p * PAGE_SIZE, PAGE_SIZE), PAGE_SIZE)
        return (pltpu.make_async_copy(cc_ref.at[pg], bufc.at[sl, pos], sem.at[0, sl]),
                pltpu.make_async_copy(cr_ref.at[pg], bufr.at[sl, p], sem.at[1, sl]))

    def issue(bb, sl):
        for p in range(n_pages):
            for c in page_copies(bb, sl, p):
                c.start(priority=p % 2)

    def wait(sl):
        pltpu.make_async_copy(bufc.at[sl], bufc.at[sl], sem.at[0, sl]).wait()
        pltpu.make_async_copy(bufr.at[sl], bufr.at[sl], sem.at[1, sl]).wait()

    @pl.when(b == 0)
    def _():
        issue(0, 0)

    wait(slot)
    issue(jnp.minimum(b + 1, nb - 1), 1 - slot)

    ql = qlat_ref[0]
    qr = q_ref[0][:, 0:ROPE_DIM]
    cn = ckvn_ref[0]
    rn = krn_ref[0]
    chunk = min(KEY_CHUNK, past)
    scores = []
    for c in range(past // chunk):
        pos = slice(c * chunk, (c + 1) * chunk)
        kc = bufc[slot, pos, :]
        kcb[pos, :] = kc.astype(BF16)
        pages = range(c * chunk // PAGE_SIZE, (c + 1) * chunk // PAGE_SIZE)
        krb[:, pos] = jnp.concatenate([bufr[slot, p] for p in pages], axis=1).astype(BF16)
        scores.append(_dot(ql, kc.T.astype(BF16)) + _dot(qr, krb[:, pos]))
    s_new = (jnp.sum(ql.astype(F32) * cn, axis=-1, keepdims=True)
             + jnp.sum(qr.astype(F32) * rn, axis=-1, keepdims=True))
    m = s_new
    for s in scores:
        m = jnp.maximum(m, jnp.max(s, axis=-1, keepdims=True))
    p_new = jnp.exp2(s_new - m)
    l = p_new
    o = p_new * cn
    for c, s in enumerate(scores):
        p = jnp.exp2(s - m)
        l = l + jnp.sum(p, axis=-1, keepdims=True)
        o = o + _dot(p.astype(BF16), kcb[c * chunk:(c + 1) * chunk, :])
    olat_ref[0] = o / l

    @pl.when(b == nb - 1)
    def _():
        wait(1 - slot)


def _sample_attention(page_table, qlat, q, ckvn, krn, cache_c, cache_rt):
    n, n_pages = page_table.shape
    past = n_pages * PAGE_SIZE
    blk = lambda d1, d2: pl.BlockSpec((1, d1, d2), lambda b, pt: (b, 0, 0))
    grid_spec = pltpu.PrefetchScalarGridSpec(
        num_scalar_prefetch=1,
        grid=(n,),
        in_specs=[blk(N_HEADS, KV_LORA), blk(N_HEADS, HEAD_W), blk(1, KV_LORA), blk(1, ROPE_DIM),
                  pl.BlockSpec(memory_space=pl.ANY), pl.BlockSpec(memory_space=pl.ANY)],
        out_specs=blk(N_HEADS, KV_LORA),
        scratch_shapes=[pltpu.VMEM((2, past, KV_LORA), F32), pltpu.VMEM((2, n_pages, ROPE_DIM, PAGE_SIZE), F32),
                        pltpu.VMEM((past, KV_LORA), BF16), pltpu.VMEM((ROPE_DIM, past), BF16),
                        pltpu.SemaphoreType.DMA((2, 2))],
    )
    return pl.pallas_call(
        functools.partial(_sample_attention_body, n_pages),
        grid_spec=grid_spec,
        out_shape=jax.ShapeDtypeStruct((n, N_HEADS, KV_LORA), F32),
        compiler_params=_cparams("arbitrary"),
    )(page_table, qlat.reshape(n, N_HEADS, KV_LORA), q.reshape(n, N_HEADS, HEAD_W),
      ckvn.reshape(n, 1, KV_LORA), krn.reshape(n, 1, ROPE_DIM), cache_c, cache_rt)


def _route(logits):
    lane_i = lax.broadcasted_iota(jnp.int32, (logits.shape[0], LANES), 1)
    lane = lane_i.astype(F32)
    first_at = lambda hit: jnp.min(jnp.where(hit, lane, float(LANES)), axis=-1, keepdims=True)
    lg = jnp.where(lane_i < N_GROUPS, logits[:, 0:LANES], NEG)
    mg = jnp.max(lg, axis=-1, keepdims=True)
    p_grp = 1.0 / jnp.sum(jnp.exp(lg - mg), axis=-1, keepdims=True)
    grp = first_at(lg == mg)
    lane_grp = (lane_i // EXP_PER_GROUP).astype(F32)
    le = jnp.where(lane_grp == grp, logits[:, LANES:2 * LANES], NEG)
    top1 = jnp.max(le, axis=-1, keepdims=True)
    i1 = first_at(le == top1)
    le2 = jnp.where(lane == i1, NEG, le)
    top2 = jnp.max(le2, axis=-1, keepdims=True)
    i2 = first_at(le2 == top2)
    e2 = jnp.exp(top2 - top1)
    w1 = p_grp / (1.0 + e2)
    w2 = p_grp * e2 / (1.0 + e2)
    out = jnp.where(lane_i == R_EID, i1, 0.0)
    out = jnp.where(lane_i == R_EID + 1, i2, out)
    out = jnp.where(lane_i == R_WT, w1, out)
    return jnp.where(lane_i == R_WT + 1, w2, out)


def _add_ranks(route, ltri, count_ref):
    n = route.shape[0]
    lane_i = lax.broadcasted_iota(jnp.int32, (n, LANES), 1)
    lane = lane_i.astype(F32)
    hit1 = lane == route[:, R_EID:R_EID + 1]
    hit2 = lane == route[:, R_EID + 1:R_EID + 2]
    chosen = jnp.where(hit1, 1.0, jnp.where(hit2, 1.0, 0.0))
    before = _dot(ltri, chosen.astype(BF16)) + count_ref[...]
    r1 = jnp.sum(jnp.where(hit1, before, 0.0), axis=-1, keepdims=True)
    r2 = jnp.sum(jnp.where(hit2, before, 0.0), axis=-1, keepdims=True)
    count_ref[...] = count_ref[...] + jnp.sum(chosen, axis=0, keepdims=True)
    route = jnp.where(lane_i == R_RANK, r1, route)
    return jnp.where(lane_i == R_RANK + 1, r2, route)


def _merge_rows(alpha, x, bconv, y_attn, wg_ref, wbc_ref, wo_ref, lg_ref, lb_ref, wr_ref, br_ref):
    g = _dot(x.astype(BF16), wg_ref[...])
    y_conv = _dot(bconv, wbc_ref[...])
    m = jax.nn.sigmoid(g[:, 0:D_MODEL]) * y_conv + jax.nn.sigmoid(g[:, D_MODEL:]) * y_attn
    mix = _dot(m.astype(BF16), wo_ref[...])
    h1 = _layernorm(alpha * x + mix, lg_ref[...], lb_ref[...])
    return h1, _route(_dot(h1.astype(BF16), wr_ref[...]) + br_ref[...])


def _merge_sample_body(alpha, x_ref, bconv_ref, olat_ref, wbd_ref, wba_ref, *refs):
    w_refs, (h1_ref, route_ref) = refs[:-2], refs[-2:]
    o = _dot(olat_ref[...].astype(BF16), wbd_ref[...]).astype(BF16)
    h1_ref[...], route_ref[...] = _merge_rows(alpha, x_ref[...], bconv_ref[...], _dot(o, wba_ref[...]), *w_refs)


def _merge_prompt_body(alpha, n_tiles, x_ref, bconv_ref, ot_ref, h1s_ref, routes_ref, wba_ref, *refs):
    w_refs, (h1t_ref, route_ref, routet_ref, counts_ref, ltri_ref, count_ref) = refs[:-6], refs[-6:]
    i = pl.program_id(0)
    tm = x_ref.shape[0]

    @pl.when(i == 0)
    def _():
        r = lax.broadcasted_iota(jnp.int32, (tm, tm), 0)
        c = lax.broadcasted_iota(jnp.int32, (tm, tm), 1)
        ltri_ref[...] = jnp.where(c < r, 1.0, 0.0).astype(BF16)
        count_ref[...] = jnp.zeros(count_ref.shape, F32)

    @pl.when(i < n_tiles)
    def _():
        y_attn = _dot_tn(ot_ref[...], wba_ref[...])
        h1, route = _merge_rows(alpha, x_ref[...], bconv_ref[...], y_attn, *w_refs)
        for c in range(ROW_TILE):
            _tile_rows(h1t_ref, c, tm)[...] = h1[:, c * LANES:(c + 1) * LANES]
        route_ref[...] = _add_ranks(route, ltri_ref[...], count_ref)

    @pl.when(i == n_tiles)
    def _():
        n_s = h1s_ref.shape[0]
        h1t_ref[...] = jnp.zeros(h1t_ref.shape, F32)
        route_ref[...] = jnp.zeros(route_ref.shape, F32)
        for c in range(ROW_TILE):
            _tile_rows(h1t_ref, c, n_s)[...] = h1s_ref[:, c * LANES:(c + 1) * LANES]
        route_ref[0:n_s, :] = _add_ranks(routes_ref[...], ltri_ref[0:n_s, 0:n_s], count_ref)

    counts_ref[...] = jnp.broadcast_to(count_ref[...], counts_ref.shape)
    routet_ref[...] = route_ref[...].T[0:SUBLANES, :]


def _merge_sample(alpha, x, bconv, olat, wbd, wba, ws):
    n = x.shape[0]
    args = (x, bconv, olat, wbd, wba) + tuple(ws)
    out_shape = [jax.ShapeDtypeStruct((n, D_MODEL), F32), jax.ShapeDtypeStruct((n, LANES), F32)]
    return pl.pallas_call(
        functools.partial(_merge_sample_body, alpha),
        grid=(1,),
        in_specs=[_full(a.shape) for a in args],
        out_specs=[_full(s.shape) for s in out_shape],
        out_shape=out_shape,
        compiler_params=_cparams("arbitrary"),
    )(*args)


def _merge_prompt(alpha, x, bconv, ot, h1_s, route_s, wba, ws):
    t = x.shape[0]
    tm = TM
    n_tiles = t // tm
    t_all = t + h1_s.shape[0]
    assert h1_s.shape[0] <= tm
    clamp = lambda i: jnp.minimum(i, n_tiles - 1)
    rows_in = lambda w: pl.BlockSpec((tm, w), lambda i: (clamp(i), 0))
    return pl.pallas_call(
        functools.partial(_merge_prompt_body, alpha, n_tiles),
        grid=(n_tiles + 1,),
        in_specs=[rows_in(D_MODEL), rows_in(D_CONV), pl.BlockSpec((V_W, tm), lambda i: (0, clamp(i))),
                  _full(h1_s.shape), _full(route_s.shape), _full(wba.shape)] + [_full(w.shape) for w in ws],
        out_specs=[pl.BlockSpec((tm * ROW_TILE, LANES), lambda i: (i, 0)), pl.BlockSpec((tm, LANES), lambda i: (i, 0)),
                   pl.BlockSpec((SUBLANES, tm), lambda i: (0, i)), _full((SUBLANES, LANES))],
        out_shape=[jax.ShapeDtypeStruct((t_all * ROW_TILE, LANES), F32), jax.ShapeDtypeStruct((t_all, LANES), F32),
                   jax.ShapeDtypeStruct((SUBLANES, t_all), F32), jax.ShapeDtypeStruct((SUBLANES, LANES), F32)],
        scratch_shapes=[pltpu.VMEM((tm, tm), BF16), pltpu.VMEM((1, LANES), F32)],
        compiler_params=_cparams("arbitrary"),
    )(x, bconv, ot, h1_s, route_s, wba, *ws)


SLOT_GROUP = 16


def _slot_tokens_body(rows_ref, dest_ref, tok_ref):
    def clear_row(j, c):
        for l in range(LANES):
            tok_ref[j * LANES + l] = 0
        return c
    lax.fori_loop(0, rows_ref[1], clear_row, 0)

    rows_k = rows_ref[0]
    for k in range(TOP_K):
        def fill_row(j, c):
            for g in range(0, LANES, SLOT_GROUP):
                ds = [dest_ref[k * rows_k + j, g + l] for l in range(SLOT_GROUP)]
                for l, d in enumerate(ds):
                    tok_ref[d] = j * LANES + g + l
            return c
        lax.fori_loop(0, rows_k, fill_row, 0)


def _slot_tokens(dest, n_blocks):
    n_slots = n_blocks * MOE_BLOCK
    n_tok = dest.shape[1]
    assert n_slots % LANES == 0 and n_tok % LANES == 0
    dest = dest.reshape(-1, LANES)
    grid_spec = pltpu.PrefetchScalarGridSpec(
        num_scalar_prefetch=2,
        grid=(1,),
        in_specs=[],
        out_specs=pl.BlockSpec(memory_space=pltpu.SMEM),
    )
    return pl.pallas_call(
        _slot_tokens_body,
        grid_spec=grid_spec,
        out_shape=jax.ShapeDtypeStruct((n_slots,), jnp.int32),
        compiler_params=_cparams("arbitrary"),
    )(jnp.array([n_tok // LANES, n_slots // LANES], jnp.int32), dest)


def _row_tile(i):
    return pl.ds(pl.multiple_of(i * ROW_TILE, ROW_TILE), ROW_TILE)


def _experts_body(blk_e_ref, used_ref, nvalid_ref, tok_ref, h1t_ref, wg_ref, wu_ref, wd_ref, ys_ref,
                  xbuf, xb, wgb, wub, wdb, sem_in):
    b = pl.program_id(0)
    ring = lambda bb: lax.rem(bb, GATHER_RING)
    used = used_ref[0]
    blk_rows = MOE_BLOCK * ROW_TILE
    changed = jnp.logical_or(b == 0, blk_e_ref[b] != blk_e_ref[jnp.maximum(b - 1, 0)])
    last = used - 1
    clamp = lambda v: jnp.minimum(v, last)

    def in_copy(bb, sl, r):
        tok = tok_ref[bb * MOE_BLOCK + r]
        return pltpu.make_async_copy(h1t_ref.at[_row_tile(tok)], xbuf.at[sl, _row_tile(r)], sem_in.at[sl])

    def issue_full(v):
        bits = MOE_BLOCK.bit_length() - 1
        for i in range(MOE_BLOCK):
            r = int(format(i, f"0{bits}b")[::-1], 2)
            in_copy(clamp(v), ring(v), r).start(priority=1)

    def issue_partial(v):
        lax.fori_loop(0, nvalid_ref[clamp(v)], lambda r, c: (in_copy(clamp(v), ring(v), r).start(), c)[1], 0)

    def is_full(v):
        return nvalid_ref[clamp(v)] == MOE_BLOCK

    def wait_in(v):
        sl = ring(v)

        @pl.when(is_full(v))
        def _():
            pltpu.make_async_copy(h1t_ref.at[pl.ds(0, blk_rows)], xbuf.at[sl], sem_in.at[sl]).wait()

        @pl.when(jnp.logical_not(is_full(v)))
        def _():
            lax.fori_loop(0, nvalid_ref[clamp(v)], lambda r, c: (in_copy(clamp(v), sl, r).wait(), c)[1], 0)

    @pl.when(jnp.logical_and(b == 0, used > 0))
    def _():
        xbuf[...] = jnp.zeros(xbuf.shape, F32)
        for a in range(GATHER_RING - 1):
            @pl.when(is_full(a))
            def _():
                issue_full(a)

            @pl.when(jnp.logical_not(is_full(a)))
            def _():
                issue_partial(a)

    @pl.when(jnp.logical_and(b < used, changed))
    def _():
        wgb[...] = wg_ref[0].astype(BF16)
        wub[...] = wu_ref[0].astype(BF16)
        wdb[...] = wd_ref[0].astype(BF16)

    ahead = b + GATHER_RING - 1

    def block(issue_ahead):
        for c in range(ROW_TILE):
            xb[:, c * LANES:(c + 1) * LANES] = _tile_rows(xbuf.at[ring(b)], c, MOE_BLOCK)[...].astype(BF16)
        if issue_ahead:
            issue_full(ahead)
        x = xb[...]
        g = _dot(x, wgb[...])
        u = _dot(x, wub[...])
        h = (g * jax.nn.sigmoid(g) * u).astype(BF16)
        y = _dot(h, wdb[...])
        for c in range(ROW_TILE):
            _tile_rows(ys_ref, c, MOE_BLOCK)[...] = y[:, c * LANES:(c + 1) * LANES]

    @pl.when(b < used)
    def _():
        wait_in(b)

    @pl.when(jnp.logical_and(b < used, is_full(ahead)))
    def _():
        block(True)

    @pl.when(jnp.logical_and(b < used, jnp.logical_not(is_full(ahead))))
    def _():
        block(False)
        issue_partial(ahead)

    @pl.when(b == last)
    def _():
        for a in range(1, GATHER_RING):
            wait_in(b + a)

    @pl.when(b >= used)
    def _():
        ys_ref[...] = jnp.zeros(ys_ref.shape, F32)


def _experts(blk_e, used, nvalid, slot_tok, h1t, wg, wu, wd):
    n_blocks = blk_e.shape[0]
    blk_rows = MOE_BLOCK * ROW_TILE
    by_expert = lambda b, e, u, n, a: (e[b], 0, 0)
    grid_spec = pltpu.PrefetchScalarGridSpec(
        num_scalar_prefetch=4,
        grid=(n_blocks,),
        in_specs=[pl.BlockSpec(memory_space=pl.ANY),
                  pl.BlockSpec((1, D_MODEL, D_EXPERT), by_expert), pl.BlockSpec((1, D_MODEL, D_EXPERT), by_expert),
                  pl.BlockSpec((1, D_EXPERT, D_MODEL), by_expert)],
        out_specs=pl.BlockSpec((blk_rows, LANES), lambda b, e, u, n, a: (b, 0)),
        scratch_shapes=[pltpu.VMEM((GATHER_RING, blk_rows, LANES), F32),
                        pltpu.VMEM((MOE_BLOCK, D_MODEL), BF16),
                        pltpu.VMEM((D_MODEL, D_EXPERT), BF16), pltpu.VMEM((D_MODEL, D_EXPERT), BF16),
                        pltpu.VMEM((D_EXPERT, D_MODEL), BF16),
                        pltpu.SemaphoreType.DMA((GATHER_RING,))],
    )
    return pl.pallas_call(
        _experts_body,
        grid_spec=grid_spec,
        out_shape=jax.ShapeDtypeStruct((n_blocks * blk_rows, LANES), F32),
        compiler_params=_cparams("arbitrary"),
    )(blk_e, used, nvalid, slot_tok, h1t, wg, wu, wd)


def _combine_body(alpha, tok0, n_all, dest_ref, ys_ref, h1t_ref, route_ref, lg_ref, lb_ref, out_ref, gbuf, sem):
    i = pl.program_id(0)
    nt = pl.num_programs(0)
    slot = i % 2
    tm = out_ref.shape[0]

    def issue(ii, sl):
        base = tok0 + ii * tm
        for r in range(tm):
            for k in range(TOP_K):
                d = dest_ref[base + k * n_all + r]
                pltpu.make_async_copy(ys_ref.at[_row_tile(d)], gbuf.at[sl, k, _row_tile(r)],
                                      sem.at[sl]).start(priority=k)

    def wait(sl):
        for k in range(TOP_K):
            pltpu.make_async_copy(ys_ref.at[pl.ds(0, tm * ROW_TILE)], gbuf.at[sl, k], sem.at[sl]).wait()

    @pl.when(i == 0)
    def _():
        issue(0, 0)

    wait(slot)
    issue(jnp.minimum(i + 1, nt - 1), 1 - slot)

    route = route_ref[...]
    w0 = jnp.broadcast_to(route[:, R_WT:R_WT + 1], (tm, LANES))
    w1 = jnp.broadcast_to(route[:, R_WT + 1:R_WT + 2], (tm, LANES))
    z = [alpha * _tile_rows(h1t_ref, c, tm)[...]
         + w0 * _tile_rows(gbuf.at[slot, 0], c, tm)[...] + w1 * _tile_rows(gbuf.at[slot, 1], c, tm)[...]
         for c in range(ROW_TILE)]
    mu = sum(jnp.sum(zc, axis=-1, keepdims=True) for zc in z) / D_MODEL
    zc = [v - mu for v in z]
    var = sum(jnp.sum(v * v, axis=-1, keepdims=True) for v in zc) / D_MODEL
    rstd = lax.rsqrt(var + LN_EPS)
    for c in range(ROW_TILE):
        sl = slice(c * LANES, (c + 1) * LANES)
        out_ref[:, sl] = zc[c] * rstd * lg_ref[:, sl] + lb_ref[:, sl]

    @pl.when(i == nt - 1)
    def _():
        wait(1 - slot)


def _combine(alpha, tok0, n_tok, tm, dest, ys, h1t, route, lg, lb):
    assert n_tok % tm == 0 and tok0 % tm == 0
    blk0 = tok0 // tm
    grid_spec = pltpu.PrefetchScalarGridSpec(
        num_scalar_prefetch=1,
        grid=(n_tok // tm,),
        in_specs=[pl.BlockSpec(memory_space=pl.ANY),
                  pl.BlockSpec((tm * ROW_TILE, LANES), lambda i, d: (blk0 + i, 0)),
                  pl.BlockSpec((tm, LANES), lambda i, d: (blk0 + i, 0)),
                  pl.BlockSpec(lg.shape, lambda i, d: (0, 0)), pl.BlockSpec(lb.shape, lambda i, d: (0, 0))],
        out_specs=pl.BlockSpec((tm, D_MODEL), lambda i, d: (i, 0)),
        scratch_shapes=[pltpu.VMEM((2, TOP_K, tm * ROW_TILE, LANES), F32), pltpu.SemaphoreType.DMA((2,))],
    )
    return pl.pallas_call(
        functools.partial(_combine_body, alpha, tok0, route.shape[0]),
        grid_spec=grid_spec,
        out_shape=jax.ShapeDtypeStruct((n_tok, D_MODEL), F32),
        compiler_params=_cparams("arbitrary"),
    )(dest, ys, h1t, route, lg, lb)


def _rope_tables(pos):
    inv = ROPE_BASE ** (-(jnp.arange(ROPE_HALF, dtype=F32) * 2.0 / ROPE_DIM))
    ang = pos.astype(F32)[:, None] * inv[None, :]
    cos, sin = jnp.cos(ang), jnp.sin(ang)
    n = pos.shape[0]
    a = jnp.concatenate([cos, cos, jnp.ones((n, LANES - ROPE_DIM), F32)], axis=1)
    bm = jnp.concatenate([-sin, jnp.zeros((n, LANES - ROPE_HALF), F32)], axis=1)
    cp = jnp.concatenate([jnp.zeros((n, ROPE_HALF), F32), sin, jnp.zeros((n, LANES - ROPE_DIM), F32)], axis=1)
    k_tabs = jnp.stack([a, bm, cp])
    return jnp.concatenate([k_tabs * Q_SCALE, k_tabs]), jnp.stack([cos.T, sin.T]) * Q_SCALE


def _head_blocks(w_rope, w_nope):
    k = w_nope.shape[0]
    pad = jnp.zeros((k, N_HEADS, HEAD_W - QK_DIM), w_nope.dtype)
    return jnp.concatenate([w_rope, w_nope, pad], axis=-1).reshape(k, QK_W)


def _pack_weights(w_in, w_uq, w_uk, w_uv, router_w_group, router_b_group, router_w_expert, router_b_expert):
    d = w_in.shape[0]
    c_kr = 3 * D_CONV + Q_LORA + KV_LORA
    w1 = jnp.concatenate([w_in[:, :c_kr], w_in[:, c_kr:c_kr + ROPE_DIM],
                          jnp.zeros((d, LANES - ROPE_DIM), w_in.dtype)], axis=1).astype(BF16)
    wg = w_in[:, c_kr + ROPE_DIM:].astype(BF16)
    uq = w_uq.reshape(Q_LORA, N_HEADS, QK_DIM)
    wuq = _head_blocks(uq[..., NOPE_DIM:], uq[..., :NOPE_DIM]).astype(BF16)
    wuk = _head_blocks(jnp.zeros((KV_LORA, N_HEADS, ROPE_DIM), w_uk.dtype), w_uk).astype(BF16)
    wukt = jnp.transpose(wuk.reshape(KV_LORA, N_HEADS, HEAD_W), (1, 2, 0))
    wuvt = w_uv.reshape(KV_LORA, V_W).T.astype(BF16)
    eye = jnp.eye(N_HEADS, dtype=w_uv.dtype)
    wbd = jnp.einsum('chd,hg->hcgd', w_uv, eye).reshape(N_HEADS * KV_LORA, V_W).astype(BF16)
    wr = jnp.zeros((d, ROUTE_W), F32)
    wr = wr.at[:, 0:N_GROUPS].set(router_w_group).at[:, LANES:LANES + N_EXPERTS].set(router_w_expert).astype(BF16)
    br = jnp.zeros((1, ROUTE_W), F32)
    br = br.at[0, 0:N_GROUPS].set(router_b_group).at[0, LANES:LANES + N_EXPERTS].set(router_b_expert)
    return w1, wg, wuq, wuq.T, wuk, wukt, wuvt, wbd, wr, br


def _dispatch_plan(route_t, counts, n_blocks):
    counts = counts[0, 0:N_EXPERTS].astype(jnp.int32)
    pcounts = (counts + MOE_BLOCK - 1) // MOE_BLOCK * MOE_BLOCK
    pend = jnp.cumsum(pcounts).astype(jnp.int32)
    pstart = (pend - pcounts).astype(F32)
    eid = route_t[R_EID:R_EID + TOP_K]
    rank = route_t[R_RANK:R_RANK + TOP_K]
    base = jnp.zeros_like(eid)
    for e in range(N_EXPERTS):
        base = jnp.where(eid == float(e), pstart[e], base)
    dest = (base + rank).astype(jnp.int32)
    blk_start = jnp.arange(n_blocks, dtype=jnp.int32) * MOE_BLOCK
    blk_e = jnp.sum((blk_start[:, None] >= pend[None, :]).astype(jnp.int32), axis=1)
    blk_e = jnp.minimum(blk_e, N_EXPERTS - 1).astype(jnp.int32)
    used = (pend[-1:] // MOE_BLOCK).astype(jnp.int32)
    filled_end = (pend - pcounts + counts)[blk_e]
    nvalid = jnp.where(blk_start < pend[-1], jnp.clip(filled_end - blk_start, 0, MOE_BLOCK), 0).astype(jnp.int32)
    return dest, blk_e, used, nvalid


def kernel(x_prompt, x_sample, cache_ckv, cache_krope, state_conv, page_table, w_in, conv_w, q_norm_g, w_uq,
           kv_norm_g, w_uk, w_uv, w_br_conv, w_br_attn, w_o, ln1_g, ln1_b, router_w_group, router_b_group,
           router_w_expert, router_b_expert, w_gate, w_up, w_down, ln2_g, ln2_b):
    depth = w_in.shape[0]
    alpha = (2 * depth) ** 0.25
    n_p, t_p, _ = x_prompt.shape
    n_s, t_s, _ = x_sample.shape
    assert t_s == 1 and t_p % TM == 0 and n_s % LANES == 0
    rows_p = n_p * t_p
    rows_all = rows_p + n_s
    past = page_table.shape[1] * PAGE_SIZE
    tab_p, tabt_p = _rope_tables(jnp.arange(t_p))
    tab_s, _ = _rope_tables(jnp.full((n_s,), past, jnp.int32))
    n_blocks = -(-(rows_all * TOP_K) // MOE_BLOCK) + N_EXPERTS

    h_p = x_prompt.reshape(rows_p, D_MODEL)
    h_s = x_sample.reshape(n_s, D_MODEL)
    ckv_p, kr_p, cv_p, ckv_s, kr_s, cv_s = [], [], [], [], [], []
    row = lambda v: v.reshape(1, -1)
    for l in range(depth):
        w1, wg, wuq, wuqt, wuk, wukt, wuvt, wbd, wr, br = _pack_weights(
            w_in[l], w_uq[l], w_uk[l], w_uv[l], router_w_group[l], router_b_group[l],
            router_w_expert[l], router_b_expert[l])
        qg, kvg = row(q_norm_g[l]), row(kv_norm_g[l])
        wba = w_br_attn[l].astype(BF16)
        merge_w = (wg, w_br_conv[l].astype(BF16), w_o[l].astype(BF16), row(ln1_g[l]), row(ln1_b[l]), wr, br)

        bconv, qt, k, vt, ckv, krt, nconv = _inproj_prompt(h_p, tab_p, tabt_p, w1, wuqt, wuk, wuvt, qg, kvg, conv_w[l],
                                                          n_p, t_p)
        ot = _prompt_attention(qt, k, vt, n_p, t_p)

        st = state_conv[l]
        bconv_s, q_s, qlat, ckvn_s, krn_s, u_s = _inproj_sample(
            h_s, tab_s, w1, wuq, wukt, qg, kvg, conv_w[l], st[:, 0], st[:, 1])
        olat = _sample_attention(page_table, qlat, q_s, ckvn_s, krn_s, cache_ckv[l],
                                 jnp.swapaxes(cache_krope[l], 1, 2))
        h1_s, route_s = _merge_sample(alpha, h_s, bconv_s, olat.reshape(n_s, N_HEADS * KV_LORA), wbd, wba, merge_w)
        h1t, route, route_t, counts = _merge_prompt(alpha, h_p, bconv, ot, h1_s, route_s, wba, merge_w)

        dest, blk_e, used, nvalid = _dispatch_plan(route_t, counts, n_blocks)
        slot_tok = _slot_tokens(dest, n_blocks)
        ys = _experts(blk_e, used, nvalid, slot_tok, h1t, w_gate[l], w_up[l], w_down[l])
        dest = dest.reshape(-1)
        ln2 = (row(ln2_g[l]), row(ln2_b[l]))
        h_p = _combine(alpha, 0, rows_p, TM_COMBINE, dest, ys, h1t, route, *ln2)
        h_s = _combine(alpha, rows_p, n_s, n_s, dest, ys, h1t, route, *ln2)

        ckv_p.append(ckv.reshape(n_p, t_p, KV_LORA))
        kr_p.append(jnp.swapaxes(krt, 1, 2))
        cv_p.append(nconv)
        ckv_s.append(ckvn_s.reshape(n_s, 1, KV_LORA))
        kr_s.append(krn_s.reshape(n_s, 1, ROPE_DIM))
        cv_s.append(jnp.stack([st[:, 1], u_s], axis=1))
    return (h_p.reshape(n_p, t_p, D_MODEL), h_s.reshape(n_s, 1, D_MODEL), jnp.stack(ckv_p), jnp.stack(kr_p),
            jnp.stack(cv_p), jnp.stack(ckv_s), jnp.stack(kr_s), jnp.stack(cv_s))
```

```python
import functools

import jax
import jax.numpy as jnp
from jax import lax
from jax.experimental import pallas as pl
from jax.experimental.pallas import tpu as pltpu

F32 = jnp.float32
BF16 = jnp.bfloat16

D_MODEL = 1024
D_CONV = 512
CONV_W = 3
N_HEADS = 8
Q_LORA = 384
KV_LORA = 256
NOPE_DIM = 64
ROPE_DIM = 32
ROPE_HALF = ROPE_DIM // 2
V_DIM = 64
QK_DIM = NOPE_DIM + ROPE_DIM
ROPE_BASE = 10000.0
ATTN_SCALE = QK_DIM ** -0.5
LOG2E = 1.4426950408889634
Q_SCALE = ATTN_SCALE * LOG2E
N_GROUPS = 4
EXP_PER_GROUP = 8
N_EXPERTS = N_GROUPS * EXP_PER_GROUP
TOP_K = 2
D_EXPERT = 512
MOE_BLOCK = 256
PAGE_SIZE = 128
LN_EPS = 1e-5
RMS_EPS = 1e-6

LANES = 128
SUBLANES = 8
ROW_TILE = D_MODEL // LANES
assert ROW_TILE == SUBLANES
HEAD_W = LANES
QK_W = N_HEADS * HEAD_W
V_W = N_HEADS * V_DIM
_C_BG, _C_CG, _C_H = 0, D_CONV, 2 * D_CONV
_C_CQ = 3 * D_CONV
_C_CKV = _C_CQ + Q_LORA
_C_KR = _C_CKV + KV_LORA
W1_COLS = _C_KR + LANES
ROUTE_W = 2 * LANES
R_EID, R_WT, R_RANK = 0, 2, 4
NEG = -1e30
VMEM_LIMIT = 56 * 1024 * 1024

TM = 512
TM_INPROJ = 1024
KEY_CHUNK = 2048
ATTN_HEADS_PER_LOOP = 8
TQ = 512
TM_COMBINE = 256
GATHER_RING = 4


def _cparams(*sem):
    return pltpu.CompilerParams(dimension_semantics=sem, vmem_limit_bytes=VMEM_LIMIT)


def _dot(a, b):
    return jnp.dot(a, b, preferred_element_type=F32)


def _dot_nt(a, b):
    return lax.dot_general(a, b, (((1,), (1,)), ((), ())), preferred_element_type=F32)


def _dot_tn(a, b):
    return lax.dot_general(a, b, (((0,), (0,)), ((), ())), preferred_element_type=F32)


def _rms(x, g):
    return x * lax.rsqrt(jnp.mean(x * x, axis=-1, keepdims=True) + RMS_EPS) * g


def _layernorm(x, g, b):
    mu = jnp.mean(x, axis=-1, keepdims=True)
    xc = x - mu
    var = jnp.mean(xc * xc, axis=-1, keepdims=True)
    return xc * lax.rsqrt(var + LN_EPS) * g + b


def _rope(xh, a, bm, cp):
    return xh * a + pltpu.roll(xh, LANES - ROPE_HALF, 1) * bm + pltpu.roll(xh, ROPE_HALF, 1) * cp


def _full(shape):
    nd = len(shape)
    return pl.BlockSpec(shape, lambda *_: (0,) * nd)


def _tile_rows(ref, c, n):
    return ref.at[pl.ds(c, n, stride=ROW_TILE), :]


def _inproj_common(x_ref, tab_ref, w1_ref, qg_ref, kvg_ref, ckv_ref):
    xb = x_ref[...].astype(BF16)

    def seg(lo, hi):
        return _dot(xb, w1_ref[:, lo:hi])

    b_g = seg(_C_BG, _C_CG)
    u = seg(_C_CG, _C_H) * seg(_C_H, _C_CQ)
    cqn = _rms(seg(_C_CQ, _C_CKV), qg_ref[...]).astype(BF16)
    ckvn = _rms(seg(_C_CKV, _C_KR), kvg_ref[...])
    ckv_ref[...] = ckvn
    krr = _rope(seg(_C_KR, W1_COLS), tab_ref[3], tab_ref[4], tab_ref[5])
    return b_g, u, cqn, ckvn.astype(BF16), krr


def _inproj_prompt_body(tiles_per_seq, x_ref, tab_ref, tabt_ref, w1_ref, wuqt_ref, wuk_ref, wuvt_ref, qg_ref, kvg_ref,
                        cw_ref, bconv_ref, qt_ref, k_ref, vt_ref, ckv_ref, krt_ref, nconv_ref, uext_ref):
    tm = x_ref.shape[0]
    b_g, u, cqn, cb, krr = _inproj_common(x_ref, tab_ref, w1_ref, qg_ref, kvg_ref, ckv_ref)
    krt_ref[0] = krr.T[0:ROPE_DIM, :]
    qft = _dot_nt(wuqt_ref[...], cqn)
    cos, sin = tabt_ref[0], tabt_ref[1]
    for h in range(N_HEADS):
        r0 = h * HEAD_W
        x1 = qft[r0:r0 + ROPE_HALF, :]
        x2 = qft[r0 + ROPE_HALF:r0 + ROPE_DIM, :]
        qt_ref[r0:r0 + ROPE_HALF, :] = (x1 * cos - x2 * sin).astype(BF16)
        qt_ref[r0 + ROPE_HALF:r0 + ROPE_DIM, :] = (x1 * sin + x2 * cos).astype(BF16)
        qt_ref[r0 + ROPE_DIM:r0 + HEAD_W, :] = (qft[r0 + ROPE_DIM:r0 + HEAD_W, :] * Q_SCALE).astype(BF16)
    kn = _dot(cb, wuk_ref[...])
    for h in range(N_HEADS):
        sl = slice(h * HEAD_W, (h + 1) * HEAD_W)
        k_ref[:, sl] = (kn[:, sl] + krr).astype(BF16)
    vt_ref[...] = _dot_nt(wuvt_ref[...], cb).astype(BF16)

    first = (pl.program_id(0) % tiles_per_seq) == 0

    @pl.when(first)
    def _():
        uext_ref[0:8, :] = jnp.zeros((8, D_CONV), F32)

    @pl.when(jnp.logical_not(first))
    def _():
        uext_ref[0:8, :] = uext_ref[tm:tm + 8, :]

    uext_ref[8:8 + tm, :] = u
    conv = cw_ref[0:1, :] * uext_ref[6:6 + tm, :] + cw_ref[1:2, :] * uext_ref[7:7 + tm, :] + cw_ref[2:3, :] * u
    bconv_ref[...] = (b_g * conv).astype(BF16)
    nconv_ref[0] = u[tm - (CONV_W - 1):tm, :]


def _inproj_sample_body(x_ref, tab_ref, w1_ref, wuq_ref, wukt_ref, qg_ref, kvg_ref, cw_ref, s0_ref, s1_ref,
                        bconv_ref, q_ref, qlat_ref, ckv_ref, kr_ref, u_ref):
    b_g, u, cqn, _, krr = _inproj_common(x_ref, tab_ref, w1_ref, qg_ref, kvg_ref, ckv_ref)
    kr_ref[...] = krr[:, 0:ROPE_DIM]
    qf = _dot(cqn, wuq_ref[...])
    for h in range(N_HEADS):
        sl = slice(h * HEAD_W, (h + 1) * HEAD_W)
        qh = _rope(qf[:, sl], tab_ref[0], tab_ref[1], tab_ref[2]).astype(BF16)
        q_ref[:, sl] = qh
        qlat_ref[:, h * KV_LORA:(h + 1) * KV_LORA] = _dot(qh, wukt_ref[h]).astype(BF16)
    conv = cw_ref[0:1, :] * s0_ref[...] + cw_ref[1:2, :] * s1_ref[...] + cw_ref[2:3, :] * u
    bconv_ref[...] = (b_g * conv).astype(BF16)
    u_ref[...] = u


def _inproj_prompt(x, tab, tabt, w1, wuqt, wuk, wuvt, qg, kvg, cw, n_seq, seq):
    t = x.shape[0]
    tm = TM_INPROJ
    assert seq % tm == 0
    tiles_per_seq = seq // tm
    rows = lambda w: pl.BlockSpec((tm, w), lambda i: (i, 0))
    cols = lambda h: pl.BlockSpec((h, tm), lambda i: (0, i))
    return pl.pallas_call(
        functools.partial(_inproj_prompt_body, tiles_per_seq),
        grid=(t // tm,),
        in_specs=[rows(D_MODEL),
                  pl.BlockSpec((6, tm, LANES), lambda i: (0, i % tiles_per_seq, 0)),
                  pl.BlockSpec((2, ROPE_HALF, tm), lambda i: (0, 0, i % tiles_per_seq)),
                  _full(w1.shape), _full(wuqt.shape), _full(wuk.shape), _full(wuvt.shape),
                  _full(qg.shape), _full(kvg.shape), _full(cw.shape)],
        out_specs=[rows(D_CONV), cols(QK_W), rows(QK_W), cols(V_W), rows(KV_LORA),
                   pl.BlockSpec((1, ROPE_DIM, tm), lambda i: (i // tiles_per_seq, 0, i % tiles_per_seq)),
                   pl.BlockSpec((1, CONV_W - 1, D_CONV), lambda i: (i // tiles_per_seq, 0, 0))],
        out_shape=[jax.ShapeDtypeStruct((t, D_CONV), BF16), jax.ShapeDtypeStruct((QK_W, t), BF16),
                   jax.ShapeDtypeStruct((t, QK_W), BF16), jax.ShapeDtypeStruct((V_W, t), BF16),
                   jax.ShapeDtypeStruct((t, KV_LORA), F32), jax.ShapeDtypeStruct((n_seq, ROPE_DIM, seq), F32),
                   jax.ShapeDtypeStruct((n_seq, CONV_W - 1, D_CONV), F32)],
        scratch_shapes=[pltpu.VMEM((tm + 8, D_CONV), F32)],
        compiler_params=_cparams("arbitrary"),
    )(x, tab, tabt, w1, wuqt, wuk, wuvt, qg, kvg, cw)


def _inproj_sample(x, tab, w1, wuq, wukt, qg, kvg, cw, s0, s1):
    n = x.shape[0]
    args = (x, tab, w1, wuq, wukt, qg, kvg, cw, s0, s1)
    out_shape = [jax.ShapeDtypeStruct((n, D_CONV), BF16), jax.ShapeDtypeStruct((n, QK_W), BF16),
                 jax.ShapeDtypeStruct((n, N_HEADS * KV_LORA), BF16), jax.ShapeDtypeStruct((n, KV_LORA), F32),
                 jax.ShapeDtypeStruct((n, ROPE_DIM), F32), jax.ShapeDtypeStruct((n, D_CONV), F32)]
    return pl.pallas_call(
        _inproj_sample_body,
        grid=(1,),
        in_specs=[_full(a.shape) for a in args],
        out_specs=[_full(s.shape) for s in out_shape],
        out_shape=out_shape,
        compiler_params=_cparams("arbitrary"),
    )(*args)


def _prompt_attention_body(qt_ref, k_ref, vt_ref, ot_ref, acc_ref, m_ref, l_ref, sa_ref, sb_ref):
    tq = qt_ref.shape[1]
    i = pl.program_id(1)

    for hg in range(N_HEADS // ATTN_HEADS_PER_LOOP):
        heads = tuple(range(hg * ATTN_HEADS_PER_LOOP, (hg + 1) * ATTN_HEADS_PER_LOOP))
        acc_ref[...] = jnp.zeros(acc_ref.shape, F32)
        m_ref[...] = jnp.full(m_ref.shape, NEG, F32)
        l_ref[...] = jnp.zeros(l_ref.shape, F32)

        def scores(j, s_ref):
            off = pl.multiple_of(j * tq, tq)
            for idx, h in enumerate(heads):
                s_ref[idx] = _dot(k_ref[pl.ds(off, tq), h * HEAD_W:(h + 1) * HEAD_W],
                                  qt_ref[h * HEAD_W:(h + 1) * HEAD_W, :])

        def consume(j, s_ref, masked):
            off = pl.multiple_of(j * tq, tq)
            for idx, h in enumerate(heads):
                s = s_ref[idx]
                if masked:
                    krow = lax.broadcasted_iota(jnp.int32, (tq, tq), 0)
                    qcol = lax.broadcasted_iota(jnp.int32, (tq, tq), 1)
                    s = jnp.where(krow <= qcol, s, NEG)
                m = m_ref[idx]
                m_new = jnp.maximum(m, jnp.max(s, axis=0, keepdims=True))
                alpha = jnp.exp2(m - m_new)
                p = jnp.exp2(s - m_new)
                m_ref[idx] = m_new
                l_ref[idx] = alpha * l_ref[idx] + jnp.sum(p, axis=0, keepdims=True)
                vblk = vt_ref[h * V_DIM:(h + 1) * V_DIM, pl.ds(off, tq)]
                acc_ref[idx] = alpha * acc_ref[idx] + _dot(vblk, p.astype(BF16))

        scores(0, sa_ref)

        def pair(pp, c):
            j = 2 * pp
            scores(j + 1, sb_ref)
            consume(j, sa_ref, False)
            scores(j + 2, sa_ref)
            consume(j + 1, sb_ref, False)
            return c
        lax.fori_loop(0, i // 2, pair, 0)

        @pl.when(i % 2 == 1)
        def _():
            scores(i, sb_ref)
            consume(i - 1, sa_ref, False)
            consume(i, sb_ref, True)

        @pl.when(i % 2 == 0)
        def _():
            consume(i, sa_ref, True)

        for idx, h in enumerate(heads):
            ot_ref[h * V_DIM:(h + 1) * V_DIM, :] = (acc_ref[idx] / l_ref[idx]).astype(BF16)


def _prompt_attention(qt, k, vt, n_seq, seq):
    t = k.shape[0]
    tq = TQ
    nq = seq // tq
    return pl.pallas_call(
        _prompt_attention_body,
        grid=(n_seq, nq),
        in_specs=[pl.BlockSpec((QK_W, tq), lambda b, i: (0, b * nq + i)),
                  pl.BlockSpec((seq, QK_W), lambda b, i: (b, 0)),
                  pl.BlockSpec((V_W, seq), lambda b, i: (0, b))],
        out_specs=pl.BlockSpec((V_W, tq), lambda b, i: (0, b * nq + i)),
        out_shape=jax.ShapeDtypeStruct((V_W, t), BF16),
        scratch_shapes=[pltpu.VMEM((ATTN_HEADS_PER_LOOP, V_DIM, tq), F32),
                        pltpu.VMEM((ATTN_HEADS_PER_LOOP, 1, tq), F32), pltpu.VMEM((ATTN_HEADS_PER_LOOP, 1, tq), F32),
                        pltpu.VMEM((ATTN_HEADS_PER_LOOP, tq, tq), F32), pltpu.VMEM((ATTN_HEADS_PER_LOOP, tq, tq), F32)],
        compiler_params=_cparams("arbitrary", "arbitrary"),
    )(qt, k, vt)


def _sample_attention_body(n_pages, pt_ref, qlat_ref, q_ref, ckvn_ref, krn_ref, cc_ref, cr_ref, olat_ref,
                           bufc, bufr, kcb, krb, sem):
    b = pl.program_id(0)
    nb = pl.num_programs(0)
    slot = b % 2
    past = n_pages * PAGE_SIZE

    def page_copies(bb, sl, p):
        pg = pt_ref[bb, p]
        pos = pl.ds(pl.multiple_of(p * PAGE_SIZE, PAGE_SIZE), PAGE_SIZE)
        return (pltpu.make_async_copy(cc_ref.at[pg], bufc.at[sl, pos], sem.at[0, sl]),
                pltpu.make_async_copy(cr_ref.at[pg], bufr.at[sl, :, pos], sem.at[1, sl]))

    def issue(bb, sl):
        for p in range(n_pages):
            for c in page_copies(bb, sl, p):
                c.start(priority=p % 2)

    def wait(sl):
        pltpu.make_async_copy(bufc.at[sl], bufc.at[sl], sem.at[0, sl]).wait()
        pltpu.make_async_copy(bufr.at[sl], bufr.at[sl], sem.at[1, sl]).wait()

    @pl.when(b == 0)
    def _():
        issue(0, 0)

    wait(slot)
    issue(jnp.minimum(b + 1, nb - 1), 1 - slot)

    ql = qlat_ref[0]
    qr = q_ref[0][:, 0:ROPE_DIM]
    cn = ckvn_ref[0]
    rn = krn_ref[0]
    chunk = min(KEY_CHUNK, past)
    scores = []
    for c in range(past // chunk):
        pos = slice(c * chunk, (c + 1) * chunk)
        kc = bufc[slot, pos, :]
        kcb[pos, :] = kc.astype(BF16)
        krb[:, pos] = bufr[slot, :, pos].astype(BF16)
        scores.append(_dot(ql, kc.T.astype(BF16)) + _dot(qr, krb[:, pos]))
    s_new = (jnp.sum(ql.astype(F32) * cn, axis=-1, keepdims=True)
             + jnp.sum(qr.astype(F32) * rn, axis=-1, keepdims=True))
    m = s_new
    for s in scores:
        m = jnp.maximum(m, jnp.max(s, axis=-1, keepdims=True))
    p_new = jnp.exp2(s_new - m)
    l = p_new
    o = p_new * cn
    for c, s in enumerate(scores):
        p = jnp.exp2(s - m)
        l = l + jnp.sum(p, axis=-1, keepdims=True)
        o = o + _dot(p.astype(BF16), kcb[c * chunk:(c + 1) * chunk, :])
    olat_ref[0] = o / l

    @pl.when(b == nb - 1)
    def _():
        wait(1 - slot)


def _sample_attention(page_table, qlat, q, ckvn, krn, cache_c, cache_rt):
    n, n_pages = page_table.shape
    past = n_pages * PAGE_SIZE
    blk = lambda d1, d2: pl.BlockSpec((1, d1, d2), lambda b, pt: (b, 0, 0))
    grid_spec = pltpu.PrefetchScalarGridSpec(
        num_scalar_prefetch=1,
        grid=(n,),
        in_specs=[blk(N_HEADS, KV_LORA), blk(N_HEADS, HEAD_W), blk(1, KV_LORA), blk(1, ROPE_DIM),
                  pl.BlockSpec(memory_space=pl.ANY), pl.BlockSpec(memory_space=pl.ANY)],
        out_specs=blk(N_HEADS, KV_LORA),
        scratch_shapes=[pltpu.VMEM((2, past, KV_LORA), F32), pltpu.VMEM((2, ROPE_DIM, past), F32),
                        pltpu.VMEM((past, KV_LORA), BF16), pltpu.VMEM((ROPE_DIM, past), BF16),
                        pltpu.SemaphoreType.DMA((2, 2))],
    )
    return pl.pallas_call(
        functools.partial(_sample_attention_body, n_pages),
        grid_spec=grid_spec,
        out_shape=jax.ShapeDtypeStruct((n, N_HEADS, KV_LORA), F32),
        compiler_params=_cparams("arbitrary"),
    )(page_table, qlat.reshape(n, N_HEADS, KV_LORA), q.reshape(n, N_HEADS, HEAD_W),
      ckvn.reshape(n, 1, KV_LORA), krn.reshape(n, 1, ROPE_DIM), cache_c, cache_rt)


def _route(logits):
    lane_i = lax.broadcasted_iota(jnp.int32, (logits.shape[0], LANES), 1)
    lane = lane_i.astype(F32)
    first_at = lambda hit: jnp.min(jnp.where(hit, lane, float(LANES)), axis=-1, keepdims=True)
    lg = jnp.where(lane_i < N_GROUPS, logits[:, 0:LANES], NEG)
    mg = jnp.max(lg, axis=-1, keepdims=True)
    p_grp = 1.0 / jnp.sum(jnp.exp(lg - mg), axis=-1, keepdims=True)
    grp = first_at(lg == mg)
    lane_grp = (lane_i // EXP_PER_GROUP).astype(F32)
    le = jnp.where(lane_grp == grp, logits[:, LANES:2 * LANES], NEG)
    top1 = jnp.max(le, axis=-1, keepdims=True)
    i1 = first_at(le == top1)
    le2 = jnp.where(lane == i1, NEG, le)
    top2 = jnp.max(le2, axis=-1, keepdims=True)
    i2 = first_at(le2 == top2)
    e2 = jnp.exp(top2 - top1)
    w1 = p_grp / (1.0 + e2)
    w2 = p_grp * e2 / (1.0 + e2)
    out = jnp.where(lane_i == R_EID, i1, 0.0)
    out = jnp.where(lane_i == R_EID + 1, i2, out)
    out = jnp.where(lane_i == R_WT, w1, out)
    return jnp.where(lane_i == R_WT + 1, w2, out)


def _add_ranks(route, ltri, count_ref):
    n = route.shape[0]
    lane_i = lax.broadcasted_iota(jnp.int32, (n, LANES), 1)
    lane = lane_i.astype(F32)
    hit1 = lane == route[:, R_EID:R_EID + 1]
    hit2 = lane == route[:, R_EID + 1:R_EID + 2]
    chosen = jnp.where(hit1, 1.0, jnp.where(hit2, 1.0, 0.0))
    before = _dot(ltri, chosen.astype(BF16)) + count_ref[...]
    r1 = jnp.sum(jnp.where(hit1, before, 0.0), axis=-1, keepdims=True)
    r2 = jnp.sum(jnp.where(hit2, before, 0.0), axis=-1, keepdims=True)
    count_ref[...] = count_ref[...] + jnp.sum(chosen, axis=0, keepdims=True)
    route = jnp.where(lane_i == R_RANK, r1, route)
    return jnp.where(lane_i == R_RANK + 1, r2, route)


def _merge_rows(alpha, x, bconv, y_attn, wg_ref, wbc_ref, wo_ref, lg_ref, lb_ref, wr_ref, br_ref):
    g = _dot(x.astype(BF16), wg_ref[...])
    y_conv = _dot(bconv, wbc_ref[...])
    m = jax.nn.sigmoid(g[:, 0:D_MODEL]) * y_conv + jax.nn.sigmoid(g[:, D_MODEL:]) * y_attn
    mix = _dot(m.astype(BF16), wo_ref[...])
    h1 = _layernorm(alpha * x + mix, lg_ref[...], lb_ref[...])
    return h1, _route(_dot(h1.astype(BF16), wr_ref[...]) + br_ref[...])


def _merge_sample_body(alpha, x_ref, bconv_ref, olat_ref, wbd_ref, wba_ref, *refs):
    w_refs, (h1_ref, route_ref) = refs[:-2], refs[-2:]
    o = _dot(olat_ref[...].astype(BF16), wbd_ref[...]).astype(BF16)
    h1_ref[...], route_ref[...] = _merge_rows(alpha, x_ref[...], bconv_ref[...], _dot(o, wba_ref[...]), *w_refs)


def _merge_prompt_body(alpha, n_tiles, x_ref, bconv_ref, ot_ref, h1s_ref, routes_ref, wba_ref, *refs):
    w_refs, (h1t_ref, route_ref, routet_ref, counts_ref, ltri_ref, count_ref) = refs[:-6], refs[-6:]
    i = pl.program_id(0)
    tm = x_ref.shape[0]

    @pl.when(i == 0)
    def _():
        r = lax.broadcasted_iota(jnp.int32, (tm, tm), 0)
        c = lax.broadcasted_iota(jnp.int32, (tm, tm), 1)
        ltri_ref[...] = jnp.where(c < r, 1.0, 0.0).astype(BF16)
        count_ref[...] = jnp.zeros(count_ref.shape, F32)

    @pl.when(i < n_tiles)
    def _():
        y_attn = _dot_tn(ot_ref[...], wba_ref[...])
        h1, route = _merge_rows(alpha, x_ref[...], bconv_ref[...], y_attn, *w_refs)
        for c in range(ROW_TILE):
            _tile_rows(h1t_ref, c, tm)[...] = h1[:, c * LANES:(c + 1) * LANES]
        route_ref[...] = _add_ranks(route, ltri_ref[...], count_ref)

    @pl.when(i == n_tiles)
    def _():
        n_s = h1s_ref.shape[0]
        h1t_ref[...] = jnp.zeros(h1t_ref.shape, F32)
        route_ref[...] = jnp.zeros(route_ref.shape, F32)
        for c in range(ROW_TILE):
            _tile_rows(h1t_ref, c, n_s)[...] = h1s_ref[:, c * LANES:(c + 1) * LANES]
        route_ref[0:n_s, :] = _add_ranks(routes_ref[...], ltri_ref[0:n_s, 0:n_s], count_ref)

    counts_ref[...] = jnp.broadcast_to(count_ref[...], counts_ref.shape)
    routet_ref[...] = route_ref[...].T[0:SUBLANES, :]


def _merge_sample(alpha, x, bconv, olat, wbd, wba, ws):
    n = x.shape[0]
    args = (x, bconv, olat, wbd, wba) + tuple(ws)
    out_shape = [jax.ShapeDtypeStruct((n, D_MODEL), F32), jax.ShapeDtypeStruct((n, LANES), F32)]
    return pl.pallas_call(
        functools.partial(_merge_sample_body, alpha),
        grid=(1,),
        in_specs=[_full(a.shape) for a in args],
        out_specs=[_full(s.shape) for s in out_shape],
        out_shape=out_shape,
        compiler_params=_cparams("arbitrary"),
    )(*args)


def _merge_prompt(alpha, x, bconv, ot, h1_s, route_s, wba, ws):
    t = x.shape[0]
    tm = TM
    n_tiles = t // tm
    t_all = t + h1_s.shape[0]
    assert h1_s.shape[0] <= tm
    clamp = lambda i: jnp.minimum(i, n_tiles - 1)
    rows_in = lambda w: pl.BlockSpec((tm, w), lambda i: (clamp(i), 0))
    return pl.pallas_call(
        functools.partial(_merge_prompt_body, alpha, n_tiles),
        grid=(n_tiles + 1,),
        in_specs=[rows_in(D_MODEL), rows_in(D_CONV), pl.BlockSpec((V_W, tm), lambda i: (0, clamp(i))),
                  _full(h1_s.shape), _full(route_s.shape), _full(wba.shape)] + [_full(w.shape) for w in ws],
        out_specs=[pl.BlockSpec((tm * ROW_TILE, LANES), lambda i: (i, 0)), pl.BlockSpec((tm, LANES), lambda i: (i, 0)),
                   pl.BlockSpec((SUBLANES, tm), lambda i: (0, i)), _full((SUBLANES, LANES))],
        out_shape=[jax.ShapeDtypeStruct((t_all * ROW_TILE, LANES), F32), jax.ShapeDtypeStruct((t_all, LANES), F32),
                   jax.ShapeDtypeStruct((SUBLANES, t_all), F32), jax.ShapeDtypeStruct((SUBLANES, LANES), F32)],
        scratch_shapes=[pltpu.VMEM((tm, tm), BF16), pltpu.VMEM((1, LANES), F32)],
        compiler_params=_cparams("arbitrary"),
    )(x, bconv, ot, h1_s, route_s, wba, *ws)


SLOT_GROUP = 16


def _slot_tokens_body(rows_ref, fill_ref, pend_ref, dest_ref, tok_ref):
    def clear(s, c):
        tok_ref[s] = 0
        return c
    for e in range(N_EXPERTS):
        lax.fori_loop(fill_ref[e], pend_ref[e], clear, 0)
    lax.fori_loop(pend_ref[N_EXPERTS - 1], rows_ref[1] * LANES, clear, 0)

    rows_k = rows_ref[0]
    for k in range(TOP_K):
        def fill_row(j, c):
            for g in range(0, LANES, SLOT_GROUP):
                ds = [dest_ref[k * rows_k + j, g + l] for l in range(SLOT_GROUP)]
                for l, d in enumerate(ds):
                    tok_ref[d] = j * LANES + g + l
            return c
        lax.fori_loop(0, rows_k, fill_row, 0)


def _slot_tokens(dest, fill_end, pend, n_blocks):
    n_slots = n_blocks * MOE_BLOCK
    n_tok = dest.shape[1] * LANES
    assert n_slots % LANES == 0
    dest = dest.reshape(-1, LANES)
    grid_spec = pltpu.PrefetchScalarGridSpec(
        num_scalar_prefetch=4,
        grid=(1,),
        in_specs=[],
        out_specs=pl.BlockSpec(memory_space=pltpu.SMEM),
    )
    return pl.pallas_call(
        _slot_tokens_body,
        grid_spec=grid_spec,
        out_shape=jax.ShapeDtypeStruct((n_slots,), jnp.int32),
        compiler_params=_cparams("arbitrary"),
    )(jnp.array([n_tok // LANES, n_slots // LANES], jnp.int32), fill_end, pend, dest)


def _row_tile(i):
    return pl.ds(pl.multiple_of(i * ROW_TILE, ROW_TILE), ROW_TILE)


def _experts_body(blk_e_ref, used_ref, nvalid_ref, tok_ref, h1t_ref, wg_ref, wu_ref, wd_ref, ys_ref,
                  xbuf, xb, wgb, wub, wdb, sem_in):
    b = pl.program_id(0)
    ring = lambda bb: lax.rem(bb, GATHER_RING)
    used = used_ref[0]
    blk_rows = MOE_BLOCK * ROW_TILE
    changed = jnp.logical_or(b == 0, blk_e_ref[b] != blk_e_ref[jnp.maximum(b - 1, 0)])
    last = used - 1
    clamp = lambda v: jnp.minimum(v, last)

    def in_copy(bb, sl, r):
        tok = tok_ref[bb * MOE_BLOCK + r]
        return pltpu.make_async_copy(h1t_ref.at[_row_tile(tok)], xbuf.at[sl, _row_tile(r)], sem_in.at[sl])

    def issue_full(v):
        bits = MOE_BLOCK.bit_length() - 1
        for i in range(MOE_BLOCK):
            r = int(format(i, f"0{bits}b")[::-1], 2)
            in_copy(clamp(v), ring(v), r).start(priority=1)

    def issue_partial(v):
        lax.fori_loop(0, nvalid_ref[clamp(v)], lambda r, c: (in_copy(clamp(v), ring(v), r).start(), c)[1], 0)

    def is_full(v):
        return nvalid_ref[clamp(v)] == MOE_BLOCK

    def wait_in(v):
        sl = ring(v)

        @pl.when(is_full(v))
        def _():
            pltpu.make_async_copy(h1t_ref.at[pl.ds(0, blk_rows)], xbuf.at[sl], sem_in.at[sl]).wait()

        @pl.when(jnp.logical_not(is_full(v)))
        def _():
            lax.fori_loop(0, nvalid_ref[clamp(v)], lambda r, c: (in_copy(clamp(v), sl, r).wait(), c)[1], 0)

    @pl.when(jnp.logical_and(b == 0, used > 0))
    def _():
        xbuf[...] = jnp.zeros(xbuf.shape, F32)
        for a in range(GATHER_RING - 1):
            @pl.when(is_full(a))
            def _():
                issue_full(a)

            @pl.when(jnp.logical_not(is_full(a)))
            def _():
                issue_partial(a)

    @pl.when(jnp.logical_and(b < used, changed))
    def _():
        wgb[...] = wg_ref[0].astype(BF16)
        wub[...] = wu_ref[0].astype(BF16)
        wdb[...] = wd_ref[0].astype(BF16)

    ahead = b + GATHER_RING - 1

    def block(issue_ahead):
        for c in range(ROW_TILE):
            xb[:, c * LANES:(c + 1) * LANES] = _tile_rows(xbuf.at[ring(b)], c, MOE_BLOCK)[...].astype(BF16)
        if issue_ahead:
            issue_full(ahead)
        x = xb[...]
        g = _dot(x, wgb[...])
        u = _dot(x, wub[...])
        h = (g * jax.nn.sigmoid(g) * u).astype(BF16)
        y = _dot(h, wdb[...])
        for c in range(ROW_TILE):
            _tile_rows(ys_ref, c, MOE_BLOCK)[...] = y[:, c * LANES:(c + 1) * LANES]

    @pl.when(b < used)
    def _():
        wait_in(b)

    @pl.when(jnp.logical_and(b < used, is_full(ahead)))
    def _():
        block(True)

    @pl.when(jnp.logical_and(b < used, jnp.logical_not(is_full(ahead))))
    def _():
        block(False)
        issue_partial(ahead)

    @pl.when(b == last)
    def _():
        for a in range(1, GATHER_RING):
            wait_in(b + a)

    @pl.when(b >= used)
    def _():
        ys_ref[...] = jnp.zeros(ys_ref.shape, F32)


def _experts(blk_e, used, nvalid, slot_tok, h1t, wg, wu, wd):
    n_blocks = blk_e.shape[0]
    blk_rows = MOE_BLOCK * ROW_TILE
    by_expert = lambda b, e, u, n, a: (e[b], 0, 0)
    grid_spec = pltpu.PrefetchScalarGridSpec(
        num_scalar_prefetch=4,
        grid=(n_blocks,),
        in_specs=[pl.BlockSpec(memory_space=pl.ANY),
                  pl.BlockSpec((1, D_MODEL, D_EXPERT), by_expert), pl.BlockSpec((1, D_MODEL, D_EXPERT), by_expert),
                  pl.BlockSpec((1, D_EXPERT, D_MODEL), by_expert)],
        out_specs=pl.BlockSpec((blk_rows, LANES), lambda b, e, u, n, a: (b, 0)),
        scratch_shapes=[pltpu.VMEM((GATHER_RING, blk_rows, LANES), F32),
                        pltpu.VMEM((MOE_BLOCK, D_MODEL), BF16),
                        pltpu.VMEM((D_MODEL, D_EXPERT), BF16), pltpu.VMEM((D_MODEL, D_EXPERT), BF16),
                        pltpu.VMEM((D_EXPERT, D_MODEL), BF16),
                        pltpu.SemaphoreType.DMA((GATHER_RING,))],
    )
    return pl.pallas_call(
        _experts_body,
        grid_spec=grid_spec,
        out_shape=jax.ShapeDtypeStruct((n_blocks * blk_rows, LANES), F32),
        compiler_params=_cparams("arbitrary"),
    )(blk_e, used, nvalid, slot_tok, h1t, wg, wu, wd)


def _combine_body(alpha, tok0, n_all, dest_ref, ys_ref, h1t_ref, route_ref, lg_ref, lb_ref, out_ref, gbuf, sem):
    i = pl.program_id(0)
    nt = pl.num_programs(0)
    slot = i % 2
    tm = out_ref.shape[0]

    def issue(ii, sl):
        base = tok0 + ii * tm
        for r in range(tm):
            for k in range(TOP_K):
                d = dest_ref[base + k * n_all + r]
                pltpu.make_async_copy(ys_ref.at[_row_tile(d)], gbuf.at[sl, k, _row_tile(r)],
                                      sem.at[sl]).start(priority=k)

    def wait(sl):
        for k in range(TOP_K):
            pltpu.make_async_copy(ys_ref.at[pl.ds(0, tm * ROW_TILE)], gbuf.at[sl, k], sem.at[sl]).wait()

    @pl.when(i == 0)
    def _():
        issue(0, 0)

    wait(slot)
    issue(jnp.minimum(i + 1, nt - 1), 1 - slot)

    route = route_ref[...]
    w0 = jnp.broadcast_to(route[:, R_WT:R_WT + 1], (tm, LANES))
    w1 = jnp.broadcast_to(route[:, R_WT + 1:R_WT + 2], (tm, LANES))
    z = [alpha * _tile_rows(h1t_ref, c, tm)[...]
         + w0 * _tile_rows(gbuf.at[slot, 0], c, tm)[...] + w1 * _tile_rows(gbuf.at[slot, 1], c, tm)[...]
         for c in range(ROW_TILE)]
    mu = sum(jnp.sum(zc, axis=-1, keepdims=True) for zc in z) / D_MODEL
    zc = [v - mu for v in z]
    var = sum(jnp.sum(v * v, axis=-1, keepdims=True) for v in zc) / D_MODEL
    rstd = lax.rsqrt(var + LN_EPS)
    for c in range(ROW_TILE):
        sl = slice(c * LANES, (c + 1) * LANES)
        out_ref[:, sl] = zc[c] * rstd * lg_ref[:, sl] + lb_ref[:, sl]

    @pl.when(i == nt - 1)
    def _():
        wait(1 - slot)


def _combine(alpha, tok0, n_tok, tm, dest, ys, h1t, route, lg, lb):
    assert n_tok % tm == 0 and tok0 % tm == 0
    blk0 = tok0 // tm
    grid_spec = pltpu.PrefetchScalarGridSpec(
        num_scalar_prefetch=1,
        grid=(n_tok // tm,),
        in_specs=[pl.BlockSpec(memory_space=pl.ANY),
                  pl.BlockSpec((tm * ROW_TILE, LANES), lambda i, d: (blk0 + i, 0)),
                  pl.BlockSpec((tm, LANES), lambda i, d: (blk0 + i, 0)),
                  pl.BlockSpec(lg.shape, lambda i, d: (0, 0)), pl.BlockSpec(lb.shape, lambda i, d: (0, 0))],
        out_specs=pl.BlockSpec((tm, D_MODEL), lambda i, d: (i, 0)),
        scratch_shapes=[pltpu.VMEM((2, TOP_K, tm * ROW_TILE, LANES), F32), pltpu.SemaphoreType.DMA((2,))],
    )
    return pl.pallas_call(
        functools.partial(_combine_body, alpha, tok0, route.shape[0]),
        grid_spec=grid_spec,
        out_shape=jax.ShapeDtypeStruct((n_tok, D_MODEL), F32),
        compiler_params=_cparams("arbitrary"),
    )(dest, ys, h1t, route, lg, lb)


def _rope_tables(pos):
    inv = ROPE_BASE ** (-(jnp.arange(ROPE_HALF, dtype=F32) * 2.0 / ROPE_DIM))
    ang = pos.astype(F32)[:, None] * inv[None, :]
    cos, sin = jnp.cos(ang), jnp.sin(ang)
    n = pos.shape[0]
    a = jnp.concatenate([cos, cos, jnp.ones((n, LANES - ROPE_DIM), F32)], axis=1)
    bm = jnp.concatenate([-sin, jnp.zeros((n, LANES - ROPE_HALF), F32)], axis=1)
    cp = jnp.concatenate([jnp.zeros((n, ROPE_HALF), F32), sin, jnp.zeros((n, LANES - ROPE_DIM), F32)], axis=1)
    k_tabs = jnp.stack([a, bm, cp])
    return jnp.concatenate([k_tabs * Q_SCALE, k_tabs]), jnp.stack([cos.T, sin.T]) * Q_SCALE


def _head_blocks(w_rope, w_nope):
    k = w_nope.shape[0]
    pad = jnp.zeros((k, N_HEADS, HEAD_W - QK_DIM), w_nope.dtype)
    return jnp.concatenate([w_rope, w_nope, pad], axis=-1).reshape(k, QK_W)


def _pack_weights(w_in, w_uq, w_uk, w_uv, router_w_group, router_b_group, router_w_expert, router_b_expert):
    d = w_in.shape[0]
    c_kr = 3 * D_CONV + Q_LORA + KV_LORA
    w1 = jnp.concatenate([w_in[:, :c_kr], w_in[:, c_kr:c_kr + ROPE_DIM],
                          jnp.zeros((d, LANES - ROPE_DIM), w_in.dtype)], axis=1).astype(BF16)
    wg = w_in[:, c_kr + ROPE_DIM:].astype(BF16)
    uq = w_uq.reshape(Q_LORA, N_HEADS, QK_DIM)
    wuq = _head_blocks(uq[..., NOPE_DIM:], uq[..., :NOPE_DIM]).astype(BF16)
    wuk = _head_blocks(jnp.zeros((KV_LORA, N_HEADS, ROPE_DIM), w_uk.dtype), w_uk).astype(BF16)
    wukt = jnp.transpose(wuk.reshape(KV_LORA, N_HEADS, HEAD_W), (1, 2, 0))
    wuvt = w_uv.reshape(KV_LORA, V_W).T.astype(BF16)
    eye = jnp.eye(N_HEADS, dtype=w_uv.dtype)
    wbd = jnp.einsum('chd,hg->hcgd', w_uv, eye).reshape(N_HEADS * KV_LORA, V_W).astype(BF16)
    wr = jnp.zeros((d, ROUTE_W), F32)
    wr = wr.at[:, 0:N_GROUPS].set(router_w_group).at[:, LANES:LANES + N_EXPERTS].set(router_w_expert).astype(BF16)
    br = jnp.zeros((1, ROUTE_W), F32)
    br = br.at[0, 0:N_GROUPS].set(router_b_group).at[0, LANES:LANES + N_EXPERTS].set(router_b_expert)
    return w1, wg, wuq, wuq.T, wuk, wukt, wuvt, wbd, wr, br


def _dispatch_plan(route_t, counts, n_blocks):
    counts = counts[0, 0:N_EXPERTS].astype(jnp.int32)
    pcounts = (counts + MOE_BLOCK - 1) // MOE_BLOCK * MOE_BLOCK
    pend = jnp.cumsum(pcounts).astype(jnp.int32)
    pstart = (pend - pcounts).astype(F32)
    eid = route_t[R_EID:R_EID + TOP_K].reshape(TOP_K, -1, LANES)
    rank = route_t[R_RANK:R_RANK + TOP_K].reshape(TOP_K, -1, LANES)
    base = jnp.zeros_like(eid)
    for e in range(N_EXPERTS):
        base = jnp.where(eid == float(e), pstart[e], base)
    dest = (base + rank).astype(jnp.int32)
    blk_start = jnp.arange(n_blocks, dtype=jnp.int32) * MOE_BLOCK
    blk_e = jnp.sum((blk_start[:, None] >= pend[None, :]).astype(jnp.int32), axis=1)
    blk_e = jnp.minimum(blk_e, N_EXPERTS - 1).astype(jnp.int32)
    used = (pend[-1:] // MOE_BLOCK).astype(jnp.int32)
    fill_end = pend - pcounts + counts
    nvalid = jnp.where(blk_start < pend[-1], jnp.clip(fill_end[blk_e] - blk_start, 0, MOE_BLOCK), 0).astype(jnp.int32)
    return dest, fill_end, pend, blk_e, used, nvalid


def kernel(x_prompt, x_sample, cache_ckv, cache_krope, state_conv, page_table, w_in, conv_w, q_norm_g, w_uq,
           kv_norm_g, w_uk, w_uv, w_br_conv, w_br_attn, w_o, ln1_g, ln1_b, router_w_group, router_b_group,
           router_w_expert, router_b_expert, w_gate, w_up, w_down, ln2_g, ln2_b):
    depth = w_in.shape[0]
    alpha = (2 * depth) ** 0.25
    n_p, t_p, _ = x_prompt.shape
    n_s, t_s, _ = x_sample.shape
    assert t_s == 1 and t_p % TM == 0 and n_s % LANES == 0
    rows_p = n_p * t_p
    rows_all = rows_p + n_s
    past = page_table.shape[1] * PAGE_SIZE
    tab_p, tabt_p = _rope_tables(jnp.arange(t_p))
    tab_s, _ = _rope_tables(jnp.full((n_s,), past, jnp.int32))
    n_blocks = -(-(rows_all * TOP_K) // MOE_BLOCK) + N_EXPERTS

    h_p = x_prompt.reshape(rows_p, D_MODEL)
    h_s = x_sample.reshape(n_s, D_MODEL)
    ckv_p, kr_p, cv_p, ckv_s, kr_s, cv_s = [], [], [], [], [], []
    row = lambda v: v.reshape(1, -1)
    for l in range(depth):
        w1, wg, wuq, wuqt, wuk, wukt, wuvt, wbd, wr, br = _pack_weights(
            w_in[l], w_uq[l], w_uk[l], w_uv[l], router_w_group[l], router_b_group[l],
            router_w_expert[l], router_b_expert[l])
        qg, kvg = row(q_norm_g[l]), row(kv_norm_g[l])
        wba = w_br_attn[l].astype(BF16)
        merge_w = (wg, w_br_conv[l].astype(BF16), w_o[l].astype(BF16), row(ln1_g[l]), row(ln1_b[l]), wr, br)

        bconv, qt, k, vt, ckv, krt, nconv = _inproj_prompt(h_p, tab_p, tabt_p, w1, wuqt, wuk, wuvt, qg, kvg, conv_w[l],
                                                          n_p, t_p)
        ot = _prompt_attention(qt, k, vt, n_p, t_p)

        st = state_conv[l]
        bconv_s, q_s, qlat, ckvn_s, krn_s, u_s = _inproj_sample(
            h_s, tab_s, w1, wuq, wukt, qg, kvg, conv_w[l], st[:, 0], st[:, 1])
        olat = _sample_attention(page_table, qlat, q_s, ckvn_s, krn_s, cache_ckv[l],
                                 jnp.swapaxes(cache_krope[l], 1, 2))
        h1_s, route_s = _merge_sample(alpha, h_s, bconv_s, olat.reshape(n_s, N_HEADS * KV_LORA), wbd, wba, merge_w)
        h1t, route, route_t, counts = _merge_prompt(alpha, h_p, bconv, ot, h1_s, route_s, wba, merge_w)

        dest, fill_end, pend, blk_e, used, nvalid = _dispatch_plan(route_t, counts, n_blocks)
        slot_tok = _slot_tokens(dest, fill_end, pend, n_blocks)
        ys = _experts(blk_e, used, nvalid, slot_tok, h1t, w_gate[l], w_up[l], w_down[l])
        dest = dest.reshape(-1)
        ln2 = (row(ln2_g[l]), row(ln2_b[l]))
        h_p = _combine(alpha, 0, rows_p, TM_COMBINE, dest, ys, h1t, route, *ln2)
        h_s = _combine(alpha, rows_p, n_s, n_s, dest, ys, h1t, route, *ln2)

        ckv_p.append(ckv.reshape(n_p, t_p, KV_LORA))
        kr_p.append(jnp.swapaxes(krt, 1, 2))
        cv_p.append(nconv)
        ckv_s.append(ckvn_s.reshape(n_s, 1, KV_LORA))
        kr_s.append(krn_s.reshape(n_s, 1, ROPE_DIM))
        cv_s.append(jnp.stack([st[:, 1], u_s], axis=1))
    return (h_p.reshape(n_p, t_p, D_MODEL), h_s.reshape(n_s, 1, D_MODEL), jnp.stack(ckv_p), jnp.stack(kr_p),
            jnp.stack(cv_p), jnp.stack(ckv_s), jnp.stack(kr_s), jnp.stack(cv_s))
```

```python
import functools

import jax
import jax.numpy as jnp
from jax import lax
from jax.experimental import pallas as pl
from jax.experimental.pallas import tpu as pltpu

F32 = jnp.float32
BF16 = jnp.bfloat16

D_MODEL = 1024
D_CONV = 512
CONV_W = 3
N_HEADS = 8
Q_LORA = 384
KV_LORA = 256
NOPE_DIM = 64
ROPE_DIM = 32
ROPE_HALF = ROPE_DIM // 2
V_DIM = 64
QK_DIM = NOPE_DIM + ROPE_DIM
ROPE_BASE = 10000.0
ATTN_SCALE = QK_DIM ** -0.5
LOG2E = 1.4426950408889634
Q_SCALE = ATTN_SCALE * LOG2E
N_GROUPS = 4
EXP_PER_GROUP = 8
N_EXPERTS = N_GROUPS * EXP_PER_GROUP
TOP_K = 2
D_EXPERT = 512
MOE_BLOCK = 256
PAGE_SIZE = 128
LN_EPS = 1e-5
RMS_EPS = 1e-6

LANES = 128
SUBLANES = 8
ROW_TILE = D_MODEL // LANES
assert ROW_TILE == SUBLANES
HEAD_W = LANES
QK_W = N_HEADS * HEAD_W
V_W = N_HEADS * V_DIM
_C_BG, _C_CG, _C_H = 0, D_CONV, 2 * D_CONV
_C_CQ = 3 * D_CONV
_C_CKV = _C_CQ + Q_LORA
_C_KR = _C_CKV + KV_LORA
W1_COLS = _C_KR + LANES
ROUTE_W = 2 * LANES
R_EID, R_WT, R_RANK = 0, 2, 4
NEG = -1e30
VMEM_LIMIT = 56 * 1024 * 1024

TM = 512
TM_INPROJ = 1024
KEY_CHUNK = 2048
ATTN_HEADS_PER_LOOP = 8
TQ = 512
TM_COMBINE = 128
GATHER_RING = 4


def _cparams(*sem):
    return pltpu.CompilerParams(dimension_semantics=sem, vmem_limit_bytes=VMEM_LIMIT)


def _dot(a, b):
    return jnp.dot(a, b, preferred_element_type=F32)


def _dot_nt(a, b):
    return lax.dot_general(a, b, (((1,), (1,)), ((), ())), preferred_element_type=F32)


def _dot_tn(a, b):
    return lax.dot_general(a, b, (((0,), (0,)), ((), ())), preferred_element_type=F32)


def _rms(x, g):
    return x * lax.rsqrt(jnp.mean(x * x, axis=-1, keepdims=True) + RMS_EPS) * g


def _layernorm(x, g, b):
    mu = jnp.mean(x, axis=-1, keepdims=True)
    xc = x - mu
    var = jnp.mean(xc * xc, axis=-1, keepdims=True)
    return xc * lax.rsqrt(var + LN_EPS) * g + b


def _rope(xh, a, bm, cp):
    return xh * a + pltpu.roll(xh, LANES - ROPE_HALF, 1) * bm + pltpu.roll(xh, ROPE_HALF, 1) * cp


def _full(shape):
    nd = len(shape)
    return pl.BlockSpec(shape, lambda *_: (0,) * nd)


def _tile_rows(ref, c, n):
    return ref.at[pl.ds(c, n, stride=ROW_TILE), :]


def _inproj_common(x_ref, tab_ref, w1_ref, qg_ref, kvg_ref, ckv_ref):
    xb = x_ref[...].astype(BF16)

    def seg(lo, hi):
        return _dot(xb, w1_ref[:, lo:hi])

    b_g = seg(_C_BG, _C_CG)
    u = seg(_C_CG, _C_H) * seg(_C_H, _C_CQ)
    cqn = _rms(seg(_C_CQ, _C_CKV), qg_ref[...]).astype(BF16)
    ckvn = _rms(seg(_C_CKV, _C_KR), kvg_ref[...])
    ckv_ref[...] = ckvn
    krr = _rope(seg(_C_KR, W1_COLS), tab_ref[3], tab_ref[4], tab_ref[5])
    return b_g, u, cqn, ckvn.astype(BF16), krr


def _inproj_prompt_body(tiles_per_seq, x_ref, tab_ref, tabt_ref, w1_ref, wuqt_ref, wuk_ref, wuvt_ref, qg_ref, kvg_ref,
                        cw_ref, bconv_ref, qt_ref, k_ref, vt_ref, ckv_ref, krt_ref, nconv_ref, uext_ref):
    tm = x_ref.shape[0]
    b_g, u, cqn, cb, krr = _inproj_common(x_ref, tab_ref, w1_ref, qg_ref, kvg_ref, ckv_ref)
    krt_ref[0] = krr.T[0:ROPE_DIM, :]
    qft = _dot_nt(wuqt_ref[...], cqn)
    cos, sin = tabt_ref[0], tabt_ref[1]
    for h in range(N_HEADS):
        r0 = h * HEAD_W
        x1 = qft[r0:r0 + ROPE_HALF, :]
        x2 = qft[r0 + ROPE_HALF:r0 + ROPE_DIM, :]
        qt_ref[r0:r0 + ROPE_HALF, :] = (x1 * cos - x2 * sin).astype(BF16)
        qt_ref[r0 + ROPE_HALF:r0 + ROPE_DIM, :] = (x1 * sin + x2 * cos).astype(BF16)
        qt_ref[r0 + ROPE_DIM:r0 + HEAD_W, :] = (qft[r0 + ROPE_DIM:r0 + HEAD_W, :] * Q_SCALE).astype(BF16)
    kn = _dot(cb, wuk_ref[...])
    for h in range(N_HEADS):
        sl = slice(h * HEAD_W, (h + 1) * HEAD_W)
        k_ref[:, sl] = (kn[:, sl] + krr).astype(BF16)
    vt_ref[...] = _dot_nt(wuvt_ref[...], cb).astype(BF16)

    first = (pl.program_id(0) % tiles_per_seq) == 0

    @pl.when(first)
    def _():
        uext_ref[0:8, :] = jnp.zeros((8, D_CONV), F32)

    @pl.when(jnp.logical_not(first))
    def _():
        uext_ref[0:8, :] = uext_ref[tm:tm + 8, :]

    uext_ref[8:8 + tm, :] = u
    conv = cw_ref[0:1, :] * uext_ref[6:6 + tm, :] + cw_ref[1:2, :] * uext_ref[7:7 + tm, :] + cw_ref[2:3, :] * u
    bconv_ref[...] = (b_g * conv).astype(BF16)
    nconv_ref[0] = u[tm - (CONV_W - 1):tm, :]


def _inproj_sample_body(x_ref, tab_ref, w1_ref, wuq_ref, wukt_ref, qg_ref, kvg_ref, cw_ref, s0_ref, s1_ref,
                        bconv_ref, q_ref, qlat_ref, ckv_ref, kr_ref, u_ref):
    b_g, u, cqn, _, krr = _inproj_common(x_ref, tab_ref, w1_ref, qg_ref, kvg_ref, ckv_ref)
    kr_ref[...] = krr[:, 0:ROPE_DIM]
    qf = _dot(cqn, wuq_ref[...])
    for h in range(N_HEADS):
        sl = slice(h * HEAD_W, (h + 1) * HEAD_W)
        qh = _rope(qf[:, sl], tab_ref[0], tab_ref[1], tab_ref[2]).astype(BF16)
        q_ref[:, sl] = qh
        qlat_ref[:, h * KV_LORA:(h + 1) * KV_LORA] = _dot(qh, wukt_ref[h]).astype(BF16)
    conv = cw_ref[0:1, :] * s0_ref[...] + cw_ref[1:2, :] * s1_ref[...] + cw_ref[2:3, :] * u
    bconv_ref[...] = (b_g * conv).astype(BF16)
    u_ref[...] = u


def _inproj_prompt(x, tab, tabt, w1, wuqt, wuk, wuvt, qg, kvg, cw, n_seq, seq):
    t = x.shape[0]
    tm = TM_INPROJ
    assert seq % tm == 0
    tiles_per_seq = seq // tm
    rows = lambda w: pl.BlockSpec((tm, w), lambda i: (i, 0))
    cols = lambda h: pl.BlockSpec((h, tm), lambda i: (0, i))
    return pl.pallas_call(
        functools.partial(_inproj_prompt_body, tiles_per_seq),
        grid=(t // tm,),
        in_specs=[rows(D_MODEL),
                  pl.BlockSpec((6, tm, LANES), lambda i: (0, i % tiles_per_seq, 0)),
                  pl.BlockSpec((2, ROPE_HALF, tm), lambda i: (0, 0, i % tiles_per_seq)),
                  _full(w1.shape), _full(wuqt.shape), _full(wuk.shape), _full(wuvt.shape),
                  _full(qg.shape), _full(kvg.shape), _full(cw.shape)],
        out_specs=[rows(D_CONV), cols(QK_W), rows(QK_W), cols(V_W), rows(KV_LORA),
                   pl.BlockSpec((1, ROPE_DIM, tm), lambda i: (i // tiles_per_seq, 0, i % tiles_per_seq)),
                   pl.BlockSpec((1, CONV_W - 1, D_CONV), lambda i: (i // tiles_per_seq, 0, 0))],
        out_shape=[jax.ShapeDtypeStruct((t, D_CONV), BF16), jax.ShapeDtypeStruct((QK_W, t), BF16),
                   jax.ShapeDtypeStruct((t, QK_W), BF16), jax.ShapeDtypeStruct((V_W, t), BF16),
                   jax.ShapeDtypeStruct((t, KV_LORA), F32), jax.ShapeDtypeStruct((n_seq, ROPE_DIM, seq), F32),
                   jax.ShapeDtypeStruct((n_seq, CONV_W - 1, D_CONV), F32)],
        scratch_shapes=[pltpu.VMEM((tm + 8, D_CONV), F32)],
        compiler_params=_cparams("arbitrary"),
    )(x, tab, tabt, w1, wuqt, wuk, wuvt, qg, kvg, cw)


def _inproj_sample(x, tab, w1, wuq, wukt, qg, kvg, cw, s0, s1):
    n = x.shape[0]
    args = (x, tab, w1, wuq, wukt, qg, kvg, cw, s0, s1)
    out_shape = [jax.ShapeDtypeStruct((n, D_CONV), BF16), jax.ShapeDtypeStruct((n, QK_W), BF16),
                 jax.ShapeDtypeStruct((n, N_HEADS * KV_LORA), BF16), jax.ShapeDtypeStruct((n, KV_LORA), F32),
                 jax.ShapeDtypeStruct((n, ROPE_DIM), F32), jax.ShapeDtypeStruct((n, D_CONV), F32)]
    return pl.pallas_call(
        _inproj_sample_body,
        grid=(1,),
        in_specs=[_full(a.shape) for a in args],
        out_specs=[_full(s.shape) for s in out_shape],
        out_shape=out_shape,
        compiler_params=_cparams("arbitrary"),
    )(*args)


def _prompt_attention_body(qt_ref, k_ref, vt_ref, ot_ref, acc_ref, m_ref, l_ref, sa_ref, sb_ref):
    tq = qt_ref.shape[1]
    i = pl.program_id(1)

    for hg in range(N_HEADS // ATTN_HEADS_PER_LOOP):
        heads = tuple(range(hg * ATTN_HEADS_PER_LOOP, (hg + 1) * ATTN_HEADS_PER_LOOP))
        acc_ref[...] = jnp.zeros(acc_ref.shape, F32)
        m_ref[...] = jnp.full(m_ref.shape, NEG, F32)
        l_ref[...] = jnp.zeros(l_ref.shape, F32)

        def scores(j, s_ref):
            off = pl.multiple_of(j * tq, tq)
            for idx, h in enumerate(heads):
                s_ref[idx] = _dot(k_ref[pl.ds(off, tq), h * HEAD_W:(h + 1) * HEAD_W],
                                  qt_ref[h * HEAD_W:(h + 1) * HEAD_W, :])

        def consume(j, s_ref, masked):
            off = pl.multiple_of(j * tq, tq)
            for idx, h in enumerate(heads):
                s = s_ref[idx]
                if masked:
                    krow = lax.broadcasted_iota(jnp.int32, (tq, tq), 0)
                    qcol = lax.broadcasted_iota(jnp.int32, (tq, tq), 1)
                    s = jnp.where(krow <= qcol, s, NEG)
                m = m_ref[idx]
                m_new = jnp.maximum(m, jnp.max(s, axis=0, keepdims=True))
                alpha = jnp.exp2(m - m_new)
                p = jnp.exp2(s - m_new)
                m_ref[idx] = m_new
                l_ref[idx] = alpha * l_ref[idx] + jnp.sum(p, axis=0, keepdims=True)
                vblk = vt_ref[h * V_DIM:(h + 1) * V_DIM, pl.ds(off, tq)]
                acc_ref[idx] = alpha * acc_ref[idx] + _dot(vblk, p.astype(BF16))

        scores(0, sa_ref)

        def pair(pp, c):
            j = 2 * pp
            scores(j + 1, sb_ref)
            consume(j, sa_ref, False)
            scores(j + 2, sa_ref)
            consume(j + 1, sb_ref, False)
            return c
        lax.fori_loop(0, i // 2, pair, 0)

        @pl.when(i % 2 == 1)
        def _():
            scores(i, sb_ref)
            consume(i - 1, sa_ref, False)
            consume(i, sb_ref, True)

        @pl.when(i % 2 == 0)
        def _():
            consume(i, sa_ref, True)

        for idx, h in enumerate(heads):
            ot_ref[h * V_DIM:(h + 1) * V_DIM, :] = (acc_ref[idx] / l_ref[idx]).astype(BF16)


def _prompt_attention(qt, k, vt, n_seq, seq):
    t = k.shape[0]
    tq = TQ
    nq = seq // tq
    return pl.pallas_call(
        _prompt_attention_body,
        grid=(n_seq, nq),
        in_specs=[pl.BlockSpec((QK_W, tq), lambda b, i: (0, b * nq + i)),
                  pl.BlockSpec((seq, QK_W), lambda b, i: (b, 0)),
                  pl.BlockSpec((V_W, seq), lambda b, i: (0, b))],
        out_specs=pl.BlockSpec((V_W, tq), lambda b, i: (0, b * nq + i)),
        out_shape=jax.ShapeDtypeStruct((V_W, t), BF16),
        scratch_shapes=[pltpu.VMEM((ATTN_HEADS_PER_LOOP, V_DIM, tq), F32),
                        pltpu.VMEM((ATTN_HEADS_PER_LOOP, 1, tq), F32), pltpu.VMEM((ATTN_HEADS_PER_LOOP, 1, tq), F32),
                        pltpu.VMEM((ATTN_HEADS_PER_LOOP, tq, tq), F32), pltpu.VMEM((ATTN_HEADS_PER_LOOP, tq, tq), F32)],
        compiler_params=_cparams("arbitrary", "arbitrary"),
    )(qt, k, vt)


def _sample_attention_body(n_pages, pt_ref, qlat_ref, q_ref, ckvn_ref, krn_ref, cc_ref, cr_ref, olat_ref,
                           bufc, bufr, kcb, krb, sem):
    b = pl.program_id(0)
    nb = pl.num_programs(0)
    slot = b % 2
    past = n_pages * PAGE_SIZE

    def page_copies(bb, sl, p):
        pg = pt_ref[bb, p]
        pos = pl.ds(pl.multiple_of(p * PAGE_SIZE, PAGE_SIZE), PAGE_SIZE)
        return (pltpu.make_async_copy(cc_ref.at[pg], bufc.at[sl, pos], sem.at[0, sl]),
                pltpu.make_async_copy(cr_ref.at[pg], bufr.at[sl, :, pos], sem.at[1, sl]))

    def issue(bb, sl):
        for p in range(n_pages):
            for c in page_copies(bb, sl, p):
                c.start(priority=p % 2)

    def wait(sl):
        pltpu.make_async_copy(bufc.at[sl], bufc.at[sl], sem.at[0, sl]).wait()
        pltpu.make_async_copy(bufr.at[sl], bufr.at[sl], sem.at[1, sl]).wait()

    @pl.when(b == 0)
    def _():
        issue(0, 0)

    wait(slot)
    issue(jnp.minimum(b + 1, nb - 1), 1 - slot)

    ql = qlat_ref[0]
    qr = q_ref[0][:, 0:ROPE_DIM]
    cn = ckvn_ref[0]
    rn = krn_ref[0]
    chunk = min(KEY_CHUNK, past)
    scores = []
    for c in range(past // chunk):
        pos = slice(c * chunk, (c + 1) * chunk)
        kc = bufc[slot, pos, :]
        kcb[pos, :] = kc.astype(BF16)
        krb[:, pos] = bufr[slot, :, pos].astype(BF16)
        scores.append(_dot(ql, kc.T.astype(BF16)) + _dot(qr, krb[:, pos]))
    s_new = (jnp.sum(ql.astype(F32) * cn, axis=-1, keepdims=True)
             + jnp.sum(qr.astype(F32) * rn, axis=-1, keepdims=True))
    m = s_new
    for s in scores:
        m = jnp.maximum(m, jnp.max(s, axis=-1, keepdims=True))
    p_new = jnp.exp2(s_new - m)
    l = p_new
    o = p_new * cn
    for c, s in enumerate(scores):
        p = jnp.exp2(s - m)
        l = l + jnp.sum(p, axis=-1, keepdims=True)
        o = o + _dot(p.astype(BF16), kcb[c * chunk:(c + 1) * chunk, :])
    olat_ref[0] = o / l

    @pl.when(b == nb - 1)
    def _():
        wait(1 - slot)


def _sample_attention(page_table, qlat, q, ckvn, krn, cache_c, cache_rt):
    n, n_pages = page_table.shape
    past = n_pages * PAGE_SIZE
    blk = lambda d1, d2: pl.BlockSpec((1, d1, d2), lambda b, pt: (b, 0, 0))
    grid_spec = pltpu.PrefetchScalarGridSpec(
        num_scalar_prefetch=1,
        grid=(n,),
        in_specs=[blk(N_HEADS, KV_LORA), blk(N_HEADS, HEAD_W), blk(1, KV_LORA), blk(1, ROPE_DIM),
                  pl.BlockSpec(memory_space=pl.ANY), pl.BlockSpec(memory_space=pl.ANY)],
        out_specs=blk(N_HEADS, KV_LORA),
        scratch_shapes=[pltpu.VMEM((2, past, KV_LORA), F32), pltpu.VMEM((2, ROPE_DIM, past), F32),
                        pltpu.VMEM((past, KV_LORA), BF16), pltpu.VMEM((ROPE_DIM, past), BF16),
                        pltpu.SemaphoreType.DMA((2, 2))],
    )
    return pl.pallas_call(
        functools.partial(_sample_attention_body, n_pages),
        grid_spec=grid_spec,
        out_shape=jax.ShapeDtypeStruct((n, N_HEADS, KV_LORA), F32),
        compiler_params=_cparams("arbitrary"),
    )(page_table, qlat.reshape(n, N_HEADS, KV_LORA), q.reshape(n, N_HEADS, HEAD_W),
      ckvn.reshape(n, 1, KV_LORA), krn.reshape(n, 1, ROPE_DIM), cache_c, cache_rt)


def _route(logits):
    lane_i = lax.broadcasted_iota(jnp.int32, (logits.shape[0], LANES), 1)
    lane = lane_i.astype(F32)
    first_at = lambda hit: jnp.min(jnp.where(hit, lane, float(LANES)), axis=-1, keepdims=True)
    lg = jnp.where(lane_i < N_GROUPS, logits[:, 0:LANES], NEG)
    mg = jnp.max(lg, axis=-1, keepdims=True)
    p_grp = 1.0 / jnp.sum(jnp.exp(lg - mg), axis=-1, keepdims=True)
    grp = first_at(lg == mg)
    lane_grp = (lane_i // EXP_PER_GROUP).astype(F32)
    le = jnp.where(lane_grp == grp, logits[:, LANES:2 * LANES], NEG)
    top1 = jnp.max(le, axis=-1, keepdims=True)
    i1 = first_at(le == top1)
    le2 = jnp.where(lane == i1, NEG, le)
    top2 = jnp.max(le2, axis=-1, keepdims=True)
    i2 = first_at(le2 == top2)
    e2 = jnp.exp(top2 - top1)
    w1 = p_grp / (1.0 + e2)
    w2 = p_grp * e2 / (1.0 + e2)
    out = jnp.where(lane_i == R_EID, i1, 0.0)
    out = jnp.where(lane_i == R_EID + 1, i2, out)
    out = jnp.where(lane_i == R_WT, w1, out)
    return jnp.where(lane_i == R_WT + 1, w2, out)


def _add_ranks(route, ltri, count_ref):
    n = route.shape[0]
    lane_i = lax.broadcasted_iota(jnp.int32, (n, LANES), 1)
    lane = lane_i.astype(F32)
    hit1 = lane == route[:, R_EID:R_EID + 1]
    hit2 = lane == route[:, R_EID + 1:R_EID + 2]
    chosen = jnp.where(hit1, 1.0, jnp.where(hit2, 1.0, 0.0))
    before = _dot(ltri, chosen.astype(BF16)) + count_ref[...]
    r1 = jnp.sum(jnp.where(hit1, before, 0.0), axis=-1, keepdims=True)
    r2 = jnp.sum(jnp.where(hit2, before, 0.0), axis=-1, keepdims=True)
    count_ref[...] = count_ref[...] + jnp.sum(chosen, axis=0, keepdims=True)
    route = jnp.where(lane_i == R_RANK, r1, route)
    return jnp.where(lane_i == R_RANK + 1, r2, route)


def _merge_rows(alpha, x, bconv, y_attn, wg_ref, wbc_ref, wo_ref, lg_ref, lb_ref, wr_ref, br_ref):
    g = _dot(x.astype(BF16), wg_ref[...])
    y_conv = _dot(bconv, wbc_ref[...])
    m = jax.nn.sigmoid(g[:, 0:D_MODEL]) * y_conv + jax.nn.sigmoid(g[:, D_MODEL:]) * y_attn
    mix = _dot(m.astype(BF16), wo_ref[...])
    h1 = _layernorm(alpha * x + mix, lg_ref[...], lb_ref[...])
    return h1, _route(_dot(h1.astype(BF16), wr_ref[...]) + br_ref[...])


def _merge_sample_body(alpha, x_ref, bconv_ref, olat_ref, wbd_ref, wba_ref, *refs):
    w_refs, (h1_ref, route_ref) = refs[:-2], refs[-2:]
    o = _dot(olat_ref[...].astype(BF16), wbd_ref[...]).astype(BF16)
    h1_ref[...], route_ref[...] = _merge_rows(alpha, x_ref[...], bconv_ref[...], _dot(o, wba_ref[...]), *w_refs)


def _merge_prompt_body(alpha, n_tiles, x_ref, bconv_ref, ot_ref, h1s_ref, routes_ref, wba_ref, *refs):
    w_refs, (h1t_ref, route_ref, routet_ref, counts_ref, ltri_ref, count_ref) = refs[:-6], refs[-6:]
    i = pl.program_id(0)
    tm = x_ref.shape[0]

    @pl.when(i == 0)
    def _():
        r = lax.broadcasted_iota(jnp.int32, (tm, tm), 0)
        c = lax.broadcasted_iota(jnp.int32, (tm, tm), 1)
        ltri_ref[...] = jnp.where(c < r, 1.0, 0.0).astype(BF16)
        count_ref[...] = jnp.zeros(count_ref.shape, F32)

    @pl.when(i < n_tiles)
    def _():
        y_attn = _dot_tn(ot_ref[...], wba_ref[...])
        h1, route = _merge_rows(alpha, x_ref[...], bconv_ref[...], y_attn, *w_refs)
        for c in range(ROW_TILE):
            _tile_rows(h1t_ref, c, tm)[...] = h1[:, c * LANES:(c + 1) * LANES]
        route_ref[...] = _add_ranks(route, ltri_ref[...], count_ref)

    @pl.when(i == n_tiles)
    def _():
        n_s = h1s_ref.shape[0]
        h1t_ref[...] = jnp.zeros(h1t_ref.shape, F32)
        route_ref[...] = jnp.zeros(route_ref.shape, F32)
        for c in range(ROW_TILE):
            _tile_rows(h1t_ref, c, n_s)[...] = h1s_ref[:, c * LANES:(c + 1) * LANES]
        route_ref[0:n_s, :] = _add_ranks(routes_ref[...], ltri_ref[0:n_s, 0:n_s], count_ref)

    counts_ref[...] = jnp.broadcast_to(count_ref[...], counts_ref.shape)
    routet_ref[...] = route_ref[...].T[0:SUBLANES, :]


def _merge_sample(alpha, x, bconv, olat, wbd, wba, ws):
    n = x.shape[0]
    args = (x, bconv, olat, wbd, wba) + tuple(ws)
    out_shape = [jax.ShapeDtypeStruct((n, D_MODEL), F32), jax.ShapeDtypeStruct((n, LANES), F32)]
    return pl.pallas_call(
        functools.partial(_merge_sample_body, alpha),
        grid=(1,),
        in_specs=[_full(a.shape) for a in args],
        out_specs=[_full(s.shape) for s in out_shape],
        out_shape=out_shape,
        compiler_params=_cparams("arbitrary"),
    )(*args)


def _merge_prompt(alpha, x, bconv, ot, h1_s, route_s, wba, ws):
    t = x.shape[0]
    tm = TM
    n_tiles = t // tm
    t_all = t + h1_s.shape[0]
    assert h1_s.shape[0] <= tm
    clamp = lambda i: jnp.minimum(i, n_tiles - 1)
    rows_in = lambda w: pl.BlockSpec((tm, w), lambda i: (clamp(i), 0))
    return pl.pallas_call(
        functools.partial(_merge_prompt_body, alpha, n_tiles),
        grid=(n_tiles + 1,),
        in_specs=[rows_in(D_MODEL), rows_in(D_CONV), pl.BlockSpec((V_W, tm), lambda i: (0, clamp(i))),
                  _full(h1_s.shape), _full(route_s.shape), _full(wba.shape)] + [_full(w.shape) for w in ws],
        out_specs=[pl.BlockSpec((tm * ROW_TILE, LANES), lambda i: (i, 0)), pl.BlockSpec((tm, LANES), lambda i: (i, 0)),
                   pl.BlockSpec((SUBLANES, tm), lambda i: (0, i)), _full((SUBLANES, LANES))],
        out_shape=[jax.ShapeDtypeStruct((t_all * ROW_TILE, LANES), F32), jax.ShapeDtypeStruct((t_all, LANES), F32),
                   jax.ShapeDtypeStruct((SUBLANES, t_all), F32), jax.ShapeDtypeStruct((SUBLANES, LANES), F32)],
        scratch_shapes=[pltpu.VMEM((tm, tm), BF16), pltpu.VMEM((1, LANES), F32)],
        compiler_params=_cparams("arbitrary"),
    )(x, bconv, ot, h1_s, route_s, wba, *ws)


SLOT_GROUP = 16


def _slot_tokens_body(rows_ref, dest_ref, tok_ref):
    def clear_row(j, c):
        for l in range(LANES):
            tok_ref[j * LANES + l] = 0
        return c
    lax.fori_loop(0, rows_ref[1], clear_row, 0)

    rows_k = rows_ref[0]
    for k in range(TOP_K):
        def fill_row(j, c):
            for g in range(0, LANES, SLOT_GROUP):
                ds = [dest_ref[k * rows_k + j, g + l] for l in range(SLOT_GROUP)]
                for l, d in enumerate(ds):
                    tok_ref[d] = j * LANES + g + l
            return c
        lax.fori_loop(0, rows_k, fill_row, 0)


def _slot_tokens(dest, n_blocks):
    n_slots = n_blocks * MOE_BLOCK
    n_tok = dest.shape[1]
    assert n_slots % LANES == 0 and n_tok % LANES == 0
    dest = dest.reshape(-1, LANES)
    grid_spec = pltpu.PrefetchScalarGridSpec(
        num_scalar_prefetch=2,
        grid=(1,),
        in_specs=[],
        out_specs=pl.BlockSpec(memory_space=pltpu.SMEM),
    )
    return pl.pallas_call(
        _slot_tokens_body,
        grid_spec=grid_spec,
        out_shape=jax.ShapeDtypeStruct((n_slots,), jnp.int32),
        compiler_params=_cparams("arbitrary"),
    )(jnp.array([n_tok // LANES, n_slots // LANES], jnp.int32), dest)


def _row_tile(i):
    return pl.ds(pl.multiple_of(i * ROW_TILE, ROW_TILE), ROW_TILE)


def _experts_body(blk_e_ref, used_ref, nvalid_ref, tok_ref, h1t_ref, wg_ref, wu_ref, wd_ref, ys_ref,
                  xbuf, xb, wgb, wub, wdb, sem_in):
    b = pl.program_id(0)
    ring = lambda bb: lax.rem(bb, GATHER_RING)
    used = used_ref[0]
    blk_rows = MOE_BLOCK * ROW_TILE
    changed = jnp.logical_or(b == 0, blk_e_ref[b] != blk_e_ref[jnp.maximum(b - 1, 0)])
    last = used - 1
    clamp = lambda v: jnp.minimum(v, last)

    def in_copy(bb, sl, r):
        tok = tok_ref[bb * MOE_BLOCK + r]
        return pltpu.make_async_copy(h1t_ref.at[_row_tile(tok)], xbuf.at[sl, _row_tile(r)], sem_in.at[sl])

    def issue_full(v):
        bits = MOE_BLOCK.bit_length() - 1
        for i in range(MOE_BLOCK):
            r = int(format(i, f"0{bits}b")[::-1], 2)
            in_copy(clamp(v), ring(v), r).start(priority=1)

    def issue_partial(v):
        lax.fori_loop(0, nvalid_ref[clamp(v)], lambda r, c: (in_copy(clamp(v), ring(v), r).start(), c)[1], 0)

    def is_full(v):
        return nvalid_ref[clamp(v)] == MOE_BLOCK

    def wait_in(v):
        sl = ring(v)

        @pl.when(is_full(v))
        def _():
            pltpu.make_async_copy(h1t_ref.at[pl.ds(0, blk_rows)], xbuf.at[sl], sem_in.at[sl]).wait()

        @pl.when(jnp.logical_not(is_full(v)))
        def _():
            lax.fori_loop(0, nvalid_ref[clamp(v)], lambda r, c: (in_copy(clamp(v), sl, r).wait(), c)[1], 0)

    @pl.when(jnp.logical_and(b == 0, used > 0))
    def _():
        xbuf[...] = jnp.zeros(xbuf.shape, F32)
        for a in range(GATHER_RING - 1):
            @pl.when(is_full(a))
            def _():
                issue_full(a)

            @pl.when(jnp.logical_not(is_full(a)))
            def _():
                issue_partial(a)

    @pl.when(jnp.logical_and(b < used, changed))
    def _():
        wgb[...] = wg_ref[0].astype(BF16)
        wub[...] = wu_ref[0].astype(BF16)
        wdb[...] = wd_ref[0].astype(BF16)

    ahead = b + GATHER_RING - 1

    def block(issue_ahead):
        for c in range(ROW_TILE):
            xb[:, c * LANES:(c + 1) * LANES] = _tile_rows(xbuf.at[ring(b)], c, MOE_BLOCK)[...].astype(BF16)
        if issue_ahead:
            issue_full(ahead)
        x = xb[...]
        g = _dot(x, wgb[...])
        u = _dot(x, wub[...])
        h = (g * jax.nn.sigmoid(g) * u).astype(BF16)
        y = _dot(h, wdb[...])
        for c in range(ROW_TILE):
            _tile_rows(ys_ref, c, MOE_BLOCK)[...] = y[:, c * LANES:(c + 1) * LANES]

    @pl.when(b < used)
    def _():
        wait_in(b)

    @pl.when(jnp.logical_and(b < used, is_full(ahead)))
    def _():
        block(True)

    @pl.when(jnp.logical_and(b < used, jnp.logical_not(is_full(ahead))))
    def _():
        block(False)
        issue_partial(ahead)

    @pl.when(b == last)
    def _():
        for a in range(1, GATHER_RING):
            wait_in(b + a)

    @pl.when(b >= used)
    def _():
        ys_ref[...] = jnp.zeros(ys_ref.shape, F32)


def _experts(blk_e, used, nvalid, slot_tok, h1t, wg, wu, wd):
    n_blocks = blk_e.shape[0]
    blk_rows = MOE_BLOCK * ROW_TILE
    by_expert = lambda b, e, u, n, a: (e[b], 0, 0)
    grid_spec = pltpu.PrefetchScalarGridSpec(
        num_scalar_prefetch=4,
        grid=(n_blocks,),
        in_specs=[pl.BlockSpec(memory_space=pl.ANY),
                  pl.BlockSpec((1, D_MODEL, D_EXPERT), by_expert), pl.BlockSpec((1, D_MODEL, D_EXPERT), by_expert),
                  pl.BlockSpec((1, D_EXPERT, D_MODEL), by_expert)],
        out_specs=pl.BlockSpec((blk_rows, LANES), lambda b, e, u, n, a: (b, 0)),
        scratch_shapes=[pltpu.VMEM((GATHER_RING, blk_rows, LANES), F32),
                        pltpu.VMEM((MOE_BLOCK, D_MODEL), BF16),
                        pltpu.VMEM((D_MODEL, D_EXPERT), BF16), pltpu.VMEM((D_MODEL, D_EXPERT), BF16),
                        pltpu.VMEM((D_EXPERT, D_MODEL), BF16),
                        pltpu.SemaphoreType.DMA((GATHER_RING,))],
    )
    return pl.pallas_call(
        _experts_body,
        grid_spec=grid_spec,
        out_shape=jax.ShapeDtypeStruct((n_blocks * blk_rows, LANES), F32),
        compiler_params=_cparams("arbitrary"),
    )(blk_e, used, nvalid, slot_tok, h1t, wg, wu, wd)


def _combine_body(alpha, tok0, n_all, dest_ref, ys_ref, h1t_ref, route_ref, lg_ref, lb_ref, out_ref, gbuf, sem):
    i = pl.program_id(0)
    nt = pl.num_programs(0)
    slot = i % 2
    tm = out_ref.shape[0]

    def issue(ii, sl):
        base = tok0 + ii * tm
        for r in range(tm):
            for k in range(TOP_K):
                d = dest_ref[base + k * n_all + r]
                pltpu.make_async_copy(ys_ref.at[_row_tile(d)], gbuf.at[sl, k, _row_tile(r)],
                                      sem.at[sl]).start(priority=k)

    def wait(sl):
        for k in range(TOP_K):
            pltpu.make_async_copy(ys_ref.at[pl.ds(0, tm * ROW_TILE)], gbuf.at[sl, k], sem.at[sl]).wait()

    @pl.when(i == 0)
    def _():
        issue(0, 0)

    wait(slot)
    issue(jnp.minimum(i + 1, nt - 1), 1 - slot)

    route = route_ref[...]
    w0 = jnp.broadcast_to(route[:, R_WT:R_WT + 1], (tm, LANES))
    w1 = jnp.broadcast_to(route[:, R_WT + 1:R_WT + 2], (tm, LANES))
    z = [alpha * _tile_rows(h1t_ref, c, tm)[...]
         + w0 * _tile_rows(gbuf.at[slot, 0], c, tm)[...] + w1 * _tile_rows(gbuf.at[slot, 1], c, tm)[...]
         for c in range(ROW_TILE)]
    mu = sum(jnp.sum(zc, axis=-1, keepdims=True) for zc in z) / D_MODEL
    zc = [v - mu for v in z]
    var = sum(jnp.sum(v * v, axis=-1, keepdims=True) for v in zc) / D_MODEL
    rstd = lax.rsqrt(var + LN_EPS)
    for c in range(ROW_TILE):
        sl = slice(c * LANES, (c + 1) * LANES)
        out_ref[:, sl] = zc[c] * rstd * lg_ref[:, sl] + lb_ref[:, sl]

    @pl.when(i == nt - 1)
    def _():
        wait(1 - slot)


def _combine(alpha, tok0, n_tok, tm, dest, ys, h1t, route, lg, lb):
    assert n_tok % tm == 0 and tok0 % tm == 0
    blk0 = tok0 // tm
    grid_spec = pltpu.PrefetchScalarGridSpec(
        num_scalar_prefetch=1,
        grid=(n_tok // tm,),
        in_specs=[pl.BlockSpec(memory_space=pl.ANY),
                  pl.BlockSpec((tm * ROW_TILE, LANES), lambda i, d: (blk0 + i, 0)),
                  pl.BlockSpec((tm, LANES), lambda i, d: (blk0 + i, 0)),
                  pl.BlockSpec(lg.shape, lambda i, d: (0, 0)), pl.BlockSpec(lb.shape, lambda i, d: (0, 0))],
        out_specs=pl.BlockSpec((tm, D_MODEL), lambda i, d: (i, 0)),
        scratch_shapes=[pltpu.VMEM((2, TOP_K, tm * ROW_TILE, LANES), F32), pltpu.SemaphoreType.DMA((2,))],
    )
    return pl.pallas_call(
        functools.partial(_combine_body, alpha, tok0, route.shape[0]),
        grid_spec=grid_spec,
        out_shape=jax.ShapeDtypeStruct((n_tok, D_MODEL), F32),
        compiler_params=_cparams("arbitrary"),
    )(dest, ys, h1t, route, lg, lb)


def _rope_tables(pos):
    inv = ROPE_BASE ** (-(jnp.arange(ROPE_HALF, dtype=F32) * 2.0 / ROPE_DIM))
    ang = pos.astype(F32)[:, None] * inv[None, :]
    cos, sin = jnp.cos(ang), jnp.sin(ang)
    n = pos.shape[0]
    a = jnp.concatenate([cos, cos, jnp.ones((n, LANES - ROPE_DIM), F32)], axis=1)
    bm = jnp.concatenate([-sin, jnp.zeros((n, LANES - ROPE_HALF), F32)], axis=1)
    cp = jnp.concatenate([jnp.zeros((n, ROPE_HALF), F32), sin, jnp.zeros((n, LANES - ROPE_DIM), F32)], axis=1)
    k_tabs = jnp.stack([a, bm, cp])
    return jnp.concatenate([k_tabs * Q_SCALE, k_tabs]), jnp.stack([cos.T, sin.T]) * Q_SCALE


def _head_blocks(w_rope, w_nope):
    k = w_nope.shape[0]
    pad = jnp.zeros((k, N_HEADS, HEAD_W - QK_DIM), w_nope.dtype)
    return jnp.concatenate([w_rope, w_nope, pad], axis=-1).reshape(k, QK_W)


def _pack_weights(w_in, w_uq, w_uk, w_uv, router_w_group, router_b_group, router_w_expert, router_b_expert):
    d = w_in.shape[0]
    c_kr = 3 * D_CONV + Q_LORA + KV_LORA
    w1 = jnp.concatenate([w_in[:, :c_kr], w_in[:, c_kr:c_kr + ROPE_DIM],
                          jnp.zeros((d, LANES - ROPE_DIM), w_in.dtype)], axis=1).astype(BF16)
    wg = w_in[:, c_kr + ROPE_DIM:].astype(BF16)
    uq = w_uq.reshape(Q_LORA, N_HEADS, QK_DIM)
    wuq = _head_blocks(uq[..., NOPE_DIM:], uq[..., :NOPE_DIM]).astype(BF16)
    wuk = _head_blocks(jnp.zeros((KV_LORA, N_HEADS, ROPE_DIM), w_uk.dtype), w_uk).astype(BF16)
    wukt = jnp.transpose(wuk.reshape(KV_LORA, N_HEADS, HEAD_W), (1, 2, 0))
    wuvt = w_uv.reshape(KV_LORA, V_W).T.astype(BF16)
    eye = jnp.eye(N_HEADS, dtype=w_uv.dtype)
    wbd = jnp.einsum('chd,hg->hcgd', w_uv, eye).reshape(N_HEADS * KV_LORA, V_W).astype(BF16)
    wr = jnp.zeros((d, ROUTE_W), F32)
    wr = wr.at[:, 0:N_GROUPS].set(router_w_group).at[:, LANES:LANES + N_EXPERTS].set(router_w_expert).astype(BF16)
    br = jnp.zeros((1, ROUTE_W), F32)
    br = br.at[0, 0:N_GROUPS].set(router_b_group).at[0, LANES:LANES + N_EXPERTS].set(router_b_expert)
    return w1, wg, wuq, wuq.T, wuk, wukt, wuvt, wbd, wr, br


def _dispatch_plan(route_t, counts, n_blocks):
    counts = counts[0, 0:N_EXPERTS].astype(jnp.int32)
    pcounts = (counts + MOE_BLOCK - 1) // MOE_BLOCK * MOE_BLOCK
    pend = jnp.cumsum(pcounts).astype(jnp.int32)
    pstart = (pend - pcounts).astype(F32)
    eid = route_t[R_EID:R_EID + TOP_K]
    rank = route_t[R_RANK:R_RANK + TOP_K]
    base = jnp.zeros_like(eid)
    for e in range(N_EXPERTS):
        base = jnp.where(eid == float(e), pstart[e], base)
    dest = (base + rank).astype(jnp.int32)
    blk_start = jnp.arange(n_blocks, dtype=jnp.int32) * MOE_BLOCK
    blk_e = jnp.sum((blk_start[:, None] >= pend[None, :]).astype(jnp.int32), axis=1)
    blk_e = jnp.minimum(blk_e, N_EXPERTS - 1).astype(jnp.int32)
    used = (pend[-1:] // MOE_BLOCK).astype(jnp.int32)
    filled_end = (pend - pcounts + counts)[blk_e]
    nvalid = jnp.where(blk_start < pend[-1], jnp.clip(filled_end - blk_start, 0, MOE_BLOCK), 0).astype(jnp.int32)
    return dest, blk_e, used, nvalid


def kernel(x_prompt, x_sample, cache_ckv, cache_krope, state_conv, page_table, w_in, conv_w, q_norm_g, w_uq,
           kv_norm_g, w_uk, w_uv, w_br_conv, w_br_attn, w_o, ln1_g, ln1_b, router_w_group, router_b_group,
           router_w_expert, router_b_expert, w_gate, w_up, w_down, ln2_g, ln2_b):
    depth = w_in.shape[0]
    alpha = (2 * depth) ** 0.25
    n_p, t_p, _ = x_prompt.shape
    n_s, t_s, _ = x_sample.shape
    assert t_s == 1 and t_p % TM == 0 and n_s % LANES == 0
    rows_p = n_p * t_p
    rows_all = rows_p + n_s
    past = page_table.shape[1] * PAGE_SIZE
    tab_p, tabt_p = _rope_tables(jnp.arange(t_p))
    tab_s, _ = _rope_tables(jnp.full((n_s,), past, jnp.int32))
    n_blocks = -(-(rows_all * TOP_K) // MOE_BLOCK) + N_EXPERTS

    h_p = x_prompt.reshape(rows_p, D_MODEL)
    h_s = x_sample.reshape(n_s, D_MODEL)
    ckv_p, kr_p, cv_p, ckv_s, kr_s, cv_s = [], [], [], [], [], []
    row = lambda v: v.reshape(1, -1)
    for l in range(depth):
        w1, wg, wuq, wuqt, wuk, wukt, wuvt, wbd, wr, br = _pack_weights(
            w_in[l], w_uq[l], w_uk[l], w_uv[l], router_w_group[l], router_b_group[l],
            router_w_expert[l], router_b_expert[l])
        qg, kvg = row(q_norm_g[l]), row(kv_norm_g[l])
        wba = w_br_attn[l].astype(BF16)
        merge_w = (wg, w_br_conv[l].astype(BF16), w_o[l].astype(BF16), row(ln1_g[l]), row(ln1_b[l]), wr, br)

        bconv, qt, k, vt, ckv, krt, nconv = _inproj_prompt(h_p, tab_p, tabt_p, w1, wuqt, wuk, wuvt, qg, kvg, conv_w[l],
                                                          n_p, t_p)
        ot = _prompt_attention(qt, k, vt, n_p, t_p)

        st = state_conv[l]
        bconv_s, q_s, qlat, ckvn_s, krn_s, u_s = _inproj_sample(
            h_s, tab_s, w1, wuq, wukt, qg, kvg, conv_w[l], st[:, 0], st[:, 1])
        olat = _sample_attention(page_table, qlat, q_s, ckvn_s, krn_s, cache_ckv[l],
                                 jnp.swapaxes(cache_krope[l], 1, 2))
        h1_s, route_s = _merge_sample(alpha, h_s, bconv_s, olat.reshape(n_s, N_HEADS * KV_LORA), wbd, wba, merge_w)
        h1t, route, route_t, counts = _merge_prompt(alpha, h_p, bconv, ot, h1_s, route_s, wba, merge_w)

        dest, blk_e, used, nvalid = _dispatch_plan(route_t, counts, n_blocks)
        slot_tok = _slot_tokens(dest, n_blocks)
        ys = _experts(blk_e, used, nvalid, slot_tok, h1t, w_gate[l], w_up[l], w_down[l])
        dest = dest.reshape(-1)
        ln2 = (row(ln2_g[l]), row(ln2_b[l]))
        h_p = _combine(alpha, 0, rows_p, TM_COMBINE, dest, ys, h1t, route, *ln2)
        h_s = _combine(alpha, rows_p, n_s, n_s, dest, ys, h1t, route, *ln2)

        ckv_p.append(ckv.reshape(n_p, t_p, KV_LORA))
        kr_p.append(jnp.swapaxes(krt, 1, 2))
        cv_p.append(nconv)
        ckv_s.append(ckvn_s.reshape(n_s, 1, KV_LORA))
        kr_s.append(krn_s.reshape(n_s, 1, ROPE_DIM))
        cv_s.append(jnp.stack([st[:, 1], u_s], axis=1))
    return (h_p.reshape(n_p, t_p, D_MODEL), h_s.reshape(n_s, 1, D_MODEL), jnp.stack(ckv_p), jnp.stack(kr_p),
            jnp.stack(cv_p), jnp.stack(ckv_s), jnp.stack(kr_s), jnp.stack(cv_s))
```
